```python
import math
import jax, jax.numpy as jnp
from jax import lax
import numpy as np

D_MODEL = 1024
BATCH = 8
SEQ = 16384
DEPTH = 4

N_META = 16
N_HEADS = 16
N_KV_HEADS = 4
HEAD_DIM = 64
Q_GROUP = N_HEADS // N_KV_HEADS
WINDOW = 128
BLOCK = 128
SSM_WIDTH = D_MODEL // 2
SSM_GROUP_CH = 16
SSM_GROUPS = SSM_WIDTH // SSM_GROUP_CH
SSM_STATE = 64
N_DIR = 2
DT_MIN = 1e-3
DT_MAX = 1e-1
D_FF = 2816
EPS = 1e-6
NEG = -1e30

ATTN_Q = N_HEADS * HEAD_DIM
ATTN_KV = N_KV_HEADS * HEAD_DIM
SPLITS = [ATTN_Q, ATTN_Q + ATTN_KV, ATTN_Q + 2 * ATTN_KV,
          ATTN_Q + 2 * ATTN_KV + SSM_WIDTH,
          ATTN_Q + 2 * ATTN_KV + SSM_WIDTH + D_MODEL]
IN_COLS = ATTN_Q + 2 * ATTN_KV + SSM_WIDTH + 2 * D_MODEL

kernel_name = "hybrid_s5_swa_macaron_encoder"


def alibi_slopes():
    s = 2.0 ** (-8.0 * np.arange(1, N_HEADS + 1) / N_HEADS)
    return jnp.asarray(s.reshape(N_KV_HEADS, Q_GROUP), dtype=jnp.float32)


def rmsnorm(x, g):
    xf = x.astype(jnp.float32)
    y = xf * lax.rsqrt(jnp.mean(xf * xf, axis=-1, keepdims=True) + EPS)
    return (y * g.astype(jnp.float32)).astype(x.dtype)


def swiglu(h, w_gate, w_up, w_down):
    return (jax.nn.silu(h @ w_gate) * (h @ w_up)) @ w_down


def s5_direction(ug, lam_re, lam_im, log_dt, b_re, b_im, c_re, c_im, reverse):
    f32 = jnp.float32
    lam = lax.complex(lam_re.astype(f32), lam_im.astype(f32))
    dt = jnp.exp(log_dt.astype(f32))[:, None]
    lam_bar = jnp.exp(lam * dt)
    b_bar = ((lam_bar - 1.0) / lam)[:, :, None] * lax.complex(b_re.astype(f32), b_im.astype(f32))
    c_mat = lax.complex(c_re.astype(f32), c_im.astype(f32))
    bu = jnp.einsum('blgc,gpc->blgp', ug, b_bar)
    a = jnp.broadcast_to(lam_bar, bu.shape)

    def combine(e1, e2):
        a1, b1 = e1
        a2, b2 = e2
        return a1 * a2, a2 * b1 + b2

    _, states = lax.associative_scan(combine, (a, bu), reverse=reverse, axis=1)
    return jnp.einsum('blgp,gcp->blgc', states, c_mat).real


def ssm_branch(u, lam_re, lam_im, log_dt, b_re, b_im, c_re, c_im, d, w_glu):
    B, L, _ = u.shape
    uf = u.astype(jnp.float32)
    ug = uf.reshape(B, L, SSM_GROUPS, SSM_GROUP_CH)
    y = jnp.zeros_like(ug)
    for dr in range(N_DIR):
        y = y + s5_direction(ug, lam_re[dr], lam_im[dr], log_dt[dr], b_re[dr], b_im[dr],
                             c_re[dr], c_im[dr], reverse=(dr == 1))
    y = y.reshape(B, L, SSM_WIDTH) + d.astype(jnp.float32) * uf
    z = jax.nn.gelu(y).astype(u.dtype)
    return z * jax.nn.sigmoid(z @ w_glu)


def windowed_gqa(q, k, v, sink):
    B, L = q.shape[:2]
    pad = BLOCK - N_META
    Lp = L + pad
    nb = Lp // BLOCK
    k_meta, v_meta = k[:, :N_META], v[:, :N_META]

    def to_blocks(t):
        tp = jnp.pad(t, ((0, 0), (pad, 0), (0, 0), (0, 0)))
        return tp.reshape(B, nb, BLOCK, *t.shape[2:])

    def band(t):
        tb = jnp.pad(to_blocks(t), ((0, 0), (1, 1), (0, 0), (0, 0), (0, 0)))
        return jnp.concatenate([tb[:, :-2], tb[:, 1:-1], tb[:, 2:]], axis=2)

    qb = to_blocks(q).reshape(B, nb, BLOCK, N_KV_HEADS, Q_GROUP, HEAD_DIM)
    kband, vband = band(k), band(v)
    scale = HEAD_DIM ** -0.5

    qi = jnp.arange(BLOCK)
    sj = jnp.arange(3 * BLOCK)
    dist = jnp.abs(qi[:, None] + BLOCK - sj[None, :])
    kpos = (jnp.arange(nb)[:, None] - 1) * BLOCK + sj[None, :]
    kvalid = (kpos >= BLOCK) & (kpos < Lp)
    valid = (dist <= WINDOW)[None] & kvalid[:, None, :]
    slopes = alibi_slopes()

    s_band = jnp.einsum('bnqkgd,bnskd->bnkgqs', qb, kband,
                        preferred_element_type=jnp.float32) * scale
    s_band = s_band - slopes[:, :, None, None] * dist.astype(jnp.float32)
    s_band = jnp.where(valid[None, :, None, None], s_band, NEG)
    s_meta = jnp.einsum('bnqkgd,bmkd->bnkgqm', qb, k_meta,
                        preferred_element_type=jnp.float32) * scale
    s_sink = jnp.broadcast_to(sink.astype(jnp.float32).reshape(N_KV_HEADS, Q_GROUP)[None, None, :, :, None, None],
                              (B, nb, N_KV_HEADS, Q_GROUP, BLOCK, 1))
    p = jax.nn.softmax(jnp.concatenate([s_band, s_meta, s_sink], axis=-1), axis=-1)
    p_band = p[..., :3 * BLOCK].astype(v.dtype)
    p_meta = p[..., 3 * BLOCK:3 * BLOCK + N_META].astype(v.dtype)
    out = (jnp.einsum('bnkgqs,bnskd->bnqkgd', p_band, vband)
           + jnp.einsum('bnkgqm,bmkd->bnqkgd', p_meta, v_meta))
    return out.reshape(B, Lp, N_HEADS * HEAD_DIM)[:, pad:]


def mixer(h, w_in, lam_re, lam_im, log_dt, b_re, b_im, c_re, c_im, d, w_glu, sink,
          w_branch_ssm, w_branch_attn, w_out):
    B, L, _ = h.shape
    proj = h @ w_in
    q, k, v, u, g_s, g_a = jnp.split(proj, SPLITS, axis=-1)
    y_attn = windowed_gqa(q.reshape(B, L, N_HEADS, HEAD_DIM),
                          k.reshape(B, L, N_KV_HEADS, HEAD_DIM),
                          v.reshape(B, L, N_KV_HEADS, HEAD_DIM), sink)
    y_ssm = ssm_branch(u, lam_re, lam_im, log_dt, b_re, b_im, c_re, c_im, d, w_glu)
    merged = (jax.nn.sigmoid(g_s) * (y_ssm @ w_branch_ssm)
              + jax.nn.sigmoid(g_a) * (y_attn @ w_branch_attn))
    return merged @ w_out


def _fwd_setup_inputs(seed: int = 0) -> dict:
    key = jax.random.key(seed)
    ks = jax.random.split(key, 32)
    f32 = jnp.float32

    def nrm(k, shape, fan_in):
        return jax.random.normal(k, shape, f32) * fan_in ** -0.5

    def gain(k, shape):
        return 1.0 + 0.05 * jax.random.normal(k, shape, f32)

    G, P, CH = SSM_GROUPS, SSM_STATE, SSM_GROUP_CH
    n = jnp.arange(P, dtype=f32)
    return {
        "x": jax.random.normal(ks[0], (BATCH, SEQ, D_MODEL), f32),
        "meta_tokens": jax.random.normal(ks[1], (N_META, D_MODEL), f32),
        "ffn1_norm": gain(ks[2], (DEPTH, D_MODEL)),
        "ffn1_w_gate": nrm(ks[3], (DEPTH, D_MODEL, D_FF), D_MODEL),
        "ffn1_w_up": nrm(ks[4], (DEPTH, D_MODEL, D_FF), D_MODEL),
        "ffn1_w_down": nrm(ks[5], (DEPTH, D_FF, D_MODEL), D_FF),
        "mix_norm": gain(ks[6], (DEPTH, D_MODEL)),
        "w_in": nrm(ks[7], (DEPTH, D_MODEL, IN_COLS), D_MODEL),
        "ssm_lam_re": -0.5 + 0.01 * jax.random.normal(ks[8], (DEPTH, N_DIR, G, P), f32),
        "ssm_lam_im": math.pi * n + 0.01 * jax.random.normal(ks[9], (DEPTH, N_DIR, G, P), f32),
        "ssm_log_dt": jax.random.uniform(ks[10], (DEPTH, N_DIR, G), f32,
                                         minval=math.log(DT_MIN), maxval=math.log(DT_MAX)),
        "ssm_b_re": nrm(ks[11], (DEPTH, N_DIR, G, P, CH), 2 * CH),
        "ssm_b_im": nrm(ks[12], (DEPTH, N_DIR, G, P, CH), 2 * CH),
        "ssm_c_re": nrm(ks[13], (DEPTH, N_DIR, G, CH, P), 2 * P),
        "ssm_c_im": nrm(ks[14], (DEPTH, N_DIR, G, CH, P), 2 * P),
        "ssm_d": jax.random.normal(ks[15], (DEPTH, SSM_WIDTH), f32),
        "ssm_w_glu": nrm(ks[16], (DEPTH, SSM_WIDTH, SSM_WIDTH), SSM_WIDTH),
        "attn_sink": 0.5 * jax.random.normal(ks[17], (DEPTH, N_HEADS), f32),
        "w_branch_ssm": nrm(ks[18], (DEPTH, SSM_WIDTH, D_MODEL), SSM_WIDTH),
        "w_branch_attn": nrm(ks[19], (DEPTH, ATTN_Q, D_MODEL), ATTN_Q),
        "w_out": nrm(ks[20], (DEPTH, D_MODEL, D_MODEL), D_MODEL),
        "ffn2_norm": gain(ks[21], (DEPTH, D_MODEL)),
        "ffn2_w_gate": nrm(ks[22], (DEPTH, D_MODEL, D_FF), D_MODEL),
        "ffn2_w_up": nrm(ks[23], (DEPTH, D_MODEL, D_FF), D_MODEL),
        "ffn2_w_down": nrm(ks[24], (DEPTH, D_FF, D_MODEL), D_FF),
        "final_norm": gain(ks[25], (D_MODEL,)),
    }


def _fwd_reference(x, meta_tokens, ffn1_norm, ffn1_w_gate, ffn1_w_up, ffn1_w_down,
              mix_norm, w_in, ssm_lam_re, ssm_lam_im, ssm_log_dt, ssm_b_re, ssm_b_im,
              ssm_c_re, ssm_c_im, ssm_d, ssm_w_glu, attn_sink, w_branch_ssm,
              w_branch_attn, w_out, ffn2_norm, ffn2_w_gate, ffn2_w_up, ffn2_w_down,
              final_norm):
    B = x.shape[0]
    meta = jnp.broadcast_to(meta_tokens.astype(x.dtype)[None], (B, N_META, D_MODEL))
    h = jnp.concatenate([meta, x], axis=1)
    for l in range(DEPTH):
        h = h + 0.5 * swiglu(rmsnorm(h, ffn1_norm[l]), ffn1_w_gate[l], ffn1_w_up[l], ffn1_w_down[l])
        h = h + mixer(rmsnorm(h, mix_norm[l]), w_in[l], ssm_lam_re[l], ssm_lam_im[l],
                      ssm_log_dt[l], ssm_b_re[l], ssm_b_im[l], ssm_c_re[l], ssm_c_im[l],
                      ssm_d[l], ssm_w_glu[l], attn_sink[l], w_branch_ssm[l],
                      w_branch_attn[l], w_out[l])
        h = h + 0.5 * swiglu(rmsnorm(h, ffn2_norm[l]), ffn2_w_gate[l], ffn2_w_up[l], ffn2_w_down[l])
    h = rmsnorm(h, final_norm)
    return h[:, N_META:]


import jax as _jax
import jax.numpy as _jnp

TWIN_FORMAT = 'train_step'
FWD_PARAMS = ['x', 'meta_tokens', 'ffn1_norm', 'ffn1_w_gate', 'ffn1_w_up', 'ffn1_w_down', 'mix_norm', 'w_in', 'ssm_lam_re', 'ssm_lam_im', 'ssm_log_dt', 'ssm_b_re', 'ssm_b_im', 'ssm_c_re', 'ssm_c_im', 'ssm_d', 'ssm_w_glu', 'attn_sink', 'w_branch_ssm', 'w_branch_attn', 'w_out', 'ffn2_norm', 'ffn2_w_gate', 'ffn2_w_up', 'ffn2_w_down', 'final_norm']
TWIN_WEIGHTS = ['meta_tokens', 'ffn1_norm', 'ffn1_w_gate', 'ffn1_w_up', 'ffn1_w_down', 'mix_norm', 'w_in', 'ssm_lam_re', 'ssm_lam_im', 'ssm_log_dt', 'ssm_b_re', 'ssm_b_im', 'ssm_c_re', 'ssm_c_im', 'ssm_d', 'ssm_w_glu', 'attn_sink', 'w_branch_ssm', 'w_branch_attn', 'w_out', 'ffn2_norm', 'ffn2_w_gate', 'ffn2_w_up', 'ffn2_w_down', 'final_norm']
TWIN_DIFF_INPUT = 'x'
TWIN_INPUTS = ['x', 'meta_tokens', 'ffn1_norm', 'ffn1_w_gate', 'ffn1_w_up', 'ffn1_w_down', 'mix_norm', 'w_in', 'ssm_lam_re', 'ssm_lam_im', 'ssm_log_dt', 'ssm_b_re', 'ssm_b_im', 'ssm_c_re', 'ssm_c_im', 'ssm_d', 'ssm_w_glu', 'attn_sink', 'w_branch_ssm', 'w_branch_attn', 'w_out', 'ffn2_norm', 'ffn2_w_gate', 'ffn2_w_up', 'ffn2_w_down', 'final_norm', 'loss_target', 'm_meta_tokens', 'm_ffn1_norm', 'm_ffn1_w_gate', 'm_ffn1_w_up', 'm_ffn1_w_down', 'm_mix_norm', 'm_w_in', 'm_ssm_lam_re', 'm_ssm_lam_im', 'm_ssm_log_dt', 'm_ssm_b_re', 'm_ssm_b_im', 'm_ssm_c_re', 'm_ssm_c_im', 'm_ssm_d', 'm_ssm_w_glu', 'm_attn_sink', 'm_w_branch_ssm', 'm_w_branch_attn', 'm_w_out', 'm_ffn2_norm', 'm_ffn2_w_gate', 'm_ffn2_w_up', 'm_ffn2_w_down', 'm_final_norm', 'v_meta_tokens', 'v_ffn1_norm', 'v_ffn1_w_gate', 'v_ffn1_w_up', 'v_ffn1_w_down', 'v_mix_norm', 'v_w_in', 'v_ssm_lam_re', 'v_ssm_lam_im', 'v_ssm_log_dt', 'v_ssm_b_re', 'v_ssm_b_im', 'v_ssm_c_re', 'v_ssm_c_im', 'v_ssm_d', 'v_ssm_w_glu', 'v_attn_sink', 'v_w_branch_ssm', 'v_w_branch_attn', 'v_w_out', 'v_ffn2_norm', 'v_ffn2_w_gate', 'v_ffn2_w_up', 'v_ffn2_w_down', 'v_final_norm']
TWIN_OUTPUTS = ['loss', 'grad_x', 'grad_meta_tokens', 'grad_ffn1_norm', 'grad_ffn1_w_gate', 'grad_ffn1_w_up', 'grad_ffn1_w_down', 'grad_mix_norm', 'grad_w_in', 'grad_ssm_lam_re', 'grad_ssm_lam_im', 'grad_ssm_log_dt', 'grad_ssm_b_re', 'grad_ssm_b_im', 'grad_ssm_c_re', 'grad_ssm_c_im', 'grad_ssm_d', 'grad_ssm_w_glu', 'grad_attn_sink', 'grad_w_branch_ssm', 'grad_w_branch_attn', 'grad_w_out', 'grad_ffn2_norm', 'grad_ffn2_w_gate', 'grad_ffn2_w_up', 'grad_ffn2_w_down', 'grad_final_norm', 'delta_meta_tokens', 'delta_ffn1_norm', 'delta_ffn1_w_gate', 'delta_ffn1_w_up', 'delta_ffn1_w_down', 'delta_mix_norm', 'delta_w_in', 'delta_ssm_lam_re', 'delta_ssm_lam_im', 'delta_ssm_log_dt', 'delta_ssm_b_re', 'delta_ssm_b_im', 'delta_ssm_c_re', 'delta_ssm_c_im', 'delta_ssm_d', 'delta_ssm_w_glu', 'delta_attn_sink', 'delta_w_branch_ssm', 'delta_w_branch_attn', 'delta_w_out', 'delta_ffn2_norm', 'delta_ffn2_w_gate', 'delta_ffn2_w_up', 'delta_ffn2_w_down', 'delta_final_norm', 'new_m_meta_tokens', 'new_m_ffn1_norm', 'new_m_ffn1_w_gate', 'new_m_ffn1_w_up', 'new_m_ffn1_w_down', 'new_m_mix_norm', 'new_m_w_in', 'new_m_ssm_lam_re', 'new_m_ssm_lam_im', 'new_m_ssm_log_dt', 'new_m_ssm_b_re', 'new_m_ssm_b_im', 'new_m_ssm_c_re', 'new_m_ssm_c_im', 'new_m_ssm_d', 'new_m_ssm_w_glu', 'new_m_attn_sink', 'new_m_w_branch_ssm', 'new_m_w_branch_attn', 'new_m_w_out', 'new_m_ffn2_norm', 'new_m_ffn2_w_gate', 'new_m_ffn2_w_up', 'new_m_ffn2_w_down', 'new_m_final_norm', 'new_v_meta_tokens', 'new_v_ffn1_norm', 'new_v_ffn1_w_gate', 'new_v_ffn1_w_up', 'new_v_ffn1_w_down', 'new_v_mix_norm', 'new_v_w_in', 'new_v_ssm_lam_re', 'new_v_ssm_lam_im', 'new_v_ssm_log_dt', 'new_v_ssm_b_re', 'new_v_ssm_b_im', 'new_v_ssm_c_re', 'new_v_ssm_c_im', 'new_v_ssm_d', 'new_v_ssm_w_glu', 'new_v_attn_sink', 'new_v_w_branch_ssm', 'new_v_w_branch_attn', 'new_v_w_out', 'new_v_ffn2_norm', 'new_v_ffn2_w_gate', 'new_v_ffn2_w_up', 'new_v_ffn2_w_down', 'new_v_final_norm']
TWIN_LEAF_KINDS = {'loss': 'loss', 'grad_x': 'grad_x', 'grad_meta_tokens': 'grad_w', 'grad_ffn1_norm': 'grad_w', 'grad_ffn1_w_gate': 'grad_w', 'grad_ffn1_w_up': 'grad_w', 'grad_ffn1_w_down': 'grad_w', 'grad_mix_norm': 'grad_w', 'grad_w_in': 'grad_w', 'grad_ssm_lam_re': 'grad_w', 'grad_ssm_lam_im': 'grad_w', 'grad_ssm_log_dt': 'grad_w', 'grad_ssm_b_re': 'grad_w', 'grad_ssm_b_im': 'grad_w', 'grad_ssm_c_re': 'grad_w', 'grad_ssm_c_im': 'grad_w', 'grad_ssm_d': 'grad_w', 'grad_ssm_w_glu': 'grad_w', 'grad_attn_sink': 'grad_w', 'grad_w_branch_ssm': 'grad_w', 'grad_w_branch_attn': 'grad_w', 'grad_w_out': 'grad_w', 'grad_ffn2_norm': 'grad_w', 'grad_ffn2_w_gate': 'grad_w', 'grad_ffn2_w_up': 'grad_w', 'grad_ffn2_w_down': 'grad_w', 'grad_final_norm': 'grad_w', 'delta_meta_tokens': 'delta_w', 'delta_ffn1_norm': 'delta_w', 'delta_ffn1_w_gate': 'delta_w', 'delta_ffn1_w_up': 'delta_w', 'delta_ffn1_w_down': 'delta_w', 'delta_mix_norm': 'delta_w', 'delta_w_in': 'delta_w', 'delta_ssm_lam_re': 'delta_w', 'delta_ssm_lam_im': 'delta_w', 'delta_ssm_log_dt': 'delta_w', 'delta_ssm_b_re': 'delta_w', 'delta_ssm_b_im': 'delta_w', 'delta_ssm_c_re': 'delta_w', 'delta_ssm_c_im': 'delta_w', 'delta_ssm_d': 'delta_w', 'delta_ssm_w_glu': 'delta_w', 'delta_attn_sink': 'delta_w', 'delta_w_branch_ssm': 'delta_w', 'delta_w_branch_attn': 'delta_w', 'delta_w_out': 'delta_w', 'delta_ffn2_norm': 'delta_w', 'delta_ffn2_w_gate': 'delta_w', 'delta_ffn2_w_up': 'delta_w', 'delta_ffn2_w_down': 'delta_w', 'delta_final_norm': 'delta_w', 'new_m_meta_tokens': 'new_m', 'new_m_ffn1_norm': 'new_m', 'new_m_ffn1_w_gate': 'new_m', 'new_m_ffn1_w_up': 'new_m', 'new_m_ffn1_w_down': 'new_m', 'new_m_mix_norm': 'new_m', 'new_m_w_in': 'new_m', 'new_m_ssm_lam_re': 'new_m', 'new_m_ssm_lam_im': 'new_m', 'new_m_ssm_log_dt': 'new_m', 'new_m_ssm_b_re': 'new_m', 'new_m_ssm_b_im': 'new_m', 'new_m_ssm_c_re': 'new_m', 'new_m_ssm_c_im': 'new_m', 'new_m_ssm_d': 'new_m', 'new_m_ssm_w_glu': 'new_m', 'new_m_attn_sink': 'new_m', 'new_m_w_branch_ssm': 'new_m', 'new_m_w_branch_attn': 'new_m', 'new_m_w_out': 'new_m', 'new_m_ffn2_norm': 'new_m', 'new_m_ffn2_w_gate': 'new_m', 'new_m_ffn2_w_up': 'new_m', 'new_m_ffn2_w_down': 'new_m', 'new_m_final_norm': 'new_m', 'new_v_meta_tokens': 'new_v', 'new_v_ffn1_norm': 'new_v', 'new_v_ffn1_w_gate': 'new_v', 'new_v_ffn1_w_up': 'new_v', 'new_v_ffn1_w_down': 'new_v', 'new_v_mix_norm': 'new_v', 'new_v_w_in': 'new_v', 'new_v_ssm_lam_re': 'new_v', 'new_v_ssm_lam_im': 'new_v', 'new_v_ssm_log_dt': 'new_v', 'new_v_ssm_b_re': 'new_v', 'new_v_ssm_b_im': 'new_v', 'new_v_ssm_c_re': 'new_v', 'new_v_ssm_c_im': 'new_v', 'new_v_ssm_d': 'new_v', 'new_v_ssm_w_glu': 'new_v', 'new_v_attn_sink': 'new_v', 'new_v_w_branch_ssm': 'new_v', 'new_v_w_branch_attn': 'new_v', 'new_v_w_out': 'new_v', 'new_v_ffn2_norm': 'new_v', 'new_v_ffn2_w_gate': 'new_v', 'new_v_ffn2_w_up': 'new_v', 'new_v_ffn2_w_down': 'new_v', 'new_v_final_norm': 'new_v'}


def _forward(args):
    return _fwd_reference(*[args[k] for k in FWD_PARAMS])


def _output_shape():
    def fwd():
        inp = _fwd_setup_inputs(0)
        return _fwd_reference(*[inp[k] for k in FWD_PARAMS])
    out = _jax.eval_shape(fwd)
    return out.shape, out.dtype

N_MICROBATCH = 1
ADAM_LR = 0.001
ADAM_B1 = 0.9
ADAM_B2 = 0.999
ADAM_EPS = 1e-08
ADAM_WD = 0.01
ADAM_STEP = 10
PER_EXAMPLE_BATCH_AXIS = {'x': 0, 'loss_target': 0}
SHARED_INPUTS = []
_WEIGHT_DTYPES = {'meta_tokens': _jnp.float32, 'ffn1_norm': _jnp.float32, 'ffn1_w_gate': _jnp.float32, 'ffn1_w_up': _jnp.float32, 'ffn1_w_down': _jnp.float32, 'mix_norm': _jnp.float32, 'w_in': _jnp.float32, 'ssm_lam_re': _jnp.float32, 'ssm_lam_im': _jnp.float32, 'ssm_log_dt': _jnp.float32, 'ssm_b_re': _jnp.float32, 'ssm_b_im': _jnp.float32, 'ssm_c_re': _jnp.float32, 'ssm_c_im': _jnp.float32, 'ssm_d': _jnp.float32, 'ssm_w_glu': _jnp.float32, 'attn_sink': _jnp.float32, 'w_branch_ssm': _jnp.float32, 'w_branch_attn': _jnp.float32, 'w_out': _jnp.float32, 'ffn2_norm': _jnp.float32, 'ffn2_w_gate': _jnp.float32, 'ffn2_w_up': _jnp.float32, 'ffn2_w_down': _jnp.float32, 'final_norm': _jnp.float32}
MOMENT_SCALE = {'meta_tokens': 4.110978e-02, 'ffn1_norm': 1.557123e-01, 'ffn1_w_gate': 6.727148e-02, 'ffn1_w_up': 6.560586e-02, 'ffn1_w_down': 1.090144e-01, 'mix_norm': 1.249193e-01, 'w_in': 6.071800e-02, 'ssm_lam_re': 6.462192e-03, 'ssm_lam_im': 6.440493e-03, 'ssm_log_dt': 3.960290e+00, 'ssm_b_re': 3.721103e-03, 'ssm_b_im': 3.641176e-03, 'ssm_c_re': 7.322765e-03, 'ssm_c_im': 7.370692e-03, 'ssm_d': 1.464977e-01, 'ssm_w_glu': 3.311566e-02, 'attn_sink': 1.401026e-02, 'w_branch_ssm': 1.042521e-01, 'w_branch_attn': 8.755712e-02, 'w_out': 1.393123e-01, 'ffn2_norm': 1.434159e-01, 'ffn2_w_gate': 6.099468e-02, 'ffn2_w_up': 5.995700e-02, 'ffn2_w_down': 9.965840e-02, 'final_norm': 1.280611e+02}


def _to_microbatches(a, axis):
    t = _jnp.moveaxis(a, axis, 0)
    t = t.reshape((N_MICROBATCH, t.shape[0] // N_MICROBATCH) + t.shape[1:])
    return _jnp.moveaxis(t, 1, axis + 1)


def setup_inputs(seed: int = 0) -> dict:
    inp = _fwd_setup_inputs(seed)
    key = _jax.random.fold_in(_jax.random.key(seed), 7919)
    shape, _ = _output_shape()
    out = dict(inp)
    out["loss_target"] = _jax.random.normal(_jax.random.fold_in(key, 0), shape, _jnp.float32)
    for i, name in enumerate(TWIN_WEIGHTS):
        w = inp[name].astype(_jnp.float32)
        if MOMENT_SCALE is None:
            s = _jnp.sqrt(_jnp.mean(_jnp.square(w)) + 1e-30)
        else:
            s = MOMENT_SCALE[name]
        km, kv = _jax.random.split(_jax.random.fold_in(key, i + 1))
        out[name] = w
        out["m_" + name] = s * _jax.random.normal(km, w.shape, _jnp.float32)
        out["v_" + name] = (s * s) * _jax.random.uniform(kv, w.shape, _jnp.float32, 0.5, 1.5)
    if N_MICROBATCH > 1:
        for name, axis in PER_EXAMPLE_BATCH_AXIS.items():
            out[name] = _to_microbatches(out[name], axis)
    return {'x': out['x'], 'meta_tokens': out['meta_tokens'], 'ffn1_norm': out['ffn1_norm'], 'ffn1_w_gate': out['ffn1_w_gate'], 'ffn1_w_up': out['ffn1_w_up'], 'ffn1_w_down': out['ffn1_w_down'], 'mix_norm': out['mix_norm'], 'w_in': out['w_in'], 'ssm_lam_re': out['ssm_lam_re'], 'ssm_lam_im': out['ssm_lam_im'], 'ssm_log_dt': out['ssm_log_dt'], 'ssm_b_re': out['ssm_b_re'], 'ssm_b_im': out['ssm_b_im'], 'ssm_c_re': out['ssm_c_re'], 'ssm_c_im': out['ssm_c_im'], 'ssm_d': out['ssm_d'], 'ssm_w_glu': out['ssm_w_glu'], 'attn_sink': out['attn_sink'], 'w_branch_ssm': out['w_branch_ssm'], 'w_branch_attn': out['w_branch_attn'], 'w_out': out['w_out'], 'ffn2_norm': out['ffn2_norm'], 'ffn2_w_gate': out['ffn2_w_gate'], 'ffn2_w_up': out['ffn2_w_up'], 'ffn2_w_down': out['ffn2_w_down'], 'final_norm': out['final_norm'], 'loss_target': out['loss_target'], 'm_meta_tokens': out['m_meta_tokens'], 'm_ffn1_norm': out['m_ffn1_norm'], 'm_ffn1_w_gate': out['m_ffn1_w_gate'], 'm_ffn1_w_up': out['m_ffn1_w_up'], 'm_ffn1_w_down': out['m_ffn1_w_down'], 'm_mix_norm': out['m_mix_norm'], 'm_w_in': out['m_w_in'], 'm_ssm_lam_re': out['m_ssm_lam_re'], 'm_ssm_lam_im': out['m_ssm_lam_im'], 'm_ssm_log_dt': out['m_ssm_log_dt'], 'm_ssm_b_re': out['m_ssm_b_re'], 'm_ssm_b_im': out['m_ssm_b_im'], 'm_ssm_c_re': out['m_ssm_c_re'], 'm_ssm_c_im': out['m_ssm_c_im'], 'm_ssm_d': out['m_ssm_d'], 'm_ssm_w_glu': out['m_ssm_w_glu'], 'm_attn_sink': out['m_attn_sink'], 'm_w_branch_ssm': out['m_w_branch_ssm'], 'm_w_branch_attn': out['m_w_branch_attn'], 'm_w_out': out['m_w_out'], 'm_ffn2_norm': out['m_ffn2_norm'], 'm_ffn2_w_gate': out['m_ffn2_w_gate'], 'm_ffn2_w_up': out['m_ffn2_w_up'], 'm_ffn2_w_down': out['m_ffn2_w_down'], 'm_final_norm': out['m_final_norm'], 'v_meta_tokens': out['v_meta_tokens'], 'v_ffn1_norm': out['v_ffn1_norm'], 'v_ffn1_w_gate': out['v_ffn1_w_gate'], 'v_ffn1_w_up': out['v_ffn1_w_up'], 'v_ffn1_w_down': out['v_ffn1_w_down'], 'v_mix_norm': out['v_mix_norm'], 'v_w_in': out['v_w_in'], 'v_ssm_lam_re': out['v_ssm_lam_re'], 'v_ssm_lam_im': out['v_ssm_lam_im'], 'v_ssm_log_dt': out['v_ssm_log_dt'], 'v_ssm_b_re': out['v_ssm_b_re'], 'v_ssm_b_im': out['v_ssm_b_im'], 'v_ssm_c_re': out['v_ssm_c_re'], 'v_ssm_c_im': out['v_ssm_c_im'], 'v_ssm_d': out['v_ssm_d'], 'v_ssm_w_glu': out['v_ssm_w_glu'], 'v_attn_sink': out['v_attn_sink'], 'v_w_branch_ssm': out['v_w_branch_ssm'], 'v_w_branch_attn': out['v_w_branch_attn'], 'v_w_out': out['v_w_out'], 'v_ffn2_norm': out['v_ffn2_norm'], 'v_ffn2_w_gate': out['v_ffn2_w_gate'], 'v_ffn2_w_up': out['v_ffn2_w_up'], 'v_ffn2_w_down': out['v_ffn2_w_down'], 'v_final_norm': out['v_final_norm']}


def _loss(weights, diff, rest, loss_target):
    with _jax.named_scope("forward"):
        args = {**rest, TWIN_DIFF_INPUT: diff, **{k: w.astype(_WEIGHT_DTYPES[k]) for k, w in weights.items()}}
        y = _forward(args)
    with _jax.named_scope("loss_head"):
        err = _jnp.square(y.astype(_jnp.float32) - loss_target)
        return 0.5 * _jnp.sum(_jnp.mean(err, axis=-1)) if err.ndim else 0.5 * err


def _adamw(w, g, m, v):
    m = ADAM_B1 * m + (1.0 - ADAM_B1) * g
    v = ADAM_B2 * v + (1.0 - ADAM_B2) * _jnp.square(g)
    m_hat = m / (1.0 - ADAM_B1 ** ADAM_STEP)
    v_hat = v / (1.0 - ADAM_B2 ** ADAM_STEP)
    delta = -ADAM_LR * (m_hat / (_jnp.sqrt(v_hat) + ADAM_EPS) + ADAM_WD * w)
    return delta, m, v


def reference(x, meta_tokens, ffn1_norm, ffn1_w_gate, ffn1_w_up, ffn1_w_down, mix_norm, w_in, ssm_lam_re, ssm_lam_im, ssm_log_dt, ssm_b_re, ssm_b_im, ssm_c_re, ssm_c_im, ssm_d, ssm_w_glu, attn_sink, w_branch_ssm, w_branch_attn, w_out, ffn2_norm, ffn2_w_gate, ffn2_w_up, ffn2_w_down, final_norm, loss_target, m_meta_tokens, m_ffn1_norm, m_ffn1_w_gate, m_ffn1_w_up, m_ffn1_w_down, m_mix_norm, m_w_in, m_ssm_lam_re, m_ssm_lam_im, m_ssm_log_dt, m_ssm_b_re, m_ssm_b_im, m_ssm_c_re, m_ssm_c_im, m_ssm_d, m_ssm_w_glu, m_attn_sink, m_w_branch_ssm, m_w_branch_attn, m_w_out, m_ffn2_norm, m_ffn2_w_gate, m_ffn2_w_up, m_ffn2_w_down, m_final_norm, v_meta_tokens, v_ffn1_norm, v_ffn1_w_gate, v_ffn1_w_up, v_ffn1_w_down, v_mix_norm, v_w_in, v_ssm_lam_re, v_ssm_lam_im, v_ssm_log_dt, v_ssm_b_re, v_ssm_b_im, v_ssm_c_re, v_ssm_c_im, v_ssm_d, v_ssm_w_glu, v_attn_sink, v_w_branch_ssm, v_w_branch_attn, v_w_out, v_ffn2_norm, v_ffn2_w_gate, v_ffn2_w_up, v_ffn2_w_down, v_final_norm):
    given = dict(x=x, meta_tokens=meta_tokens, ffn1_norm=ffn1_norm, ffn1_w_gate=ffn1_w_gate, ffn1_w_up=ffn1_w_up, ffn1_w_down=ffn1_w_down, mix_norm=mix_norm, w_in=w_in, ssm_lam_re=ssm_lam_re, ssm_lam_im=ssm_lam_im, ssm_log_dt=ssm_log_dt, ssm_b_re=ssm_b_re, ssm_b_im=ssm_b_im, ssm_c_re=ssm_c_re, ssm_c_im=ssm_c_im, ssm_d=ssm_d, ssm_w_glu=ssm_w_glu, attn_sink=attn_sink, w_branch_ssm=w_branch_ssm, w_branch_attn=w_branch_attn, w_out=w_out, ffn2_norm=ffn2_norm, ffn2_w_gate=ffn2_w_gate, ffn2_w_up=ffn2_w_up, ffn2_w_down=ffn2_w_down, final_norm=final_norm, loss_target=loss_target, m_meta_tokens=m_meta_tokens, m_ffn1_norm=m_ffn1_norm, m_ffn1_w_gate=m_ffn1_w_gate, m_ffn1_w_up=m_ffn1_w_up, m_ffn1_w_down=m_ffn1_w_down, m_mix_norm=m_mix_norm, m_w_in=m_w_in, m_ssm_lam_re=m_ssm_lam_re, m_ssm_lam_im=m_ssm_lam_im, m_ssm_log_dt=m_ssm_log_dt, m_ssm_b_re=m_ssm_b_re, m_ssm_b_im=m_ssm_b_im, m_ssm_c_re=m_ssm_c_re, m_ssm_c_im=m_ssm_c_im, m_ssm_d=m_ssm_d, m_ssm_w_glu=m_ssm_w_glu, m_attn_sink=m_attn_sink, m_w_branch_ssm=m_w_branch_ssm, m_w_branch_attn=m_w_branch_attn, m_w_out=m_w_out, m_ffn2_norm=m_ffn2_norm, m_ffn2_w_gate=m_ffn2_w_gate, m_ffn2_w_up=m_ffn2_w_up, m_ffn2_w_down=m_ffn2_w_down, m_final_norm=m_final_norm, v_meta_tokens=v_meta_tokens, v_ffn1_norm=v_ffn1_norm, v_ffn1_w_gate=v_ffn1_w_gate, v_ffn1_w_up=v_ffn1_w_up, v_ffn1_w_down=v_ffn1_w_down, v_mix_norm=v_mix_norm, v_w_in=v_w_in, v_ssm_lam_re=v_ssm_lam_re, v_ssm_lam_im=v_ssm_lam_im, v_ssm_log_dt=v_ssm_log_dt, v_ssm_b_re=v_ssm_b_re, v_ssm_b_im=v_ssm_b_im, v_ssm_c_re=v_ssm_c_re, v_ssm_c_im=v_ssm_c_im, v_ssm_d=v_ssm_d, v_ssm_w_glu=v_ssm_w_glu, v_attn_sink=v_attn_sink, v_w_branch_ssm=v_w_branch_ssm, v_w_branch_attn=v_w_branch_attn, v_w_out=v_w_out, v_ffn2_norm=v_ffn2_norm, v_ffn2_w_gate=v_ffn2_w_gate, v_ffn2_w_up=v_ffn2_w_up, v_ffn2_w_down=v_ffn2_w_down, v_final_norm=v_final_norm)
    weights = {n: given[n] for n in TWIN_WEIGHTS}
    shared = {n: given[n] for n in SHARED_INPUTS}
    per_example = {n: given[n] for n in ['x']}
    grad_fn = _jax.value_and_grad(_loss, argnums=(0, 1))

    def one_microbatch(ex, loss_target):
        ex = dict(ex)
        diff = ex.pop(TWIN_DIFF_INPUT)
        return grad_fn(weights, diff, {**shared, **ex}, loss_target)

    if N_MICROBATCH == 1:
        loss, (grad_w, grad_x) = one_microbatch(per_example, given["loss_target"])
    else:
        def body(carry, xs):
            loss_sum, grad_sum = carry
            l_k, (gw_k, gx_k) = one_microbatch(xs[0], xs[1])
            with _jax.named_scope("update"):
                return (loss_sum + l_k, _jax.tree.map(_jnp.add, grad_sum, gw_k)), gx_k

        init = (_jnp.zeros((), _jnp.float32), _jax.tree.map(_jnp.zeros_like, weights))
        (loss, grad_w), grad_x = _jax.lax.scan(body, init, (per_example, given["loss_target"]))
    with _jax.named_scope("update"):
        delta_w, new_m, new_v = {}, {}, {}
        for n in TWIN_WEIGHTS:
            delta_w[n], new_m[n], new_v[n] = _adamw(weights[n], grad_w[n], given["m_" + n], given["v_" + n])
    return (loss, grad_x, *[grad_w[n] for n in TWIN_WEIGHTS], *[delta_w[n] for n in TWIN_WEIGHTS],
            *[new_m[n] for n in TWIN_WEIGHTS], *[new_v[n] for n in TWIN_WEIGHTS])
```

```python
import functools
import math

import numpy as np
import jax
import jax.numpy as jnp
from jax import lax
from jax.experimental import pallas as pl
from jax.experimental.pallas import tpu as pltpu

F32 = jnp.float32
BF16 = jnp.bfloat16

D_MODEL = 1024
N_META = 16
N_HEADS = 16
N_KV_HEADS = 4
HEAD_DIM = 64
Q_GROUP = N_HEADS // N_KV_HEADS
WINDOW = 128
BLOCK = 128
PAD = BLOCK - N_META
SSM_WIDTH = 512
SSM_GROUP_CH = 16
SSM_GROUPS = 32
SSM_STATE = 64
N_STATE = SSM_GROUPS * SSM_STATE
SUPER = 4
D_FF = 2816
EPS = 1e-6
NEG = -1e30
ATTN_SCALE = HEAD_DIM ** -0.5
SLOPES = [float(2.0 ** (-8.0 * (h + 1) / N_HEADS)) for h in range(N_HEADS)]

ADAM_LR = 0.001
ADAM_B1 = 0.9
ADAM_B2 = 0.999
ADAM_EPS = 1e-08
ADAM_WD = 0.01
ADAM_STEP = 10

V7X_VMEM_LIMIT_BYTES = 52 * 1024 * 1024
ROW_TILE = 384
SCAN_CHUNK = 128
MESH_AXES = ("x", "y", "c")

BIG = ["ffn1_w_gate", "ffn1_w_up", "ffn1_w_down", "ffn2_w_gate", "ffn2_w_up", "ffn2_w_down",
       "w_in", "ssm_w_glu", "w_branch_ssm", "w_branch_attn", "w_out"]
COL_SHARDED = {"ffn1_w_gate", "ffn1_w_up", "ffn2_w_gate", "ffn2_w_up", "w_in", "w_branch_ssm"}
SMALL = ["meta_tokens", "ffn1_norm", "mix_norm", "ffn2_norm", "final_norm", "ssm_lam_re", "ssm_lam_im", "ssm_log_dt",
         "ssm_b_re", "ssm_b_im", "ssm_c_re", "ssm_c_im", "ssm_d", "attn_sink"]
WEIGHTS = ["meta_tokens", "ffn1_norm", "ffn1_w_gate", "ffn1_w_up", "ffn1_w_down", "mix_norm", "w_in", "ssm_lam_re",
           "ssm_lam_im", "ssm_log_dt", "ssm_b_re", "ssm_b_im", "ssm_c_re", "ssm_c_im", "ssm_d", "ssm_w_glu", "attn_sink",
           "w_branch_ssm", "w_branch_attn", "w_out", "ffn2_norm", "ffn2_w_gate", "ffn2_w_up", "ffn2_w_down", "final_norm"]


def _dot(a, b):
    return lax.dot_general(a.astype(BF16), b.astype(BF16), (((1,), (0,)), ((), ())), preferred_element_type=F32)


def _dot_nt(a, b):
    return lax.dot_general(a.astype(BF16), b.astype(BF16), (((1,), (1,)), ((), ())), preferred_element_type=F32)


def _dot_tn(a, b):
    return lax.dot_general(a.astype(BF16), b.astype(BF16), (((0,), (0,)), ((), ())), preferred_element_type=F32)


def _sigmoid(x):
    return 1.0 / (1.0 + jnp.exp(-x))


_GELU_C = math.sqrt(2.0 / math.pi)


def _gelu(x):
    return 0.5 * x * (1.0 + jnp.tanh(_GELU_C * (x + 0.044715 * x * x * x)))


def _gelu_grad(x):
    th = jnp.tanh(_GELU_C * (x + 0.044715 * x * x * x))
    return 0.5 * (1.0 + th) + 0.5 * x * (1.0 - th * th) * _GELU_C * (1.0 + 3.0 * 0.044715 * x * x)


def _rms_fwd(x, g):
    r = lax.rsqrt(jnp.mean(x * x, axis=-1, keepdims=True) + EPS)
    return x * r * g


def _rms_bwd(x, g, dn):
    r = lax.rsqrt(jnp.mean(x * x, axis=-1, keepdims=True) + EPS)
    xh = x * r
    t = dn * g
    dx = r * (t - xh * jnp.mean(t * xh, axis=-1, keepdims=True))
    return dx, jnp.sum(dn * xh, axis=0, keepdims=True)


def _compiler_params():
    return pltpu.CompilerParams(dimension_semantics=("arbitrary",), vmem_limit_bytes=V7X_VMEM_LIMIT_BYTES)


def _rows(arr, width=None, cb=0):
    return (arr, arr.shape[1] if width is None else width, cb)


def _rowk(name, fn, rows, fulls, outs, accs=(), tm=ROW_TILE, smem=(), n_rows=None):
    n = rows[0][0].shape[0] if n_rows is None else n_rows
    assert n % tm == 0, (name, n, tm)
    in_specs, args = [], []
    for s in smem:
        in_specs.append(pl.BlockSpec(memory_space=pltpu.SMEM))
        args.append(s)
    for r in rows:
        if callable(r[2]):
            in_specs.append(pl.BlockSpec(r[1], r[2]))
        else:
            in_specs.append(pl.BlockSpec((tm, r[1]), functools.partial(lambda i, cb: (i, cb), cb=r[2])))
        args.append(r[0])
    for f in fulls:
        in_specs.append(pl.BlockSpec(memory_space=pl.ANY))
        args.append(f)
    out_specs, out_shape = [], []
    for w, dt in outs:
        out_specs.append(pl.BlockSpec((tm, w), lambda i: (i, 0)))
        out_shape.append(jax.ShapeDtypeStruct((n, w), dt))
    for shp in accs:
        out_specs.append(pl.BlockSpec(shp, functools.partial(lambda i, nd: (0,) * nd, nd=len(shp))))
        out_shape.append(jax.ShapeDtypeStruct(shp, F32))
    ns, nr, nf, no, na = len(smem), len(rows), len(fulls), len(outs), len(accs)
    scratch = [pltpu.VMEM(f.shape, f.dtype) for f in fulls]
    if nf:
        scratch.append(pltpu.SemaphoreType.DMA((nf,)))

    def body(*refs):
        i = pl.program_id(0)
        sm = refs[:ns]
        rr = refs[ns:ns + nr]
        fh = refs[ns + nr:ns + nr + nf]
        oo = refs[ns + nr + nf:ns + nr + nf + no]
        aa = refs[ns + nr + nf + no:ns + nr + nf + no + na]
        fv = refs[ns + nr + nf + no + na:ns + nr + nf + no + na + nf]
        if nf:
            sem = refs[-1]

            @pl.when(i == 0)
            def _():
                cps = [pltpu.make_async_copy(fh[j], fv[j], sem.at[j]) for j in range(nf)]
                for cp in cps:
                    cp.start()
                for cp in cps:
                    cp.wait()
        res = fn(i, *sm, *rr, *fv)
        res = tuple(res) if isinstance(res, (tuple, list)) else (res,)
        for o, v in zip(oo, res[:no]):
            o[...] = v.astype(o.dtype)
        if na:
            @pl.when(i == 0)
            def _():
                for a in aa:
                    a[...] = jnp.zeros_like(a)
            for a, v in zip(aa, res[no:]):
                a[...] += v

    return pl.pallas_call(body, grid=(n // tm,), in_specs=in_specs, out_specs=out_specs, out_shape=out_shape,
                          scratch_shapes=scratch, name=name, compiler_params=_compiler_params())(*args)


def _mm_tn(name, x, y, *, xw=None, xcb=0, tk, tn, scale=1.0, tm=ROW_TILE):
    m = x.shape[0]
    k = x.shape[1] if xw is None else xw
    nn = y.shape[1]
    assert m % tm == 0 and k % tk == 0 and nn % tn == 0, (name, m, k, nn)
    kb0 = (xcb * k) // tk

    def body(x_ref, y_ref, o_ref):
        @pl.when(pl.program_id(2) == 0)
        def _():
            o_ref[...] = jnp.zeros_like(o_ref)
        yv = y_ref[...]
        if scale != 1.0:
            yv = yv * scale
        o_ref[...] += _dot_tn(x_ref[...], yv)

    return pl.pallas_call(
        body, grid=(k // tk, nn // tn, m // tm),
        in_specs=[pl.BlockSpec((tm, tk), lambda a, b, i: (i, kb0 + a)), pl.BlockSpec((tm, tn), lambda a, b, i: (i, b))],
        out_specs=pl.BlockSpec((tk, tn), lambda a, b, i: (a, b)), out_shape=jax.ShapeDtypeStruct((k, nn), F32), name=name,
        compiler_params=pltpu.CompilerParams(dimension_semantics=("arbitrary", "arbitrary", "arbitrary"),
                                             vmem_limit_bytes=V7X_VMEM_LIMIT_BYTES))(x, y)


def _exchange(name, src, n_out, local, sends, alias=False):
    nl, nsnd = len(local), len(sends)
    out_shape = jax.ShapeDtypeStruct((n_out,) + src.shape[1:], src.dtype)

    def body(src_ref, out_ref, lsem, ssem, rsem):
        x, y, c = lax.axis_index("x"), lax.axis_index("y"), lax.axis_index("c")
        cps = []
        for j, (sf, df) in enumerate(local):
            cp = pltpu.make_async_copy(src_ref.at[sf(x, y, c)], out_ref.at[df(x, y, c)], lsem.at[j])
            cp.start()
            cps.append(cp)
        for k, ((fx, fy, fc), sf, df) in enumerate(sends):
            peer = (1 - x if fx else x, 1 - y if fy else y, 1 - c if fc else c)
            cp = pltpu.make_async_remote_copy(src_ref=src_ref.at[sf(x, y, c)], dst_ref=out_ref.at[df(x, y, c)],
                                              send_sem=ssem.at[k], recv_sem=rsem.at[k], device_id=peer,
                                              device_id_type=pl.DeviceIdType.MESH)
            cp.start()
            cps.append(cp)
        for cp in cps:
            cp.wait()

    return pl.pallas_call(
        body, in_specs=[pl.BlockSpec(memory_space=pl.ANY)], out_specs=pl.BlockSpec(memory_space=pl.ANY), out_shape=out_shape,
        scratch_shapes=[pltpu.SemaphoreType.DMA((max(nl, 1),)), pltpu.SemaphoreType.DMA((nsnd,)), pltpu.SemaphoreType.DMA((nsnd,))],
        input_output_aliases=({0: 0} if alias else {}), name=name)(src)


def _chip(x, y):
    return 2 * x + y


_OTHER_CHIPS = [(1, 0), (0, 1), (1, 1)]


def _all_gather_shards(name, shard):
    r, w = shard.shape
    half = shard.reshape(2, r // 2, w)
    own = [(lambda x, y, c, h=h: h, lambda x, y, c, h=h: 2 * _chip(x, y) + h) for h in (0, 1)]
    first = [((fx, fy, 0), lambda x, y, c: c, lambda x, y, c: 2 * _chip(x, y) + c) for fx, fy in _OTHER_CHIPS]
    g = _exchange(name + "_ici", half, 8, own, first)
    second = [((0, 0, 1),
               (lambda x, y, c, fx=fx, fy=fy: 2 * _chip(x ^ fx, y ^ fy) + c),
               (lambda x, y, c, fx=fx, fy=fy: 2 * _chip(x ^ fx, y ^ fy) + c)) for fx, fy in _OTHER_CHIPS]
    g = _exchange(name + "_d2d", g, 8, [], second, alias=True)
    return g.reshape(4, r, w)


def _add_slots(name, arr, order, tm):
    rows_ = [(arr, (None, tm, arr.shape[2]), functools.partial(lambda i, s: (s, i, 0), s=s)) for s in order]

    def fn(i, *refs):
        acc = refs[0][...]
        for r in refs[1:]:
            acc = acc + r[...]
        return acc

    return _rowk(name, fn, rows_, [], [(arr.shape[2], arr.dtype)], tm=tm, n_rows=arr.shape[1])[0]


def _reduce_scatter(name, parts):
    _, _, r, w = parts.shape
    f = parts.reshape(2, 4 * r, w)
    g = _exchange(name + "_d2d", f, 2, [(lambda x, y, c: c, lambda x, y, c: 1)],
                  [((0, 0, 1), lambda x, y, c: 1 - c, lambda x, y, c: 0)])
    p = _add_slots(name + "_add2", g, (1, 0), 384).reshape(4, r, w)
    sends = [((fx, fy, 0), (lambda x, y, c, fx=fx, fy=fy: _chip(x ^ fx, y ^ fy)), (lambda x, y, c, k=k: k))
             for k, (fx, fy) in enumerate(_OTHER_CHIPS)]
    g = _exchange(name + "_ici", p, 4, [(lambda x, y, c: _chip(x, y), lambda x, y, c: 3)], sends)
    q = _add_slots(name + "_add4", g, (3, 0, 1, 2), 496)
    g = _exchange(name + "_pair", q[None], 2, [(lambda x, y, c: 0, lambda x, y, c: c)],
                  [((0, 0, 1), lambda x, y, c: 0, lambda x, y, c: c)])
    return g.reshape(2 * r, w)


def _all_gather_all(name, vec, n_slots, flips, slot_fn):
    sends = [(f, lambda x, y, c: 0, slot_fn) for f in flips]
    return _exchange(name, vec[None], n_slots, [(lambda x, y, c: 0, slot_fn)], sends)


def _nbr_specs(arr, width, cb, nb):
    return [
        (arr, (BLOCK, width), functools.partial(lambda n, cb: (jnp.maximum(n - 1, 0), cb), cb=cb)),
        (arr, (BLOCK, width), functools.partial(lambda n, cb: (n, cb), cb=cb)),
        (arr, (BLOCK, width), functools.partial(lambda n, cb: (jnp.minimum(n + 1, nb - 1), cb), cb=cb)),
    ]


def _attn_fwd(proj, sink):
    lp = proj.shape[0]
    nb = lp // BLOCK
    kv_w = N_KV_HEADS * HEAD_DIM
    specs = _nbr_specs(proj, kv_w, 4, nb) + _nbr_specs(proj, kv_w, 5, nb)
    specs += [(proj, (BLOCK, kv_w), lambda n: (0, 4)), (proj, (BLOCK, kv_w), lambda n: (0, 5))]
    in_specs = [pl.BlockSpec(memory_space=pltpu.SMEM), pl.BlockSpec((BLOCK, D_MODEL), lambda n: (n, 0))]
    in_specs += [pl.BlockSpec(s[1], s[2]) for s in specs]

    def body(sink_ref, q_ref, kp, kc, kn, vp, vc, vn, km, vm, o_ref, lse_ref):
        n = pl.program_id(0)
        qi = lax.broadcasted_iota(jnp.int32, (BLOCK, 3 * BLOCK), 0)
        sj = lax.broadcasted_iota(jnp.int32, (BLOCK, 3 * BLOCK), 1)
        dist = jnp.abs(qi + BLOCK - sj)
        kpos = (n - 1) * BLOCK + sj
        valid = (dist <= WINDOW) & (kpos >= BLOCK) & (kpos < lp)
        distf = dist.astype(F32)
        kb = jnp.concatenate([kp[...], kc[...], kn[...]], axis=0).astype(BF16)
        vb = jnp.concatenate([vp[...], vc[...], vn[...]], axis=0).astype(BF16)
        kmeta = km[PAD:BLOCK, :].astype(BF16)
        vmeta = vm[PAD:BLOCK, :].astype(BF16)
        for h in range(N_HEADS):
            kh = h // Q_GROUP
            ksl = slice(kh * HEAD_DIM, (kh + 1) * HEAD_DIM)
            hsl = slice(h * HEAD_DIM, (h + 1) * HEAD_DIM)
            qh = (q_ref[:, hsl] * ATTN_SCALE).astype(BF16)
            s = _dot_nt(qh, kb[:, ksl]) - SLOPES[h] * distf
            s = jnp.where(valid, s, NEG)
            sm = _dot_nt(qh, kmeta[:, ksl])
            sk = sink_ref[h]
            m = jnp.maximum(jnp.maximum(jnp.max(s, axis=1, keepdims=True), jnp.max(sm, axis=1, keepdims=True)), sk)
            e = jnp.exp(s - m)
            em = jnp.exp(sm - m)
            den = jnp.sum(e, axis=1, keepdims=True) + jnp.sum(em, axis=1, keepdims=True) + jnp.exp(sk - m)
            inv = 1.0 / den
            oh = _dot(e * inv, vb[:, ksl]) + _dot(em * inv, vmeta[:, ksl])
            o_ref[:, hsl] = oh.astype(o_ref.dtype)
            lse_ref[:, h:h + 1] = m + jnp.log(den)

    return pl.pallas_call(
        body, grid=(nb,), in_specs=in_specs,
        out_specs=[pl.BlockSpec((BLOCK, D_MODEL), lambda n: (n, 0)), pl.BlockSpec((BLOCK, N_HEADS), lambda n: (n, 0))],
        out_shape=[jax.ShapeDtypeStruct((lp, D_MODEL), BF16), jax.ShapeDtypeStruct((lp, N_HEADS), F32)],
        name="attn_fwd", compiler_params=_compiler_params())(sink, proj, *[s[0] for s in specs])


def _attn_bwd(proj, sink, o, lse, do):
    lp = proj.shape[0]
    nb = lp // BLOCK
    kv_w = N_KV_HEADS * HEAD_DIM
    specs = (_nbr_specs(proj, D_MODEL, 0, nb) + _nbr_specs(proj, kv_w, 4, nb) + _nbr_specs(proj, kv_w, 5, nb)
             + [(proj, (BLOCK, kv_w), lambda n: (0, 4)), (proj, (BLOCK, kv_w), lambda n: (0, 5))]
             + _nbr_specs(do, D_MODEL, 0, nb) + _nbr_specs(o, D_MODEL, 0, nb) + _nbr_specs(lse, N_HEADS, 0, nb))
    in_specs = [pl.BlockSpec(memory_space=pltpu.SMEM)] + [pl.BlockSpec(s[1], s[2]) for s in specs]

    def body(sink_ref, qp, qc, qn, kp, kc, kn, vp, vc, vn, km, vm, dop, doc, don, op, oc, on, lp_, lc, ln,
             dq_ref, dk_ref, dv_ref, dkm_ref, dvm_ref, dsk_ref):
        n = pl.program_id(0)

        @pl.when(n == 0)
        def _():
            dkm_ref[...] = jnp.zeros_like(dkm_ref)
            dvm_ref[...] = jnp.zeros_like(dvm_ref)
            dsk_ref[...] = jnp.zeros_like(dsk_ref)

        qi = lax.broadcasted_iota(jnp.int32, (BLOCK, 3 * BLOCK), 0)
        sj = lax.broadcasted_iota(jnp.int32, (BLOCK, 3 * BLOCK), 1)
        dist_q = jnp.abs(qi + BLOCK - sj)
        kpos = (n - 1) * BLOCK + sj
        valid_q = (dist_q <= WINDOW) & (kpos >= BLOCK) & (kpos < lp)
        distf_q = dist_q.astype(F32)
        bi = lax.broadcasted_iota(jnp.int32, (3 * BLOCK, BLOCK), 0)
        kj = lax.broadcasted_iota(jnp.int32, (3 * BLOCK, BLOCK), 1)
        dist_k = jnp.abs(bi - BLOCK - kj)
        qpos = (n - 1) * BLOCK + bi
        valid_k = (dist_k <= WINDOW) & (qpos >= 0) & (qpos < lp) & (n >= 1)
        distf_k = dist_k.astype(F32)

        kb = jnp.concatenate([kp[...], kc[...], kn[...]], axis=0).astype(BF16)
        vb = jnp.concatenate([vp[...], vc[...], vn[...]], axis=0).astype(BF16)
        kcur = kc[...].astype(BF16)
        vcur = vc[...].astype(BF16)
        kmeta = km[PAD:BLOCK, :].astype(BF16)
        vmeta = vm[PAD:BLOCK, :].astype(BF16)
        lane = lax.broadcasted_iota(jnp.int32, (1, BLOCK), 1)
        dsink = jnp.zeros((1, BLOCK), F32)
        for kh in range(N_KV_HEADS):
            ksl = slice(kh * HEAD_DIM, (kh + 1) * HEAD_DIM)
            dk_acc = jnp.zeros((BLOCK, HEAD_DIM), F32)
            dv_acc = jnp.zeros((BLOCK, HEAD_DIM), F32)
            dkm_acc = jnp.zeros((N_META, HEAD_DIM), F32)
            dvm_acc = jnp.zeros((N_META, HEAD_DIM), F32)
            for g in range(Q_GROUP):
                h = kh * Q_GROUP + g
                hsl = slice(h * HEAD_DIM, (h + 1) * HEAD_DIM)
                qh = (qc[:, hsl] * ATTN_SCALE).astype(BF16)
                doh = doc[:, hsl]
                delta = jnp.sum(doh * oc[:, hsl].astype(F32), axis=1, keepdims=True)
                lse_h = lc[:, h:h + 1]
                s = _dot_nt(qh, kb[:, ksl]) - SLOPES[h] * distf_q
                p = jnp.exp(jnp.where(valid_q, s, NEG) - lse_h)
                pm = jnp.exp(_dot_nt(qh, kmeta[:, ksl]) - lse_h)
                ps = jnp.exp(sink_ref[h] - lse_h)
                dohb = doh.astype(BF16)
                ds = p * (_dot_nt(dohb, vb[:, ksl]) - delta)
                dsm = pm * (_dot_nt(dohb, vmeta[:, ksl]) - delta)
                dq_ref[:, hsl] = (ATTN_SCALE * (_dot(ds, kb[:, ksl]) + _dot(dsm, kmeta[:, ksl]))).astype(dq_ref.dtype)
                dkm_acc = dkm_acc + _dot_tn(dsm, qh)
                dvm_acc = dvm_acc + _dot_tn(pm, dohb)
                dsink = dsink + jnp.where(lane == h, -jnp.sum(ps * delta), 0.0)
                qbh = (jnp.concatenate([qp[:, hsl], qc[:, hsl], qn[:, hsl]], axis=0) * ATTN_SCALE).astype(BF16)
                dob = jnp.concatenate([dop[:, hsl], doc[:, hsl], don[:, hsl]], axis=0)
                ob = jnp.concatenate([op[:, hsl], oc[:, hsl], on[:, hsl]], axis=0).astype(F32)
                delta_b = jnp.sum(dob * ob, axis=1, keepdims=True)
                lse_b = jnp.concatenate([lp_[:, h:h + 1], lc[:, h:h + 1], ln[:, h:h + 1]], axis=0)
                st = _dot_nt(qbh, kcur[:, ksl]) - SLOPES[h] * distf_k
                pt = jnp.exp(jnp.where(valid_k, st, NEG) - lse_b)
                dobb = dob.astype(BF16)
                dv_acc = dv_acc + _dot_tn(pt, dobb)
                dst = pt * (_dot_nt(dobb, vcur[:, ksl]) - delta_b)
                dk_acc = dk_acc + _dot_tn(dst, qbh)
            dk_ref[:, ksl] = dk_acc
            dv_ref[:, ksl] = dv_acc
            dkm_ref[:, ksl] += dkm_acc
            dvm_ref[:, ksl] += dvm_acc
        dsk_ref[...] += dsink

    blk = lambda w: pl.BlockSpec((BLOCK, w), lambda n: (n, 0))
    fix = lambda shp: pl.BlockSpec(shp, lambda n: (0, 0))
    return pl.pallas_call(
        body, grid=(nb,), in_specs=in_specs,
        out_specs=[blk(D_MODEL), blk(kv_w), blk(kv_w), fix((N_META, kv_w)), fix((N_META, kv_w)), fix((1, BLOCK))],
        out_shape=[jax.ShapeDtypeStruct((lp, D_MODEL), BF16), jax.ShapeDtypeStruct((lp, kv_w), F32),
                   jax.ShapeDtypeStruct((lp, kv_w), F32), jax.ShapeDtypeStruct((N_META, kv_w), F32),
                   jax.ShapeDtypeStruct((N_META, kv_w), F32), jax.ShapeDtypeStruct((1, BLOCK), F32)],
        name="attn_bwd", compiler_params=_compiler_params())(sink, *[s[0] for s in specs])


def _scan(name, bu, tables, reverse):
    lp = bu.shape[0]
    t = SCAN_CHUNK
    nc = lp // t
    p2r, p2i, apr, api = tables
    idx = (lambda i: nc - 1 - i) if reverse else (lambda i: i)

    def body(br_ref, bi_ref, p2r_ref, p2i_ref, apr_ref, api_ref, x_ref, cr_ref, ci_ref):
        @pl.when(pl.program_id(0) == 0)
        def _():
            cr_ref[...] = jnp.zeros_like(cr_ref)
            ci_ref[...] = jnp.zeros_like(ci_ref)

        xr = br_ref[...]
        xi = bi_ref[...]
        row = lax.broadcasted_iota(jnp.int32, xr.shape, 0)
        for k in range(7):
            s = 1 << k
            if reverse:
                shr, shi, keep = pltpu.roll(xr, t - s, axis=0), pltpu.roll(xi, t - s, axis=0), row < t - s
            else:
                shr, shi, keep = pltpu.roll(xr, s, axis=0), pltpu.roll(xi, s, axis=0), row >= s
            shr = jnp.where(keep, shr, 0.0)
            shi = jnp.where(keep, shi, 0.0)
            pr = p2r_ref[k:k + 1, :]
            pi = p2i_ref[k:k + 1, :]
            xr, xi = xr + pr * shr - pi * shi, xi + pr * shi + pi * shr
        cr = cr_ref[...]
        ci = ci_ref[...]
        ar = apr_ref[...]
        ai = api_ref[...]
        xr, xi = xr + ar * cr - ai * ci, xi + ar * ci + ai * cr
        x_ref[:, :N_STATE] = xr
        x_ref[:, N_STATE:] = xi
        edge = 0 if reverse else t - 1
        cr_ref[...] = xr[edge:edge + 1, :]
        ci_ref[...] = xi[edge:edge + 1, :]

    fix = lambda shp: pl.BlockSpec(shp, lambda i: (0, 0))
    return pl.pallas_call(
        body, grid=(nc,),
        in_specs=[pl.BlockSpec((t, N_STATE), lambda i: (idx(i), 0)), pl.BlockSpec((t, N_STATE), lambda i: (idx(i), 1)),
                  fix((8, N_STATE)), fix((8, N_STATE)), fix((t, N_STATE)), fix((t, N_STATE))],
        out_specs=pl.BlockSpec((t, 2 * N_STATE), lambda i: (idx(i), 0)),
        out_shape=jax.ShapeDtypeStruct((lp, 2 * N_STATE), F32),
        scratch_shapes=[pltpu.VMEM((1, N_STATE), F32), pltpu.VMEM((1, N_STATE), F32)],
        name=name, compiler_params=_compiler_params())(bu, bu, p2r, p2i, apr, api)


def _scan_tables(lam_re, lam_im, log_dt, conj):
    dt = jnp.exp(log_dt)[:, None]
    er = (lam_re * dt).reshape(1, N_STATE)
    ei = (lam_im * dt).reshape(1, N_STATE)
    if conj:
        ei = -ei
    k2 = jnp.asarray([[float(1 << min(k, 7))] for k in range(8)], F32)
    kt = jnp.arange(1, SCAN_CHUNK + 1, dtype=F32)[:, None]
    p2r, p2i = jnp.exp(k2 * er) * jnp.cos(k2 * ei), jnp.exp(k2 * er) * jnp.sin(k2 * ei)
    apr, api = jnp.exp(kt * er) * jnp.cos(kt * ei), jnp.exp(kt * er) * jnp.sin(kt * ei)
    return p2r, p2i, apr, api


def _ssm_prep(lam_re, lam_im, log_dt, b_re, b_im, c_re, c_im):
    dt = jnp.exp(log_dt)[:, None]
    er = jnp.exp(lam_re * dt)
    ar, ai = er * jnp.cos(lam_im * dt), er * jnp.sin(lam_im * dt)
    nr, ni = ar - 1.0, ai
    den = lam_re * lam_re + lam_im * lam_im
    cr, ci = (nr * lam_re + ni * lam_im) / den, (ni * lam_re - nr * lam_im) / den
    bbr = cr[:, :, None] * b_re - ci[:, :, None] * b_im
    bbi = cr[:, :, None] * b_im + ci[:, :, None] * b_re
    eye = jnp.eye(8, dtype=F32)

    def in_map(b):
        b = b.reshape(SUPER, 8, SSM_STATE, SSM_GROUP_CH).transpose(0, 1, 3, 2)
        return (b[:, :, :, None, :] * eye[None, :, None, :, None]).reshape(SUPER, 128, 512)

    def out_map(cm):
        cm = cm.reshape(SUPER, 8, SSM_GROUP_CH, SSM_STATE).transpose(0, 1, 3, 2)
        return (cm[:, :, :, None, :] * eye[None, :, None, :, None]).reshape(SUPER, 512, 128)

    wb = jnp.concatenate([in_map(bbr), in_map(bbi)], axis=2)
    wc = jnp.concatenate([out_map(c_re), -out_map(c_im)], axis=1)
    return ar.reshape(1, N_STATE), ai.reshape(1, N_STATE), wb, wc


def _ssm_in(name, proj, wb):
    def fn(i, u_ref, wb_ref):
        parts = [_dot(u_ref[:, 128 * j:128 * (j + 1)], wb_ref[j]) for j in range(SUPER)]
        return jnp.concatenate([p[:, :512] for p in parts] + [p[:, 512:] for p in parts], axis=1)

    return _rowk(name, fn, [_rows(proj, SSM_WIDTH, 3)], [wb.astype(BF16)], [(2 * N_STATE, F32)])[0]


def _state_cols(ref, j):
    return jnp.concatenate([ref[:, 512 * j:512 * (j + 1)], ref[:, N_STATE + 512 * j:N_STATE + 512 * (j + 1)]], axis=1)


def _ssm_param_grads(name, g, x, proj, dys, reverse):
    lp = x.shape[0]
    tm = SCAN_CHUNK
    nt = lp // tm
    if reverse:
        bnd = (x, (8, N_STATE), lambda i: (jnp.minimum((i + 1) * (tm // 8), lp // 8 - 1), 0))
        bnd_i = (x, (8, N_STATE), lambda i: (jnp.minimum((i + 1) * (tm // 8), lp // 8 - 1), 1))
    else:
        bnd = (x, (8, N_STATE), lambda i: (jnp.maximum(i * (tm // 8) - 1, 0), 0))
        bnd_i = (x, (8, N_STATE), lambda i: (jnp.maximum(i * (tm // 8) - 1, 0), 1))

    def fn(i, g_ref, x_ref, br_ref, bi_ref, u_ref, dy_ref):
        dwb = jnp.concatenate([_dot_tn(u_ref[:, 128 * j:128 * (j + 1)], _state_cols(g_ref, j)) for j in range(SUPER)], axis=0)
        dwc = jnp.concatenate([_dot_tn(_state_cols(x_ref, j), dy_ref[:, 128 * j:128 * (j + 1)]) for j in range(SUPER)], axis=0)
        gr, gi = g_ref[:, :N_STATE], g_ref[:, N_STATE:]
        xr, xi = x_ref[:, :N_STATE], x_ref[:, N_STATE:]
        row = lax.broadcasted_iota(jnp.int32, xr.shape, 0)
        if reverse:
            edge = jnp.where(i == nt - 1, 0.0, 1.0)
            pr = jnp.where(row == tm - 1, br_ref[0:1, :] * edge, pltpu.roll(xr, tm - 1, axis=0))
            pi = jnp.where(row == tm - 1, bi_ref[0:1, :] * edge, pltpu.roll(xi, tm - 1, axis=0))
        else:
            edge = jnp.where(i == 0, 0.0, 1.0)
            pr = jnp.where(row == 0, br_ref[7:8, :] * edge, pltpu.roll(xr, 1, axis=0))
            pi = jnp.where(row == 0, bi_ref[7:8, :] * edge, pltpu.roll(xi, 1, axis=0))
        gar = jnp.sum(gr * pr + gi * pi, axis=0, keepdims=True)
        gai = jnp.sum(gi * pr - gr * pi, axis=0, keepdims=True)
        return dwb, dwc, gar, gai

    return _rowk(name, fn, [_rows(g), _rows(x), bnd, bnd_i, _rows(proj, SSM_WIDTH, 3), _rows(dys)], [], [],
                 accs=[(SUPER * 128, 1024), (SUPER * 1024, 128), (1, N_STATE), (1, N_STATE)], tm=tm)


def _row_ids(i, tm, width):
    return i * tm + lax.broadcasted_iota(jnp.int32, (tm, width), 0)


def _ffn_fwd(tag, h, gain, wg, wu, wd):
    n = _rowk(tag + "_norm", lambda i, x, g: _rms_fwd(x[...], g[...]), [_rows(h)], [gain], [(D_MODEL, BF16)])[0]

    def up(i, n_ref, wg_ref, wu_ref):
        a = _dot(n_ref[...], wg_ref[...])
        b = _dot(n_ref[...], wu_ref[...])
        return a, b, a * _sigmoid(a) * b

    a, b, act = _rowk(tag + "_up", up, [_rows(n)], [wg, wu], [(D_FF, BF16)] * 3, tm=192)
    out = _rowk(tag + "_down", lambda i, act_ref, h_ref, wd_ref: h_ref[...] + 0.5 * _dot(act_ref[...], wd_ref[...]),
                [_rows(act), _rows(h)], [wd], [(D_MODEL, F32)])[0]
    return out, (h, n, a, b, act)


def _ffn_bwd(tag, dh, saved, gain, wg, wu, wd):
    h, n, a, b, act = saved

    def bwd1(i, dh_ref, a_ref, b_ref, wd_ref):
        dact = 0.5 * _dot_nt(dh_ref[...], wd_ref[...])
        av = a_ref[...].astype(F32)
        sg = _sigmoid(av)
        return dact * b_ref[...].astype(F32) * (sg * (1.0 + av * (1.0 - sg))), dact * av * sg

    da, db = _rowk(tag + "_bwd_act", bwd1, [_rows(dh), _rows(a), _rows(b)], [wd], [(D_FF, BF16)] * 2, tm=192)

    def bwd2(i, da_ref, db_ref, h_ref, dh_ref, wg_ref, wu_ref, g_ref):
        dn = _dot_nt(da_ref[...], wg_ref[...]) + _dot_nt(db_ref[...], wu_ref[...])
        dx, dg = _rms_bwd(h_ref[...], g_ref[...], dn)
        return dh_ref[...] + dx, dg

    dh_in, dgain = _rowk(tag + "_bwd_in", bwd2, [_rows(da), _rows(db), _rows(h), _rows(dh)], [wg, wu, gain],
                         [(D_MODEL, F32)], accs=[(1, D_MODEL)])
    dwd = _mm_tn(tag + "_dwd", act, dh, tk=D_FF // 2, tn=D_MODEL, scale=0.5)
    dwg = _mm_tn(tag + "_dwg", n, da, tk=D_MODEL, tn=D_FF // 2)
    dwu = _mm_tn(tag + "_dwu", n, db, tk=D_MODEL, tn=D_FF // 2)
    return dh_in, dgain, dwg, dwu, dwd


def _mixer_fwd(h, lw, ssm):
    lp = h.shape[0]
    n = _rowk("mix_norm", lambda i, x, g: _rms_fwd(x[...], g[...]), [_rows(h)], [lw["mix_norm"]], [(D_MODEL, BF16)])[0]
    proj = _rowk("mix_in", lambda i, n_ref, w_ref: _dot(n_ref[...], w_ref[...]), [_rows(n)], [lw["w_in"]],
                 [(4 * D_MODEL, F32)])[0]
    yattn, lse = _attn_fwd(proj, lw["attn_sink"])
    xs = []
    for d in range(2):
        bu = _ssm_in(f"ssm_in{d}", proj, ssm[d]["wb"])
        xs.append(_scan(f"ssm_scan{d}", bu, ssm[d]["tab_fwd"], reverse=(d == 1)))

    def ssm_out(i, x0_ref, x1_ref, u_ref, wc0_ref, wc1_ref, d_ref, wglu_ref):
        ys = jnp.concatenate([_dot(_state_cols(x0_ref, j), wc0_ref[j]) + _dot(_state_cols(x1_ref, j), wc1_ref[j])
                              for j in range(SUPER)], axis=1)
        ys = ys + d_ref[...] * u_ref[...]
        z = _gelu(ys)
        return ys, z * _sigmoid(_dot(z, wglu_ref[...]))

    ys, yssm = _rowk("ssm_out", ssm_out, [_rows(xs[0]), _rows(xs[1]), _rows(proj, SSM_WIDTH, 3)],
                     [ssm[0]["wc"].astype(BF16), ssm[1]["wc"].astype(BF16), lw["ssm_d"], lw["ssm_w_glu"]],
                     [(SSM_WIDTH, F32), (SSM_WIDTH, BF16)], tm=SCAN_CHUNK)

    def merge(i, ys_ref, ya_ref, gs_ref, ga_ref, wbs_ref, wba_ref):
        bs = _dot(ys_ref[...], wbs_ref[...])
        ba = _dot(ya_ref[...], wba_ref[...])
        m = _sigmoid(gs_ref[...]) * bs + _sigmoid(ga_ref[...]) * ba
        return bs, ba, jnp.where(_row_ids(i, ROW_TILE, D_MODEL) >= PAD, m, 0.0)

    bs, ba, merged = _rowk("mix_merge", merge, [_rows(yssm), _rows(yattn), _rows(proj, D_MODEL, 2), _rows(proj, D_MODEL, 3)],
                           [lw["w_branch_ssm"], lw["w_branch_attn"]], [(D_MODEL, BF16)] * 3)
    out = _rowk("mix_out", lambda i, m_ref, h_ref, w_ref: h_ref[...] + _dot(m_ref[...], w_ref[...]),
                [_rows(merged), _rows(h)], [lw["w_out"]], [(D_MODEL, F32)])[0]
    return out, (h, n, proj, yattn, lse, xs, ys, yssm, bs, ba, merged)


def _mixer_bwd(dh, saved, lw, ssm):
    h, n, proj, yattn, lse, xs, ys, yssm, bs, ba, merged = saved

    def bwd1(i, dh_ref, gs_ref, ga_ref, bs_ref, ba_ref, w_ref):
        dm = _dot_nt(dh_ref[...], w_ref[...])
        dm = jnp.where(_row_ids(i, ROW_TILE, D_MODEL) >= PAD, dm, 0.0)
        sgs = _sigmoid(gs_ref[...])
        sga = _sigmoid(ga_ref[...])
        return (dm * sgs, dm * sga, dm * bs_ref[...].astype(F32) * sgs * (1.0 - sgs),
                dm * ba_ref[...].astype(F32) * sga * (1.0 - sga))

    dbs, dba, dgs, dga = _rowk("mix_bwd_merge", bwd1,
                               [_rows(dh), _rows(proj, D_MODEL, 2), _rows(proj, D_MODEL, 3), _rows(bs), _rows(ba)],
                               [lw["w_out"]], [(D_MODEL, BF16)] * 4)
    dw_out = _mm_tn("mix_dw_out", merged, dh, tk=D_MODEL, tn=D_MODEL)
    dw_bs = _mm_tn("mix_dw_bs", yssm, dbs, tk=SSM_WIDTH, tn=D_MODEL)
    dw_ba = _mm_tn("mix_dw_ba", yattn, dba, tk=D_MODEL, tn=D_MODEL)

    def bwd2(i, dbs_ref, dba_ref, ys_ref, u_ref, wbs_ref, wba_ref, wglu_ref, d_ref):
        dyssm = _dot_nt(dbs_ref[...], wbs_ref[...])
        dyattn = _dot_nt(dba_ref[...], wba_ref[...])
        ysv = ys_ref[...]
        z = _gelu(ysv)
        sg = _sigmoid(_dot(z, wglu_ref[...]))
        dt = dyssm * z * sg * (1.0 - sg)
        dz = dyssm * sg + _dot_nt(dt, wglu_ref[...])
        dys = dz * _gelu_grad(ysv)
        return dyattn, dys, z, dt, jnp.sum(dys * u_ref[...], axis=0, keepdims=True)

    dyattn, dys, z, dt, dd = _rowk("mix_bwd_ssm_out", bwd2, [_rows(dbs), _rows(dba), _rows(ys), _rows(proj, SSM_WIDTH, 3)],
                                   [lw["w_branch_ssm"], lw["w_branch_attn"], lw["ssm_w_glu"], lw["ssm_d"]],
                                   [(D_MODEL, F32), (SSM_WIDTH, F32), (SSM_WIDTH, BF16), (SSM_WIDTH, BF16)],
                                   accs=[(1, SSM_WIDTH)])
    dw_glu = _mm_tn("mix_dw_glu", z, dt, tk=SSM_WIDTH, tn=SSM_WIDTH)

    gs, ssm_cot = [], []
    for d in range(2):
        def gd_fn(i, dy_ref, wc_ref):
            parts = [_dot_nt(dy_ref[:, 128 * j:128 * (j + 1)], wc_ref[j]) for j in range(SUPER)]
            return jnp.concatenate([p[:, :512] for p in parts] + [p[:, 512:] for p in parts], axis=1)

        gd = _rowk(f"ssm_bwd_in{d}", gd_fn, [_rows(dys)], [ssm[d]["wc"].astype(BF16)], [(2 * N_STATE, F32)])[0]
        g = _scan(f"ssm_bwd_scan{d}", gd, ssm[d]["tab_bwd"], reverse=(d == 0))
        gs.append(g)
        ssm_cot.append(_ssm_param_grads(f"ssm_param_grads{d}", g, xs[d], proj, dys, reverse=(d == 1)))

    def du_fn(i, g0_ref, g1_ref, dys_ref, wb0_ref, wb1_ref, d_ref):
        du = jnp.concatenate([_dot_nt(_state_cols(g0_ref, j), wb0_ref[j]) + _dot_nt(_state_cols(g1_ref, j), wb1_ref[j])
                              for j in range(SUPER)], axis=1)
        return du + d_ref[...] * dys_ref[...]

    du = _rowk("ssm_bwd_du", du_fn, [_rows(gs[0]), _rows(gs[1]), _rows(dys)],
               [ssm[0]["wb"].astype(BF16), ssm[1]["wb"].astype(BF16), lw["ssm_d"]], [(SSM_WIDTH, BF16)], tm=SCAN_CHUNK)[0]

    dq, dk, dv, dkm, dvm, dsink = _attn_bwd(proj, lw["attn_sink"], yattn, lse, dyattn)

    def dproj_fn(i, dq_ref, dk_ref, dv_ref, du_ref, dgs_ref, dga_ref, dkm_ref, dvm_ref):
        first = jnp.where(i == 0, 1.0, 0.0)
        zeros = lambda r: jnp.zeros((r, N_KV_HEADS * HEAD_DIM), F32)
        place = lambda m: jnp.concatenate([zeros(PAD), m[...] * first, zeros(ROW_TILE - BLOCK)], axis=0)
        dp = jnp.concatenate([dq_ref[...].astype(F32), dk_ref[...] + place(dkm_ref), dv_ref[...] + place(dvm_ref),
                              du_ref[...].astype(F32), dgs_ref[...].astype(F32), dga_ref[...].astype(F32)], axis=1)
        return jnp.where(_row_ids(i, ROW_TILE, 4 * D_MODEL) >= PAD, dp, 0.0)

    dproj = _rowk("mix_bwd_dproj", dproj_fn, [_rows(dq), _rows(dk), _rows(dv), _rows(du), _rows(dgs), _rows(dga)],
                  [dkm, dvm], [(4 * D_MODEL, BF16)])[0]

    def bwd_in(i, dp_ref, h_ref, dh_ref, w_ref, g_ref):
        dx, dg = _rms_bwd(h_ref[...], g_ref[...], _dot_nt(dp_ref[...], w_ref[...]))
        return dh_ref[...] + dx, dg

    dh_in, dgain = _rowk("mix_bwd_in", bwd_in, [_rows(dproj), _rows(h), _rows(dh)], [lw["w_in"], lw["mix_norm"]],
                         [(D_MODEL, F32)], accs=[(1, D_MODEL)])
    dw_in = _mm_tn("mix_dw_in", n, dproj, tk=D_MODEL, tn=2 * D_MODEL)
    grads = {"w_out": dw_out, "w_branch_ssm": dw_bs, "w_branch_attn": dw_ba, "ssm_w_glu": dw_glu, "w_in": dw_in,
             "mix_norm": dgain, "ssm_d": dd, "attn_sink": dsink[0, :N_HEADS]}
    return dh_in, grads, ssm_cot


def _loss_head(h, gain, target):
    lp = h.shape[0]

    def fn(i, h_ref, t_ref, g_ref):
        x = h_ref[...]
        y = _rms_fwd(x, g_ref[...])
        live = jnp.where(i == 0, 0.0, 1.0)
        dy = (y - t_ref[...]) * live
        loss = 0.5 * jnp.sum(dy * dy) / D_MODEL
        dx, dg = _rms_bwd(x, g_ref[...], dy * (1.0 / D_MODEL))
        return dx, jnp.full((1, BLOCK), loss, F32), dg

    tgt = (target, (BLOCK, D_MODEL), lambda i: (jnp.maximum(i - 1, 0), 0))
    return _rowk("loss_head", fn, [_rows(h), tgt], [gain], [(D_MODEL, F32)], accs=[(1, BLOCK), (1, D_MODEL)], tm=BLOCK)


def _adamw(name, w, g, m, v, tm):
    def fn(i, w_ref, g_ref, m_ref, v_ref):
        gv = g_ref[...]
        mn = ADAM_B1 * m_ref[...] + (1.0 - ADAM_B1) * gv
        vn = ADAM_B2 * v_ref[...] + (1.0 - ADAM_B2) * (gv * gv)
        m_hat = mn / (1.0 - ADAM_B1 ** ADAM_STEP)
        v_hat = vn / (1.0 - ADAM_B2 ** ADAM_STEP)
        return -ADAM_LR * (m_hat / (jnp.sqrt(v_hat) + ADAM_EPS) + ADAM_WD * w_ref[...]), mn, vn

    wd = w.shape[1]
    return _rowk(name, fn, [_rows(w), _rows(g), _rows(m), _rows(v)], [], [(wd, F32)] * 3, tm=tm)


def _shard_rows(name):
    return {"ffn1_w_gate": 704, "ffn1_w_up": 704, "ffn1_w_down": 704, "ffn2_w_gate": 704, "ffn2_w_up": 704, "ffn2_w_down": 704,
            "w_in": 1024, "ssm_w_glu": 64, "w_branch_ssm": 128, "w_branch_attn": 256, "w_out": 256}[name]


def _full_shape(name):
    return {"ffn1_w_gate": (D_MODEL, D_FF), "ffn1_w_up": (D_MODEL, D_FF), "ffn1_w_down": (D_FF, D_MODEL),
            "ffn2_w_gate": (D_MODEL, D_FF), "ffn2_w_up": (D_MODEL, D_FF), "ffn2_w_down": (D_FF, D_MODEL),
            "w_in": (D_MODEL, 4 * D_MODEL), "ssm_w_glu": (SSM_WIDTH, SSM_WIDTH), "w_branch_ssm": (SSM_WIDTH, D_MODEL),
            "w_branch_attn": (D_MODEL, D_MODEL), "w_out": (D_MODEL, D_MODEL)}[name]


def _unflatten_gathered(gathered):
    out, r0 = {}, 0
    for name in BIG:
        r = _shard_rows(name)
        k, nn = _full_shape(name)
        piece = gathered[:, r0:r0 + r, :]
        if name in COL_SHARDED:
            out[name] = piece.reshape(4, k, nn // 4).transpose(1, 0, 2).reshape(k, nn)
        else:
            out[name] = piece.reshape(k, nn)
        r0 += r
    return out


def _flatten_full(grads):
    per_shard = []
    for s in range(4):
        pieces = []
        for name in BIG:
            k, nn = _full_shape(name)
            g = grads[name]
            piece = g[:, s * (nn // 4):(s + 1) * (nn // 4)] if name in COL_SHARDED else g[s * (k // 4):(s + 1) * (k // 4), :]
            pieces.append(piece.reshape(-1, 1024))
        per_shard.append(jnp.concatenate(pieces, axis=0))
    f = jnp.stack(per_shard)
    return f.reshape(4, 2, f.shape[1] // 2, 1024).transpose(1, 0, 2, 3)


def _shard_2d(a):
    return a.reshape(-1, a.shape[-1])


def kernel(x, meta_tokens, ffn1_norm, ffn1_w_gate, ffn1_w_up, ffn1_w_down, mix_norm, w_in, ssm_lam_re, ssm_lam_im, ssm_log_dt, ssm_b_re, ssm_b_im, ssm_c_re, ssm_c_im, ssm_d, ssm_w_glu, attn_sink, w_branch_ssm, w_branch_attn, w_out, ffn2_norm, ffn2_w_gate, ffn2_w_up, ffn2_w_down, final_norm, loss_target, m_meta_tokens, m_ffn1_norm, m_ffn1_w_gate, m_ffn1_w_up, m_ffn1_w_down, m_mix_norm, m_w_in, m_ssm_lam_re, m_ssm_lam_im, m_ssm_log_dt, m_ssm_b_re, m_ssm_b_im, m_ssm_c_re, m_ssm_c_im, m_ssm_d, m_ssm_w_glu, m_attn_sink, m_w_branch_ssm, m_w_branch_attn, m_w_out, m_ffn2_norm, m_ffn2_w_gate, m_ffn2_w_up, m_ffn2_w_down, m_final_norm, v_meta_tokens, v_ffn1_norm, v_ffn1_w_gate, v_ffn1_w_up, v_ffn1_w_down, v_mix_norm, v_w_in, v_ssm_lam_re, v_ssm_lam_im, v_ssm_log_dt, v_ssm_b_re, v_ssm_b_im, v_ssm_c_re, v_ssm_c_im, v_ssm_d, v_ssm_w_glu, v_attn_sink, v_w_branch_ssm, v_w_branch_attn, v_w_out, v_ffn2_norm, v_ffn2_w_gate, v_ffn2_w_up, v_ffn2_w_down, v_final_norm):
    args = dict(locals())
    w = {k: args[k] for k in WEIGHTS}
    mom = {k: args["m_" + k] for k in WEIGHTS}
    var = {k: args["v_" + k] for k in WEIGHTS}
    depth = ffn1_norm.shape[0]
    seq = x.shape[1]
    xi, yi, ci = lax.axis_index("x"), lax.axis_index("y"), lax.axis_index("c")
    chip = 2 * xi + yi
    me = 4 * xi + 2 * yi + ci

    same_core = [(fx, fy, 0) for fx, fy in _OTHER_CHIPS]
    meta_all = _all_gather_all("gather_meta", meta_tokens, 4, same_core, lambda x_, y_, c_: _chip(x_, y_))
    meta_full = meta_all.transpose(1, 0, 2).reshape(N_META, D_MODEL)
    layer_w = []
    for l in range(depth):
        flat = jnp.concatenate([w[name][l].reshape(-1, 1024) for name in BIG], axis=0).astype(BF16)
        lw = _unflatten_gathered(_all_gather_shards("gather_w", flat))
        for name in ("ffn1_norm", "mix_norm", "ffn2_norm"):
            lw[name] = w[name][l].reshape(1, D_MODEL)
        lw["ssm_d"] = ssm_d[l].reshape(1, SSM_WIDTH)
        lw["attn_sink"] = attn_sink[l]
        layer_w.append(lw)

    ssm_params = ("ssm_lam_re", "ssm_lam_im", "ssm_log_dt", "ssm_b_re", "ssm_b_im", "ssm_c_re", "ssm_c_im")
    ssm, ssm_vjp = [], []
    for l in range(depth):
        dirs, vjps = [], []
        for d in range(2):
            prm = tuple(w[k][l, d] for k in ssm_params)
            (ar, ai, wb, wc), pull = jax.vjp(_ssm_prep, *prm)
            fwd_t = _scan_tables(prm[0], prm[1], prm[2], conj=False)
            bwd_t = _scan_tables(prm[0], prm[1], prm[2], conj=True)
            flip = lambda t: (t[0], t[1], t[2][::-1], t[3][::-1])
            dirs.append({"wb": wb, "wc": wc, "tab_fwd": flip(fwd_t) if d == 1 else fwd_t,
                         "tab_bwd": flip(bwd_t) if d == 0 else bwd_t})
            vjps.append(pull)
        ssm.append(dirs)
        ssm_vjp.append(vjps)

    h = jnp.concatenate([jnp.zeros((PAD, D_MODEL), F32), meta_full, x[0]], axis=0)
    saved = []
    for l in range(depth):
        lw = layer_w[l]
        h, s1 = _ffn_fwd("ffn", h, lw["ffn1_norm"], lw["ffn1_w_gate"], lw["ffn1_w_up"], lw["ffn1_w_down"])
        h, s2 = _mixer_fwd(h, lw, ssm[l])
        h, s3 = _ffn_fwd("ffn", h, lw["ffn2_norm"], lw["ffn2_w_gate"], lw["ffn2_w_up"], lw["ffn2_w_down"])
        saved.append((s1, s2, s3))
    dh, loss_part, d_final = _loss_head(h, final_norm.reshape(1, D_MODEL), loss_target[0])
    loss = lax.psum(loss_part[0, 0], MESH_AXES)

    small_g = {k: [None] * depth for k in SMALL if k not in ("meta_tokens", "final_norm")}
    big_g = {k: [None] * depth for k in BIG}
    for l in reversed(range(depth)):
        lw = layer_w[l]
        s1, s2, s3 = saved[l]
        full = {}
        dh, dg, full["ffn2_w_gate"], full["ffn2_w_up"], full["ffn2_w_down"] = _ffn_bwd(
            "ffn", dh, s3, lw["ffn2_norm"], lw["ffn2_w_gate"], lw["ffn2_w_up"], lw["ffn2_w_down"])
        small_g["ffn2_norm"][l] = dg[0]
        dh, mg, ssm_cot = _mixer_bwd(dh, s2, lw, ssm[l])
        for k in ("w_out", "w_branch_ssm", "w_branch_attn", "ssm_w_glu", "w_in"):
            full[k] = mg[k]
        small_g["mix_norm"][l] = mg["mix_norm"][0]
        small_g["ssm_d"][l] = mg["ssm_d"][0]
        small_g["attn_sink"][l] = mg["attn_sink"]
        per_dir = []
        for d in range(2):
            dwb, dwc, gar, gai = ssm_cot[d]
            per_dir.append(ssm_vjp[l][d]((gar, gai, dwb.reshape(SUPER, 128, 1024), dwc.reshape(SUPER, 1024, 128))))
        for j, k in enumerate(ssm_params):
            small_g[k][l] = jnp.stack([per_dir[0][j], per_dir[1][j]])
        dh, dg, full["ffn1_w_gate"], full["ffn1_w_up"], full["ffn1_w_down"] = _ffn_bwd(
            "ffn", dh, s1, lw["ffn1_norm"], lw["ffn1_w_gate"], lw["ffn1_w_up"], lw["ffn1_w_down"])
        small_g["ffn1_norm"][l] = dg[0]
        reduced = _reduce_scatter("reduce_w", _flatten_full(full))
        r0 = 0
        for name in BIG:
            r = _shard_rows(name)
            big_g[name][l] = reduced[r0:r0 + r].reshape(w[name].shape[1:])
            r0 += r

    grad_x = dh[BLOCK:][None]
    small_list = [dh[PAD:BLOCK].reshape(-1)]
    for k in SMALL[1:]:
        small_list.append(d_final.reshape(-1) if k == "final_norm" else jnp.stack(small_g[k]).reshape(-1))
    small_vec = jnp.concatenate(small_list)
    n_small = small_vec.shape[0]
    rows_small = -(-n_small // (64 * 1024)) * 64
    small_vec = jnp.pad(small_vec, (0, rows_small * 1024 - n_small)).reshape(rows_small, 1024)
    everyone = [(fx, fy, fc) for fx in (0, 1) for fy in (0, 1) for fc in (0, 1)][1:]
    small_all = _all_gather_all("gather_small", small_vec, 8, everyone, lambda x_, y_, c_: 4 * x_ + 2 * y_ + c_)
    small_sum = _add_slots("sum_small", small_all, tuple(range(8)), 64).reshape(-1)

    grads, deltas, new_m, new_v = {}, {}, {}, {}
    off = 0
    flat_w, flat_m, flat_v, flat_g = [], [], [], []
    for k in SMALL:
        size = (N_META * D_MODEL) if k == "meta_tokens" else int(np.prod(w[k].shape))
        g = small_sum[off:off + size]
        off += size
        if k == "meta_tokens":
            g = lax.dynamic_slice(g.reshape(N_META, D_MODEL), (0, chip * (D_MODEL // 4)), (N_META, D_MODEL // 4))
            grads[k] = g
            deltas[k], new_m[k], new_v[k] = _adamw("adamw_meta", w[k], g, mom[k], var[k], N_META)
        else:
            grads[k] = g.reshape(w[k].shape)
            flat_g.append(g)
            flat_w.append(w[k].reshape(-1))
            flat_m.append(mom[k].reshape(-1))
            flat_v.append(var[k].reshape(-1))
    n_flat = sum(a.shape[0] for a in flat_g)
    rows_flat = -(-n_flat // (64 * 1024)) * 64
    pack = lambda parts, fill: jnp.pad(jnp.concatenate(parts), (0, rows_flat * 1024 - n_flat),
                                       constant_values=fill).reshape(rows_flat, 1024)
    sd, sm_, sv = _adamw("adamw_small", pack(flat_w, 0.0), pack(flat_g, 0.0), pack(flat_m, 0.0), pack(flat_v, 1.0), 64)
    off = 0
    for k in SMALL:
        if k == "meta_tokens":
            continue
        size = int(np.prod(w[k].shape))
        for dst, src in ((deltas, sd), (new_m, sm_), (new_v, sv)):
            dst[k] = src.reshape(-1)[off:off + size].reshape(w[k].shape)
        off += size
    for k in BIG:
        g = jnp.stack(big_g[k])
        grads[k] = g
        rows_k = _shard_2d(g).shape[0]
        tm = 512 if rows_k % 512 == 0 else rows_k // depth
        d_, m_, v_ = _adamw("adamw_" + k, _shard_2d(w[k]), _shard_2d(g), _shard_2d(mom[k]), _shard_2d(var[k]), tm)
        deltas[k], new_m[k], new_v[k] = d_.reshape(g.shape), m_.reshape(g.shape), v_.reshape(g.shape)

    return (loss, grad_x, *[grads[k] for k in WEIGHTS], *[deltas[k] for k in WEIGHTS],
            *[new_m[k] for k in WEIGHTS], *[new_v[k] for k in WEIGHTS])
```

```python
import functools
import math

import numpy as np
import jax
import jax.numpy as jnp
from jax import lax
from jax.experimental import pallas as pl
from jax.experimental.pallas import tpu as pltpu

F32 = jnp.float32
BF16 = jnp.bfloat16

D_MODEL = 1024
N_META = 16
N_HEADS = 16
N_KV_HEADS = 4
HEAD_DIM = 64
Q_GROUP = N_HEADS // N_KV_HEADS
WINDOW = 128
BLOCK = 128
PAD = BLOCK - N_META
SSM_WIDTH = 512
SSM_GROUP_CH = 16
SSM_GROUPS = 32
SSM_STATE = 64
N_STATE = SSM_GROUPS * SSM_STATE
SUPER = 4
D_FF = 2816
EPS = 1e-6
NEG = -1e30
ATTN_SCALE = HEAD_DIM ** -0.5
SLOPES = [float(2.0 ** (-8.0 * (h + 1) / N_HEADS)) for h in range(N_HEADS)]

ADAM_LR = 0.001
ADAM_B1 = 0.9
ADAM_B2 = 0.999
ADAM_EPS = 1e-08
ADAM_WD = 0.01
ADAM_STEP = 10

V7X_VMEM_LIMIT_BYTES = 52 * 1024 * 1024
ROW_TILE = 384
SCAN_CHUNK = 128
MESH_AXES = ("x", "y", "c")

BIG = ["ffn1_w_gate", "ffn1_w_up", "ffn1_w_down", "ffn2_w_gate", "ffn2_w_up", "ffn2_w_down",
       "w_in", "ssm_w_glu", "w_branch_ssm", "w_branch_attn", "w_out"]
COL_SHARDED = {"ffn1_w_gate", "ffn1_w_up", "ffn2_w_gate", "ffn2_w_up", "w_in", "w_branch_ssm"}
SMALL = ["meta_tokens", "ffn1_norm", "mix_norm", "ffn2_norm", "final_norm", "ssm_lam_re", "ssm_lam_im", "ssm_log_dt",
         "ssm_b_re", "ssm_b_im", "ssm_c_re", "ssm_c_im", "ssm_d", "attn_sink"]
WEIGHTS = ["meta_tokens", "ffn1_norm", "ffn1_w_gate", "ffn1_w_up", "ffn1_w_down", "mix_norm", "w_in", "ssm_lam_re",
           "ssm_lam_im", "ssm_log_dt", "ssm_b_re", "ssm_b_im", "ssm_c_re", "ssm_c_im", "ssm_d", "ssm_w_glu", "attn_sink",
           "w_branch_ssm", "w_branch_attn", "w_out", "ffn2_norm", "ffn2_w_gate", "ffn2_w_up", "ffn2_w_down", "final_norm"]


def _dot(a, b):
    return lax.dot_general(a.astype(BF16), b.astype(BF16), (((1,), (0,)), ((), ())), preferred_element_type=F32)


def _dot_nt(a, b):
    return lax.dot_general(a.astype(BF16), b.astype(BF16), (((1,), (1,)), ((), ())), preferred_element_type=F32)


def _dot_tn(a, b):
    return lax.dot_general(a.astype(BF16), b.astype(BF16), (((0,), (0,)), ((), ())), preferred_element_type=F32)


def _sigmoid(x):
    return 1.0 / (1.0 + jnp.exp(-x))


_GELU_C = math.sqrt(2.0 / math.pi)


def _gelu(x):
    return 0.5 * x * (1.0 + jnp.tanh(_GELU_C * (x + 0.044715 * x * x * x)))


def _gelu_grad(x):
    th = jnp.tanh(_GELU_C * (x + 0.044715 * x * x * x))
    return 0.5 * (1.0 + th) + 0.5 * x * (1.0 - th * th) * _GELU_C * (1.0 + 3.0 * 0.044715 * x * x)


def _rms_fwd(x, g):
    r = lax.rsqrt(jnp.mean(x * x, axis=-1, keepdims=True) + EPS)
    return x * r * g


def _rms_bwd(x, g, dn):
    r = lax.rsqrt(jnp.mean(x * x, axis=-1, keepdims=True) + EPS)
    xh = x * r
    t = dn * g
    dx = r * (t - xh * jnp.mean(t * xh, axis=-1, keepdims=True))
    return dx, jnp.sum(dn * xh, axis=0, keepdims=True)


def _compiler_params():
    return pltpu.CompilerParams(dimension_semantics=("arbitrary",), vmem_limit_bytes=V7X_VMEM_LIMIT_BYTES)


def _rows(arr, width=None, cb=0):
    return (arr, arr.shape[1] if width is None else width, cb)


def _rowk(name, fn, rows, fulls, outs, accs=(), tm=ROW_TILE, smem=(), n_rows=None):
    n = rows[0][0].shape[0] if n_rows is None else n_rows
    assert n % tm == 0, (name, n, tm)
    in_specs, args = [], []
    for s in smem:
        in_specs.append(pl.BlockSpec(memory_space=pltpu.SMEM))
        args.append(s)
    for r in rows:
        if callable(r[2]):
            in_specs.append(pl.BlockSpec(r[1], r[2]))
        else:
            in_specs.append(pl.BlockSpec((tm, r[1]), functools.partial(lambda i, cb: (i, cb), cb=r[2])))
        args.append(r[0])
    for f in fulls:
        in_specs.append(pl.BlockSpec(memory_space=pl.ANY))
        args.append(f)
    out_specs, out_shape = [], []
    for w, dt in outs:
        out_specs.append(pl.BlockSpec((tm, w), lambda i: (i, 0)))
        out_shape.append(jax.ShapeDtypeStruct((n, w), dt))
    for shp in accs:
        out_specs.append(pl.BlockSpec(shp, functools.partial(lambda i, nd: (0,) * nd, nd=len(shp))))
        out_shape.append(jax.ShapeDtypeStruct(shp, F32))
    ns, nr, nf, no, na = len(smem), len(rows), len(fulls), len(outs), len(accs)
    scratch = [pltpu.VMEM(f.shape, f.dtype) for f in fulls]
    if nf:
        scratch.append(pltpu.SemaphoreType.DMA((nf,)))

    def body(*refs):
        i = pl.program_id(0)
        sm = refs[:ns]
        rr = refs[ns:ns + nr]
        fh = refs[ns + nr:ns + nr + nf]
        oo = refs[ns + nr + nf:ns + nr + nf + no]
        aa = refs[ns + nr + nf + no:ns + nr + nf + no + na]
        fv = refs[ns + nr + nf + no + na:ns + nr + nf + no + na + nf]
        if nf:
            sem = refs[-1]

            @pl.when(i == 0)
            def _():
                cps = [pltpu.make_async_copy(fh[j], fv[j], sem.at[j]) for j in range(nf)]
                for cp in cps:
                    cp.start()
                for cp in cps:
                    cp.wait()
        res = fn(i, *sm, *rr, *fv)
        res = tuple(res) if isinstance(res, (tuple, list)) else (res,)
        for o, v in zip(oo, res[:no]):
            o[...] = v.astype(o.dtype)
        if na:
            @pl.when(i == 0)
            def _():
                for a in aa:
                    a[...] = jnp.zeros_like(a)
            for a, v in zip(aa, res[no:]):
                a[...] += v

    return pl.pallas_call(body, grid=(n // tm,), in_specs=in_specs, out_specs=out_specs, out_shape=out_shape,
                          scratch_shapes=scratch, name=name, compiler_params=_compiler_params())(*args)


def _mm_tn(name, x, y, *, xw=None, xcb=0, tk, tn, scale=1.0, tm=ROW_TILE):
    m = x.shape[0]
    k = x.shape[1] if xw is None else xw
    nn = y.shape[1]
    assert m % tm == 0 and k % tk == 0 and nn % tn == 0, (name, m, k, nn)
    kb0 = (xcb * k) // tk

    def body(x_ref, y_ref, o_ref):
        @pl.when(pl.program_id(2) == 0)
        def _():
            o_ref[...] = jnp.zeros_like(o_ref)
        yv = y_ref[...]
        if scale != 1.0:
            yv = yv * scale
        o_ref[...] += _dot_tn(x_ref[...], yv)

    return pl.pallas_call(
        body, grid=(k // tk, nn // tn, m // tm),
        in_specs=[pl.BlockSpec((tm, tk), lambda a, b, i: (i, kb0 + a)), pl.BlockSpec((tm, tn), lambda a, b, i: (i, b))],
        out_specs=pl.BlockSpec((tk, tn), lambda a, b, i: (a, b)), out_shape=jax.ShapeDtypeStruct((k, nn), F32), name=name,
        compiler_params=pltpu.CompilerParams(dimension_semantics=("arbitrary", "arbitrary", "arbitrary"),
                                             vmem_limit_bytes=V7X_VMEM_LIMIT_BYTES))(x, y)


def _exchange(name, src, n_out, local, sends, alias=False):
    nl, nsnd = len(local), len(sends)
    out_shape = jax.ShapeDtypeStruct((n_out,) + src.shape[1:], src.dtype)

    def body(src_ref, out_ref, lsem, ssem, rsem):
        x, y, c = lax.axis_index("x"), lax.axis_index("y"), lax.axis_index("c")
        cps = []
        for j, (sf, df) in enumerate(local):
            cp = pltpu.make_async_copy(src_ref.at[sf(x, y, c)], out_ref.at[df(x, y, c)], lsem.at[j])
            cp.start()
            cps.append(cp)
        for k, ((fx, fy, fc), sf, df) in enumerate(sends):
            peer = (1 - x if fx else x, 1 - y if fy else y, 1 - c if fc else c)
            cp = pltpu.make_async_remote_copy(src_ref=src_ref.at[sf(x, y, c)], dst_ref=out_ref.at[df(x, y, c)],
                                              send_sem=ssem.at[k], recv_sem=rsem.at[k], device_id=peer,
                                              device_id_type=pl.DeviceIdType.MESH)
            cp.start()
            cps.append(cp)
        for cp in cps:
            cp.wait()

    return pl.pallas_call(
        body, in_specs=[pl.BlockSpec(memory_space=pl.ANY)], out_specs=pl.BlockSpec(memory_space=pl.ANY), out_shape=out_shape,
        scratch_shapes=[pltpu.SemaphoreType.DMA((max(nl, 1),)), pltpu.SemaphoreType.DMA((nsnd,)), pltpu.SemaphoreType.DMA((nsnd,))],
        input_output_aliases=({0: 0} if alias else {}), name=name)(src)


def _chip(x, y):
    return 2 * x + y


_OTHER_CHIPS = [(1, 0), (0, 1), (1, 1)]


def _all_gather_shards(name, shard):
    r, w = shard.shape
    half = shard.reshape(2, r // 2, w)
    first = [((fx, fy, 0), lambda x, y, c: c, lambda x, y, c: 2 * _chip(x, y) + c) for fx, fy in _OTHER_CHIPS]
    g = _exchange(name + "_ici", half, 8, [], first)
    second = [((0, 0, 1),
               (lambda x, y, c, fx=fx, fy=fy: 2 * _chip(x ^ fx, y ^ fy) + c),
               (lambda x, y, c, fx=fx, fy=fy: 2 * _chip(x ^ fx, y ^ fy) + c)) for fx, fy in _OTHER_CHIPS]
    g = _exchange(name + "_d2d", g, 8, [], second, alias=True).reshape(4, r, w)
    mine = lax.broadcasted_iota(jnp.int32, (4, 1, 1), 0) == _chip(lax.axis_index("x"), lax.axis_index("y"))
    return jnp.where(mine, shard[None], g)


def _slot_sum(name, terms, sel, tm, out_slots=None, out_slot=None):
    rows, w = terms[0][0].shape[1:]
    dtype = terms[0][0].dtype

    def imap(slot):
        if isinstance(slot, int):
            return lambda i, s: (slot, i, 0)
        return lambda i, s: (s[slot[1]], i, 0)

    in_specs = [pl.BlockSpec((None, tm, w), imap(sl)) for _, sl in terms]
    if out_slots is None:
        out_spec, out_shape = pl.BlockSpec((tm, w), lambda i, s: (i, 0)), (rows, w)
    else:
        out_spec, out_shape = pl.BlockSpec((None, tm, w), imap(out_slot)), (out_slots, rows, w)

    def body(sel_ref, *refs):
        acc = refs[0][...]
        for r in refs[1:-1]:
            acc = acc + r[...]
        refs[-1][...] = acc

    grid_spec = pltpu.PrefetchScalarGridSpec(num_scalar_prefetch=1, grid=(rows // tm,), in_specs=in_specs, out_specs=out_spec)
    return pl.pallas_call(body, grid_spec=grid_spec, out_shape=jax.ShapeDtypeStruct(out_shape, dtype), name=name,
                          compiler_params=_compiler_params())(sel, *[a for a, _ in terms])


def _reduce_scatter(name, parts, sel):
    _, _, r, w = parts.shape
    f = parts.reshape(2, 4 * r, w)
    got = _exchange(name + "_d2d", f, 1, [], [((0, 0, 1), lambda x, y, c: 1 - c, lambda x, y, c: 0)])
    p = _slot_sum(name + "_add2", [(f, ("sel", 0)), (got, 0)], sel, 384).reshape(4, r, w)
    sends = [((fx, fy, 0), (lambda x, y, c, fx=fx, fy=fy: _chip(x ^ fx, y ^ fy)), (lambda x, y, c, k=k: k))
             for k, (fx, fy) in enumerate(_OTHER_CHIPS)]
    got = _exchange(name + "_ici", p, 3, [], sends)
    q = _slot_sum(name + "_add4", [(p, ("sel", 1)), (got, 0), (got, 1), (got, 2)], sel, 496, out_slots=2, out_slot=("sel", 0))
    q = _exchange(name + "_pair", q, 2, [], [((0, 0, 1), lambda x, y, c: c, lambda x, y, c: c)], alias=True)
    return q.reshape(2 * r, w)


def _all_gather_all(name, vec, n_slots, flips, slot_fn):
    sends = [(f, lambda x, y, c: 0, slot_fn) for f in flips]
    g = _exchange(name, vec[None], n_slots, [], sends)
    mine = lax.broadcasted_iota(jnp.int32, (n_slots, 1, 1), 0) == slot_fn(*(lax.axis_index(a) for a in MESH_AXES))
    return jnp.where(mine, vec[None], g)


def _nbr_specs(arr, width, cb, nb):
    return [
        (arr, (BLOCK, width), functools.partial(lambda n, cb: (jnp.maximum(n - 1, 0), cb), cb=cb)),
        (arr, (BLOCK, width), functools.partial(lambda n, cb: (n, cb), cb=cb)),
        (arr, (BLOCK, width), functools.partial(lambda n, cb: (jnp.minimum(n + 1, nb - 1), cb), cb=cb)),
    ]


def _head(h):
    return slice(h * HEAD_DIM, (h + 1) * HEAD_DIM)


def _row_group(n_groups, rows_per_group):
    r = lax.broadcasted_iota(jnp.int32, (n_groups * rows_per_group, 1), 0)
    grp = jnp.zeros_like(r)
    for g in range(1, n_groups):
        grp = grp + jnp.where(r >= g * rows_per_group, 1, 0)
    return grp


def _by_group(grp, vals):
    out = vals[-1]
    for g in range(len(vals) - 2, -1, -1):
        out = jnp.where(grp == g, vals[g], out)
    return out


def _attn_fwd(proj, sink):
    lp = proj.shape[0]
    nb = lp // BLOCK
    kv_w = N_KV_HEADS * HEAD_DIM
    specs = _nbr_specs(proj, kv_w, 4, nb) + _nbr_specs(proj, kv_w, 5, nb)
    specs += [(proj, (BLOCK, kv_w), lambda n: (0, 4)), (proj, (BLOCK, kv_w), lambda n: (0, 5))]
    in_specs = [pl.BlockSpec(memory_space=pltpu.SMEM), pl.BlockSpec((BLOCK, D_MODEL), lambda n: (n, 0))]
    in_specs += [pl.BlockSpec(s[1], s[2]) for s in specs]

    def body(sink_ref, q_ref, kp, kc, kn, vp, vc, vn, km, vm, o_ref, lse_ref):
        n = pl.program_id(0)
        qi = lax.broadcasted_iota(jnp.int32, (BLOCK, 3 * BLOCK), 0)
        sj = lax.broadcasted_iota(jnp.int32, (BLOCK, 3 * BLOCK), 1)
        dist = jnp.abs(qi + BLOCK - sj)
        kpos = (n - 1) * BLOCK + sj
        valid = (dist <= WINDOW) & (kpos >= BLOCK) & (kpos < lp)
        distf = dist.astype(F32)
        kb = jnp.concatenate([kp[...], kc[...], kn[...]], axis=0).astype(BF16)
        vb = jnp.concatenate([vp[...], vc[...], vn[...]], axis=0).astype(BF16)
        kmeta = km[PAD:BLOCK, :].astype(BF16)
        vmeta = vm[PAD:BLOCK, :].astype(BF16)
        valid4 = jnp.concatenate([valid] * Q_GROUP, axis=0)
        distf4 = jnp.concatenate([distf] * Q_GROUP, axis=0)
        grp = _row_group(Q_GROUP, BLOCK)
        for kh in range(N_KV_HEADS):
            ksl = slice(kh * HEAD_DIM, (kh + 1) * HEAD_DIM)
            heads = [kh * Q_GROUP + g for g in range(Q_GROUP)]
            slope = _by_group(grp, [SLOPES[h] for h in heads])
            sk = _by_group(grp, [sink_ref[h] for h in heads])
            q4 = (jnp.concatenate([q_ref[:, _head(h)] for h in heads], axis=0) * ATTN_SCALE).astype(BF16)
            s = jnp.where(valid4, _dot_nt(q4, kb[:, ksl]) - slope * distf4, NEG)
            sm = _dot_nt(q4, kmeta[:, ksl])
            m = jnp.maximum(jnp.maximum(jnp.max(s, axis=1, keepdims=True), jnp.max(sm, axis=1, keepdims=True)), sk)
            e = jnp.exp(s - m)
            em = jnp.exp(sm - m)
            den = jnp.sum(e, axis=1, keepdims=True) + jnp.sum(em, axis=1, keepdims=True) + jnp.exp(sk - m)
            o4 = (_dot(e, vb[:, ksl]) + _dot(em, vmeta[:, ksl])) * (1.0 / den)
            lse4 = m + jnp.log(den)
            for g, h in enumerate(heads):
                o_ref[:, _head(h)] = o4[g * BLOCK:(g + 1) * BLOCK].astype(o_ref.dtype)
                lse_ref[:, h:h + 1] = lse4[g * BLOCK:(g + 1) * BLOCK]

    return pl.pallas_call(
        body, grid=(nb,), in_specs=in_specs,
        out_specs=[pl.BlockSpec((BLOCK, D_MODEL), lambda n: (n, 0)), pl.BlockSpec((BLOCK, N_HEADS), lambda n: (n, 0))],
        out_shape=[jax.ShapeDtypeStruct((lp, D_MODEL), BF16), jax.ShapeDtypeStruct((lp, N_HEADS), F32)],
        name="attn_fwd", compiler_params=_compiler_params())(sink, proj, *[s[0] for s in specs])


def _attn_delta(do, o):
    lp = do.shape[0]
    sel = (lax.broadcasted_iota(jnp.int32, (N_HEADS, D_MODEL), 1) // HEAD_DIM
           == lax.broadcasted_iota(jnp.int32, (N_HEADS, D_MODEL), 0)).astype(BF16)

    def body(do_ref, o_ref, sel_ref, d_ref, dt_ref):
        prod = do_ref[...] * o_ref[...].astype(F32)
        hi = prod.astype(BF16)
        lo = (prod - hi.astype(F32)).astype(BF16)
        d_ref[...] = _dot_nt(hi, sel_ref[...]) + _dot_nt(lo, sel_ref[...])
        dt_ref[...] = _dot_nt(sel_ref[...], hi) + _dot_nt(sel_ref[...], lo)

    return pl.pallas_call(
        body, grid=(lp // BLOCK,),
        in_specs=[pl.BlockSpec((BLOCK, D_MODEL), lambda n: (n, 0)), pl.BlockSpec((BLOCK, D_MODEL), lambda n: (n, 0)),
                  pl.BlockSpec((N_HEADS, D_MODEL), lambda n: (0, 0))],
        out_specs=[pl.BlockSpec((BLOCK, N_HEADS), lambda n: (n, 0)), pl.BlockSpec((N_HEADS, BLOCK), lambda n: (0, n))],
        out_shape=[jax.ShapeDtypeStruct((lp, N_HEADS), F32), jax.ShapeDtypeStruct((N_HEADS, lp), F32)],
        name="attn_delta", compiler_params=_compiler_params())(do, o, sel)


def _attn_bwd(proj, sink, o, lse, do):
    lp = proj.shape[0]
    nb = lp // BLOCK
    kv_w = N_KV_HEADS * HEAD_DIM
    delta, delta_t = _attn_delta(do, o)
    lse_t = lse.T
    row_nbrs = lambda arr: [
        (arr, (N_HEADS, BLOCK), lambda n: (0, jnp.maximum(n - 1, 0))), (arr, (N_HEADS, BLOCK), lambda n: (0, n)),
        (arr, (N_HEADS, BLOCK), lambda n: (0, jnp.minimum(n + 1, nb - 1)))]
    specs = (_nbr_specs(proj, D_MODEL, 0, nb) + _nbr_specs(proj, kv_w, 4, nb) + _nbr_specs(proj, kv_w, 5, nb)
             + [(proj, (BLOCK, kv_w), lambda n: (0, 4)), (proj, (BLOCK, kv_w), lambda n: (0, 5))]
             + _nbr_specs(do, D_MODEL, 0, nb) + [(lse, (BLOCK, N_HEADS), lambda n: (n, 0)), (delta, (BLOCK, N_HEADS), lambda n: (n, 0))]
             + row_nbrs(lse_t) + row_nbrs(delta_t))
    in_specs = [pl.BlockSpec(memory_space=pltpu.SMEM)] + [pl.BlockSpec(s[1], s[2]) for s in specs]

    def body(sink_ref, qp, qc, qn, kp, kc, kn, vp, vc, vn, km, vm, dop, doc, don, lc, dc, ltp, ltc, ltn, dtp, dtc, dtn,
             dq_ref, dk_ref, dv_ref, dkm_ref, dvm_ref, dsk_ref):
        n = pl.program_id(0)

        @pl.when(n == 0)
        def _():
            dkm_ref[...] = jnp.zeros_like(dkm_ref)
            dvm_ref[...] = jnp.zeros_like(dvm_ref)
            dsk_ref[...] = jnp.zeros_like(dsk_ref)

        qi = lax.broadcasted_iota(jnp.int32, (BLOCK, 3 * BLOCK), 0)
        sj = lax.broadcasted_iota(jnp.int32, (BLOCK, 3 * BLOCK), 1)
        dist_q = jnp.abs(qi + BLOCK - sj)
        kpos = (n - 1) * BLOCK + sj
        valid_q = (dist_q <= WINDOW) & (kpos >= BLOCK) & (kpos < lp)
        distf_q = dist_q.astype(F32)
        bi = lax.broadcasted_iota(jnp.int32, (BLOCK, 3 * BLOCK), 1)
        kj = lax.broadcasted_iota(jnp.int32, (BLOCK, 3 * BLOCK), 0)
        dist_k = jnp.abs(bi - BLOCK - kj)
        qpos = (n - 1) * BLOCK + bi
        valid_k = (dist_k <= WINDOW) & (qpos >= 0) & (qpos < lp) & (n >= 1)
        distf_k = dist_k.astype(F32)

        kb = jnp.concatenate([kp[...], kc[...], kn[...]], axis=0).astype(BF16)
        vb = jnp.concatenate([vp[...], vc[...], vn[...]], axis=0).astype(BF16)
        kcur = kc[...].astype(BF16)
        vcur = vc[...].astype(BF16)
        kmeta = km[PAD:BLOCK, :].astype(BF16)
        vmeta = vm[PAD:BLOCK, :].astype(BF16)
        lane = lax.broadcasted_iota(jnp.int32, (1, BLOCK), 1)
        dsink = jnp.zeros((1, BLOCK), F32)
        valid_q4 = jnp.concatenate([valid_q] * Q_GROUP, axis=0)
        distf_q4 = jnp.concatenate([distf_q] * Q_GROUP, axis=0)
        valid_k4 = jnp.concatenate([valid_k] * Q_GROUP, axis=1)
        distf_k4 = jnp.concatenate([distf_k] * Q_GROUP, axis=1)
        grp_q = _row_group(Q_GROUP, BLOCK)
        lane_k = lax.broadcasted_iota(jnp.int32, (1, Q_GROUP * 3 * BLOCK), 1)
        grp_k = sum(jnp.where(lane_k >= g * 3 * BLOCK, 1, 0) for g in range(1, Q_GROUP))
        for kh in range(N_KV_HEADS):
            ksl = slice(kh * HEAD_DIM, (kh + 1) * HEAD_DIM)
            heads = [kh * Q_GROUP + g for g in range(Q_GROUP)]
            slopes = [SLOPES[h] for h in heads]
            q4 = (jnp.concatenate([qc[:, _head(h)] for h in heads], axis=0) * ATTN_SCALE).astype(BF16)
            do4 = jnp.concatenate([doc[:, _head(h)] for h in heads], axis=0)
            delta = jnp.concatenate([dc[:, h:h + 1] for h in heads], axis=0)
            lse4 = jnp.concatenate([lc[:, h:h + 1] for h in heads], axis=0)
            s = _dot_nt(q4, kb[:, ksl]) - _by_group(grp_q, slopes) * distf_q4
            p = jnp.exp(jnp.where(valid_q4, s, NEG) - lse4)
            pm = jnp.exp(_dot_nt(q4, kmeta[:, ksl]) - lse4)
            ps = jnp.exp(_by_group(grp_q, [sink_ref[h] for h in heads]) - lse4)
            do4b = do4.astype(BF16)
            ds = p * (_dot_nt(do4b, vb[:, ksl]) - delta)
            dsm = pm * (_dot_nt(do4b, vmeta[:, ksl]) - delta)
            dq4 = ATTN_SCALE * (_dot(ds, kb[:, ksl]) + _dot(dsm, kmeta[:, ksl]))
            dsk4 = ps * delta
            for g, h in enumerate(heads):
                dq_ref[:, _head(h)] = dq4[g * BLOCK:(g + 1) * BLOCK].astype(dq_ref.dtype)
                dsink = dsink + jnp.where(lane == h, -jnp.sum(dsk4[g * BLOCK:(g + 1) * BLOCK]), 0.0)
            dkm_ref[:, ksl] += _dot_tn(dsm, q4)
            dvm_ref[:, ksl] += _dot_tn(pm, do4b)
            band = lambda a, b, c_: jnp.concatenate([r[:, _head(h)] for h in heads for r in (a, b, c_)], axis=0)
            qb4 = (band(qp, qc, qn) * ATTN_SCALE).astype(BF16)
            dob4b = band(dop, doc, don).astype(BF16)
            delta_b = jnp.concatenate([r[h:h + 1, :] for h in heads for r in (dtp, dtc, dtn)], axis=1)
            lse_b = jnp.concatenate([r[h:h + 1, :] for h in heads for r in (ltp, ltc, ltn)], axis=1)
            st = _dot_nt(kcur[:, ksl], qb4) - _by_group(grp_k, slopes) * distf_k4
            pt = jnp.exp(jnp.where(valid_k4, st, NEG) - lse_b)
            dv_ref[:, ksl] = _dot(pt, dob4b)
            dst = pt * (_dot_nt(vcur[:, ksl], dob4b) - delta_b)
            dk_ref[:, ksl] = _dot(dst, qb4)
        dsk_ref[...] += dsink

    blk = lambda w: pl.BlockSpec((BLOCK, w), lambda n: (n, 0))
    fix = lambda shp: pl.BlockSpec(shp, lambda n: (0, 0))
    return pl.pallas_call(
        body, grid=(nb,), in_specs=in_specs,
        out_specs=[blk(D_MODEL), blk(kv_w), blk(kv_w), fix((N_META, kv_w)), fix((N_META, kv_w)), fix((1, BLOCK))],
        out_shape=[jax.ShapeDtypeStruct((lp, D_MODEL), BF16), jax.ShapeDtypeStruct((lp, kv_w), F32),
                   jax.ShapeDtypeStruct((lp, kv_w), F32), jax.ShapeDtypeStruct((N_META, kv_w), F32),
                   jax.ShapeDtypeStruct((N_META, kv_w), F32), jax.ShapeDtypeStruct((1, BLOCK), F32)],
        name="attn_bwd", compiler_params=_compiler_params())(sink, *[s[0] for s in specs])


def _scan(name, bu, tables, reverse):
    lp = bu.shape[0]
    t = SCAN_CHUNK
    nc = lp // t
    p2r, p2i, apr, api = tables
    idx = (lambda i: nc - 1 - i) if reverse else (lambda i: i)

    def body(br_ref, bi_ref, p2r_ref, p2i_ref, apr_ref, api_ref, x_ref, cr_ref, ci_ref):
        @pl.when(pl.program_id(0) == 0)
        def _():
            cr_ref[...] = jnp.zeros_like(cr_ref)
            ci_ref[...] = jnp.zeros_like(ci_ref)

        xr = br_ref[...]
        xi = bi_ref[...]
        row = lax.broadcasted_iota(jnp.int32, xr.shape, 0)
        for k in range(7):
            s = 1 << k
            if reverse:
                shr, shi, keep = pltpu.roll(xr, t - s, axis=0), pltpu.roll(xi, t - s, axis=0), row < t - s
            else:
                shr, shi, keep = pltpu.roll(xr, s, axis=0), pltpu.roll(xi, s, axis=0), row >= s
            shr = jnp.where(keep, shr, 0.0)
            shi = jnp.where(keep, shi, 0.0)
            pr = p2r_ref[k:k + 1, :]
            pi = p2i_ref[k:k + 1, :]
            xr, xi = xr + pr * shr - pi * shi, xi + pr * shi + pi * shr
        cr = cr_ref[...]
        ci = ci_ref[...]
        ar = apr_ref[...]
        ai = api_ref[...]
        xr, xi = xr + ar * cr - ai * ci, xi + ar * ci + ai * cr
        x_ref[:, :N_STATE] = xr
        x_ref[:, N_STATE:] = xi
        edge = 0 if reverse else t - 1
        cr_ref[...] = xr[edge:edge + 1, :]
        ci_ref[...] = xi[edge:edge + 1, :]

    fix = lambda shp: pl.BlockSpec(shp, lambda i: (0, 0))
    return pl.pallas_call(
        body, grid=(nc,),
        in_specs=[pl.BlockSpec((t, N_STATE), lambda i: (idx(i), 0)), pl.BlockSpec((t, N_STATE), lambda i: (idx(i), 1)),
                  fix((8, N_STATE)), fix((8, N_STATE)), fix((t, N_STATE)), fix((t, N_STATE))],
        out_specs=pl.BlockSpec((t, 2 * N_STATE), lambda i: (idx(i), 0)),
        out_shape=jax.ShapeDtypeStruct((lp, 2 * N_STATE), F32),
        scratch_shapes=[pltpu.VMEM((1, N_STATE), F32), pltpu.VMEM((1, N_STATE), F32)],
        name=name, compiler_params=_compiler_params())(bu, bu, p2r, p2i, apr, api)


def _scan_tables(lam_re, lam_im, log_dt, conj):
    dt = jnp.exp(log_dt)[:, None]
    er = (lam_re * dt).reshape(1, N_STATE)
    ei = (lam_im * dt).reshape(1, N_STATE)
    if conj:
        ei = -ei
    k2 = jnp.asarray([[float(1 << min(k, 7))] for k in range(8)], F32)
    kt = jnp.arange(1, SCAN_CHUNK + 1, dtype=F32)[:, None]
    p2r, p2i = jnp.exp(k2 * er) * jnp.cos(k2 * ei), jnp.exp(k2 * er) * jnp.sin(k2 * ei)
    apr, api = jnp.exp(kt * er) * jnp.cos(kt * ei), jnp.exp(kt * er) * jnp.sin(kt * ei)
    return p2r, p2i, apr, api


def _ssm_prep(lam_re, lam_im, log_dt, b_re, b_im, c_re, c_im):
    dt = jnp.exp(log_dt)[:, None]
    er = jnp.exp(lam_re * dt)
    ar, ai = er * jnp.cos(lam_im * dt), er * jnp.sin(lam_im * dt)
    nr, ni = ar - 1.0, ai
    den = lam_re * lam_re + lam_im * lam_im
    cr, ci = (nr * lam_re + ni * lam_im) / den, (ni * lam_re - nr * lam_im) / den
    bbr = cr[:, :, None] * b_re - ci[:, :, None] * b_im
    bbi = cr[:, :, None] * b_im + ci[:, :, None] * b_re
    eye = jnp.eye(8, dtype=F32)

    def in_map(b):
        b = b.reshape(SUPER, 8, SSM_STATE, SSM_GROUP_CH).transpose(0, 1, 3, 2)
        return (b[:, :, :, None, :] * eye[None, :, None, :, None]).reshape(SUPER, 128, 512)

    def out_map(cm):
        cm = cm.reshape(SUPER, 8, SSM_GROUP_CH, SSM_STATE).transpose(0, 1, 3, 2)
        return (cm[:, :, :, None, :] * eye[None, :, None, :, None]).reshape(SUPER, 512, 128)

    wb = jnp.concatenate([in_map(bbr), in_map(bbi)], axis=2)
    wc = jnp.concatenate([out_map(c_re), -out_map(c_im)], axis=1)
    return ar.reshape(1, N_STATE), ai.reshape(1, N_STATE), wb, wc


def _ssm_in(name, proj, wb):
    def fn(i, u_ref, wb_ref):
        parts = [_dot(u_ref[:, 128 * j:128 * (j + 1)], wb_ref[j]) for j in range(SUPER)]
        return jnp.concatenate([p[:, :512] for p in parts] + [p[:, 512:] for p in parts], axis=1)

    return _rowk(name, fn, [_rows(proj, SSM_WIDTH, 3)], [wb.astype(BF16)], [(2 * N_STATE, F32)])[0]


def _state_cols(ref, j):
    return jnp.concatenate([ref[:, 512 * j:512 * (j + 1)], ref[:, N_STATE + 512 * j:N_STATE + 512 * (j + 1)]], axis=1)


def _ssm_param_grads(name, g, x, proj, dys, reverse):
    lp = x.shape[0]
    tm = SCAN_CHUNK
    nt = lp // tm
    if reverse:
        bnd = (x, (8, N_STATE), lambda i: (jnp.minimum((i + 1) * (tm // 8), lp // 8 - 1), 0))
        bnd_i = (x, (8, N_STATE), lambda i: (jnp.minimum((i + 1) * (tm // 8), lp // 8 - 1), 1))
    else:
        bnd = (x, (8, N_STATE), lambda i: (jnp.maximum(i * (tm // 8) - 1, 0), 0))
        bnd_i = (x, (8, N_STATE), lambda i: (jnp.maximum(i * (tm // 8) - 1, 0), 1))

    def fn(i, g_ref, x_ref, br_ref, bi_ref, u_ref, dy_ref):
        dwb = jnp.concatenate([_dot_tn(u_ref[:, 128 * j:128 * (j + 1)], _state_cols(g_ref, j)) for j in range(SUPER)], axis=0)
        dwc = jnp.concatenate([_dot_tn(_state_cols(x_ref, j), dy_ref[:, 128 * j:128 * (j + 1)]) for j in range(SUPER)], axis=0)
        gr, gi = g_ref[:, :N_STATE], g_ref[:, N_STATE:]
        xr, xi = x_ref[:, :N_STATE], x_ref[:, N_STATE:]
        row = lax.broadcasted_iota(jnp.int32, xr.shape, 0)
        if reverse:
            edge = jnp.where(i == nt - 1, 0.0, 1.0)
            pr = jnp.where(row == tm - 1, br_ref[0:1, :] * edge, pltpu.roll(xr, tm - 1, axis=0))
            pi = jnp.where(row == tm - 1, bi_ref[0:1, :] * edge, pltpu.roll(xi, tm - 1, axis=0))
        else:
            edge = jnp.where(i == 0, 0.0, 1.0)
            pr = jnp.where(row == 0, br_ref[7:8, :] * edge, pltpu.roll(xr, 1, axis=0))
            pi = jnp.where(row == 0, bi_ref[7:8, :] * edge, pltpu.roll(xi, 1, axis=0))
        gar = jnp.sum(gr * pr + gi * pi, axis=0, keepdims=True)
        gai = jnp.sum(gi * pr - gr * pi, axis=0, keepdims=True)
        return dwb, dwc, gar, gai

    return _rowk(name, fn, [_rows(g), _rows(x), bnd, bnd_i, _rows(proj, SSM_WIDTH, 3), _rows(dys)], [], [],
                 accs=[(SUPER * 128, 1024), (SUPER * 1024, 128), (1, N_STATE), (1, N_STATE)], tm=tm)


def _row_ids(i, tm, width):
    return i * tm + lax.broadcasted_iota(jnp.int32, (tm, width), 0)


def _ffn_fwd(tag, h, gain, wg, wu, wd):
    n = _rowk(tag + "_norm", lambda i, x, g: _rms_fwd(x[...], g[...]), [_rows(h)], [gain], [(D_MODEL, BF16)])[0]

    def up(i, n_ref, wg_ref, wu_ref):
        a = _dot(n_ref[...], wg_ref[...])
        b = _dot(n_ref[...], wu_ref[...])
        return a, b, a * _sigmoid(a) * b

    a, b, act = _rowk(tag + "_up", up, [_rows(n)], [wg, wu], [(D_FF, BF16)] * 3, tm=192)
    out = _rowk(tag + "_down", lambda i, act_ref, h_ref, wd_ref: h_ref[...] + 0.5 * _dot(act_ref[...], wd_ref[...]),
                [_rows(act), _rows(h)], [wd], [(D_MODEL, F32)])[0]
    return out, (h, n, a, b, act)


def _ffn_bwd(tag, dh, saved, gain, wg, wu, wd):
    h, n, a, b, act = saved

    def bwd1(i, dh_ref, a_ref, b_ref, wd_ref):
        dact = 0.5 * _dot_nt(dh_ref[...], wd_ref[...])
        av = a_ref[...].astype(F32)
        sg = _sigmoid(av)
        return dact * b_ref[...].astype(F32) * (sg * (1.0 + av * (1.0 - sg))), dact * av * sg

    da, db = _rowk(tag + "_bwd_act", bwd1, [_rows(dh), _rows(a), _rows(b)], [wd], [(D_FF, BF16)] * 2, tm=192)

    def bwd2(i, da_ref, db_ref, h_ref, dh_ref, wg_ref, wu_ref, g_ref):
        dn = _dot_nt(da_ref[...], wg_ref[...]) + _dot_nt(db_ref[...], wu_ref[...])
        dx, dg = _rms_bwd(h_ref[...], g_ref[...], dn)
        return dh_ref[...] + dx, dg

    dh_in, dgain = _rowk(tag + "_bwd_in", bwd2, [_rows(da), _rows(db), _rows(h), _rows(dh)], [wg, wu, gain],
                         [(D_MODEL, F32)], accs=[(1, D_MODEL)])
    dwd = _mm_tn(tag + "_dwd", act, dh, tk=D_FF // 2, tn=D_MODEL, scale=0.5)
    dwg = _mm_tn(tag + "_dwg", n, da, tk=D_MODEL, tn=D_FF // 2)
    dwu = _mm_tn(tag + "_dwu", n, db, tk=D_MODEL, tn=D_FF // 2)
    return dh_in, dgain, dwg, dwu, dwd


def _mixer_fwd(h, lw, ssm):
    lp = h.shape[0]
    n = _rowk("mix_norm", lambda i, x, g: _rms_fwd(x[...], g[...]), [_rows(h)], [lw["mix_norm"]], [(D_MODEL, BF16)])[0]
    proj = _rowk("mix_in", lambda i, n_ref, w_ref: _dot(n_ref[...], w_ref[...]), [_rows(n)], [lw["w_in"]],
                 [(4 * D_MODEL, F32)])[0]
    yattn, lse = _attn_fwd(proj, lw["attn_sink"])
    xs = []
    for d in range(2):
        bu = _ssm_in(f"ssm_in{d}", proj, ssm[d]["wb"])
        xs.append(_scan(f"ssm_scan{d}", bu, ssm[d]["tab_fwd"], reverse=(d == 1)))

    def ssm_out(i, x0_ref, x1_ref, u_ref, wc0_ref, wc1_ref, d_ref, wglu_ref):
        ys = jnp.concatenate([_dot(_state_cols(x0_ref, j), wc0_ref[j]) + _dot(_state_cols(x1_ref, j), wc1_ref[j])
                              for j in range(SUPER)], axis=1)
        ys = ys + d_ref[...] * u_ref[...]
        z = _gelu(ys)
        return ys, z * _sigmoid(_dot(z, wglu_ref[...]))

    ys, yssm = _rowk("ssm_out", ssm_out, [_rows(xs[0]), _rows(xs[1]), _rows(proj, SSM_WIDTH, 3)],
                     [ssm[0]["wc"].astype(BF16), ssm[1]["wc"].astype(BF16), lw["ssm_d"], lw["ssm_w_glu"]],
                     [(SSM_WIDTH, F32), (SSM_WIDTH, BF16)], tm=SCAN_CHUNK)

    def merge(i, ys_ref, ya_ref, gs_ref, ga_ref, wbs_ref, wba_ref):
        bs = _dot(ys_ref[...], wbs_ref[...])
        ba = _dot(ya_ref[...], wba_ref[...])
        m = _sigmoid(gs_ref[...]) * bs + _sigmoid(ga_ref[...]) * ba
        return bs, ba, jnp.where(_row_ids(i, ROW_TILE, D_MODEL) >= PAD, m, 0.0)

    bs, ba, merged = _rowk("mix_merge", merge, [_rows(yssm), _rows(yattn), _rows(proj, D_MODEL, 2), _rows(proj, D_MODEL, 3)],
                           [lw["w_branch_ssm"], lw["w_branch_attn"]], [(D_MODEL, BF16)] * 3)
    out = _rowk("mix_out", lambda i, m_ref, h_ref, w_ref: h_ref[...] + _dot(m_ref[...], w_ref[...]),
                [_rows(merged), _rows(h)], [lw["w_out"]], [(D_MODEL, F32)])[0]
    return out, (h, n, proj, yattn, lse, xs, ys, yssm, bs, ba, merged)


def _mixer_bwd(dh, saved, lw, ssm):
    h, n, proj, yattn, lse, xs, ys, yssm, bs, ba, merged = saved

    def bwd1(i, dh_ref, gs_ref, ga_ref, bs_ref, ba_ref, w_ref):
        dm = _dot_nt(dh_ref[...], w_ref[...])
        dm = jnp.where(_row_ids(i, ROW_TILE, D_MODEL) >= PAD, dm, 0.0)
        sgs = _sigmoid(gs_ref[...])
        sga = _sigmoid(ga_ref[...])
        return (dm * sgs, dm * sga, dm * bs_ref[...].astype(F32) * sgs * (1.0 - sgs),
                dm * ba_ref[...].astype(F32) * sga * (1.0 - sga))

    dbs, dba, dgs, dga = _rowk("mix_bwd_merge", bwd1,
                               [_rows(dh), _rows(proj, D_MODEL, 2), _rows(proj, D_MODEL, 3), _rows(bs), _rows(ba)],
                               [lw["w_out"]], [(D_MODEL, BF16)] * 4)
    dw_out = _mm_tn("mix_dw_out", merged, dh, tk=D_MODEL, tn=D_MODEL)
    dw_bs = _mm_tn("mix_dw_bs", yssm, dbs, tk=SSM_WIDTH, tn=D_MODEL)
    dw_ba = _mm_tn("mix_dw_ba", yattn, dba, tk=D_MODEL, tn=D_MODEL)

    def bwd2(i, dbs_ref, dba_ref, ys_ref, u_ref, wbs_ref, wba_ref, wglu_ref, d_ref):
        dyssm = _dot_nt(dbs_ref[...], wbs_ref[...])
        dyattn = _dot_nt(dba_ref[...], wba_ref[...])
        ysv = ys_ref[...]
        z = _gelu(ysv)
        sg = _sigmoid(_dot(z, wglu_ref[...]))
        dt = dyssm * z * sg * (1.0 - sg)
        dz = dyssm * sg + _dot_nt(dt, wglu_ref[...])
        dys = dz * _gelu_grad(ysv)
        return dyattn, dys, z, dt, jnp.sum(dys * u_ref[...], axis=0, keepdims=True)

    dyattn, dys, z, dt, dd = _rowk("mix_bwd_ssm_out", bwd2, [_rows(dbs), _rows(dba), _rows(ys), _rows(proj, SSM_WIDTH, 3)],
                                   [lw["w_branch_ssm"], lw["w_branch_attn"], lw["ssm_w_glu"], lw["ssm_d"]],
                                   [(D_MODEL, F32), (SSM_WIDTH, F32), (SSM_WIDTH, BF16), (SSM_WIDTH, BF16)],
                                   accs=[(1, SSM_WIDTH)])
    dw_glu = _mm_tn("mix_dw_glu", z, dt, tk=SSM_WIDTH, tn=SSM_WIDTH)

    gs, ssm_cot = [], []
    for d in range(2):
        def gd_fn(i, dy_ref, wc_ref):
            parts = [_dot_nt(dy_ref[:, 128 * j:128 * (j + 1)], wc_ref[j]) for j in range(SUPER)]
            return jnp.concatenate([p[:, :512] for p in parts] + [p[:, 512:] for p in parts], axis=1)

        gd = _rowk(f"ssm_bwd_in{d}", gd_fn, [_rows(dys)], [ssm[d]["wc"].astype(BF16)], [(2 * N_STATE, F32)])[0]
        g = _scan(f"ssm_bwd_scan{d}", gd, ssm[d]["tab_bwd"], reverse=(d == 0))
        gs.append(g)
        ssm_cot.append(_ssm_param_grads(f"ssm_param_grads{d}", g, xs[d], proj, dys, reverse=(d == 1)))

    def du_fn(i, g0_ref, g1_ref, dys_ref, wb0_ref, wb1_ref, d_ref):
        du = jnp.concatenate([_dot_nt(_state_cols(g0_ref, j), wb0_ref[j]) + _dot_nt(_state_cols(g1_ref, j), wb1_ref[j])
                              for j in range(SUPER)], axis=1)
        return du + d_ref[...] * dys_ref[...]

    du = _rowk("ssm_bwd_du", du_fn, [_rows(gs[0]), _rows(gs[1]), _rows(dys)],
               [ssm[0]["wb"].astype(BF16), ssm[1]["wb"].astype(BF16), lw["ssm_d"]], [(SSM_WIDTH, BF16)], tm=SCAN_CHUNK)[0]

    dq, dk, dv, dkm, dvm, dsink = _attn_bwd(proj, lw["attn_sink"], yattn, lse, dyattn)

    def dproj_fn(i, dq_ref, dk_ref, dv_ref, du_ref, dgs_ref, dga_ref, dkm_ref, dvm_ref):
        first = jnp.where(i == 0, 1.0, 0.0)
        zeros = lambda r: jnp.zeros((r, N_KV_HEADS * HEAD_DIM), F32)
        place = lambda m: jnp.concatenate([zeros(PAD), m[...] * first, zeros(ROW_TILE - BLOCK)], axis=0)
        dp = jnp.concatenate([dq_ref[...].astype(F32), dk_ref[...] + place(dkm_ref), dv_ref[...] + place(dvm_ref),
                              du_ref[...].astype(F32), dgs_ref[...].astype(F32), dga_ref[...].astype(F32)], axis=1)
        return jnp.where(_row_ids(i, ROW_TILE, 4 * D_MODEL) >= PAD, dp, 0.0)

    dproj = _rowk("mix_bwd_dproj", dproj_fn, [_rows(dq), _rows(dk), _rows(dv), _rows(du), _rows(dgs), _rows(dga)],
                  [dkm, dvm], [(4 * D_MODEL, BF16)])[0]

    def bwd_in(i, dp_ref, h_ref, dh_ref, w_ref, g_ref):
        dx, dg = _rms_bwd(h_ref[...], g_ref[...], _dot_nt(dp_ref[...], w_ref[...]))
        return dh_ref[...] + dx, dg

    dh_in, dgain = _rowk("mix_bwd_in", bwd_in, [_rows(dproj), _rows(h), _rows(dh)], [lw["w_in"], lw["mix_norm"]],
                         [(D_MODEL, F32)], accs=[(1, D_MODEL)])
    dw_in = _mm_tn("mix_dw_in", n, dproj, tk=D_MODEL, tn=2 * D_MODEL)
    grads = {"w_out": dw_out, "w_branch_ssm": dw_bs, "w_branch_attn": dw_ba, "ssm_w_glu": dw_glu, "w_in": dw_in,
             "mix_norm": dgain, "ssm_d": dd, "attn_sink": dsink[0, :N_HEADS]}
    return dh_in, grads, ssm_cot


def _loss_head(h, gain, target):
    lp = h.shape[0]

    def fn(i, h_ref, t_ref, g_ref):
        x = h_ref[...]
        y = _rms_fwd(x, g_ref[...])
        live = jnp.where(i == 0, 0.0, 1.0)
        dy = (y - t_ref[...]) * live
        loss = 0.5 * jnp.sum(dy * dy) / D_MODEL
        dx, dg = _rms_bwd(x, g_ref[...], dy * (1.0 / D_MODEL))
        return dx, jnp.full((1, BLOCK), loss, F32), dg

    tgt = (target, (BLOCK, D_MODEL), lambda i: (jnp.maximum(i - 1, 0), 0))
    return _rowk("loss_head", fn, [_rows(h), tgt], [gain], [(D_MODEL, F32)], accs=[(1, BLOCK), (1, D_MODEL)], tm=BLOCK)


def _adamw(name, w, g, m, v, tm):
    def fn(i, w_ref, g_ref, m_ref, v_ref):
        gv = g_ref[...]
        mn = ADAM_B1 * m_ref[...] + (1.0 - ADAM_B1) * gv
        vn = ADAM_B2 * v_ref[...] + (1.0 - ADAM_B2) * (gv * gv)
        m_hat = mn / (1.0 - ADAM_B1 ** ADAM_STEP)
        v_hat = vn / (1.0 - ADAM_B2 ** ADAM_STEP)
        return -ADAM_LR * (m_hat / (jnp.sqrt(v_hat) + ADAM_EPS) + ADAM_WD * w_ref[...]), mn, vn

    wd = w.shape[1]
    return _rowk(name, fn, [_rows(w), _rows(g), _rows(m), _rows(v)], [], [(wd, F32)] * 3, tm=tm)


def _shard_rows(name):
    return {"ffn1_w_gate": 704, "ffn1_w_up": 704, "ffn1_w_down": 704, "ffn2_w_gate": 704, "ffn2_w_up": 704, "ffn2_w_down": 704,
            "w_in": 1024, "ssm_w_glu": 64, "w_branch_ssm": 128, "w_branch_attn": 256, "w_out": 256}[name]


def _full_shape(name):
    return {"ffn1_w_gate": (D_MODEL, D_FF), "ffn1_w_up": (D_MODEL, D_FF), "ffn1_w_down": (D_FF, D_MODEL),
            "ffn2_w_gate": (D_MODEL, D_FF), "ffn2_w_up": (D_MODEL, D_FF), "ffn2_w_down": (D_FF, D_MODEL),
            "w_in": (D_MODEL, 4 * D_MODEL), "ssm_w_glu": (SSM_WIDTH, SSM_WIDTH), "w_branch_ssm": (SSM_WIDTH, D_MODEL),
            "w_branch_attn": (D_MODEL, D_MODEL), "w_out": (D_MODEL, D_MODEL)}[name]


def _unflatten_gathered(gathered):
    out, r0 = {}, 0
    for name in BIG:
        r = _shard_rows(name)
        k, nn = _full_shape(name)
        piece = gathered[:, r0:r0 + r, :]
        if name in COL_SHARDED:
            out[name] = piece.reshape(4, k, nn // 4).transpose(1, 0, 2).reshape(k, nn)
        else:
            out[name] = piece.reshape(k, nn)
        r0 += r
    return out


def _flatten_full(grads):
    per_shard = []
    for s in range(4):
        pieces = []
        for name in BIG:
            k, nn = _full_shape(name)
            g = grads[name]
            piece = g[:, s * (nn // 4):(s + 1) * (nn // 4)] if name in COL_SHARDED else g[s * (k // 4):(s + 1) * (k // 4), :]
            pieces.append(piece.reshape(-1, 1024))
        per_shard.append(jnp.concatenate(pieces, axis=0))
    f = jnp.stack(per_shard)
    return f.reshape(4, 2, f.shape[1] // 2, 1024).transpose(1, 0, 2, 3)


def _shard_2d(a):
    return a.reshape(-1, a.shape[-1])


def kernel(x, meta_tokens, ffn1_norm, ffn1_w_gate, ffn1_w_up, ffn1_w_down, mix_norm, w_in, ssm_lam_re, ssm_lam_im, ssm_log_dt, ssm_b_re, ssm_b_im, ssm_c_re, ssm_c_im, ssm_d, ssm_w_glu, attn_sink, w_branch_ssm, w_branch_attn, w_out, ffn2_norm, ffn2_w_gate, ffn2_w_up, ffn2_w_down, final_norm, loss_target, m_meta_tokens, m_ffn1_norm, m_ffn1_w_gate, m_ffn1_w_up, m_ffn1_w_down, m_mix_norm, m_w_in, m_ssm_lam_re, m_ssm_lam_im, m_ssm_log_dt, m_ssm_b_re, m_ssm_b_im, m_ssm_c_re, m_ssm_c_im, m_ssm_d, m_ssm_w_glu, m_attn_sink, m_w_branch_ssm, m_w_branch_attn, m_w_out, m_ffn2_norm, m_ffn2_w_gate, m_ffn2_w_up, m_ffn2_w_down, m_final_norm, v_meta_tokens, v_ffn1_norm, v_ffn1_w_gate, v_ffn1_w_up, v_ffn1_w_down, v_mix_norm, v_w_in, v_ssm_lam_re, v_ssm_lam_im, v_ssm_log_dt, v_ssm_b_re, v_ssm_b_im, v_ssm_c_re, v_ssm_c_im, v_ssm_d, v_ssm_w_glu, v_attn_sink, v_w_branch_ssm, v_w_branch_attn, v_w_out, v_ffn2_norm, v_ffn2_w_gate, v_ffn2_w_up, v_ffn2_w_down, v_final_norm):
    args = dict(locals())
    w = {k: args[k] for k in WEIGHTS}
    mom = {k: args["m_" + k] for k in WEIGHTS}
    var = {k: args["v_" + k] for k in WEIGHTS}
    depth = ffn1_norm.shape[0]
    seq = x.shape[1]
    xi, yi, ci = lax.axis_index("x"), lax.axis_index("y"), lax.axis_index("c")
    chip = 2 * xi + yi
    sel = jnp.stack([ci, chip]).astype(jnp.int32)

    same_core = [(fx, fy, 0) for fx, fy in _OTHER_CHIPS]
    meta_all = _all_gather_all("gather_meta", meta_tokens, 4, same_core, lambda x_, y_, c_: _chip(x_, y_))
    meta_full = meta_all.transpose(1, 0, 2).reshape(N_META, D_MODEL)
    layer_w = []
    for l in range(depth):
        flat = jnp.concatenate([w[name][l].reshape(-1, 1024) for name in BIG], axis=0).astype(BF16)
        lw = _unflatten_gathered(_all_gather_shards("gather_w", flat))
        for name in ("ffn1_norm", "mix_norm", "ffn2_norm"):
            lw[name] = w[name][l].reshape(1, D_MODEL)
        lw["ssm_d"] = ssm_d[l].reshape(1, SSM_WIDTH)
        lw["attn_sink"] = attn_sink[l]
        layer_w.append(lw)

    ssm_params = ("ssm_lam_re", "ssm_lam_im", "ssm_log_dt", "ssm_b_re", "ssm_b_im", "ssm_c_re", "ssm_c_im")
    ssm, ssm_vjp = [], []
    for l in range(depth):
        dirs, vjps = [], []
        for d in range(2):
            prm = tuple(w[k][l, d] for k in ssm_params)
            (ar, ai, wb, wc), pull = jax.vjp(_ssm_prep, *prm)
            fwd_t = _scan_tables(prm[0], prm[1], prm[2], conj=False)
            bwd_t = _scan_tables(prm[0], prm[1], prm[2], conj=True)
            flip = lambda t: (t[0], t[1], t[2][::-1], t[3][::-1])
            dirs.append({"wb": wb, "wc": wc, "tab_fwd": flip(fwd_t) if d == 1 else fwd_t,
                         "tab_bwd": flip(bwd_t) if d == 0 else bwd_t})
            vjps.append(pull)
        ssm.append(dirs)
        ssm_vjp.append(vjps)

    h = jnp.concatenate([jnp.zeros((PAD, D_MODEL), F32), meta_full, x[0]], axis=0)
    saved = []
    for l in range(depth):
        lw = layer_w[l]
        h, s1 = _ffn_fwd("ffn", h, lw["ffn1_norm"], lw["ffn1_w_gate"], lw["ffn1_w_up"], lw["ffn1_w_down"])
        h, s2 = _mixer_fwd(h, lw, ssm[l])
        h, s3 = _ffn_fwd("ffn", h, lw["ffn2_norm"], lw["ffn2_w_gate"], lw["ffn2_w_up"], lw["ffn2_w_down"])
        saved.append((s1, s2, s3))
    dh, loss_part, d_final = _loss_head(h, final_norm.reshape(1, D_MODEL), loss_target[0])
    loss = lax.psum(loss_part[0, 0], MESH_AXES)

    small_g = {k: [None] * depth for k in SMALL if k not in ("meta_tokens", "final_norm")}
    big_g = {k: [None] * depth for k in BIG}
    for l in reversed(range(depth)):
        lw = layer_w[l]
        s1, s2, s3 = saved[l]
        full = {}
        dh, dg, full["ffn2_w_gate"], full["ffn2_w_up"], full["ffn2_w_down"] = _ffn_bwd(
            "ffn", dh, s3, lw["ffn2_norm"], lw["ffn2_w_gate"], lw["ffn2_w_up"], lw["ffn2_w_down"])
        small_g["ffn2_norm"][l] = dg[0]
        dh, mg, ssm_cot = _mixer_bwd(dh, s2, lw, ssm[l])
        for k in ("w_out", "w_branch_ssm", "w_branch_attn", "ssm_w_glu", "w_in"):
            full[k] = mg[k]
        small_g["mix_norm"][l] = mg["mix_norm"][0]
        small_g["ssm_d"][l] = mg["ssm_d"][0]
        small_g["attn_sink"][l] = mg["attn_sink"]
        per_dir = []
        for d in range(2):
            dwb, dwc, gar, gai = ssm_cot[d]
            per_dir.append(ssm_vjp[l][d]((gar, gai, dwb.reshape(SUPER, 128, 1024), dwc.reshape(SUPER, 1024, 128))))
        for j, k in enumerate(ssm_params):
            small_g[k][l] = jnp.stack([per_dir[0][j], per_dir[1][j]])
        dh, dg, full["ffn1_w_gate"], full["ffn1_w_up"], full["ffn1_w_down"] = _ffn_bwd(
            "ffn", dh, s1, lw["ffn1_norm"], lw["ffn1_w_gate"], lw["ffn1_w_up"], lw["ffn1_w_down"])
        small_g["ffn1_norm"][l] = dg[0]
        reduced = _reduce_scatter("reduce_w", _flatten_full(full), sel)
        r0 = 0
        for name in BIG:
            r = _shard_rows(name)
            big_g[name][l] = reduced[r0:r0 + r].reshape(w[name].shape[1:])
            r0 += r

    grad_x = dh[BLOCK:][None]
    small_list = [dh[PAD:BLOCK].reshape(-1)]
    for k in SMALL[1:]:
        small_list.append(d_final.reshape(-1) if k == "final_norm" else jnp.stack(small_g[k]).reshape(-1))
    small_vec = jnp.concatenate(small_list)
    n_small = small_vec.shape[0]
    rows_small = -(-n_small // (64 * 1024)) * 64
    small_vec = jnp.pad(small_vec, (0, rows_small * 1024 - n_small)).reshape(rows_small, 1024)
    everyone = [(fx, fy, fc) for fx in (0, 1) for fy in (0, 1) for fc in (0, 1)][1:]
    small_all = _all_gather_all("gather_small", small_vec, 8, everyone, lambda x_, y_, c_: 4 * x_ + 2 * y_ + c_)
    small_sum = _slot_sum("sum_small", [(small_all, k) for k in range(8)], sel, 64).reshape(-1)

    grads, deltas, new_m, new_v = {}, {}, {}, {}
    off = 0
    flat_w, flat_m, flat_v, flat_g = [], [], [], []
    for k in SMALL:
        size = (N_META * D_MODEL) if k == "meta_tokens" else int(np.prod(w[k].shape))
        g = small_sum[off:off + size]
        off += size
        if k == "meta_tokens":
            g = lax.dynamic_slice(g.reshape(N_META, D_MODEL), (0, chip * (D_MODEL // 4)), (N_META, D_MODEL // 4))
            grads[k] = g
            deltas[k], new_m[k], new_v[k] = _adamw("adamw_meta", w[k], g, mom[k], var[k], N_META)
        else:
            grads[k] = g.reshape(w[k].shape)
            flat_g.append(g)
            flat_w.append(w[k].reshape(-1))
            flat_m.append(mom[k].reshape(-1))
            flat_v.append(var[k].reshape(-1))
    n_flat = sum(a.shape[0] for a in flat_g)
    rows_flat = -(-n_flat // (64 * 1024)) * 64
    pack = lambda parts, fill: jnp.pad(jnp.concatenate(parts), (0, rows_flat * 1024 - n_flat),
                                       constant_values=fill).reshape(rows_flat, 1024)
    sd, sm_, sv = _adamw("adamw_small", pack(flat_w, 0.0), pack(flat_g, 0.0), pack(flat_m, 0.0), pack(flat_v, 1.0), 64)
    off = 0
    for k in SMALL:
        if k == "meta_tokens":
            continue
        size = int(np.prod(w[k].shape))
        for dst, src in ((deltas, sd), (new_m, sm_), (new_v, sv)):
            dst[k] = src.reshape(-1)[off:off + size].reshape(w[k].shape)
        off += size
    for k in BIG:
        g = jnp.stack(big_g[k])
        grads[k] = g
        rows_k = _shard_2d(g).shape[0]
        tm = 512 if rows_k % 512 == 0 else rows_k // depth
        d_, m_, v_ = _adamw("adamw_" + k, _shard_2d(w[k]), _shard_2d(g), _shard_2d(mom[k]), _shard_2d(var[k]), tm)
        deltas[k], new_m[k], new_v[k] = d_.reshape(g.shape), m_.reshape(g.shape), v_.reshape(g.shape)

    return (loss, grad_x, *[grads[k] for k in WEIGHTS], *[deltas[k] for k in WEIGHTS],
            *[new_m[k] for k in WEIGHTS], *[new_v[k] for k in WEIGHTS])
```

```python
import functools
import math

import numpy as np
import jax
import jax.numpy as jnp
from jax import lax
from jax.experimental import pallas as pl
from jax.experimental.pallas import tpu as pltpu

F32 = jnp.float32
BF16 = jnp.bfloat16

D_MODEL = 1024
N_META = 16
N_HEADS = 16
N_KV_HEADS = 4
HEAD_DIM = 64
Q_GROUP = N_HEADS // N_KV_HEADS
WINDOW = 128
BLOCK = 128
PAD = BLOCK - N_META
SSM_WIDTH = 512
SSM_GROUP_CH = 16
SSM_GROUPS = 32
SSM_STATE = 64
N_STATE = SSM_GROUPS * SSM_STATE
SUPER = 4
D_FF = 2816
EPS = 1e-6
NEG = -1e30
ATTN_SCALE = HEAD_DIM ** -0.5
SLOPES = [float(2.0 ** (-8.0 * (h + 1) / N_HEADS)) for h in range(N_HEADS)]

ADAM_LR = 0.001
ADAM_B1 = 0.9
ADAM_B2 = 0.999
ADAM_EPS = 1e-08
ADAM_WD = 0.01
ADAM_STEP = 10

V7X_VMEM_LIMIT_BYTES = 52 * 1024 * 1024
ROW_TILE = 384
MESH_AXES = ("x", "y", "c")

BIG = ["ffn1_w_gate", "ffn1_w_up", "ffn1_w_down", "ffn2_w_gate", "ffn2_w_up", "ffn2_w_down",
       "w_in", "ssm_w_glu", "w_branch_ssm", "w_branch_attn", "w_out"]
COL_SHARDED = {"ffn1_w_gate", "ffn1_w_up", "ffn2_w_gate", "ffn2_w_up", "w_in", "w_branch_ssm"}
SMALL = ["meta_tokens", "ffn1_norm", "mix_norm", "ffn2_norm", "final_norm", "ssm_lam_re", "ssm_lam_im", "ssm_log_dt",
         "ssm_b_re", "ssm_b_im", "ssm_c_re", "ssm_c_im", "ssm_d", "attn_sink"]
WEIGHTS = ["meta_tokens", "ffn1_norm", "ffn1_w_gate", "ffn1_w_up", "ffn1_w_down", "mix_norm", "w_in", "ssm_lam_re",
           "ssm_lam_im", "ssm_log_dt", "ssm_b_re", "ssm_b_im", "ssm_c_re", "ssm_c_im", "ssm_d", "ssm_w_glu", "attn_sink",
           "w_branch_ssm", "w_branch_attn", "w_out", "ffn2_norm", "ffn2_w_gate", "ffn2_w_up", "ffn2_w_down", "final_norm"]


def _dot(a, b):
    return lax.dot_general(a.astype(BF16), b.astype(BF16), (((1,), (0,)), ((), ())), preferred_element_type=F32)


def _dot_nt(a, b):
    return lax.dot_general(a.astype(BF16), b.astype(BF16), (((1,), (1,)), ((), ())), preferred_element_type=F32)


def _dot_tn(a, b):
    return lax.dot_general(a.astype(BF16), b.astype(BF16), (((0,), (0,)), ((), ())), preferred_element_type=F32)


def _sigmoid(x):
    return 0.5 * jnp.tanh(0.5 * x) + 0.5


_GELU_C = math.sqrt(2.0 / math.pi)


def _gelu(x):
    return 0.5 * x * (1.0 + jnp.tanh(_GELU_C * (x + 0.044715 * x * x * x)))


def _gelu_grad(x):
    th = jnp.tanh(_GELU_C * (x + 0.044715 * x * x * x))
    return 0.5 * (1.0 + th) + 0.5 * x * (1.0 - th * th) * _GELU_C * (1.0 + 3.0 * 0.044715 * x * x)


def _rms_fwd(x, g):
    r = lax.rsqrt(jnp.mean(x * x, axis=-1, keepdims=True) + EPS)
    return x * r * g


def _rms_bwd(x, g, dn):
    r = lax.rsqrt(jnp.mean(x * x, axis=-1, keepdims=True) + EPS)
    xh = x * r
    t = dn * g
    dx = r * (t - xh * jnp.mean(t * xh, axis=-1, keepdims=True))
    return dx, jnp.sum(dn * xh, axis=0, keepdims=True)


def _compiler_params():
    return pltpu.CompilerParams(dimension_semantics=("arbitrary",), vmem_limit_bytes=V7X_VMEM_LIMIT_BYTES)


def _rows(arr, width=None, cb=0):
    return (arr, arr.shape[1] if width is None else width, cb)


def _rowk(name, fn, rows, fulls, outs, accs=(), tm=ROW_TILE, smem=(), n_rows=None):
    n = rows[0][0].shape[0] if n_rows is None else n_rows
    assert n % tm == 0, (name, n, tm)
    in_specs, args = [], []
    for s in smem:
        in_specs.append(pl.BlockSpec(memory_space=pltpu.SMEM))
        args.append(s)
    for r in rows:
        if callable(r[2]):
            in_specs.append(pl.BlockSpec(r[1], r[2]))
        else:
            in_specs.append(pl.BlockSpec((tm, r[1]), functools.partial(lambda i, cb: (i, cb), cb=r[2])))
        args.append(r[0])
    for f in fulls:
        in_specs.append(pl.BlockSpec(memory_space=pl.ANY))
        args.append(f)
    out_specs, out_shape = [], []
    for w, dt in outs:
        out_specs.append(pl.BlockSpec((tm, w), lambda i: (i, 0)))
        out_shape.append(jax.ShapeDtypeStruct((n, w), dt))
    for shp in accs:
        out_specs.append(pl.BlockSpec(shp, functools.partial(lambda i, nd: (0,) * nd, nd=len(shp))))
        out_shape.append(jax.ShapeDtypeStruct(shp, F32))
    ns, nr, nf, no, na = len(smem), len(rows), len(fulls), len(outs), len(accs)
    scratch = [pltpu.VMEM(f.shape, f.dtype) for f in fulls]
    if nf:
        scratch.append(pltpu.SemaphoreType.DMA((nf,)))

    def body(*refs):
        i = pl.program_id(0)
        sm = refs[:ns]
        rr = refs[ns:ns + nr]
        fh = refs[ns + nr:ns + nr + nf]
        oo = refs[ns + nr + nf:ns + nr + nf + no]
        aa = refs[ns + nr + nf + no:ns + nr + nf + no + na]
        fv = refs[ns + nr + nf + no + na:ns + nr + nf + no + na + nf]
        if nf:
            sem = refs[-1]

            @pl.when(i == 0)
            def _():
                cps = [pltpu.make_async_copy(fh[j], fv[j], sem.at[j]) for j in range(nf)]
                for cp in cps:
                    cp.start()
                for cp in cps:
                    cp.wait()
        res = fn(i, *sm, *rr, *fv)
        res = tuple(res) if isinstance(res, (tuple, list)) else (res,)
        for o, v in zip(oo, res[:no]):
            o[...] = v.astype(o.dtype)
        if na:
            @pl.when(i == 0)
            def _():
                for a in aa:
                    a[...] = jnp.zeros_like(a)
            for a, v in zip(aa, res[no:]):
                a[...] += v

    return pl.pallas_call(body, grid=(n // tm,), in_specs=in_specs, out_specs=out_specs, out_shape=out_shape,
                          scratch_shapes=scratch, name=name, compiler_params=_compiler_params())(*args)


def _mm_tn(name, x, y, *, xw=None, xcb=0, tk, tn, scale=1.0, tm=ROW_TILE):
    m = x.shape[0]
    k = x.shape[1] if xw is None else xw
    nn = y.shape[1]
    assert m % tm == 0 and k % tk == 0 and nn % tn == 0, (name, m, k, nn)
    kb0 = (xcb * k) // tk

    def body(x_ref, y_ref, o_ref):
        @pl.when(pl.program_id(2) == 0)
        def _():
            o_ref[...] = jnp.zeros_like(o_ref)
        yv = y_ref[...]
        if scale != 1.0:
            yv = yv * scale
        o_ref[...] += _dot_tn(x_ref[...], yv)

    return pl.pallas_call(
        body, grid=(k // tk, nn // tn, m // tm),
        in_specs=[pl.BlockSpec((tm, tk), lambda a, b, i: (i, kb0 + a)), pl.BlockSpec((tm, tn), lambda a, b, i: (i, b))],
        out_specs=pl.BlockSpec((tk, tn), lambda a, b, i: (a, b)), out_shape=jax.ShapeDtypeStruct((k, nn), F32), name=name,
        compiler_params=pltpu.CompilerParams(dimension_semantics=("arbitrary", "arbitrary", "arbitrary"),
                                             vmem_limit_bytes=V7X_VMEM_LIMIT_BYTES))(x, y)


def _exchange(name, src, n_out, local, sends, alias=False):
    nl, nsnd = len(local), len(sends)
    out_shape = jax.ShapeDtypeStruct((n_out,) + src.shape[1:], src.dtype)

    def body(src_ref, out_ref, lsem, ssem, rsem):
        x, y, c = lax.axis_index("x"), lax.axis_index("y"), lax.axis_index("c")
        cps = []
        for j, (sf, df) in enumerate(local):
            cp = pltpu.make_async_copy(src_ref.at[sf(x, y, c)], out_ref.at[df(x, y, c)], lsem.at[j])
            cp.start()
            cps.append(cp)
        for k, ((fx, fy, fc), sf, df) in enumerate(sends):
            peer = (1 - x if fx else x, 1 - y if fy else y, 1 - c if fc else c)
            cp = pltpu.make_async_remote_copy(src_ref=src_ref.at[sf(x, y, c)], dst_ref=out_ref.at[df(x, y, c)],
                                              send_sem=ssem.at[k], recv_sem=rsem.at[k], device_id=peer,
                                              device_id_type=pl.DeviceIdType.MESH)
            cp.start()
            cps.append(cp)
        for cp in cps:
            cp.wait()

    return pl.pallas_call(
        body, in_specs=[pl.BlockSpec(memory_space=pl.ANY)], out_specs=pl.BlockSpec(memory_space=pl.ANY), out_shape=out_shape,
        scratch_shapes=[pltpu.SemaphoreType.DMA((max(nl, 1),)), pltpu.SemaphoreType.DMA((nsnd,)), pltpu.SemaphoreType.DMA((nsnd,))],
        input_output_aliases=({0: 0} if alias else {}), name=name)(src)


def _chip(x, y):
    return 2 * x + y


_OTHER_CHIPS = [(1, 0), (0, 1), (1, 1)]


def _all_gather_shards(name, shard):
    r, w = shard.shape
    half = shard.reshape(2, r // 2, w)
    first = [((fx, fy, 0), lambda x, y, c: c, lambda x, y, c: 2 * _chip(x, y) + c) for fx, fy in _OTHER_CHIPS]
    g = _exchange(name + "_ici", half, 8, [], first)
    second = [((0, 0, 1),
               (lambda x, y, c, fx=fx, fy=fy: 2 * _chip(x ^ fx, y ^ fy) + c),
               (lambda x, y, c, fx=fx, fy=fy: 2 * _chip(x ^ fx, y ^ fy) + c)) for fx, fy in _OTHER_CHIPS]
    g = _exchange(name + "_d2d", g, 8, [], second, alias=True).reshape(4, r, w)
    mine = lax.broadcasted_iota(jnp.int32, (4, 1, 1), 0) == _chip(lax.axis_index("x"), lax.axis_index("y"))
    return jnp.where(mine, shard[None], g)


def _slot_sum(name, terms, sel, tm, out_slots=None, out_slot=None, also_bf16=False):
    rows, w = terms[0][0].shape[1:]

    def imap(slot):
        if isinstance(slot, int):
            return lambda i, s: (slot, i, 0)
        return lambda i, s: (s[slot[1]], i, 0)

    in_specs = [pl.BlockSpec((None, tm, w), imap(sl)) for _, sl in terms]
    if out_slots is None:
        out_specs, out_shape = [pl.BlockSpec((tm, w), lambda i, s: (i, 0))], [jax.ShapeDtypeStruct((rows, w), F32)]
    else:
        out_specs = [pl.BlockSpec((None, tm, w), imap(out_slot))]
        out_shape = [jax.ShapeDtypeStruct((out_slots, rows, w), F32)]
    if also_bf16:
        out_specs.append(pl.BlockSpec((tm, w), lambda i, s: (i, 0)))
        out_shape.append(jax.ShapeDtypeStruct((rows, w), BF16))
    n_in = len(terms)

    def body(sel_ref, *refs):
        acc = refs[0][...].astype(F32)
        for r in refs[1:n_in]:
            acc = acc + r[...].astype(F32)
        refs[n_in][...] = acc
        if also_bf16:
            refs[n_in + 1][...] = acc.astype(BF16)

    grid_spec = pltpu.PrefetchScalarGridSpec(num_scalar_prefetch=1, grid=(rows // tm,), in_specs=in_specs, out_specs=out_specs)
    out = pl.pallas_call(body, grid_spec=grid_spec, out_shape=out_shape, name=name,
                         compiler_params=_compiler_params())(sel, *[a for a, _ in terms])
    return out if also_bf16 else out[0]


def _reduce_scatter(name, parts, sel):
    _, _, r, w = parts.shape
    f = parts.reshape(2, 4 * r, w)
    got = _exchange(name + "_d2d", f, 1, [], [((0, 0, 1), lambda x, y, c: 1 - c, lambda x, y, c: 0)])
    p, p16 = _slot_sum(name + "_add2", [(f, ("sel", 0)), (got, 0)], sel, 384, also_bf16=True)
    sends = [((fx, fy, 0), (lambda x, y, c, fx=fx, fy=fy: _chip(x ^ fx, y ^ fy)), (lambda x, y, c, k=k: k))
             for k, (fx, fy) in enumerate(_OTHER_CHIPS)]
    got = _exchange(name + "_ici", p16.reshape(4, r, w), 3, [], sends)
    q = _slot_sum(name + "_add4", [(p.reshape(4, r, w), ("sel", 1)), (got, 0), (got, 1), (got, 2)], sel, 496,
                  out_slots=2, out_slot=("sel", 0))
    q = _exchange(name + "_pair", q, 2, [], [((0, 0, 1), lambda x, y, c: c, lambda x, y, c: c)], alias=True)
    return q.reshape(2 * r, w)


def _all_gather_all(name, vec, n_slots, flips, slot_fn):
    sends = [(f, lambda x, y, c: 0, slot_fn) for f in flips]
    g = _exchange(name, vec[None], n_slots, [], sends)
    mine = lax.broadcasted_iota(jnp.int32, (n_slots, 1, 1), 0) == slot_fn(*(lax.axis_index(a) for a in MESH_AXES))
    return jnp.where(mine, vec[None], g)


def _nbr_specs(arr, width, cb, nb):
    return [
        (arr, (BLOCK, width), functools.partial(lambda n, cb: (jnp.maximum(n - 1, 0), cb), cb=cb)),
        (arr, (BLOCK, width), functools.partial(lambda n, cb: (n, cb), cb=cb)),
        (arr, (BLOCK, width), functools.partial(lambda n, cb: (jnp.minimum(n + 1, nb - 1), cb), cb=cb)),
    ]


def _head(h):
    return slice(h * HEAD_DIM, (h + 1) * HEAD_DIM)


def _row_group(n_groups, rows_per_group):
    r = lax.broadcasted_iota(jnp.int32, (n_groups * rows_per_group, 1), 0)
    grp = jnp.zeros_like(r)
    for g in range(1, n_groups):
        grp = grp + jnp.where(r >= g * rows_per_group, 1, 0)
    return grp


def _by_group(grp, vals):
    out = vals[-1]
    for g in range(len(vals) - 2, -1, -1):
        out = jnp.where(grp == g, vals[g], out)
    return out


def _attn_fwd(proj, sink):
    lp = proj.shape[0]
    nb = lp // BLOCK
    kv_w = N_KV_HEADS * HEAD_DIM
    specs = _nbr_specs(proj, kv_w, 4, nb) + _nbr_specs(proj, kv_w, 5, nb)
    specs += [(proj, (BLOCK, kv_w), lambda n: (0, 4)), (proj, (BLOCK, kv_w), lambda n: (0, 5))]
    in_specs = [pl.BlockSpec(memory_space=pltpu.SMEM), pl.BlockSpec((BLOCK, D_MODEL), lambda n: (n, 0))]
    in_specs += [pl.BlockSpec(s[1], s[2]) for s in specs]

    def body(sink_ref, q_ref, kp, kc, kn, vp, vc, vn, km, vm, o_ref, lse_ref):
        n = pl.program_id(0)
        qi = lax.broadcasted_iota(jnp.int32, (BLOCK, 3 * BLOCK), 0)
        sj = lax.broadcasted_iota(jnp.int32, (BLOCK, 3 * BLOCK), 1)
        dist = jnp.abs(qi + BLOCK - sj)
        kpos = (n - 1) * BLOCK + sj
        valid = (dist <= WINDOW) & (kpos >= BLOCK) & (kpos < lp)
        distf = dist.astype(F32)
        kb = jnp.concatenate([kp[...], kc[...], kn[...]], axis=0).astype(BF16)
        vb = jnp.concatenate([vp[...], vc[...], vn[...]], axis=0).astype(BF16)
        kmeta = km[PAD:BLOCK, :].astype(BF16)
        vmeta = vm[PAD:BLOCK, :].astype(BF16)
        valid4 = jnp.concatenate([valid] * Q_GROUP, axis=0)
        distf4 = jnp.concatenate([distf] * Q_GROUP, axis=0)
        grp = _row_group(Q_GROUP, BLOCK)
        for kh in range(N_KV_HEADS):
            ksl = slice(kh * HEAD_DIM, (kh + 1) * HEAD_DIM)
            heads = [kh * Q_GROUP + g for g in range(Q_GROUP)]
            slope = _by_group(grp, [SLOPES[h] for h in heads])
            sk = _by_group(grp, [sink_ref[h] for h in heads])
            q4 = (jnp.concatenate([q_ref[:, _head(h)] for h in heads], axis=0) * ATTN_SCALE).astype(BF16)
            s = jnp.where(valid4, _dot_nt(q4, kb[:, ksl]) - slope * distf4, NEG)
            sm = _dot_nt(q4, kmeta[:, ksl])
            m = jnp.maximum(jnp.maximum(jnp.max(s, axis=1, keepdims=True), jnp.max(sm, axis=1, keepdims=True)), sk)
            e = jnp.exp(s - m)
            em = jnp.exp(sm - m)
            den = jnp.sum(e, axis=1, keepdims=True) + jnp.sum(em, axis=1, keepdims=True) + jnp.exp(sk - m)
            o4 = (_dot(e, vb[:, ksl]) + _dot(em, vmeta[:, ksl])) * (1.0 / den)
            lse4 = m + jnp.log(den)
            for g, h in enumerate(heads):
                o_ref[:, _head(h)] = o4[g * BLOCK:(g + 1) * BLOCK].astype(o_ref.dtype)
                lse_ref[:, h:h + 1] = lse4[g * BLOCK:(g + 1) * BLOCK]

    return pl.pallas_call(
        body, grid=(nb,), in_specs=in_specs,
        out_specs=[pl.BlockSpec((BLOCK, D_MODEL), lambda n: (n, 0)), pl.BlockSpec((BLOCK, N_HEADS), lambda n: (n, 0))],
        out_shape=[jax.ShapeDtypeStruct((lp, D_MODEL), BF16), jax.ShapeDtypeStruct((lp, N_HEADS), F32)],
        name="attn_fwd", compiler_params=_compiler_params())(sink, proj, *[s[0] for s in specs])


def _attn_delta(do, o):
    lp = do.shape[0]
    sel = (lax.broadcasted_iota(jnp.int32, (N_HEADS, D_MODEL), 1) // HEAD_DIM
           == lax.broadcasted_iota(jnp.int32, (N_HEADS, D_MODEL), 0)).astype(BF16)

    def body(do_ref, o_ref, sel_ref, d_ref, dt_ref):
        prod = do_ref[...] * o_ref[...].astype(F32)
        hi = prod.astype(BF16)
        lo = (prod - hi.astype(F32)).astype(BF16)
        d_ref[...] = _dot_nt(hi, sel_ref[...]) + _dot_nt(lo, sel_ref[...])
        dt_ref[...] = _dot_nt(sel_ref[...], hi) + _dot_nt(sel_ref[...], lo)

    return pl.pallas_call(
        body, grid=(lp // BLOCK,),
        in_specs=[pl.BlockSpec((BLOCK, D_MODEL), lambda n: (n, 0)), pl.BlockSpec((BLOCK, D_MODEL), lambda n: (n, 0)),
                  pl.BlockSpec((N_HEADS, D_MODEL), lambda n: (0, 0))],
        out_specs=[pl.BlockSpec((BLOCK, N_HEADS), lambda n: (n, 0)), pl.BlockSpec((N_HEADS, BLOCK), lambda n: (0, n))],
        out_shape=[jax.ShapeDtypeStruct((lp, N_HEADS), F32), jax.ShapeDtypeStruct((N_HEADS, lp), F32)],
        name="attn_delta", compiler_params=_compiler_params())(do, o, sel)


def _attn_bwd(proj, sink, o, lse, do):
    lp = proj.shape[0]
    nb = lp // BLOCK
    kv_w = N_KV_HEADS * HEAD_DIM
    delta, delta_t = _attn_delta(do, o)
    lse_t = lse.T
    row_nbrs = lambda arr: [
        (arr, (N_HEADS, BLOCK), lambda n: (0, jnp.maximum(n - 1, 0))), (arr, (N_HEADS, BLOCK), lambda n: (0, n)),
        (arr, (N_HEADS, BLOCK), lambda n: (0, jnp.minimum(n + 1, nb - 1)))]
    specs = (_nbr_specs(proj, D_MODEL, 0, nb) + _nbr_specs(proj, kv_w, 4, nb) + _nbr_specs(proj, kv_w, 5, nb)
             + [(proj, (BLOCK, kv_w), lambda n: (0, 4)), (proj, (BLOCK, kv_w), lambda n: (0, 5))]
             + _nbr_specs(do, D_MODEL, 0, nb) + [(lse, (BLOCK, N_HEADS), lambda n: (n, 0)), (delta, (BLOCK, N_HEADS), lambda n: (n, 0))]
             + row_nbrs(lse_t) + row_nbrs(delta_t))
    in_specs = [pl.BlockSpec(memory_space=pltpu.SMEM)] + [pl.BlockSpec(s[1], s[2]) for s in specs]

    def body(sink_ref, qp, qc, qn, kp, kc, kn, vp, vc, vn, km, vm, dop, doc, don, lc, dc, ltp, ltc, ltn, dtp, dtc, dtn,
             dq_ref, dk_ref, dv_ref, dkm_ref, dvm_ref, dsk_ref):
        n = pl.program_id(0)

        @pl.when(n == 0)
        def _():
            dkm_ref[...] = jnp.zeros_like(dkm_ref)
            dvm_ref[...] = jnp.zeros_like(dvm_ref)
            dsk_ref[...] = jnp.zeros_like(dsk_ref)

        qi = lax.broadcasted_iota(jnp.int32, (BLOCK, 3 * BLOCK), 0)
        sj = lax.broadcasted_iota(jnp.int32, (BLOCK, 3 * BLOCK), 1)
        dist_q = jnp.abs(qi + BLOCK - sj)
        kpos = (n - 1) * BLOCK + sj
        valid_q = (dist_q <= WINDOW) & (kpos >= BLOCK) & (kpos < lp)
        distf_q = dist_q.astype(F32)
        bi = lax.broadcasted_iota(jnp.int32, (BLOCK, 3 * BLOCK), 1)
        kj = lax.broadcasted_iota(jnp.int32, (BLOCK, 3 * BLOCK), 0)
        dist_k = jnp.abs(bi - BLOCK - kj)
        qpos = (n - 1) * BLOCK + bi
        valid_k = (dist_k <= WINDOW) & (qpos >= 0) & (qpos < lp) & (n >= 1)
        distf_k = dist_k.astype(F32)

        kb = jnp.concatenate([kp[...], kc[...], kn[...]], axis=0).astype(BF16)
        vb = jnp.concatenate([vp[...], vc[...], vn[...]], axis=0).astype(BF16)
        kcur = kc[...].astype(BF16)
        vcur = vc[...].astype(BF16)
        kmeta = km[PAD:BLOCK, :].astype(BF16)
        vmeta = vm[PAD:BLOCK, :].astype(BF16)
        lane = lax.broadcasted_iota(jnp.int32, (1, BLOCK), 1)
        dsink = jnp.zeros((1, BLOCK), F32)
        valid_q4 = jnp.concatenate([valid_q] * Q_GROUP, axis=0)
        distf_q4 = jnp.concatenate([distf_q] * Q_GROUP, axis=0)
        valid_k4 = jnp.concatenate([valid_k] * Q_GROUP, axis=1)
        distf_k4 = jnp.concatenate([distf_k] * Q_GROUP, axis=1)
        grp_q = _row_group(Q_GROUP, BLOCK)
        lane_k = lax.broadcasted_iota(jnp.int32, (1, Q_GROUP * 3 * BLOCK), 1)
        grp_k = sum(jnp.where(lane_k >= g * 3 * BLOCK, 1, 0) for g in range(1, Q_GROUP))
        for kh in range(N_KV_HEADS):
            ksl = slice(kh * HEAD_DIM, (kh + 1) * HEAD_DIM)
            heads = [kh * Q_GROUP + g for g in range(Q_GROUP)]
            slopes = [SLOPES[h] for h in heads]
            q4 = (jnp.concatenate([qc[:, _head(h)] for h in heads], axis=0) * ATTN_SCALE).astype(BF16)
            do4 = jnp.concatenate([doc[:, _head(h)] for h in heads], axis=0)
            delta = jnp.concatenate([dc[:, h:h + 1] for h in heads], axis=0)
            lse4 = jnp.concatenate([lc[:, h:h + 1] for h in heads], axis=0)
            s = _dot_nt(q4, kb[:, ksl]) - _by_group(grp_q, slopes) * distf_q4
            p = jnp.exp(jnp.where(valid_q4, s, NEG) - lse4)
            pm = jnp.exp(_dot_nt(q4, kmeta[:, ksl]) - lse4)
            ps = jnp.exp(_by_group(grp_q, [sink_ref[h] for h in heads]) - lse4)
            do4b = do4.astype(BF16)
            ds = p * (_dot_nt(do4b, vb[:, ksl]) - delta)
            dsm = pm * (_dot_nt(do4b, vmeta[:, ksl]) - delta)
            dq4 = ATTN_SCALE * (_dot(ds, kb[:, ksl]) + _dot(dsm, kmeta[:, ksl]))
            dsk4 = ps * delta
            for g, h in enumerate(heads):
                dq_ref[:, _head(h)] = dq4[g * BLOCK:(g + 1) * BLOCK].astype(dq_ref.dtype)
                dsink = dsink + jnp.where(lane == h, -jnp.sum(dsk4[g * BLOCK:(g + 1) * BLOCK]), 0.0)
            dkm_ref[:, ksl] += _dot_tn(dsm, q4)
            dvm_ref[:, ksl] += _dot_tn(pm, do4b)
            band = lambda a, b, c_: jnp.concatenate([r[:, _head(h)] for h in heads for r in (a, b, c_)], axis=0)
            qb4 = (band(qp, qc, qn) * ATTN_SCALE).astype(BF16)
            dob4b = band(dop, doc, don).astype(BF16)
            delta_b = jnp.concatenate([r[h:h + 1, :] for h in heads for r in (dtp, dtc, dtn)], axis=1)
            lse_b = jnp.concatenate([r[h:h + 1, :] for h in heads for r in (ltp, ltc, ltn)], axis=1)
            st = _dot_nt(kcur[:, ksl], qb4) - _by_group(grp_k, slopes) * distf_k4
            pt = jnp.exp(jnp.where(valid_k4, st, NEG) - lse_b)
            dv_ref[:, ksl] = _dot(pt, dob4b)
            dst = pt * (_dot_nt(vcur[:, ksl], dob4b) - delta_b)
            dk_ref[:, ksl] = _dot(dst, qb4)
        dsk_ref[...] += dsink

    blk = lambda w: pl.BlockSpec((BLOCK, w), lambda n: (n, 0))
    fix = lambda shp: pl.BlockSpec(shp, lambda n: (0, 0))
    return pl.pallas_call(
        body, grid=(nb,), in_specs=in_specs,
        out_specs=[blk(D_MODEL), blk(kv_w), blk(kv_w), fix((N_META, kv_w)), fix((N_META, kv_w)), fix((1, BLOCK))],
        out_shape=[jax.ShapeDtypeStruct((lp, D_MODEL), BF16), jax.ShapeDtypeStruct((lp, kv_w), F32),
                   jax.ShapeDtypeStruct((lp, kv_w), F32), jax.ShapeDtypeStruct((N_META, kv_w), F32),
                   jax.ShapeDtypeStruct((N_META, kv_w), F32), jax.ShapeDtypeStruct((1, BLOCK), F32)],
        name="attn_bwd", compiler_params=_compiler_params())(sink, *[s[0] for s in specs])


N_SEG = 8
SSM_TILE = 384


def _to_segments(a):
    lp, w = a.shape
    return a.reshape(N_SEG, lp // N_SEG, w).transpose(1, 0, 2).reshape(lp, w)


def _from_segments(a):
    lp, w = a.shape
    return a.reshape(lp // N_SEG, N_SEG, w).transpose(1, 0, 2).reshape(lp, w)


def _complex_power(ar, ai, n):
    rr, ri = jnp.ones_like(ar), jnp.zeros_like(ai)
    while n:
        if n & 1:
            rr, ri = rr * ar - ri * ai, rr * ai + ri * ar
        ar, ai = ar * ar - ai * ai, 2.0 * ar * ai
        n >>= 1
    return rr, ri


def _segment_starts(finals, a_seg, reverse):
    fr, fi = finals[:, :N_STATE], finals[:, N_STATE:]
    ar, ai = a_seg[:, :N_STATE], a_seg[:, N_STATE:]
    row = lax.broadcasted_iota(jnp.int32, (N_SEG, N_STATE), 0)
    pr = jnp.zeros((1, N_STATE), F32)
    pi = jnp.zeros((1, N_STATE), F32)
    sr = jnp.zeros((N_SEG, N_STATE), F32)
    si = jnp.zeros((N_SEG, N_STATE), F32)
    for s in (range(N_SEG - 1, -1, -1) if reverse else range(N_SEG)):
        sr = jnp.where(row == s, pr, sr)
        si = jnp.where(row == s, pi, si)
        pr, pi = fr[s:s + 1] + ar * pr - ai * pi, fi[s:s + 1] + ar * pi + ai * pr
    return jnp.concatenate([sr, si], axis=1)


def _recurrence(buf_ref, st_ref, a_ref, reverse):
    steps = SSM_TILE // N_SEG
    half = N_STATE // 2
    for c0 in (0, half):
        re = slice(c0, c0 + half)
        im = slice(N_STATE + c0, N_STATE + c0 + half)
        ar = jnp.broadcast_to(a_ref[:, re], (N_SEG, half))
        ai = jnp.broadcast_to(a_ref[:, im], (N_SEG, half))

        def step(k, carry, re=re, im=im, ar=ar, ai=ai):
            xr, xi = carry
            r0 = pl.multiple_of((steps - 1 - k if reverse else k) * N_SEG, N_SEG)
            nr = ar * xr - ai * xi + buf_ref[pl.ds(r0, N_SEG), re]
            ni = ar * xi + ai * xr + buf_ref[pl.ds(r0, N_SEG), im]
            buf_ref[pl.ds(r0, N_SEG), re] = nr
            buf_ref[pl.ds(r0, N_SEG), im] = ni
            return nr, ni

        xr, xi = lax.fori_loop(0, steps, step, (st_ref[:, re], st_ref[:, im]), unroll=2)
        st_ref[:, re] = xr
        st_ref[:, im] = xi


def _copy_in(pairs, sem):
    cps = [pltpu.make_async_copy(src, dst, sem.at[j]) for j, (src, dst) in enumerate(pairs)]
    for cp in cps:
        cp.start()
    for cp in cps:
        cp.wait()


def _ssm_fwd_dir(name, u_seg, wb, wc, a, a_seg, reverse):
    lp = u_seg.shape[0]
    nt = lp // SSM_TILE
    tile = (lambda i: nt - 1 - i) if reverse else (lambda i: i)
    first = tile(0)
    held = lambda p, i: (p * tile(i) + (1 - p) * first, 0)

    def body(u_ref, wb_hbm, wc_hbm, a_ref, aseg_ref, x_ref, y_ref, wb_ref, wc_ref, buf_ref, st_ref, sem):
        p, i = pl.program_id(0), pl.program_id(1)

        @pl.when((p == 0) & (i == 0))
        def _():
            _copy_in([(wb_hbm, wb_ref), (wc_hbm, wc_ref)], sem)
            st_ref[...] = jnp.zeros_like(st_ref)

        @pl.when((p == 1) & (i == 0))
        def _():
            st_ref[...] = _segment_starts(st_ref[...], aseg_ref[...], reverse)

        def states_into(dst_ref):
            for j in range(SUPER):
                part = _dot(u_ref[:, 128 * j:128 * (j + 1)], wb_ref[j])
                dst_ref[:, 512 * j:512 * (j + 1)] = part[:, :512]
                dst_ref[:, N_STATE + 512 * j:N_STATE + 512 * (j + 1)] = part[:, 512:]
            _recurrence(dst_ref, st_ref, a_ref, reverse)

        @pl.when(p == 0)
        def _():
            states_into(buf_ref)

        @pl.when(p == 1)
        def _():
            states_into(x_ref)
            y_ref[...] = jnp.concatenate([_dot(_state_cols(x_ref, j), wc_ref[j]) for j in range(SUPER)], axis=1)

    fix = lambda shp: pl.BlockSpec(shp, lambda p, i: (0, 0))
    return pl.pallas_call(
        body, grid=(2, nt),
        in_specs=[pl.BlockSpec((SSM_TILE, SSM_WIDTH), lambda p, i: (tile(i), 0)), pl.BlockSpec(memory_space=pl.ANY),
                  pl.BlockSpec(memory_space=pl.ANY), fix((1, 2 * N_STATE)), fix((1, 2 * N_STATE))],
        out_specs=[pl.BlockSpec((SSM_TILE, 2 * N_STATE), held), pl.BlockSpec((SSM_TILE, SSM_WIDTH), held)],
        out_shape=[jax.ShapeDtypeStruct((lp, 2 * N_STATE), F32), jax.ShapeDtypeStruct((lp, SSM_WIDTH), F32)],
        scratch_shapes=[pltpu.VMEM(wb.shape, BF16), pltpu.VMEM(wc.shape, BF16), pltpu.VMEM((SSM_TILE, 2 * N_STATE), F32),
                        pltpu.VMEM((N_SEG, 2 * N_STATE), F32), pltpu.SemaphoreType.DMA((2,))],
        name=name, compiler_params=pltpu.CompilerParams(dimension_semantics=("arbitrary", "arbitrary"),
                                                        vmem_limit_bytes=V7X_VMEM_LIMIT_BYTES))(
        u_seg, wb.astype(BF16), wc.astype(BF16), a, a_seg)


def _ssm_bwd_dir(name, dys_seg, u_seg, x_seg, wb, wc, a_conj, a_seg_conj, fwd_reverse):
    lp = u_seg.shape[0]
    nt = lp // SSM_TILE
    steps = SSM_TILE // N_SEG
    reverse = not fwd_reverse
    tile = (lambda i: nt - 1 - i) if reverse else (lambda i: i)
    first = tile(0)
    held = lambda p, i: (p * tile(i) + (1 - p) * first, 0)
    n_slab = lp // N_SEG
    if fwd_reverse:
        halo = lambda p, i: (p * jnp.minimum((tile(i) + 1) * steps, n_slab - 1), 0)
        edge = lambda p, i: (0, 0)
    else:
        halo = lambda p, i: (p * jnp.maximum(tile(i) * steps - 1, 0), 0)
        edge = lambda p, i: (n_slab - 1, 0)

    def body(dy_ref, u_ref, x_ref, halo_ref, edge_ref, wb_hbm, wc_hbm, a_ref, aseg_ref, du_ref, dwb_ref, dwc_ref, ga_ref,
             wb_ref, wc_ref, buf_ref, st_ref, sem):
        p, i = pl.program_id(0), pl.program_id(1)

        @pl.when((p == 0) & (i == 0))
        def _():
            _copy_in([(wb_hbm, wb_ref), (wc_hbm, wc_ref)], sem)
            st_ref[...] = jnp.zeros_like(st_ref)
            dwb_ref[...] = jnp.zeros_like(dwb_ref)
            dwc_ref[...] = jnp.zeros_like(dwc_ref)
            ga_ref[...] = jnp.zeros_like(ga_ref)

        @pl.when((p == 1) & (i == 0))
        def _():
            st_ref[...] = _segment_starts(st_ref[...], aseg_ref[...], reverse)

        for j in range(SUPER):
            part = _dot_nt(dy_ref[:, 128 * j:128 * (j + 1)], wc_ref[j])
            buf_ref[:, 512 * j:512 * (j + 1)] = part[:, :512]
            buf_ref[:, N_STATE + 512 * j:N_STATE + 512 * (j + 1)] = part[:, 512:]
        _recurrence(buf_ref, st_ref, a_ref, reverse)

        @pl.when(p == 1)
        def _():
            du_ref[...] = jnp.concatenate([_dot_nt(_state_cols(buf_ref, j), wb_ref[j]) for j in range(SUPER)], axis=1)
            dwb_ref[...] += jnp.concatenate(
                [_dot_tn(u_ref[:, 128 * j:128 * (j + 1)], _state_cols(buf_ref, j)) for j in range(SUPER)], axis=0)
            dwc_ref[...] += jnp.concatenate(
                [_dot_tn(dy_ref[:, 128 * j:128 * (j + 1)], _state_cols(x_ref, j)) for j in range(SUPER)], axis=0)
            row = lax.broadcasted_iota(jnp.int32, (N_SEG, 2 * N_STATE), 0)
            if fwd_reverse:
                wrap = jnp.where(row == N_SEG - 1, 0.0, pltpu.roll(edge_ref[...], N_SEG - 1, axis=0))
                open_slab = jnp.where(tile(i) == nt - 1, wrap, halo_ref[...])
                before = lambda cols: jnp.concatenate([x_ref[N_SEG:, cols], open_slab[:, cols]], axis=0)
            else:
                wrap = jnp.where(row == 0, 0.0, pltpu.roll(edge_ref[...], 1, axis=0))
                open_slab = jnp.where(tile(i) == 0, wrap, halo_ref[...])
                before = lambda cols: jnp.concatenate([open_slab[:, cols], x_ref[:SSM_TILE - N_SEG, cols]], axis=0)
            half = N_STATE // 2
            for c0 in (0, half):
                re = slice(c0, c0 + half)
                im = slice(N_STATE + c0, N_STATE + c0 + half)
                gr, gi = buf_ref[:, re], buf_ref[:, im]
                br, bi = before(re), before(im)
                fold = lambda v: jnp.sum(v.reshape(steps, N_SEG, half), axis=0)
                ga_ref[:, re] += fold(gr * br + gi * bi)
                ga_ref[:, im] += fold(gi * br - gr * bi)

    fix = lambda shp: pl.BlockSpec(shp, lambda p, i: (0, 0))
    row_tile = lambda w: pl.BlockSpec((SSM_TILE, w), lambda p, i: (tile(i), 0))
    return pl.pallas_call(
        body, grid=(2, nt),
        in_specs=[row_tile(SSM_WIDTH), row_tile(SSM_WIDTH), pl.BlockSpec((SSM_TILE, 2 * N_STATE), held),
                  pl.BlockSpec((N_SEG, 2 * N_STATE), halo), pl.BlockSpec((N_SEG, 2 * N_STATE), edge),
                  pl.BlockSpec(memory_space=pl.ANY), pl.BlockSpec(memory_space=pl.ANY), fix((1, 2 * N_STATE)), fix((1, 2 * N_STATE))],
        out_specs=[pl.BlockSpec((SSM_TILE, SSM_WIDTH), held), fix((SSM_WIDTH, 1024)), fix((SSM_WIDTH, 1024)),
                   fix((N_SEG, 2 * N_STATE))],
        out_shape=[jax.ShapeDtypeStruct((lp, SSM_WIDTH), F32), jax.ShapeDtypeStruct((SSM_WIDTH, 1024), F32),
                   jax.ShapeDtypeStruct((SSM_WIDTH, 1024), F32), jax.ShapeDtypeStruct((N_SEG, 2 * N_STATE), F32)],
        scratch_shapes=[pltpu.VMEM(wb.shape, BF16), pltpu.VMEM(wc.shape, BF16), pltpu.VMEM((SSM_TILE, 2 * N_STATE), F32),
                        pltpu.VMEM((N_SEG, 2 * N_STATE), F32), pltpu.SemaphoreType.DMA((2,))],
        name=name, compiler_params=pltpu.CompilerParams(dimension_semantics=("arbitrary", "arbitrary"),
                                                        vmem_limit_bytes=V7X_VMEM_LIMIT_BYTES))(
        dys_seg, u_seg, x_seg, x_seg, x_seg, wb.astype(BF16), wc.astype(BF16), a_conj, a_seg_conj)


def _ssm_prep(lam_re, lam_im, log_dt, b_re, b_im, c_re, c_im):
    dt = jnp.exp(log_dt)[:, None]
    er = jnp.exp(lam_re * dt)
    ar, ai = er * jnp.cos(lam_im * dt), er * jnp.sin(lam_im * dt)
    nr, ni = ar - 1.0, ai
    den = lam_re * lam_re + lam_im * lam_im
    cr, ci = (nr * lam_re + ni * lam_im) / den, (ni * lam_re - nr * lam_im) / den
    bbr = cr[:, :, None] * b_re - ci[:, :, None] * b_im
    bbi = cr[:, :, None] * b_im + ci[:, :, None] * b_re
    eye = jnp.eye(8, dtype=F32)

    def in_map(b):
        b = b.reshape(SUPER, 8, SSM_STATE, SSM_GROUP_CH).transpose(0, 1, 3, 2)
        return (b[:, :, :, None, :] * eye[None, :, None, :, None]).reshape(SUPER, 128, 512)

    def out_map(cm):
        cm = cm.reshape(SUPER, 8, SSM_GROUP_CH, SSM_STATE).transpose(0, 1, 3, 2)
        return (cm[:, :, :, None, :] * eye[None, :, None, :, None]).reshape(SUPER, 512, 128)

    wb = jnp.concatenate([in_map(bbr), in_map(bbi)], axis=2)
    wc = jnp.concatenate([out_map(c_re), -out_map(c_im)], axis=1)
    return ar.reshape(1, N_STATE), ai.reshape(1, N_STATE), wb, wc


def _state_cols(ref, j):
    return jnp.concatenate([ref[:, 512 * j:512 * (j + 1)], ref[:, N_STATE + 512 * j:N_STATE + 512 * (j + 1)]], axis=1)


def _row_ids(i, tm, width):
    return i * tm + lax.broadcasted_iota(jnp.int32, (tm, width), 0)


def _ffn_fwd(tag, h, gain, wg, wu, wd):
    n = _rowk(tag + "_norm", lambda i, x, g: _rms_fwd(x[...], g[...]), [_rows(h)], [gain], [(D_MODEL, BF16)])[0]

    def up(i, n_ref, wg_ref, wu_ref):
        a = _dot(n_ref[...], wg_ref[...])
        b = _dot(n_ref[...], wu_ref[...])
        return a, b, a * _sigmoid(a) * b

    a, b, act = _rowk(tag + "_up", up, [_rows(n)], [wg, wu], [(D_FF, BF16)] * 3, tm=192)
    out = _rowk(tag + "_down", lambda i, act_ref, h_ref, wd_ref: h_ref[...] + 0.5 * _dot(act_ref[...], wd_ref[...]),
                [_rows(act), _rows(h)], [wd], [(D_MODEL, F32)])[0]
    return out, (h, n, a, b, act)


def _ffn_bwd(tag, dh, saved, gain, wg, wu, wd):
    h, n, a, b, act = saved

    def bwd1(i, dh_ref, a_ref, b_ref, wd_ref):
        dact = 0.5 * _dot_nt(dh_ref[...], wd_ref[...])
        av = a_ref[...].astype(F32)
        sg = _sigmoid(av)
        return dact * b_ref[...].astype(F32) * (sg * (1.0 + av * (1.0 - sg))), dact * av * sg

    da, db = _rowk(tag + "_bwd_act", bwd1, [_rows(dh), _rows(a), _rows(b)], [wd], [(D_FF, BF16)] * 2, tm=192)

    def bwd2(i, da_ref, db_ref, h_ref, dh_ref, wg_ref, wu_ref, g_ref):
        dn = _dot_nt(da_ref[...], wg_ref[...]) + _dot_nt(db_ref[...], wu_ref[...])
        dx, dg = _rms_bwd(h_ref[...], g_ref[...], dn)
        return dh_ref[...] + dx, dg

    dh_in, dgain = _rowk(tag + "_bwd_in", bwd2, [_rows(da), _rows(db), _rows(h), _rows(dh)], [wg, wu, gain],
                         [(D_MODEL, F32)], accs=[(1, D_MODEL)])
    dwd = _mm_tn(tag + "_dwd", act, dh, tk=D_FF // 2, tn=D_MODEL, scale=0.5)
    dwg = _mm_tn(tag + "_dwg", n, da, tk=D_MODEL, tn=D_FF // 2)
    dwu = _mm_tn(tag + "_dwu", n, db, tk=D_MODEL, tn=D_FF // 2)
    return dh_in, dgain, dwg, dwu, dwd


def _mixer_fwd(h, lw, ssm):
    lp = h.shape[0]
    n = _rowk("mix_norm", lambda i, x, g: _rms_fwd(x[...], g[...]), [_rows(h)], [lw["mix_norm"]], [(D_MODEL, BF16)])[0]
    proj = _rowk("mix_in", lambda i, n_ref, w_ref: _dot(n_ref[...], w_ref[...]), [_rows(n)], [lw["w_in"]],
                 [(4 * D_MODEL, F32)])[0]
    yattn, lse = _attn_fwd(proj, lw["attn_sink"])
    u_seg = _to_segments(proj[:, 3 * SSM_WIDTH:4 * SSM_WIDTH])
    xs, ydir = [], []
    for d in range(2):
        x_seg, y_seg = _ssm_fwd_dir(f"ssm_fwd{d}", u_seg, ssm[d]["wb"], ssm[d]["wc"], ssm[d]["a"], ssm[d]["a_seg"],
                                    reverse=(d == 1))
        xs.append(x_seg)
        ydir.append(y_seg)

    def ssm_out(i, y0_ref, y1_ref, u_ref, d_ref, wglu_ref):
        ys = y0_ref[...] + y1_ref[...] + d_ref[...] * u_ref[...]
        z = _gelu(ys)
        return ys, z * _sigmoid(_dot(z, wglu_ref[...]))

    ys, yssm_seg = _rowk("ssm_out", ssm_out, [_rows(ydir[0]), _rows(ydir[1]), _rows(u_seg)],
                         [lw["ssm_d"], lw["ssm_w_glu"]], [(SSM_WIDTH, F32), (SSM_WIDTH, BF16)])
    yssm = _from_segments(yssm_seg)

    def merge(i, ys_ref, ya_ref, gs_ref, ga_ref, wbs_ref, wba_ref):
        bs = _dot(ys_ref[...], wbs_ref[...])
        ba = _dot(ya_ref[...], wba_ref[...])
        m = _sigmoid(gs_ref[...]) * bs + _sigmoid(ga_ref[...]) * ba
        return bs, ba, jnp.where(_row_ids(i, ROW_TILE, D_MODEL) >= PAD, m, 0.0)

    bs, ba, merged = _rowk("mix_merge", merge, [_rows(yssm), _rows(yattn), _rows(proj, D_MODEL, 2), _rows(proj, D_MODEL, 3)],
                           [lw["w_branch_ssm"], lw["w_branch_attn"]], [(D_MODEL, BF16)] * 3)
    out = _rowk("mix_out", lambda i, m_ref, h_ref, w_ref: h_ref[...] + _dot(m_ref[...], w_ref[...]),
                [_rows(merged), _rows(h)], [lw["w_out"]], [(D_MODEL, F32)])[0]
    return out, (h, n, proj, yattn, lse, u_seg, xs, ys, yssm, bs, ba, merged)


def _mixer_bwd(dh, saved, lw, ssm):
    h, n, proj, yattn, lse, u_seg, xs, ys, yssm, bs, ba, merged = saved

    def bwd1(i, dh_ref, gs_ref, ga_ref, bs_ref, ba_ref, w_ref):
        dm = _dot_nt(dh_ref[...], w_ref[...])
        dm = jnp.where(_row_ids(i, ROW_TILE, D_MODEL) >= PAD, dm, 0.0)
        sgs = _sigmoid(gs_ref[...])
        sga = _sigmoid(ga_ref[...])
        return (dm * sgs, dm * sga, dm * bs_ref[...].astype(F32) * sgs * (1.0 - sgs),
                dm * ba_ref[...].astype(F32) * sga * (1.0 - sga))

    dbs, dba, dgs, dga = _rowk("mix_bwd_merge", bwd1,
                               [_rows(dh), _rows(proj, D_MODEL, 2), _rows(proj, D_MODEL, 3), _rows(bs), _rows(ba)],
                               [lw["w_out"]], [(D_MODEL, BF16)] * 4)
    dw_out = _mm_tn("mix_dw_out", merged, dh, tk=D_MODEL, tn=D_MODEL)
    dw_bs = _mm_tn("mix_dw_bs", yssm, dbs, tk=SSM_WIDTH, tn=D_MODEL)
    dw_ba = _mm_tn("mix_dw_ba", yattn, dba, tk=D_MODEL, tn=D_MODEL)

    def bwd2(i, dbs_ref, dba_ref, wbs_ref, wba_ref):
        return _dot_nt(dba_ref[...], wba_ref[...]), _dot_nt(dbs_ref[...], wbs_ref[...])

    dyattn, dyssm = _rowk("mix_bwd_branches", bwd2, [_rows(dbs), _rows(dba)], [lw["w_branch_ssm"], lw["w_branch_attn"]],
                          [(D_MODEL, F32), (SSM_WIDTH, F32)])

    def bwd3(i, dyssm_ref, ys_ref, u_ref, wglu_ref):
        ysv = ys_ref[...]
        z = _gelu(ysv)
        sg = _sigmoid(_dot(z, wglu_ref[...]))
        dt = dyssm_ref[...] * z * sg * (1.0 - sg)
        dz = dyssm_ref[...] * sg + _dot_nt(dt, wglu_ref[...])
        dys = dz * _gelu_grad(ysv)
        return dys, z, dt, jnp.sum(dys * u_ref[...], axis=0, keepdims=True)

    dys, z, dt, dd = _rowk("mix_bwd_ssm_out", bwd3, [_rows(_to_segments(dyssm)), _rows(ys), _rows(u_seg)], [lw["ssm_w_glu"]],
                           [(SSM_WIDTH, F32), (SSM_WIDTH, BF16), (SSM_WIDTH, BF16)], accs=[(1, SSM_WIDTH)])
    dw_glu = _mm_tn("mix_dw_glu", z, dt, tk=SSM_WIDTH, tn=SSM_WIDTH)

    dus, ssm_cot = [], []
    for d in range(2):
        du_d, dwb, dwc_t, ga = _ssm_bwd_dir(f"ssm_bwd{d}", dys, u_seg, xs[d], ssm[d]["wb"], ssm[d]["wc"], ssm[d]["a_conj"],
                                            ssm[d]["a_seg_conj"], fwd_reverse=(d == 1))
        dus.append(du_d)
        ga = jnp.sum(ga, axis=0, keepdims=True)
        ssm_cot.append((ga[:, :N_STATE], ga[:, N_STATE:], dwb.reshape(SUPER, 128, 1024),
                        dwc_t.reshape(SUPER, 128, 1024).transpose(0, 2, 1)))

    du_seg = _rowk("ssm_bwd_du", lambda i, a_ref, b_ref, dys_ref, d_ref: a_ref[...] + b_ref[...] + d_ref[...] * dys_ref[...],
                   [_rows(dus[0]), _rows(dus[1]), _rows(dys)], [lw["ssm_d"]], [(SSM_WIDTH, BF16)])[0]
    du = _from_segments(du_seg)

    dq, dk, dv, dkm, dvm, dsink = _attn_bwd(proj, lw["attn_sink"], yattn, lse, dyattn)

    def dproj_fn(i, dq_ref, dk_ref, dv_ref, du_ref, dgs_ref, dga_ref, dkm_ref, dvm_ref):
        first = jnp.where(i == 0, 1.0, 0.0)
        zeros = lambda r: jnp.zeros((r, N_KV_HEADS * HEAD_DIM), F32)
        place = lambda m: jnp.concatenate([zeros(PAD), m[...] * first, zeros(ROW_TILE - BLOCK)], axis=0)
        dp = jnp.concatenate([dq_ref[...].astype(F32), dk_ref[...] + place(dkm_ref), dv_ref[...] + place(dvm_ref),
                              du_ref[...].astype(F32), dgs_ref[...].astype(F32), dga_ref[...].astype(F32)], axis=1)
        return jnp.where(_row_ids(i, ROW_TILE, 4 * D_MODEL) >= PAD, dp, 0.0)

    dproj = _rowk("mix_bwd_dproj", dproj_fn, [_rows(dq), _rows(dk), _rows(dv), _rows(du), _rows(dgs), _rows(dga)],
                  [dkm, dvm], [(4 * D_MODEL, BF16)])[0]

    def bwd_in(i, dp_ref, h_ref, dh_ref, w_ref, g_ref):
        dx, dg = _rms_bwd(h_ref[...], g_ref[...], _dot_nt(dp_ref[...], w_ref[...]))
        return dh_ref[...] + dx, dg

    dh_in, dgain = _rowk("mix_bwd_in", bwd_in, [_rows(dproj), _rows(h), _rows(dh)], [lw["w_in"], lw["mix_norm"]],
                         [(D_MODEL, F32)], accs=[(1, D_MODEL)])
    dw_in = _mm_tn("mix_dw_in", n, dproj, tk=D_MODEL, tn=2 * D_MODEL)
    grads = {"w_out": dw_out, "w_branch_ssm": dw_bs, "w_branch_attn": dw_ba, "ssm_w_glu": dw_glu, "w_in": dw_in,
             "mix_norm": dgain, "ssm_d": dd, "attn_sink": dsink[0, :N_HEADS]}
    return dh_in, grads, ssm_cot


def _loss_head(h, gain, target):
    lp = h.shape[0]

    def fn(i, h_ref, t_ref, g_ref):
        x = h_ref[...]
        y = _rms_fwd(x, g_ref[...])
        live = jnp.where(i == 0, 0.0, 1.0)
        dy = (y - t_ref[...]) * live
        loss = 0.5 * jnp.sum(dy * dy) / D_MODEL
        dx, dg = _rms_bwd(x, g_ref[...], dy * (1.0 / D_MODEL))
        return dx, jnp.full((1, BLOCK), loss, F32), dg

    tgt = (target, (BLOCK, D_MODEL), lambda i: (jnp.maximum(i - 1, 0), 0))
    return _rowk("loss_head", fn, [_rows(h), tgt], [gain], [(D_MODEL, F32)], accs=[(1, BLOCK), (1, D_MODEL)], tm=BLOCK)


def _adamw(name, w, g, m, v, tm):
    def fn(i, w_ref, g_ref, m_ref, v_ref):
        gv = g_ref[...]
        mn = ADAM_B1 * m_ref[...] + (1.0 - ADAM_B1) * gv
        vn = ADAM_B2 * v_ref[...] + (1.0 - ADAM_B2) * (gv * gv)
        m_hat = mn / (1.0 - ADAM_B1 ** ADAM_STEP)
        v_hat = vn / (1.0 - ADAM_B2 ** ADAM_STEP)
        return -ADAM_LR * (m_hat / (jnp.sqrt(v_hat) + ADAM_EPS) + ADAM_WD * w_ref[...]), mn, vn

    wd = w.shape[1]
    return _rowk(name, fn, [_rows(w), _rows(g), _rows(m), _rows(v)], [], [(wd, F32)] * 3, tm=tm)


def _shard_rows(name):
    return {"ffn1_w_gate": 704, "ffn1_w_up": 704, "ffn1_w_down": 704, "ffn2_w_gate": 704, "ffn2_w_up": 704, "ffn2_w_down": 704,
            "w_in": 1024, "ssm_w_glu": 64, "w_branch_ssm": 128, "w_branch_attn": 256, "w_out": 256}[name]


def _full_shape(name):
    return {"ffn1_w_gate": (D_MODEL, D_FF), "ffn1_w_up": (D_MODEL, D_FF), "ffn1_w_down": (D_FF, D_MODEL),
            "ffn2_w_gate": (D_MODEL, D_FF), "ffn2_w_up": (D_MODEL, D_FF), "ffn2_w_down": (D_FF, D_MODEL),
            "w_in": (D_MODEL, 4 * D_MODEL), "ssm_w_glu": (SSM_WIDTH, SSM_WIDTH), "w_branch_ssm": (SSM_WIDTH, D_MODEL),
            "w_branch_attn": (D_MODEL, D_MODEL), "w_out": (D_MODEL, D_MODEL)}[name]


def _unflatten_gathered(gathered):
    out, r0 = {}, 0
    for name in BIG:
        r = _shard_rows(name)
        k, nn = _full_shape(name)
        piece = gathered[:, r0:r0 + r, :]
        if name in COL_SHARDED:
            out[name] = piece.reshape(4, k, nn // 4).transpose(1, 0, 2).reshape(k, nn)
        else:
            out[name] = piece.reshape(k, nn)
        r0 += r
    return out


def _flatten_full(grads):
    per_shard = []
    for s in range(4):
        pieces = []
        for name in BIG:
            k, nn = _full_shape(name)
            g = grads[name]
            piece = g[:, s * (nn // 4):(s + 1) * (nn // 4)] if name in COL_SHARDED else g[s * (k // 4):(s + 1) * (k // 4), :]
            pieces.append(piece.reshape(-1, 1024))
        per_shard.append(jnp.concatenate(pieces, axis=0))
    f = jnp.stack(per_shard)
    return f.reshape(4, 2, f.shape[1] // 2, 1024).transpose(1, 0, 2, 3)


def _shard_2d(a):
    return a.reshape(-1, a.shape[-1])


def kernel(x, meta_tokens, ffn1_norm, ffn1_w_gate, ffn1_w_up, ffn1_w_down, mix_norm, w_in, ssm_lam_re, ssm_lam_im, ssm_log_dt, ssm_b_re, ssm_b_im, ssm_c_re, ssm_c_im, ssm_d, ssm_w_glu, attn_sink, w_branch_ssm, w_branch_attn, w_out, ffn2_norm, ffn2_w_gate, ffn2_w_up, ffn2_w_down, final_norm, loss_target, m_meta_tokens, m_ffn1_norm, m_ffn1_w_gate, m_ffn1_w_up, m_ffn1_w_down, m_mix_norm, m_w_in, m_ssm_lam_re, m_ssm_lam_im, m_ssm_log_dt, m_ssm_b_re, m_ssm_b_im, m_ssm_c_re, m_ssm_c_im, m_ssm_d, m_ssm_w_glu, m_attn_sink, m_w_branch_ssm, m_w_branch_attn, m_w_out, m_ffn2_norm, m_ffn2_w_gate, m_ffn2_w_up, m_ffn2_w_down, m_final_norm, v_meta_tokens, v_ffn1_norm, v_ffn1_w_gate, v_ffn1_w_up, v_ffn1_w_down, v_mix_norm, v_w_in, v_ssm_lam_re, v_ssm_lam_im, v_ssm_log_dt, v_ssm_b_re, v_ssm_b_im, v_ssm_c_re, v_ssm_c_im, v_ssm_d, v_ssm_w_glu, v_attn_sink, v_w_branch_ssm, v_w_branch_attn, v_w_out, v_ffn2_norm, v_ffn2_w_gate, v_ffn2_w_up, v_ffn2_w_down, v_final_norm):
    args = dict(locals())
    w = {k: args[k] for k in WEIGHTS}
    mom = {k: args["m_" + k] for k in WEIGHTS}
    var = {k: args["v_" + k] for k in WEIGHTS}
    depth = ffn1_norm.shape[0]
    seq = x.shape[1]
    xi, yi, ci = lax.axis_index("x"), lax.axis_index("y"), lax.axis_index("c")
    chip = 2 * xi + yi
    sel = jnp.stack([ci, chip]).astype(jnp.int32)

    same_core = [(fx, fy, 0) for fx, fy in _OTHER_CHIPS]
    meta_all = _all_gather_all("gather_meta", meta_tokens, 4, same_core, lambda x_, y_, c_: _chip(x_, y_))
    meta_full = meta_all.transpose(1, 0, 2).reshape(N_META, D_MODEL)
    layer_w = []
    for l in range(depth):
        flat = jnp.concatenate([w[name][l].reshape(-1, 1024) for name in BIG], axis=0).astype(BF16)
        lw = _unflatten_gathered(_all_gather_shards("gather_w", flat))
        for name in ("ffn1_norm", "mix_norm", "ffn2_norm"):
            lw[name] = w[name][l].reshape(1, D_MODEL)
        lw["ssm_d"] = ssm_d[l].reshape(1, SSM_WIDTH)
        lw["attn_sink"] = attn_sink[l]
        layer_w.append(lw)

    ssm_params = ("ssm_lam_re", "ssm_lam_im", "ssm_log_dt", "ssm_b_re", "ssm_b_im", "ssm_c_re", "ssm_c_im")
    ssm, ssm_vjp = [], []
    for l in range(depth):
        dirs, vjps = [], []
        for d in range(2):
            prm = tuple(w[k][l, d] for k in ssm_params)
            (ar, ai, wb, wc), pull = jax.vjp(_ssm_prep, *prm)
            a_seg = _complex_power(ar, ai, (seq + BLOCK) // N_SEG)
            conj = lambda v: jnp.concatenate([v[0], -v[1]], axis=1)
            pack = lambda v: jnp.concatenate([v[0], v[1]], axis=1)
            dirs.append({"wb": wb, "wc": wc, "a": pack((ar, ai)), "a_seg": pack(a_seg),
                         "a_conj": conj((ar, ai)), "a_seg_conj": conj(a_seg)})
            vjps.append(pull)
        ssm.append(dirs)
        ssm_vjp.append(vjps)

    h = jnp.concatenate([jnp.zeros((PAD, D_MODEL), F32), meta_full, x[0]], axis=0)
    saved = []
    for l in range(depth):
        lw = layer_w[l]
        h, s1 = _ffn_fwd("ffn", h, lw["ffn1_norm"], lw["ffn1_w_gate"], lw["ffn1_w_up"], lw["ffn1_w_down"])
        h, s2 = _mixer_fwd(h, lw, ssm[l])
        h, s3 = _ffn_fwd("ffn", h, lw["ffn2_norm"], lw["ffn2_w_gate"], lw["ffn2_w_up"], lw["ffn2_w_down"])
        saved.append((s1, s2, s3))
    dh, loss_part, d_final = _loss_head(h, final_norm.reshape(1, D_MODEL), loss_target[0])
    loss = lax.psum(loss_part[0, 0], MESH_AXES)

    small_g = {k: [None] * depth for k in SMALL if k not in ("meta_tokens", "final_norm")}
    big_g = {k: [None] * depth for k in BIG}
    for l in reversed(range(depth)):
        lw = layer_w[l]
        s1, s2, s3 = saved[l]
        full = {}
        dh, dg, full["ffn2_w_gate"], full["ffn2_w_up"], full["ffn2_w_down"] = _ffn_bwd(
            "ffn", dh, s3, lw["ffn2_norm"], lw["ffn2_w_gate"], lw["ffn2_w_up"], lw["ffn2_w_down"])
        small_g["ffn2_norm"][l] = dg[0]
        dh, mg, ssm_cot = _mixer_bwd(dh, s2, lw, ssm[l])
        for k in ("w_out", "w_branch_ssm", "w_branch_attn", "ssm_w_glu", "w_in"):
            full[k] = mg[k]
        small_g["mix_norm"][l] = mg["mix_norm"][0]
        small_g["ssm_d"][l] = mg["ssm_d"][0]
        small_g["attn_sink"][l] = mg["attn_sink"]
        per_dir = []
        for d in range(2):
            per_dir.append(ssm_vjp[l][d](ssm_cot[d]))
        for j, k in enumerate(ssm_params):
            small_g[k][l] = jnp.stack([per_dir[0][j], per_dir[1][j]])
        dh, dg, full["ffn1_w_gate"], full["ffn1_w_up"], full["ffn1_w_down"] = _ffn_bwd(
            "ffn", dh, s1, lw["ffn1_norm"], lw["ffn1_w_gate"], lw["ffn1_w_up"], lw["ffn1_w_down"])
        small_g["ffn1_norm"][l] = dg[0]
        reduced = _reduce_scatter("reduce_w", _flatten_full(full), sel)
        r0 = 0
        for name in BIG:
            r = _shard_rows(name)
            big_g[name][l] = reduced[r0:r0 + r].reshape(w[name].shape[1:])
            r0 += r

    grad_x = dh[BLOCK:][None]
    small_list = [dh[PAD:BLOCK].reshape(-1)]
    for k in SMALL[1:]:
        small_list.append(d_final.reshape(-1) if k == "final_norm" else jnp.stack(small_g[k]).reshape(-1))
    small_vec = jnp.concatenate(small_list)
    n_small = small_vec.shape[0]
    rows_small = -(-n_small // (64 * 1024)) * 64
    small_vec = jnp.pad(small_vec, (0, rows_small * 1024 - n_small)).reshape(rows_small, 1024)
    everyone = [(fx, fy, fc) for fx in (0, 1) for fy in (0, 1) for fc in (0, 1)][1:]
    small_all = _all_gather_all("gather_small", small_vec, 8, everyone, lambda x_, y_, c_: 4 * x_ + 2 * y_ + c_)
    small_sum = _slot_sum("sum_small", [(small_all, k) for k in range(8)], sel, 64).reshape(-1)

    grads, deltas, new_m, new_v = {}, {}, {}, {}
    off = 0
    flat_w, flat_m, flat_v, flat_g = [], [], [], []
    for k in SMALL:
        size = (N_META * D_MODEL) if k == "meta_tokens" else int(np.prod(w[k].shape))
        g = small_sum[off:off + size]
        off += size
        if k == "meta_tokens":
            g = lax.dynamic_slice(g.reshape(N_META, D_MODEL), (0, chip * (D_MODEL // 4)), (N_META, D_MODEL // 4))
            grads[k] = g
            deltas[k], new_m[k], new_v[k] = _adamw("adamw_meta", w[k], g, mom[k], var[k], N_META)
        else:
            grads[k] = g.reshape(w[k].shape)
            flat_g.append(g)
            flat_w.append(w[k].reshape(-1))
            flat_m.append(mom[k].reshape(-1))
            flat_v.append(var[k].reshape(-1))
    n_flat = sum(a.shape[0] for a in flat_g)
    rows_flat = -(-n_flat // (64 * 1024)) * 64
    pack = lambda parts, fill: jnp.pad(jnp.concatenate(parts), (0, rows_flat * 1024 - n_flat),
                                       constant_values=fill).reshape(rows_flat, 1024)
    sd, sm_, sv = _adamw("adamw_small", pack(flat_w, 0.0), pack(flat_g, 0.0), pack(flat_m, 0.0), pack(flat_v, 1.0), 64)
    off = 0
    for k in SMALL:
        if k == "meta_tokens":
            continue
        size = int(np.prod(w[k].shape))
        for dst, src in ((deltas, sd), (new_m, sm_), (new_v, sv)):
            dst[k] = src.reshape(-1)[off:off + size].reshape(w[k].shape)
        off += size
    for k in BIG:
        g = jnp.stack(big_g[k])
        grads[k] = g
        rows_k = _shard_2d(g).shape[0]
        tm = 512 if rows_k % 512 == 0 else rows_k // depth
        d_, m_, v_ = _adamw("adamw_" + k, _shard_2d(w[k]), _shard_2d(g), _shard_2d(mom[k]), _shard_2d(var[k]), tm)
        deltas[k], new_m[k], new_v[k] = d_.reshape(g.shape), m_.reshape(g.shape), v_.reshape(g.shape)

    return (loss, grad_x, *[grads[k] for k in WEIGHTS], *[deltas[k] for k in WEIGHTS],
            *[new_m[k] for k in WEIGHTS], *[new_v[k] for k in WEIGHTS])
```

```python
import functools
import math

import numpy as np
import jax
import jax.numpy as jnp
from jax import lax
from jax.experimental import pallas as pl
from jax.experimental.pallas import tpu as pltpu

F32 = jnp.float32
BF16 = jnp.bfloat16

D_MODEL = 1024
N_META = 16
N_HEADS = 16
N_KV_HEADS = 4
HEAD_DIM = 64
Q_GROUP = N_HEADS // N_KV_HEADS
WINDOW = 128
BLOCK = 128
PAD = BLOCK - N_META
SSM_WIDTH = 512
SSM_GROUP_CH = 16
SSM_GROUPS = 32
SSM_STATE = 64
N_STATE = SSM_GROUPS * SSM_STATE
SUPER = 4
D_FF = 2816
EPS = 1e-6
NEG = -1e30
ATTN_SCALE = HEAD_DIM ** -0.5
SLOPES = [float(2.0 ** (-8.0 * (h + 1) / N_HEADS)) for h in range(N_HEADS)]

ADAM_LR = 0.001
ADAM_B1 = 0.9
ADAM_B2 = 0.999
ADAM_EPS = 1e-08
ADAM_WD = 0.01
ADAM_STEP = 10

V7X_VMEM_LIMIT_BYTES = 52 * 1024 * 1024
ROW_TILE = 384
LONG_ROW_TILE = 1376
MESH_AXES = ("x", "y", "c")

BIG = ["ffn1_w_gate", "ffn1_w_up", "ffn1_w_down", "ffn2_w_gate", "ffn2_w_up", "ffn2_w_down",
       "w_in", "ssm_w_glu", "w_branch_ssm", "w_branch_attn", "w_out"]
COL_SHARDED = {"ffn1_w_gate", "ffn1_w_up", "ffn2_w_gate", "ffn2_w_up", "w_in", "w_branch_ssm"}
SMALL = ["meta_tokens", "ffn1_norm", "mix_norm", "ffn2_norm", "final_norm", "ssm_lam_re", "ssm_lam_im", "ssm_log_dt",
         "ssm_b_re", "ssm_b_im", "ssm_c_re", "ssm_c_im", "ssm_d", "attn_sink"]
WEIGHTS = ["meta_tokens", "ffn1_norm", "ffn1_w_gate", "ffn1_w_up", "ffn1_w_down", "mix_norm", "w_in", "ssm_lam_re",
           "ssm_lam_im", "ssm_log_dt", "ssm_b_re", "ssm_b_im", "ssm_c_re", "ssm_c_im", "ssm_d", "ssm_w_glu", "attn_sink",
           "w_branch_ssm", "w_branch_attn", "w_out", "ffn2_norm", "ffn2_w_gate", "ffn2_w_up", "ffn2_w_down", "final_norm"]


def _dot(a, b):
    return lax.dot_general(a.astype(BF16), b.astype(BF16), (((1,), (0,)), ((), ())), preferred_element_type=F32)


def _dot_nt(a, b):
    return lax.dot_general(a.astype(BF16), b.astype(BF16), (((1,), (1,)), ((), ())), preferred_element_type=F32)


def _dot_tn(a, b):
    return lax.dot_general(a.astype(BF16), b.astype(BF16), (((0,), (0,)), ((), ())), preferred_element_type=F32)


def _sigmoid(x):
    return 0.5 * jnp.tanh(0.5 * x) + 0.5


_GELU_C = math.sqrt(2.0 / math.pi)


def _gelu(x):
    return 0.5 * x * (1.0 + jnp.tanh(_GELU_C * (x + 0.044715 * x * x * x)))


def _gelu_grad(x):
    th = jnp.tanh(_GELU_C * (x + 0.044715 * x * x * x))
    return 0.5 * (1.0 + th) + 0.5 * x * (1.0 - th * th) * _GELU_C * (1.0 + 3.0 * 0.044715 * x * x)


def _rms_fwd(x, g):
    r = lax.rsqrt(jnp.mean(x * x, axis=-1, keepdims=True) + EPS)
    return x * r * g


def _rms_bwd(x, g, dn):
    r = lax.rsqrt(jnp.mean(x * x, axis=-1, keepdims=True) + EPS)
    xh = x * r
    t = dn * g
    dx = r * (t - xh * jnp.mean(t * xh, axis=-1, keepdims=True))
    return dx, jnp.sum(dn * xh, axis=0, keepdims=True)


def _compiler_params():
    return pltpu.CompilerParams(dimension_semantics=("arbitrary",), vmem_limit_bytes=V7X_VMEM_LIMIT_BYTES)


def _rows(arr, width=None, cb=0):
    return (arr, arr.shape[1] if width is None else width, cb)


def _rowk(name, fn, rows, fulls, outs, accs=(), tm=ROW_TILE, smem=(), n_rows=None, hosted=None):
    n = rows[0][0].shape[0] if n_rows is None else n_rows
    assert n % tm == 0, (name, n, tm)
    in_specs, args = [], []
    for s in smem:
        in_specs.append(pl.BlockSpec(memory_space=pltpu.SMEM))
        args.append(s)
    for r in rows:
        if callable(r[2]):
            in_specs.append(pl.BlockSpec(r[1], r[2]))
        else:
            in_specs.append(pl.BlockSpec((tm, r[1]), functools.partial(lambda i, cb: (i, cb), cb=r[2])))
        args.append(r[0])
    for f in fulls:
        in_specs.append(pl.BlockSpec(memory_space=pl.ANY))
        args.append(f)
    nh = 0 if hosted is None else 1
    if nh:
        in_specs.append(pl.BlockSpec(memory_space=pl.ANY))
        args.append(hosted[0])
    out_specs, out_shape = [], []
    for w, dt in outs:
        out_specs.append(pl.BlockSpec((tm, w), lambda i: (i, 0)))
        out_shape.append(jax.ShapeDtypeStruct((n, w), dt))
    for shp in accs:
        out_specs.append(pl.BlockSpec(shp, functools.partial(lambda i, nd: (0,) * nd, nd=len(shp))))
        out_shape.append(jax.ShapeDtypeStruct(shp, F32))
    if nh:
        out_specs.append(pl.BlockSpec(memory_space=pl.ANY))
        out_shape.append(jax.ShapeDtypeStruct((hosted[1],) + hosted[0].shape[1:], hosted[0].dtype))
    ns, nr, nf, no, na = len(smem), len(rows), len(fulls), len(outs), len(accs)
    scratch = [pltpu.VMEM(f.shape, f.dtype) for f in fulls]
    if nf:
        scratch.append(pltpu.SemaphoreType.DMA((nf,)))
    if nh:
        scratch += [pltpu.SemaphoreType.DMA((len(hosted[2]),)), pltpu.SemaphoreType.DMA((len(hosted[2]),))]
    steps = n // tm

    def body(*refs):
        i = pl.program_id(0)
        refs = list(refs)
        take = lambda k: [refs.pop(0) for _ in range(k)]
        sm, rr, fh, hsrc, oo, aa, hout, fv = take(ns), take(nr), take(nf), take(nh), take(no), take(na), take(nh), take(nf)
        if nf:
            sem = refs.pop(0)

            @pl.when(i == 0)
            def _():
                cps = [pltpu.make_async_copy(fh[j], fv[j], sem.at[j]) for j in range(nf)]
                for cp in cps:
                    cp.start()
                for cp in cps:
                    cp.wait()
        if nh:
            @pl.when(i == 0)
            def _():
                for cp in _remote_copies(hsrc[0], hout[0], refs[0], refs[1], hosted[2]):
                    cp.start()
        res = fn(i, *sm, *rr, *fv)
        res = tuple(res) if isinstance(res, (tuple, list)) else (res,)
        for o, v in zip(oo, res[:no]):
            o[...] = v.astype(o.dtype)
        if na:
            @pl.when(i == 0)
            def _():
                for a in aa:
                    a[...] = jnp.zeros_like(a)
            for a, v in zip(aa, res[no:]):
                a[...] += v
        if nh:
            @pl.when(i == steps - 1)
            def _():
                for cp in _remote_copies(hsrc[0], hout[0], refs[0], refs[1], hosted[2]):
                    cp.wait()

    return pl.pallas_call(body, grid=(steps,), in_specs=in_specs, out_specs=out_specs, out_shape=out_shape,
                          scratch_shapes=scratch, name=name, compiler_params=_compiler_params())(*args)


def _mm_tn(name, x, y, *, xw=None, xcb=0, tk, tn, scale=1.0, tm=None):
    m = x.shape[0]
    if tm is None:
        tm = LONG_ROW_TILE if m % LONG_ROW_TILE == 0 else ROW_TILE
    k = x.shape[1] if xw is None else xw
    nn = y.shape[1]
    assert m % tm == 0 and k % tk == 0 and nn % tn == 0, (name, m, k, nn)
    kb0 = (xcb * k) // tk

    def body(x_ref, y_ref, o_ref):
        @pl.when(pl.program_id(2) == 0)
        def _():
            o_ref[...] = jnp.zeros_like(o_ref)
        yv = y_ref[...]
        if scale != 1.0:
            yv = yv * scale
        o_ref[...] += _dot_tn(x_ref[...], yv)

    return pl.pallas_call(
        body, grid=(k // tk, nn // tn, m // tm),
        in_specs=[pl.BlockSpec((tm, tk), lambda a, b, i: (i, kb0 + a)), pl.BlockSpec((tm, tn), lambda a, b, i: (i, b))],
        out_specs=pl.BlockSpec((tk, tn), lambda a, b, i: (a, b)), out_shape=jax.ShapeDtypeStruct((k, nn), F32), name=name,
        compiler_params=pltpu.CompilerParams(dimension_semantics=("arbitrary", "arbitrary", "arbitrary"),
                                             vmem_limit_bytes=V7X_VMEM_LIMIT_BYTES))(x, y)


def _remote_copies(src_ref, out_ref, ssem, rsem, sends):
    x, y, c = lax.axis_index("x"), lax.axis_index("y"), lax.axis_index("c")
    cps = []
    for k, ((fx, fy, fc), sf, df) in enumerate(sends):
        peer = (1 - x if fx else x, 1 - y if fy else y, 1 - c if fc else c)
        cps.append(pltpu.make_async_remote_copy(src_ref=src_ref.at[sf(x, y, c)], dst_ref=out_ref.at[df(x, y, c)],
                                                send_sem=ssem.at[k], recv_sem=rsem.at[k], device_id=peer,
                                                device_id_type=pl.DeviceIdType.MESH))
    return cps


def _exchange(name, src, n_out, local, sends, alias=False):
    nl, nsnd = len(local), len(sends)
    out_shape = jax.ShapeDtypeStruct((n_out,) + src.shape[1:], src.dtype)

    def body(src_ref, out_ref, lsem, ssem, rsem):
        x, y, c = lax.axis_index("x"), lax.axis_index("y"), lax.axis_index("c")
        cps = [pltpu.make_async_copy(src_ref.at[sf(x, y, c)], out_ref.at[df(x, y, c)], lsem.at[j])
               for j, (sf, df) in enumerate(local)]
        cps += _remote_copies(src_ref, out_ref, ssem, rsem, sends)
        for cp in cps:
            cp.start()
        for cp in cps:
            cp.wait()

    return pl.pallas_call(
        body, in_specs=[pl.BlockSpec(memory_space=pl.ANY)], out_specs=pl.BlockSpec(memory_space=pl.ANY), out_shape=out_shape,
        scratch_shapes=[pltpu.SemaphoreType.DMA((max(nl, 1),)), pltpu.SemaphoreType.DMA((nsnd,)), pltpu.SemaphoreType.DMA((nsnd,))],
        input_output_aliases=({0: 0} if alias else {}), name=name)(src)


def _chip(x, y):
    return 2 * x + y


_OTHER_CHIPS = [(1, 0), (0, 1), (1, 1)]


def _all_gather_shards(name, shard):
    src, n_out, sends = _gather_ici_stage(shard)
    return _gather_finish(name, shard, _exchange(name + "_ici", src, n_out, [], sends))


def _gather_ici_stage(shard):
    r, w = shard.shape
    first = [((fx, fy, 0), lambda x, y, c: c, lambda x, y, c: 2 * _chip(x, y) + c) for fx, fy in _OTHER_CHIPS]
    return shard.reshape(2, r // 2, w), 8, first


def _gather_finish(name, shard, g):
    r, w = shard.shape
    second = [((0, 0, 1),
               (lambda x, y, c, fx=fx, fy=fy: 2 * _chip(x ^ fx, y ^ fy) + c),
               (lambda x, y, c, fx=fx, fy=fy: 2 * _chip(x ^ fx, y ^ fy) + c)) for fx, fy in _OTHER_CHIPS]
    g = _exchange(name + "_d2d", g, 8, [], second, alias=True).reshape(4, r, w)
    mine = lax.broadcasted_iota(jnp.int32, (4, 1, 1), 0) == _chip(lax.axis_index("x"), lax.axis_index("y"))
    return jnp.where(mine, shard[None], g)


def _slot_sum(name, terms, sel, tm, out_slots=None, out_slot=None, also_bf16=False):
    rows, w = terms[0][0].shape[1:]

    def imap(slot):
        if isinstance(slot, int):
            return lambda i, s: (slot, i, 0)
        return lambda i, s: (s[slot[1]], i, 0)

    in_specs = [pl.BlockSpec((None, tm, w), imap(sl)) for _, sl in terms]
    if out_slots is None:
        out_specs, out_shape = [pl.BlockSpec((tm, w), lambda i, s: (i, 0))], [jax.ShapeDtypeStruct((rows, w), F32)]
    else:
        out_specs = [pl.BlockSpec((None, tm, w), imap(out_slot))]
        out_shape = [jax.ShapeDtypeStruct((out_slots, rows, w), F32)]
    if also_bf16:
        out_specs.append(pl.BlockSpec((tm, w), lambda i, s: (i, 0)))
        out_shape.append(jax.ShapeDtypeStruct((rows, w), BF16))
    n_in = len(terms)

    def body(sel_ref, *refs):
        acc = refs[0][...].astype(F32)
        for r in refs[1:n_in]:
            acc = acc + r[...].astype(F32)
        refs[n_in][...] = acc
        if also_bf16:
            refs[n_in + 1][...] = acc.astype(BF16)

    grid_spec = pltpu.PrefetchScalarGridSpec(num_scalar_prefetch=1, grid=(rows // tm,), in_specs=in_specs, out_specs=out_specs)
    out = pl.pallas_call(body, grid_spec=grid_spec, out_shape=out_shape, name=name,
                         compiler_params=_compiler_params())(sel, *[a for a, _ in terms])
    return out if also_bf16 else out[0]


def _reduce_scatter(name, parts, sel):
    f, (src, n_out, sends) = _rs_pair_stage(parts)
    p, (src2, n_out2, sends2) = _rs_chip_stage(name, f, _exchange(name + "_d2d", src, n_out, [], sends), sel)
    return _rs_finish(name, p, _exchange(name + "_ici", src2, n_out2, [], sends2), sel)


def _rs_pair_stage(parts):
    _, _, r, w = parts.shape
    f = parts.reshape(2, 4 * r, w)
    return f, (f, 1, [((0, 0, 1), lambda x, y, c: 1 - c, lambda x, y, c: 0)])


def _rs_chip_stage(name, f, got, sel):
    r, w = f.shape[1] // 4, f.shape[2]
    p, p16 = _slot_sum(name + "_add2", [(f, ("sel", 0)), (got, 0)], sel, 384, also_bf16=True)
    sends = [((fx, fy, 0), (lambda x, y, c, fx=fx, fy=fy: _chip(x ^ fx, y ^ fy)), (lambda x, y, c, k=k: k))
             for k, (fx, fy) in enumerate(_OTHER_CHIPS)]
    return p.reshape(4, r, w), (p16.reshape(4, r, w), 3, sends)


def _rs_finish(name, p, got, sel):
    _, r, w = p.shape
    q = _slot_sum(name + "_add4", [(p, ("sel", 1)), (got, 0), (got, 1), (got, 2)], sel, 496, out_slots=2, out_slot=("sel", 0))
    q = _exchange(name + "_pair", q, 2, [], [((0, 0, 1), lambda x, y, c: c, lambda x, y, c: c)], alias=True)
    return q.reshape(2 * r, w)


def _all_gather_all(name, vec, n_slots, flips, slot_fn):
    sends = [(f, lambda x, y, c: 0, slot_fn) for f in flips]
    g = _exchange(name, vec[None], n_slots, [], sends)
    mine = lax.broadcasted_iota(jnp.int32, (n_slots, 1, 1), 0) == slot_fn(*(lax.axis_index(a) for a in MESH_AXES))
    return jnp.where(mine, vec[None], g)


def _nbr_specs(arr, width, cb, nb):
    return [
        (arr, (BLOCK, width), functools.partial(lambda n, cb: (jnp.maximum(n - 1, 0), cb), cb=cb)),
        (arr, (BLOCK, width), functools.partial(lambda n, cb: (n, cb), cb=cb)),
        (arr, (BLOCK, width), functools.partial(lambda n, cb: (jnp.minimum(n + 1, nb - 1), cb), cb=cb)),
    ]


def _head(h):
    return slice(h * HEAD_DIM, (h + 1) * HEAD_DIM)


def _row_group(n_groups, rows_per_group):
    r = lax.broadcasted_iota(jnp.int32, (n_groups * rows_per_group, 1), 0)
    grp = jnp.zeros_like(r)
    for g in range(1, n_groups):
        grp = grp + jnp.where(r >= g * rows_per_group, 1, 0)
    return grp


def _by_group(grp, vals):
    out = vals[-1]
    for g in range(len(vals) - 2, -1, -1):
        out = jnp.where(grp == g, vals[g], out)
    return out


def _attn_fwd(proj, sink):
    lp = proj.shape[0]
    nb = lp // BLOCK
    kv_w = N_KV_HEADS * HEAD_DIM
    specs = _nbr_specs(proj, kv_w, 4, nb) + _nbr_specs(proj, kv_w, 5, nb)
    specs += [(proj, (BLOCK, kv_w), lambda n: (0, 4)), (proj, (BLOCK, kv_w), lambda n: (0, 5))]
    in_specs = [pl.BlockSpec(memory_space=pltpu.SMEM), pl.BlockSpec((BLOCK, D_MODEL), lambda n: (n, 0))]
    in_specs += [pl.BlockSpec(s[1], s[2]) for s in specs]

    def body(sink_ref, q_ref, kp, kc, kn, vp, vc, vn, km, vm, o_ref, lse_ref):
        n = pl.program_id(0)
        qi = lax.broadcasted_iota(jnp.int32, (BLOCK, 3 * BLOCK), 0)
        sj = lax.broadcasted_iota(jnp.int32, (BLOCK, 3 * BLOCK), 1)
        dist = jnp.abs(qi + BLOCK - sj)
        kpos = (n - 1) * BLOCK + sj
        valid = (dist <= WINDOW) & (kpos >= BLOCK) & (kpos < lp)
        distf = dist.astype(F32)
        kb = jnp.concatenate([kp[...], kc[...], kn[...]], axis=0).astype(BF16)
        vb = jnp.concatenate([vp[...], vc[...], vn[...]], axis=0).astype(BF16)
        kmeta = km[PAD:BLOCK, :].astype(BF16)
        vmeta = vm[PAD:BLOCK, :].astype(BF16)
        valid4 = jnp.concatenate([valid] * Q_GROUP, axis=0)
        distf4 = jnp.concatenate([distf] * Q_GROUP, axis=0)
        grp = _row_group(Q_GROUP, BLOCK)
        for kh in range(N_KV_HEADS):
            ksl = slice(kh * HEAD_DIM, (kh + 1) * HEAD_DIM)
            heads = [kh * Q_GROUP + g for g in range(Q_GROUP)]
            slope = _by_group(grp, [SLOPES[h] for h in heads])
            sk = _by_group(grp, [sink_ref[h] for h in heads])
            q4 = (jnp.concatenate([q_ref[:, _head(h)] for h in heads], axis=0) * ATTN_SCALE).astype(BF16)
            s = jnp.where(valid4, _dot_nt(q4, kb[:, ksl]) - slope * distf4, NEG)
            sm = _dot_nt(q4, kmeta[:, ksl])
            m = jnp.maximum(jnp.maximum(jnp.max(s, axis=1, keepdims=True), jnp.max(sm, axis=1, keepdims=True)), sk)
            e = jnp.exp(s - m)
            em = jnp.exp(sm - m)
            den = jnp.sum(e, axis=1, keepdims=True) + jnp.sum(em, axis=1, keepdims=True) + jnp.exp(sk - m)
            o4 = (_dot(e, vb[:, ksl]) + _dot(em, vmeta[:, ksl])) * (1.0 / den)
            lse4 = m + jnp.log(den)
            for g, h in enumerate(heads):
                o_ref[:, _head(h)] = o4[g * BLOCK:(g + 1) * BLOCK].astype(o_ref.dtype)
                lse_ref[:, h:h + 1] = lse4[g * BLOCK:(g + 1) * BLOCK]

    return pl.pallas_call(
        body, grid=(nb,), in_specs=in_specs,
        out_specs=[pl.BlockSpec((BLOCK, D_MODEL), lambda n: (n, 0)), pl.BlockSpec((BLOCK, N_HEADS), lambda n: (n, 0))],
        out_shape=[jax.ShapeDtypeStruct((lp, D_MODEL), BF16), jax.ShapeDtypeStruct((lp, N_HEADS), F32)],
        name="attn_fwd", compiler_params=_compiler_params())(sink, proj, *[s[0] for s in specs])


def _attn_delta(do, o):
    lp = do.shape[0]
    sel = (lax.broadcasted_iota(jnp.int32, (N_HEADS, D_MODEL), 1) // HEAD_DIM
           == lax.broadcasted_iota(jnp.int32, (N_HEADS, D_MODEL), 0)).astype(BF16)

    def body(do_ref, o_ref, sel_ref, d_ref, dt_ref):
        prod = do_ref[...] * o_ref[...].astype(F32)
        hi = prod.astype(BF16)
        lo = (prod - hi.astype(F32)).astype(BF16)
        d_ref[...] = _dot_nt(hi, sel_ref[...]) + _dot_nt(lo, sel_ref[...])
        dt_ref[...] = _dot_nt(sel_ref[...], hi) + _dot_nt(sel_ref[...], lo)

    return pl.pallas_call(
        body, grid=(lp // BLOCK,),
        in_specs=[pl.BlockSpec((BLOCK, D_MODEL), lambda n: (n, 0)), pl.BlockSpec((BLOCK, D_MODEL), lambda n: (n, 0)),
                  pl.BlockSpec((N_HEADS, D_MODEL), lambda n: (0, 0))],
        out_specs=[pl.BlockSpec((BLOCK, N_HEADS), lambda n: (n, 0)), pl.BlockSpec((N_HEADS, BLOCK), lambda n: (0, n))],
        out_shape=[jax.ShapeDtypeStruct((lp, N_HEADS), F32), jax.ShapeDtypeStruct((N_HEADS, lp), F32)],
        name="attn_delta", compiler_params=_compiler_params())(do, o, sel)


def _attn_bwd(proj, sink, o, lse, do):
    lp = proj.shape[0]
    nb = lp // BLOCK
    kv_w = N_KV_HEADS * HEAD_DIM
    delta, delta_t = _attn_delta(do, o)
    lse_t = lse.T
    row_nbrs = lambda arr: [
        (arr, (N_HEADS, BLOCK), lambda n: (0, jnp.maximum(n - 1, 0))), (arr, (N_HEADS, BLOCK), lambda n: (0, n)),
        (arr, (N_HEADS, BLOCK), lambda n: (0, jnp.minimum(n + 1, nb - 1)))]
    specs = (_nbr_specs(proj, D_MODEL, 0, nb) + _nbr_specs(proj, kv_w, 4, nb) + _nbr_specs(proj, kv_w, 5, nb)
             + [(proj, (BLOCK, kv_w), lambda n: (0, 4)), (proj, (BLOCK, kv_w), lambda n: (0, 5))]
             + _nbr_specs(do, D_MODEL, 0, nb) + [(lse, (BLOCK, N_HEADS), lambda n: (n, 0)), (delta, (BLOCK, N_HEADS), lambda n: (n, 0))]
             + row_nbrs(lse_t) + row_nbrs(delta_t))
    in_specs = [pl.BlockSpec(memory_space=pltpu.SMEM)] + [pl.BlockSpec(s[1], s[2]) for s in specs]

    def body(sink_ref, qp, qc, qn, kp, kc, kn, vp, vc, vn, km, vm, dop, doc, don, lc, dc, ltp, ltc, ltn, dtp, dtc, dtn,
             dq_ref, dk_ref, dv_ref, dkm_ref, dvm_ref, dsk_ref):
        n = pl.program_id(0)

        @pl.when(n == 0)
        def _():
            dkm_ref[...] = jnp.zeros_like(dkm_ref)
            dvm_ref[...] = jnp.zeros_like(dvm_ref)
            dsk_ref[...] = jnp.zeros_like(dsk_ref)

        qi = lax.broadcasted_iota(jnp.int32, (BLOCK, 3 * BLOCK), 0)
        sj = lax.broadcasted_iota(jnp.int32, (BLOCK, 3 * BLOCK), 1)
        dist_q = jnp.abs(qi + BLOCK - sj)
        kpos = (n - 1) * BLOCK + sj
        valid_q = (dist_q <= WINDOW) & (kpos >= BLOCK) & (kpos < lp)
        distf_q = dist_q.astype(F32)
        bi = lax.broadcasted_iota(jnp.int32, (BLOCK, 3 * BLOCK), 1)
        kj = lax.broadcasted_iota(jnp.int32, (BLOCK, 3 * BLOCK), 0)
        dist_k = jnp.abs(bi - BLOCK - kj)
        qpos = (n - 1) * BLOCK + bi
        valid_k = (dist_k <= WINDOW) & (qpos >= 0) & (qpos < lp) & (n >= 1)
        distf_k = dist_k.astype(F32)

        kb = jnp.concatenate([kp[...], kc[...], kn[...]], axis=0).astype(BF16)
        vb = jnp.concatenate([vp[...], vc[...], vn[...]], axis=0).astype(BF16)
        kcur = kc[...].astype(BF16)
        vcur = vc[...].astype(BF16)
        kmeta = km[PAD:BLOCK, :].astype(BF16)
        vmeta = vm[PAD:BLOCK, :].astype(BF16)
        lane = lax.broadcasted_iota(jnp.int32, (1, BLOCK), 1)
        dsink = jnp.zeros((1, BLOCK), F32)
        valid_q4 = jnp.concatenate([valid_q] * Q_GROUP, axis=0)
        distf_q4 = jnp.concatenate([distf_q] * Q_GROUP, axis=0)
        valid_k4 = jnp.concatenate([valid_k] * Q_GROUP, axis=1)
        distf_k4 = jnp.concatenate([distf_k] * Q_GROUP, axis=1)
        grp_q = _row_group(Q_GROUP, BLOCK)
        lane_k = lax.broadcasted_iota(jnp.int32, (1, Q_GROUP * 3 * BLOCK), 1)
        grp_k = sum(jnp.where(lane_k >= g * 3 * BLOCK, 1, 0) for g in range(1, Q_GROUP))
        for kh in range(N_KV_HEADS):
            ksl = slice(kh * HEAD_DIM, (kh + 1) * HEAD_DIM)
            heads = [kh * Q_GROUP + g for g in range(Q_GROUP)]
            slopes = [SLOPES[h] for h in heads]
            q4 = (jnp.concatenate([qc[:, _head(h)] for h in heads], axis=0) * ATTN_SCALE).astype(BF16)
            do4 = jnp.concatenate([doc[:, _head(h)] for h in heads], axis=0)
            delta = jnp.concatenate([dc[:, h:h + 1] for h in heads], axis=0)
            lse4 = jnp.concatenate([lc[:, h:h + 1] for h in heads], axis=0)
            s = _dot_nt(q4, kb[:, ksl]) - _by_group(grp_q, slopes) * distf_q4
            p = jnp.exp(jnp.where(valid_q4, s, NEG) - lse4)
            pm = jnp.exp(_dot_nt(q4, kmeta[:, ksl]) - lse4)
            ps = jnp.exp(_by_group(grp_q, [sink_ref[h] for h in heads]) - lse4)
            do4b = do4.astype(BF16)
            ds = p * (_dot_nt(do4b, vb[:, ksl]) - delta)
            dsm = pm * (_dot_nt(do4b, vmeta[:, ksl]) - delta)
            dq4 = ATTN_SCALE * (_dot(ds, kb[:, ksl]) + _dot(dsm, kmeta[:, ksl]))
            dsk4 = ps * delta
            for g, h in enumerate(heads):
                dq_ref[:, _head(h)] = dq4[g * BLOCK:(g + 1) * BLOCK].astype(dq_ref.dtype)
                dsink = dsink + jnp.where(lane == h, -jnp.sum(dsk4[g * BLOCK:(g + 1) * BLOCK]), 0.0)
            dkm_ref[:, ksl] += _dot_tn(dsm, q4)
            dvm_ref[:, ksl] += _dot_tn(pm, do4b)
            band = lambda a, b, c_: jnp.concatenate([r[:, _head(h)] for h in heads for r in (a, b, c_)], axis=0)
            qb4 = (band(qp, qc, qn) * ATTN_SCALE).astype(BF16)
            dob4b = band(dop, doc, don).astype(BF16)
            delta_b = jnp.concatenate([r[h:h + 1, :] for h in heads for r in (dtp, dtc, dtn)], axis=1)
            lse_b = jnp.concatenate([r[h:h + 1, :] for h in heads for r in (ltp, ltc, ltn)], axis=1)
            st = _dot_nt(kcur[:, ksl], qb4) - _by_group(grp_k, slopes) * distf_k4
            pt = jnp.exp(jnp.where(valid_k4, st, NEG) - lse_b)
            dv_ref[:, ksl] = _dot(pt, dob4b)
            dst = pt * (_dot_nt(vcur[:, ksl], dob4b) - delta_b)
            dk_ref[:, ksl] = _dot(dst, qb4)
        dsk_ref[...] += dsink

    blk = lambda w: pl.BlockSpec((BLOCK, w), lambda n: (n, 0))
    fix = lambda shp: pl.BlockSpec(shp, lambda n: (0, 0))
    return pl.pallas_call(
        body, grid=(nb,), in_specs=in_specs,
        out_specs=[blk(D_MODEL), blk(kv_w), blk(kv_w), fix((N_META, kv_w)), fix((N_META, kv_w)), fix((1, BLOCK))],
        out_shape=[jax.ShapeDtypeStruct((lp, D_MODEL), BF16), jax.ShapeDtypeStruct((lp, kv_w), F32),
                   jax.ShapeDtypeStruct((lp, kv_w), F32), jax.ShapeDtypeStruct((N_META, kv_w), F32),
                   jax.ShapeDtypeStruct((N_META, kv_w), F32), jax.ShapeDtypeStruct((1, BLOCK), F32)],
        name="attn_bwd", compiler_params=_compiler_params())(sink, *[s[0] for s in specs])


N_SEG = 8
SSM_TILE = 384


def _to_segments(a):
    lp, w = a.shape
    return a.reshape(N_SEG, lp // N_SEG, w).transpose(1, 0, 2).reshape(lp, w)


def _from_segments(a):
    lp, w = a.shape
    return a.reshape(lp // N_SEG, N_SEG, w).transpose(1, 0, 2).reshape(lp, w)


def _complex_power(ar, ai, n):
    rr, ri = jnp.ones_like(ar), jnp.zeros_like(ai)
    while n:
        if n & 1:
            rr, ri = rr * ar - ri * ai, rr * ai + ri * ar
        ar, ai = ar * ar - ai * ai, 2.0 * ar * ai
        n >>= 1
    return rr, ri


def _segment_starts(finals, a_seg, reverse):
    fr, fi = finals[:, :N_STATE], finals[:, N_STATE:]
    ar, ai = a_seg[:, :N_STATE], a_seg[:, N_STATE:]
    row = lax.broadcasted_iota(jnp.int32, (N_SEG, N_STATE), 0)
    pr = jnp.zeros((1, N_STATE), F32)
    pi = jnp.zeros((1, N_STATE), F32)
    sr = jnp.zeros((N_SEG, N_STATE), F32)
    si = jnp.zeros((N_SEG, N_STATE), F32)
    for s in (range(N_SEG - 1, -1, -1) if reverse else range(N_SEG)):
        sr = jnp.where(row == s, pr, sr)
        si = jnp.where(row == s, pi, si)
        pr, pi = fr[s:s + 1] + ar * pr - ai * pi, fi[s:s + 1] + ar * pi + ai * pr
    return jnp.concatenate([sr, si], axis=1)


def _recurrence(buf_ref, st_ref, a_ref, reverse):
    steps = SSM_TILE // N_SEG
    half = N_STATE // 2
    for c0 in (0, half):
        re = slice(c0, c0 + half)
        im = slice(N_STATE + c0, N_STATE + c0 + half)
        ar = jnp.broadcast_to(a_ref[:, re], (N_SEG, half))
        ai = jnp.broadcast_to(a_ref[:, im], (N_SEG, half))

        def step(k, carry, re=re, im=im, ar=ar, ai=ai):
            xr, xi = carry
            r0 = pl.multiple_of((steps - 1 - k if reverse else k) * N_SEG, N_SEG)
            nr = ar * xr - ai * xi + buf_ref[pl.ds(r0, N_SEG), re]
            ni = ar * xi + ai * xr + buf_ref[pl.ds(r0, N_SEG), im]
            buf_ref[pl.ds(r0, N_SEG), re] = nr
            buf_ref[pl.ds(r0, N_SEG), im] = ni
            return nr, ni

        xr, xi = lax.fori_loop(0, steps, step, (st_ref[:, re], st_ref[:, im]), unroll=2)
        st_ref[:, re] = xr
        st_ref[:, im] = xi


def _copy_in(pairs, sem):
    cps = [pltpu.make_async_copy(src, dst, sem.at[j]) for j, (src, dst) in enumerate(pairs)]
    for cp in cps:
        cp.start()
    for cp in cps:
        cp.wait()


def _ssm_fwd_dir(name, u_seg, wb, wc, a, a_seg, reverse):
    lp = u_seg.shape[0]
    nt = lp // SSM_TILE
    tile = (lambda i: nt - 1 - i) if reverse else (lambda i: i)
    first = tile(0)
    held = lambda p, i: (p * tile(i) + (1 - p) * first, 0)

    def body(u_ref, wb_hbm, wc_hbm, a_ref, aseg_ref, x_ref, y_ref, wb_ref, wc_ref, buf_ref, st_ref, sem):
        p, i = pl.program_id(0), pl.program_id(1)

        @pl.when((p == 0) & (i == 0))
        def _():
            _copy_in([(wb_hbm, wb_ref), (wc_hbm, wc_ref)], sem)
            st_ref[...] = jnp.zeros_like(st_ref)

        @pl.when((p == 1) & (i == 0))
        def _():
            st_ref[...] = _segment_starts(st_ref[...], aseg_ref[...], reverse)

        def states_into(dst_ref):
            for j in range(SUPER):
                part = _dot(u_ref[:, 128 * j:128 * (j + 1)], wb_ref[j])
                dst_ref[:, 512 * j:512 * (j + 1)] = part[:, :512]
                dst_ref[:, N_STATE + 512 * j:N_STATE + 512 * (j + 1)] = part[:, 512:]
            _recurrence(dst_ref, st_ref, a_ref, reverse)

        @pl.when(p == 0)
        def _():
            states_into(buf_ref)

        @pl.when(p == 1)
        def _():
            states_into(x_ref)
            y_ref[...] = jnp.concatenate([_dot(_state_cols(x_ref, j), wc_ref[j]) for j in range(SUPER)], axis=1)

    fix = lambda shp: pl.BlockSpec(shp, lambda p, i: (0, 0))
    return pl.pallas_call(
        body, grid=(2, nt),
        in_specs=[pl.BlockSpec((SSM_TILE, SSM_WIDTH), lambda p, i: (tile(i), 0)), pl.BlockSpec(memory_space=pl.ANY),
                  pl.BlockSpec(memory_space=pl.ANY), fix((1, 2 * N_STATE)), fix((1, 2 * N_STATE))],
        out_specs=[pl.BlockSpec((SSM_TILE, 2 * N_STATE), held), pl.BlockSpec((SSM_TILE, SSM_WIDTH), held)],
        out_shape=[jax.ShapeDtypeStruct((lp, 2 * N_STATE), F32), jax.ShapeDtypeStruct((lp, SSM_WIDTH), F32)],
        scratch_shapes=[pltpu.VMEM(wb.shape, BF16), pltpu.VMEM(wc.shape, BF16), pltpu.VMEM((SSM_TILE, 2 * N_STATE), F32),
                        pltpu.VMEM((N_SEG, 2 * N_STATE), F32), pltpu.SemaphoreType.DMA((2,))],
        name=name, compiler_params=pltpu.CompilerParams(dimension_semantics=("arbitrary", "arbitrary"),
                                                        vmem_limit_bytes=V7X_VMEM_LIMIT_BYTES))(
        u_seg, wb.astype(BF16), wc.astype(BF16), a, a_seg)


def _ssm_bwd_dir(name, dys_seg, u_seg, x_seg, wb, wc, a_conj, a_seg_conj, fwd_reverse):
    lp = u_seg.shape[0]
    nt = lp // SSM_TILE
    steps = SSM_TILE // N_SEG
    reverse = not fwd_reverse
    tile = (lambda i: nt - 1 - i) if reverse else (lambda i: i)
    first = tile(0)
    held = lambda p, i: (p * tile(i) + (1 - p) * first, 0)
    n_slab = lp // N_SEG
    if fwd_reverse:
        halo = lambda p, i: (p * jnp.minimum((tile(i) + 1) * steps, n_slab - 1), 0)
        edge = lambda p, i: (0, 0)
    else:
        halo = lambda p, i: (p * jnp.maximum(tile(i) * steps - 1, 0), 0)
        edge = lambda p, i: (n_slab - 1, 0)

    def body(dy_ref, u_ref, x_ref, halo_ref, edge_ref, wb_hbm, wc_hbm, a_ref, aseg_ref, du_ref, dwb_ref, dwc_ref, ga_ref,
             wb_ref, wc_ref, buf_ref, st_ref, sem):
        p, i = pl.program_id(0), pl.program_id(1)

        @pl.when((p == 0) & (i == 0))
        def _():
            _copy_in([(wb_hbm, wb_ref), (wc_hbm, wc_ref)], sem)
            st_ref[...] = jnp.zeros_like(st_ref)
            dwb_ref[...] = jnp.zeros_like(dwb_ref)
            dwc_ref[...] = jnp.zeros_like(dwc_ref)
            ga_ref[...] = jnp.zeros_like(ga_ref)

        @pl.when((p == 1) & (i == 0))
        def _():
            st_ref[...] = _segment_starts(st_ref[...], aseg_ref[...], reverse)

        for j in range(SUPER):
            part = _dot_nt(dy_ref[:, 128 * j:128 * (j + 1)], wc_ref[j])
            buf_ref[:, 512 * j:512 * (j + 1)] = part[:, :512]
            buf_ref[:, N_STATE + 512 * j:N_STATE + 512 * (j + 1)] = part[:, 512:]
        _recurrence(buf_ref, st_ref, a_ref, reverse)

        @pl.when(p == 1)
        def _():
            du_ref[...] = jnp.concatenate([_dot_nt(_state_cols(buf_ref, j), wb_ref[j]) for j in range(SUPER)], axis=1)
            dwb_ref[...] += jnp.concatenate(
                [_dot_tn(u_ref[:, 128 * j:128 * (j + 1)], _state_cols(buf_ref, j)) for j in range(SUPER)], axis=0)
            dwc_ref[...] += jnp.concatenate(
                [_dot_tn(dy_ref[:, 128 * j:128 * (j + 1)], _state_cols(x_ref, j)) for j in range(SUPER)], axis=0)
            row = lax.broadcasted_iota(jnp.int32, (N_SEG, 2 * N_STATE), 0)
            if fwd_reverse:
                wrap = jnp.where(row == N_SEG - 1, 0.0, pltpu.roll(edge_ref[...], N_SEG - 1, axis=0))
                open_slab = jnp.where(tile(i) == nt - 1, wrap, halo_ref[...])
                before = lambda cols: jnp.concatenate([x_ref[N_SEG:, cols], open_slab[:, cols]], axis=0)
            else:
                wrap = jnp.where(row == 0, 0.0, pltpu.roll(edge_ref[...], 1, axis=0))
                open_slab = jnp.where(tile(i) == 0, wrap, halo_ref[...])
                before = lambda cols: jnp.concatenate([open_slab[:, cols], x_ref[:SSM_TILE - N_SEG, cols]], axis=0)
            half = N_STATE // 2
            for c0 in (0, half):
                re = slice(c0, c0 + half)
                im = slice(N_STATE + c0, N_STATE + c0 + half)
                gr, gi = buf_ref[:, re], buf_ref[:, im]
                br, bi = before(re), before(im)
                fold = lambda v: jnp.sum(v.reshape(steps, N_SEG, half), axis=0)
                ga_ref[:, re] += fold(gr * br + gi * bi)
                ga_ref[:, im] += fold(gi * br - gr * bi)

    fix = lambda shp: pl.BlockSpec(shp, lambda p, i: (0, 0))
    row_tile = lambda w: pl.BlockSpec((SSM_TILE, w), lambda p, i: (tile(i), 0))
    return pl.pallas_call(
        body, grid=(2, nt),
        in_specs=[row_tile(SSM_WIDTH), row_tile(SSM_WIDTH), pl.BlockSpec((SSM_TILE, 2 * N_STATE), held),
                  pl.BlockSpec((N_SEG, 2 * N_STATE), halo), pl.BlockSpec((N_SEG, 2 * N_STATE), edge),
                  pl.BlockSpec(memory_space=pl.ANY), pl.BlockSpec(memory_space=pl.ANY), fix((1, 2 * N_STATE)), fix((1, 2 * N_STATE))],
        out_specs=[pl.BlockSpec((SSM_TILE, SSM_WIDTH), held), fix((SSM_WIDTH, 1024)), fix((SSM_WIDTH, 1024)),
                   fix((N_SEG, 2 * N_STATE))],
        out_shape=[jax.ShapeDtypeStruct((lp, SSM_WIDTH), F32), jax.ShapeDtypeStruct((SSM_WIDTH, 1024), F32),
                   jax.ShapeDtypeStruct((SSM_WIDTH, 1024), F32), jax.ShapeDtypeStruct((N_SEG, 2 * N_STATE), F32)],
        scratch_shapes=[pltpu.VMEM(wb.shape, BF16), pltpu.VMEM(wc.shape, BF16), pltpu.VMEM((SSM_TILE, 2 * N_STATE), F32),
                        pltpu.VMEM((N_SEG, 2 * N_STATE), F32), pltpu.SemaphoreType.DMA((2,))],
        name=name, compiler_params=pltpu.CompilerParams(dimension_semantics=("arbitrary", "arbitrary"),
                                                        vmem_limit_bytes=V7X_VMEM_LIMIT_BYTES))(
        dys_seg, u_seg, x_seg, x_seg, x_seg, wb.astype(BF16), wc.astype(BF16), a_conj, a_seg_conj)


def _ssm_prep(lam_re, lam_im, log_dt, b_re, b_im, c_re, c_im):
    dt = jnp.exp(log_dt)[:, None]
    er = jnp.exp(lam_re * dt)
    ar, ai = er * jnp.cos(lam_im * dt), er * jnp.sin(lam_im * dt)
    nr, ni = ar - 1.0, ai
    den = lam_re * lam_re + lam_im * lam_im
    cr, ci = (nr * lam_re + ni * lam_im) / den, (ni * lam_re - nr * lam_im) / den
    bbr = cr[:, :, None] * b_re - ci[:, :, None] * b_im
    bbi = cr[:, :, None] * b_im + ci[:, :, None] * b_re
    eye = jnp.eye(8, dtype=F32)

    def in_map(b):
        b = b.reshape(SUPER, 8, SSM_STATE, SSM_GROUP_CH).transpose(0, 1, 3, 2)
        return (b[:, :, :, None, :] * eye[None, :, None, :, None]).reshape(SUPER, 128, 512)

    def out_map(cm):
        cm = cm.reshape(SUPER, 8, SSM_GROUP_CH, SSM_STATE).transpose(0, 1, 3, 2)
        return (cm[:, :, :, None, :] * eye[None, :, None, :, None]).reshape(SUPER, 512, 128)

    wb = jnp.concatenate([in_map(bbr), in_map(bbi)], axis=2)
    wc = jnp.concatenate([out_map(c_re), -out_map(c_im)], axis=1)
    return ar.reshape(1, N_STATE), ai.reshape(1, N_STATE), wb, wc


def _state_cols(ref, j):
    return jnp.concatenate([ref[:, 512 * j:512 * (j + 1)], ref[:, N_STATE + 512 * j:N_STATE + 512 * (j + 1)]], axis=1)


def _row_ids(i, tm, width):
    return i * tm + lax.broadcasted_iota(jnp.int32, (tm, width), 0)


def _ffn_fwd(tag, h, gain, wg, wu, wd, next_shard=None):
    n = _rowk(tag + "_norm", lambda i, x, g: _rms_fwd(x[...], g[...]), [_rows(h)], [gain], [(D_MODEL, BF16)])[0]

    def up(i, n_ref, wg_ref, wu_ref):
        a = _dot(n_ref[...], wg_ref[...])
        b = _dot(n_ref[...], wu_ref[...])
        return a, b, a * _sigmoid(a) * b

    gathered = None
    if next_shard is None:
        a, b, act = _rowk(tag + "_up", up, [_rows(n)], [wg, wu], [(D_FF, BF16)] * 3)
    else:
        a, b, act, got = _rowk(tag + "_up_gather", up, [_rows(n)], [wg, wu], [(D_FF, BF16)] * 3,
                               hosted=_gather_ici_stage(next_shard))
        gathered = _gather_finish("gather_w", next_shard, got)
    out = _rowk(tag + "_down", lambda i, act_ref, h_ref, wd_ref: h_ref[...] + 0.5 * _dot(act_ref[...], wd_ref[...]),
                [_rows(act), _rows(h)], [wd], [(D_MODEL, F32)])[0]
    return out, (h, n, a, b, act), gathered


def _ffn_bwd(tag, dh, saved, gain, wg, wu, wd, pending=None, sel=None):
    h, n, a, b, act = saved
    hosted1 = hosted2 = reduced = None
    if pending is not None:
        f, hosted1 = _rs_pair_stage(pending)
        tag = tag + "_reduce"

    def bwd1(i, dh_ref, a_ref, b_ref, wd_ref):
        dact = 0.5 * _dot_nt(dh_ref[...], wd_ref[...])
        av = a_ref[...].astype(F32)
        sg = _sigmoid(av)
        return dact * b_ref[...].astype(F32) * (sg * (1.0 + av * (1.0 - sg))), dact * av * sg

    res = _rowk(tag + "_bwd_act", bwd1, [_rows(dh), _rows(a), _rows(b)], [wd], [(D_FF, BF16)] * 2, tm=192, hosted=hosted1)
    da, db = res[0], res[1]
    if pending is not None:
        p, hosted2 = _rs_chip_stage("reduce_w", f, res[2], sel)

    def bwd2(i, da_ref, db_ref, h_ref, dh_ref, wg_ref, wu_ref, g_ref):
        dn = _dot_nt(da_ref[...], wg_ref[...]) + _dot_nt(db_ref[...], wu_ref[...])
        dx, dg = _rms_bwd(h_ref[...], g_ref[...], dn)
        return dh_ref[...] + dx, dg

    res = _rowk(tag + "_bwd_in", bwd2, [_rows(da), _rows(db), _rows(h), _rows(dh)], [wg, wu, gain],
                [(D_MODEL, F32)], accs=[(1, D_MODEL)], hosted=hosted2)
    dh_in, dgain = res[0], res[1]
    if pending is not None:
        reduced = _rs_finish("reduce_w", p, res[2], sel)
    dwd = _mm_tn("ffn_dwd", act, dh, tk=D_FF // 2, tn=D_MODEL, scale=0.5)
    dwg = _mm_tn("ffn_dwg", n, da, tk=D_MODEL, tn=D_FF // 2)
    dwu = _mm_tn("ffn_dwu", n, db, tk=D_MODEL, tn=D_FF // 2)
    return dh_in, dgain, dwg, dwu, dwd, reduced


def _mixer_fwd(h, lw, ssm):
    lp = h.shape[0]
    n = _rowk("mix_norm", lambda i, x, g: _rms_fwd(x[...], g[...]), [_rows(h)], [lw["mix_norm"]], [(D_MODEL, BF16)])[0]
    proj = _rowk("mix_in", lambda i, n_ref, w_ref: _dot(n_ref[...], w_ref[...]), [_rows(n)], [lw["w_in"]],
                 [(4 * D_MODEL, F32)])[0]
    yattn, lse = _attn_fwd(proj, lw["attn_sink"])
    u_seg = _to_segments(proj[:, 3 * SSM_WIDTH:4 * SSM_WIDTH])
    xs, ydir = [], []
    for d in range(2):
        x_seg, y_seg = _ssm_fwd_dir(f"ssm_fwd{d}", u_seg, ssm[d]["wb"], ssm[d]["wc"], ssm[d]["a"], ssm[d]["a_seg"],
                                    reverse=(d == 1))
        xs.append(x_seg)
        ydir.append(y_seg)

    def ssm_out(i, y0_ref, y1_ref, u_ref, d_ref, wglu_ref):
        ys = y0_ref[...] + y1_ref[...] + d_ref[...] * u_ref[...]
        z = _gelu(ys)
        return ys, z * _sigmoid(_dot(z, wglu_ref[...]))

    ys, yssm_seg = _rowk("ssm_out", ssm_out, [_rows(ydir[0]), _rows(ydir[1]), _rows(u_seg)],
                         [lw["ssm_d"], lw["ssm_w_glu"]], [(SSM_WIDTH, F32), (SSM_WIDTH, BF16)])
    yssm = _from_segments(yssm_seg)

    def merge(i, ys_ref, ya_ref, gs_ref, ga_ref, wbs_ref, wba_ref):
        bs = _dot(ys_ref[...], wbs_ref[...])
        ba = _dot(ya_ref[...], wba_ref[...])
        m = _sigmoid(gs_ref[...]) * bs + _sigmoid(ga_ref[...]) * ba
        return bs, ba, jnp.where(_row_ids(i, ROW_TILE, D_MODEL) >= PAD, m, 0.0)

    bs, ba, merged = _rowk("mix_merge", merge, [_rows(yssm), _rows(yattn), _rows(proj, D_MODEL, 2), _rows(proj, D_MODEL, 3)],
                           [lw["w_branch_ssm"], lw["w_branch_attn"]], [(D_MODEL, BF16)] * 3)
    out = _rowk("mix_out", lambda i, m_ref, h_ref, w_ref: h_ref[...] + _dot(m_ref[...], w_ref[...]),
                [_rows(merged), _rows(h)], [lw["w_out"]], [(D_MODEL, F32)])[0]
    return out, (h, n, proj, yattn, lse, u_seg, xs, ys, yssm, bs, ba, merged)


def _mixer_bwd(dh, saved, lw, ssm):
    h, n, proj, yattn, lse, u_seg, xs, ys, yssm, bs, ba, merged = saved

    def bwd1(i, dh_ref, gs_ref, ga_ref, bs_ref, ba_ref, w_ref):
        dm = _dot_nt(dh_ref[...], w_ref[...])
        dm = jnp.where(_row_ids(i, ROW_TILE, D_MODEL) >= PAD, dm, 0.0)
        sgs = _sigmoid(gs_ref[...])
        sga = _sigmoid(ga_ref[...])
        return (dm * sgs, dm * sga, dm * bs_ref[...].astype(F32) * sgs * (1.0 - sgs),
                dm * ba_ref[...].astype(F32) * sga * (1.0 - sga))

    dbs, dba, dgs, dga = _rowk("mix_bwd_merge", bwd1,
                               [_rows(dh), _rows(proj, D_MODEL, 2), _rows(proj, D_MODEL, 3), _rows(bs), _rows(ba)],
                               [lw["w_out"]], [(D_MODEL, BF16)] * 4)
    dw_out = _mm_tn("mix_dw_out", merged, dh, tk=D_MODEL, tn=D_MODEL)
    dw_bs = _mm_tn("mix_dw_bs", yssm, dbs, tk=SSM_WIDTH, tn=D_MODEL)
    dw_ba = _mm_tn("mix_dw_ba", yattn, dba, tk=D_MODEL, tn=D_MODEL)

    def bwd2(i, dbs_ref, dba_ref, wbs_ref, wba_ref):
        return _dot_nt(dba_ref[...], wba_ref[...]), _dot_nt(dbs_ref[...], wbs_ref[...])

    dyattn, dyssm = _rowk("mix_bwd_branches", bwd2, [_rows(dbs), _rows(dba)], [lw["w_branch_ssm"], lw["w_branch_attn"]],
                          [(D_MODEL, F32), (SSM_WIDTH, F32)])

    def bwd3(i, dyssm_ref, ys_ref, u_ref, wglu_ref):
        ysv = ys_ref[...]
        z = _gelu(ysv)
        sg = _sigmoid(_dot(z, wglu_ref[...]))
        dt = dyssm_ref[...] * z * sg * (1.0 - sg)
        dz = dyssm_ref[...] * sg + _dot_nt(dt, wglu_ref[...])
        dys = dz * _gelu_grad(ysv)
        return dys, z, dt, jnp.sum(dys * u_ref[...], axis=0, keepdims=True)

    dys, z, dt, dd = _rowk("mix_bwd_ssm_out", bwd3, [_rows(_to_segments(dyssm)), _rows(ys), _rows(u_seg)], [lw["ssm_w_glu"]],
                           [(SSM_WIDTH, F32), (SSM_WIDTH, BF16), (SSM_WIDTH, BF16)], accs=[(1, SSM_WIDTH)])
    dw_glu = _mm_tn("mix_dw_glu", z, dt, tk=SSM_WIDTH, tn=SSM_WIDTH)

    dus, ssm_cot = [], []
    for d in range(2):
        du_d, dwb, dwc_t, ga = _ssm_bwd_dir(f"ssm_bwd{d}", dys, u_seg, xs[d], ssm[d]["wb"], ssm[d]["wc"], ssm[d]["a_conj"],
                                            ssm[d]["a_seg_conj"], fwd_reverse=(d == 1))
        dus.append(du_d)
        ga = jnp.sum(ga, axis=0, keepdims=True)
        ssm_cot.append((ga[:, :N_STATE], ga[:, N_STATE:], dwb.reshape(SUPER, 128, 1024),
                        dwc_t.reshape(SUPER, 128, 1024).transpose(0, 2, 1)))

    du_seg = _rowk("ssm_bwd_du", lambda i, a_ref, b_ref, dys_ref, d_ref: a_ref[...] + b_ref[...] + d_ref[...] * dys_ref[...],
                   [_rows(dus[0]), _rows(dus[1]), _rows(dys)], [lw["ssm_d"]], [(SSM_WIDTH, BF16)])[0]
    du = _from_segments(du_seg)

    dq, dk, dv, dkm, dvm, dsink = _attn_bwd(proj, lw["attn_sink"], yattn, lse, dyattn)

    def dproj_fn(i, dq_ref, dk_ref, dv_ref, du_ref, dgs_ref, dga_ref, dkm_ref, dvm_ref):
        first = jnp.where(i == 0, 1.0, 0.0)
        zeros = lambda r: jnp.zeros((r, N_KV_HEADS * HEAD_DIM), F32)
        place = lambda m: jnp.concatenate([zeros(PAD), m[...] * first, zeros(ROW_TILE - BLOCK)], axis=0)
        dp = jnp.concatenate([dq_ref[...].astype(F32), dk_ref[...] + place(dkm_ref), dv_ref[...] + place(dvm_ref),
                              du_ref[...].astype(F32), dgs_ref[...].astype(F32), dga_ref[...].astype(F32)], axis=1)
        return jnp.where(_row_ids(i, ROW_TILE, 4 * D_MODEL) >= PAD, dp, 0.0)

    dproj = _rowk("mix_bwd_dproj", dproj_fn, [_rows(dq), _rows(dk), _rows(dv), _rows(du), _rows(dgs), _rows(dga)],
                  [dkm, dvm], [(4 * D_MODEL, BF16)])[0]

    def bwd_in(i, dp_ref, h_ref, dh_ref, w_ref, g_ref):
        dx, dg = _rms_bwd(h_ref[...], g_ref[...], _dot_nt(dp_ref[...], w_ref[...]))
        return dh_ref[...] + dx, dg

    dh_in, dgain = _rowk("mix_bwd_in", bwd_in, [_rows(dproj), _rows(h), _rows(dh)], [lw["w_in"], lw["mix_norm"]],
                         [(D_MODEL, F32)], accs=[(1, D_MODEL)])
    dw_in = _mm_tn("mix_dw_in", n, dproj, tk=D_MODEL, tn=2 * D_MODEL)
    grads = {"w_out": dw_out, "w_branch_ssm": dw_bs, "w_branch_attn": dw_ba, "ssm_w_glu": dw_glu, "w_in": dw_in,
             "mix_norm": dgain, "ssm_d": dd, "attn_sink": dsink[0, :N_HEADS]}
    return dh_in, grads, ssm_cot


def _loss_head(h, gain, target):
    lp = h.shape[0]

    def fn(i, h_ref, t_ref, g_ref):
        x = h_ref[...]
        y = _rms_fwd(x, g_ref[...])
        live = jnp.where(i == 0, 0.0, 1.0)
        dy = (y - t_ref[...]) * live
        loss = 0.5 * jnp.sum(dy * dy) / D_MODEL
        dx, dg = _rms_bwd(x, g_ref[...], dy * (1.0 / D_MODEL))
        return dx, jnp.full((1, BLOCK), loss, F32), dg

    tgt = (target, (BLOCK, D_MODEL), lambda i: (jnp.maximum(i - 1, 0), 0))
    return _rowk("loss_head", fn, [_rows(h), tgt], [gain], [(D_MODEL, F32)], accs=[(1, BLOCK), (1, D_MODEL)], tm=BLOCK)


def _adamw(name, w, g, m, v, tm):
    def fn(i, w_ref, g_ref, m_ref, v_ref):
        gv = g_ref[...]
        mn = ADAM_B1 * m_ref[...] + (1.0 - ADAM_B1) * gv
        vn = ADAM_B2 * v_ref[...] + (1.0 - ADAM_B2) * (gv * gv)
        m_hat = mn / (1.0 - ADAM_B1 ** ADAM_STEP)
        v_hat = vn / (1.0 - ADAM_B2 ** ADAM_STEP)
        return -ADAM_LR * (m_hat / (jnp.sqrt(v_hat) + ADAM_EPS) + ADAM_WD * w_ref[...]), mn, vn

    wd = w.shape[1]
    return _rowk(name, fn, [_rows(w), _rows(g), _rows(m), _rows(v)], [], [(wd, F32)] * 3, tm=tm)


def _shard_rows(name):
    return {"ffn1_w_gate": 704, "ffn1_w_up": 704, "ffn1_w_down": 704, "ffn2_w_gate": 704, "ffn2_w_up": 704, "ffn2_w_down": 704,
            "w_in": 1024, "ssm_w_glu": 64, "w_branch_ssm": 128, "w_branch_attn": 256, "w_out": 256}[name]


def _full_shape(name):
    return {"ffn1_w_gate": (D_MODEL, D_FF), "ffn1_w_up": (D_MODEL, D_FF), "ffn1_w_down": (D_FF, D_MODEL),
            "ffn2_w_gate": (D_MODEL, D_FF), "ffn2_w_up": (D_MODEL, D_FF), "ffn2_w_down": (D_FF, D_MODEL),
            "w_in": (D_MODEL, 4 * D_MODEL), "ssm_w_glu": (SSM_WIDTH, SSM_WIDTH), "w_branch_ssm": (SSM_WIDTH, D_MODEL),
            "w_branch_attn": (D_MODEL, D_MODEL), "w_out": (D_MODEL, D_MODEL)}[name]


def _unflatten_gathered(gathered):
    out, r0 = {}, 0
    for name in BIG:
        r = _shard_rows(name)
        k, nn = _full_shape(name)
        piece = gathered[:, r0:r0 + r, :]
        if name in COL_SHARDED:
            out[name] = piece.reshape(4, k, nn // 4).transpose(1, 0, 2).reshape(k, nn)
        else:
            out[name] = piece.reshape(k, nn)
        r0 += r
    return out


def _flatten_full(grads):
    per_shard = []
    for s in range(4):
        pieces = []
        for name in BIG:
            k, nn = _full_shape(name)
            g = grads[name]
            piece = g[:, s * (nn // 4):(s + 1) * (nn // 4)] if name in COL_SHARDED else g[s * (k // 4):(s + 1) * (k // 4), :]
            pieces.append(piece.reshape(-1, 1024))
        per_shard.append(jnp.concatenate(pieces, axis=0))
    f = jnp.stack(per_shard)
    return f.reshape(4, 2, f.shape[1] // 2, 1024).transpose(1, 0, 2, 3)


def _shard_2d(a):
    return a.reshape(-1, a.shape[-1])


def kernel(x, meta_tokens, ffn1_norm, ffn1_w_gate, ffn1_w_up, ffn1_w_down, mix_norm, w_in, ssm_lam_re, ssm_lam_im, ssm_log_dt, ssm_b_re, ssm_b_im, ssm_c_re, ssm_c_im, ssm_d, ssm_w_glu, attn_sink, w_branch_ssm, w_branch_attn, w_out, ffn2_norm, ffn2_w_gate, ffn2_w_up, ffn2_w_down, final_norm, loss_target, m_meta_tokens, m_ffn1_norm, m_ffn1_w_gate, m_ffn1_w_up, m_ffn1_w_down, m_mix_norm, m_w_in, m_ssm_lam_re, m_ssm_lam_im, m_ssm_log_dt, m_ssm_b_re, m_ssm_b_im, m_ssm_c_re, m_ssm_c_im, m_ssm_d, m_ssm_w_glu, m_attn_sink, m_w_branch_ssm, m_w_branch_attn, m_w_out, m_ffn2_norm, m_ffn2_w_gate, m_ffn2_w_up, m_ffn2_w_down, m_final_norm, v_meta_tokens, v_ffn1_norm, v_ffn1_w_gate, v_ffn1_w_up, v_ffn1_w_down, v_mix_norm, v_w_in, v_ssm_lam_re, v_ssm_lam_im, v_ssm_log_dt, v_ssm_b_re, v_ssm_b_im, v_ssm_c_re, v_ssm_c_im, v_ssm_d, v_ssm_w_glu, v_attn_sink, v_w_branch_ssm, v_w_branch_attn, v_w_out, v_ffn2_norm, v_ffn2_w_gate, v_ffn2_w_up, v_ffn2_w_down, v_final_norm):
    args = dict(locals())
    w = {k: args[k] for k in WEIGHTS}
    mom = {k: args["m_" + k] for k in WEIGHTS}
    var = {k: args["v_" + k] for k in WEIGHTS}
    depth = ffn1_norm.shape[0]
    seq = x.shape[1]
    xi, yi, ci = lax.axis_index("x"), lax.axis_index("y"), lax.axis_index("c")
    chip = 2 * xi + yi
    sel = jnp.stack([ci, chip]).astype(jnp.int32)

    same_core = [(fx, fy, 0) for fx, fy in _OTHER_CHIPS]
    meta_all = _all_gather_all("gather_meta", meta_tokens, 4, same_core, lambda x_, y_, c_: _chip(x_, y_))
    meta_full = meta_all.transpose(1, 0, 2).reshape(N_META, D_MODEL)
    flat_w = [jnp.concatenate([w[name][l].reshape(-1, 1024) for name in BIG], axis=0).astype(BF16) for l in range(depth)]

    def layer_weights(l, gathered):
        lw = _unflatten_gathered(gathered)
        for name in ("ffn1_norm", "mix_norm", "ffn2_norm"):
            lw[name] = w[name][l].reshape(1, D_MODEL)
        lw["ssm_d"] = ssm_d[l].reshape(1, SSM_WIDTH)
        lw["attn_sink"] = attn_sink[l]
        return lw

    layer_w = [layer_weights(0, _all_gather_shards("gather_w", flat_w[0]))]

    ssm_params = ("ssm_lam_re", "ssm_lam_im", "ssm_log_dt", "ssm_b_re", "ssm_b_im", "ssm_c_re", "ssm_c_im")
    ssm, ssm_vjp = [], []
    for l in range(depth):
        dirs, vjps = [], []
        for d in range(2):
            prm = tuple(w[k][l, d] for k in ssm_params)
            (ar, ai, wb, wc), pull = jax.vjp(_ssm_prep, *prm)
            a_seg = _complex_power(ar, ai, (seq + BLOCK) // N_SEG)
            conj = lambda v: jnp.concatenate([v[0], -v[1]], axis=1)
            pack = lambda v: jnp.concatenate([v[0], v[1]], axis=1)
            dirs.append({"wb": wb, "wc": wc, "a": pack((ar, ai)), "a_seg": pack(a_seg),
                         "a_conj": conj((ar, ai)), "a_seg_conj": conj(a_seg)})
            vjps.append(pull)
        ssm.append(dirs)
        ssm_vjp.append(vjps)

    h = jnp.concatenate([jnp.zeros((PAD, D_MODEL), F32), meta_full, x[0]], axis=0)
    saved = []
    for l in range(depth):
        lw = layer_w[l]
        h, s1, gathered = _ffn_fwd("ffn", h, lw["ffn1_norm"], lw["ffn1_w_gate"], lw["ffn1_w_up"], lw["ffn1_w_down"],
                                   next_shard=flat_w[l + 1] if l + 1 < depth else None)
        if gathered is not None:
            layer_w.append(layer_weights(l + 1, gathered))
        h, s2 = _mixer_fwd(h, lw, ssm[l])
        h, s3, _ = _ffn_fwd("ffn", h, lw["ffn2_norm"], lw["ffn2_w_gate"], lw["ffn2_w_up"], lw["ffn2_w_down"])
        saved.append((s1, s2, s3))
    dh, loss_part, d_final = _loss_head(h, final_norm.reshape(1, D_MODEL), loss_target[0])
    loss = lax.psum(loss_part[0, 0], MESH_AXES)

    small_g = {k: [None] * depth for k in SMALL if k not in ("meta_tokens", "final_norm")}
    big_g = {k: [None] * depth for k in BIG}
    def keep_shard(l, reduced):
        r0 = 0
        for name in BIG:
            r = _shard_rows(name)
            big_g[name][l] = reduced[r0:r0 + r].reshape(w[name].shape[1:])
            r0 += r

    pending = None
    for l in reversed(range(depth)):
        lw = layer_w[l]
        s1, s2, s3 = saved[l]
        full = {}
        dh, dg, full["ffn2_w_gate"], full["ffn2_w_up"], full["ffn2_w_down"], reduced = _ffn_bwd(
            "ffn", dh, s3, lw["ffn2_norm"], lw["ffn2_w_gate"], lw["ffn2_w_up"], lw["ffn2_w_down"], pending, sel)
        if pending is not None:
            keep_shard(l + 1, reduced)
        small_g["ffn2_norm"][l] = dg[0]
        dh, mg, ssm_cot = _mixer_bwd(dh, s2, lw, ssm[l])
        for k in ("w_out", "w_branch_ssm", "w_branch_attn", "ssm_w_glu", "w_in"):
            full[k] = mg[k]
        small_g["mix_norm"][l] = mg["mix_norm"][0]
        small_g["ssm_d"][l] = mg["ssm_d"][0]
        small_g["attn_sink"][l] = mg["attn_sink"]
        per_dir = []
        for d in range(2):
            per_dir.append(ssm_vjp[l][d](ssm_cot[d]))
        for j, k in enumerate(ssm_params):
            small_g[k][l] = jnp.stack([per_dir[0][j], per_dir[1][j]])
        dh, dg, full["ffn1_w_gate"], full["ffn1_w_up"], full["ffn1_w_down"], _ = _ffn_bwd(
            "ffn", dh, s1, lw["ffn1_norm"], lw["ffn1_w_gate"], lw["ffn1_w_up"], lw["ffn1_w_down"])
        small_g["ffn1_norm"][l] = dg[0]
        pending = _flatten_full(full)
    keep_shard(0, _reduce_scatter("reduce_w", pending, sel))

    grad_x = dh[BLOCK:][None]
    small_list = [dh[PAD:BLOCK].reshape(-1)]
    for k in SMALL[1:]:
        small_list.append(d_final.reshape(-1) if k == "final_norm" else jnp.stack(small_g[k]).reshape(-1))
    small_vec = jnp.concatenate(small_list)
    n_small = small_vec.shape[0]
    rows_small = -(-n_small // (64 * 1024)) * 64
    small_vec = jnp.pad(small_vec, (0, rows_small * 1024 - n_small)).reshape(rows_small, 1024)
    everyone = [(fx, fy, fc) for fx in (0, 1) for fy in (0, 1) for fc in (0, 1)][1:]
    small_all = _all_gather_all("gather_small", small_vec, 8, everyone, lambda x_, y_, c_: 4 * x_ + 2 * y_ + c_)
    small_sum = _slot_sum("sum_small", [(small_all, k) for k in range(8)], sel, 64).reshape(-1)

    grads, deltas, new_m, new_v = {}, {}, {}, {}
    off = 0
    flat_w, flat_m, flat_v, flat_g = [], [], [], []
    for k in SMALL:
        size = (N_META * D_MODEL) if k == "meta_tokens" else int(np.prod(w[k].shape))
        g = small_sum[off:off + size]
        off += size
        if k == "meta_tokens":
            g = lax.dynamic_slice(g.reshape(N_META, D_MODEL), (0, chip * (D_MODEL // 4)), (N_META, D_MODEL // 4))
            grads[k] = g
            deltas[k], new_m[k], new_v[k] = _adamw("adamw_meta", w[k], g, mom[k], var[k], N_META)
        else:
            grads[k] = g.reshape(w[k].shape)
            flat_g.append(g)
            flat_w.append(w[k].reshape(-1))
            flat_m.append(mom[k].reshape(-1))
            flat_v.append(var[k].reshape(-1))
    n_flat = sum(a.shape[0] for a in flat_g)
    rows_flat = -(-n_flat // (64 * 1024)) * 64
    pack = lambda parts, fill: jnp.pad(jnp.concatenate(parts), (0, rows_flat * 1024 - n_flat),
                                       constant_values=fill).reshape(rows_flat, 1024)
    sd, sm_, sv = _adamw("adamw_small", pack(flat_w, 0.0), pack(flat_g, 0.0), pack(flat_m, 0.0), pack(flat_v, 1.0), 64)
    off = 0
    for k in SMALL:
        if k == "meta_tokens":
            continue
        size = int(np.prod(w[k].shape))
        for dst, src in ((deltas, sd), (new_m, sm_), (new_v, sv)):
            dst[k] = src.reshape(-1)[off:off + size].reshape(w[k].shape)
        off += size
    for k in BIG:
        g = jnp.stack(big_g[k])
        grads[k] = g
        rows_k = _shard_2d(g).shape[0]
        tm = 512 if rows_k % 512 == 0 else rows_k // depth
        d_, m_, v_ = _adamw("adamw_" + k, _shard_2d(w[k]), _shard_2d(g), _shard_2d(mom[k]), _shard_2d(var[k]), tm)
        deltas[k], new_m[k], new_v[k] = d_.reshape(g.shape), m_.reshape(g.shape), v_.reshape(g.shape)

    return (loss, grad_x, *[grads[k] for k in WEIGHTS], *[deltas[k] for k in WEIGHTS],
            *[new_m[k] for k in WEIGHTS], *[new_v[k] for k in WEIGHTS])
```

```python
import functools
import math

import numpy as np
import jax
import jax.numpy as jnp
from jax import lax
from jax.experimental import pallas as pl
from jax.experimental.pallas import tpu as pltpu

F32 = jnp.float32
BF16 = jnp.bfloat16

D_MODEL = 1024
N_META = 16
N_HEADS = 16
N_KV_HEADS = 4
HEAD_DIM = 64
Q_GROUP = N_HEADS // N_KV_HEADS
WINDOW = 128
BLOCK = 128
PAD = BLOCK - N_META
SSM_WIDTH = 512
SSM_GROUP_CH = 16
SSM_GROUPS = 32
SSM_STATE = 64
N_STATE = SSM_GROUPS * SSM_STATE
SUPER = 4
D_FF = 2816
EPS = 1e-6
NEG = -1e30
ATTN_SCALE = HEAD_DIM ** -0.5
SLOPES = [float(2.0 ** (-8.0 * (h + 1) / N_HEADS)) for h in range(N_HEADS)]

ADAM_LR = 0.001
ADAM_B1 = 0.9
ADAM_B2 = 0.999
ADAM_EPS = 1e-08
ADAM_WD = 0.01
ADAM_STEP = 10

V7X_VMEM_LIMIT_BYTES = 52 * 1024 * 1024
ROW_TILE = 384
LONG_ROW_TILE = 1376
MESH_AXES = ("x", "y", "c")

BIG = ["ffn1_w_gate", "ffn1_w_up", "ffn1_w_down", "ffn2_w_gate", "ffn2_w_up", "ffn2_w_down",
       "w_in", "ssm_w_glu", "w_branch_ssm", "w_branch_attn", "w_out"]
COL_SHARDED = {"ffn1_w_gate", "ffn1_w_up", "ffn2_w_gate", "ffn2_w_up", "w_in", "w_branch_ssm"}
SMALL = ["meta_tokens", "ffn1_norm", "mix_norm", "ffn2_norm", "final_norm", "ssm_lam_re", "ssm_lam_im", "ssm_log_dt",
         "ssm_b_re", "ssm_b_im", "ssm_c_re", "ssm_c_im", "ssm_d", "attn_sink"]
WEIGHTS = ["meta_tokens", "ffn1_norm", "ffn1_w_gate", "ffn1_w_up", "ffn1_w_down", "mix_norm", "w_in", "ssm_lam_re",
           "ssm_lam_im", "ssm_log_dt", "ssm_b_re", "ssm_b_im", "ssm_c_re", "ssm_c_im", "ssm_d", "ssm_w_glu", "attn_sink",
           "w_branch_ssm", "w_branch_attn", "w_out", "ffn2_norm", "ffn2_w_gate", "ffn2_w_up", "ffn2_w_down", "final_norm"]


def _dot(a, b):
    return lax.dot_general(a.astype(BF16), b.astype(BF16), (((1,), (0,)), ((), ())), preferred_element_type=F32)


def _dot_nt(a, b):
    return lax.dot_general(a.astype(BF16), b.astype(BF16), (((1,), (1,)), ((), ())), preferred_element_type=F32)


def _dot_tn(a, b):
    return lax.dot_general(a.astype(BF16), b.astype(BF16), (((0,), (0,)), ((), ())), preferred_element_type=F32)


def _sigmoid(x):
    return 0.5 * jnp.tanh(0.5 * x) + 0.5


_GELU_C = math.sqrt(2.0 / math.pi)


def _gelu(x):
    return 0.5 * x * (1.0 + jnp.tanh(_GELU_C * (x + 0.044715 * x * x * x)))


def _gelu_grad(x):
    th = jnp.tanh(_GELU_C * (x + 0.044715 * x * x * x))
    return 0.5 * (1.0 + th) + 0.5 * x * (1.0 - th * th) * _GELU_C * (1.0 + 3.0 * 0.044715 * x * x)


def _rms_fwd(x, g):
    r = lax.rsqrt(jnp.mean(x * x, axis=-1, keepdims=True) + EPS)
    return x * r * g


def _rms_bwd(x, g, dn):
    r = lax.rsqrt(jnp.mean(x * x, axis=-1, keepdims=True) + EPS)
    xh = x * r
    t = dn * g
    dx = r * (t - xh * jnp.mean(t * xh, axis=-1, keepdims=True))
    return dx, jnp.sum(dn * xh, axis=0, keepdims=True)


def _compiler_params():
    return pltpu.CompilerParams(dimension_semantics=("arbitrary",), vmem_limit_bytes=V7X_VMEM_LIMIT_BYTES)


def _rows(arr, width=None, cb=0):
    return (arr, arr.shape[1] if width is None else width, cb)


def _rowk(name, fn, rows, fulls, outs, accs=(), tm=ROW_TILE, smem=(), n_rows=None, hosted=None, writes_outs=False):
    n = rows[0][0].shape[0] if n_rows is None else n_rows
    assert n % tm == 0, (name, n, tm)
    in_specs, args = [], []
    for s in smem:
        in_specs.append(pl.BlockSpec(memory_space=pltpu.SMEM))
        args.append(s)
    for r in rows:
        if callable(r[2]):
            in_specs.append(pl.BlockSpec(r[1], r[2]))
        else:
            in_specs.append(pl.BlockSpec((tm, r[1]), functools.partial(lambda i, cb: (i, cb), cb=r[2])))
        args.append(r[0])
    for f in fulls:
        in_specs.append(pl.BlockSpec(memory_space=pl.ANY))
        args.append(f)
    nh = 0 if hosted is None else 1
    if nh:
        in_specs.append(pl.BlockSpec(memory_space=pl.ANY))
        args.append(hosted[0])
    out_specs, out_shape = [], []
    for w, dt in outs:
        out_specs.append(pl.BlockSpec((tm, w), lambda i: (i, 0)))
        out_shape.append(jax.ShapeDtypeStruct((n, w), dt))
    for shp in accs:
        out_specs.append(pl.BlockSpec(shp, functools.partial(lambda i, nd: (0,) * nd, nd=len(shp))))
        out_shape.append(jax.ShapeDtypeStruct(shp, F32))
    if nh:
        out_specs.append(pl.BlockSpec(memory_space=pl.ANY))
        out_shape.append(jax.ShapeDtypeStruct((hosted[1],) + hosted[0].shape[1:], hosted[0].dtype))
    ns, nr, nf, no, na = len(smem), len(rows), len(fulls), len(outs), len(accs)
    scratch = [pltpu.VMEM(f.shape, f.dtype) for f in fulls]
    if nf:
        scratch.append(pltpu.SemaphoreType.DMA((nf,)))
    if nh:
        scratch += [pltpu.SemaphoreType.DMA((len(hosted[2]),)), pltpu.SemaphoreType.DMA((len(hosted[2]),))]
    steps = n // tm

    def body(*refs):
        i = pl.program_id(0)
        refs = list(refs)
        take = lambda k: [refs.pop(0) for _ in range(k)]
        sm, rr, fh, hsrc, oo, aa, hout, fv = take(ns), take(nr), take(nf), take(nh), take(no), take(na), take(nh), take(nf)
        if nf:
            sem = refs.pop(0)

            @pl.when(i == 0)
            def _():
                cps = [pltpu.make_async_copy(fh[j], fv[j], sem.at[j]) for j in range(nf)]
                for cp in cps:
                    cp.start()
                for cp in cps:
                    cp.wait()
        if nh:
            @pl.when(i == 0)
            def _():
                for cp in _remote_copies(hsrc[0], hout[0], refs[0], refs[1], hosted[2]):
                    cp.start()
        if writes_outs:
            res = fn(i, *sm, *rr, *fv, *oo)
            res = (None,) * no + (tuple(res) if isinstance(res, (tuple, list)) else ())
        else:
            res = fn(i, *sm, *rr, *fv)
            res = tuple(res) if isinstance(res, (tuple, list)) else (res,)
            for o, v in zip(oo, res[:no]):
                o[...] = v.astype(o.dtype)
        if na:
            @pl.when(i == 0)
            def _():
                for a in aa:
                    a[...] = jnp.zeros_like(a)
            for a, v in zip(aa, res[no:]):
                a[...] += v
        if nh:
            @pl.when(i == steps - 1)
            def _():
                for cp in _remote_copies(hsrc[0], hout[0], refs[0], refs[1], hosted[2]):
                    cp.wait()

    return pl.pallas_call(body, grid=(steps,), in_specs=in_specs, out_specs=out_specs, out_shape=out_shape,
                          scratch_shapes=scratch, name=name, compiler_params=_compiler_params())(*args)


def _mm_tn(name, x, y, *, xw=None, xcb=0, tk, tn, scale=1.0, tm=None):
    m = x.shape[0]
    if tm is None:
        tm = LONG_ROW_TILE if m % LONG_ROW_TILE == 0 else ROW_TILE
    k = x.shape[1] if xw is None else xw
    nn = y.shape[1]
    assert m % tm == 0 and k % tk == 0 and nn % tn == 0, (name, m, k, nn)
    kb0 = (xcb * k) // tk

    def body(x_ref, y_ref, o_ref):
        @pl.when(pl.program_id(2) == 0)
        def _():
            o_ref[...] = jnp.zeros_like(o_ref)
        yv = y_ref[...]
        if scale != 1.0:
            yv = yv * scale
        o_ref[...] += _dot_tn(x_ref[...], yv)

    return pl.pallas_call(
        body, grid=(k // tk, nn // tn, m // tm),
        in_specs=[pl.BlockSpec((tm, tk), lambda a, b, i: (i, kb0 + a)), pl.BlockSpec((tm, tn), lambda a, b, i: (i, b))],
        out_specs=pl.BlockSpec((tk, tn), lambda a, b, i: (a, b)), out_shape=jax.ShapeDtypeStruct((k, nn), F32), name=name,
        compiler_params=pltpu.CompilerParams(dimension_semantics=("arbitrary", "arbitrary", "arbitrary"),
                                             vmem_limit_bytes=V7X_VMEM_LIMIT_BYTES))(x, y)


def _remote_copies(src_ref, out_ref, ssem, rsem, sends):
    x, y, c = lax.axis_index("x"), lax.axis_index("y"), lax.axis_index("c")
    cps = []
    for k, ((fx, fy, fc), sf, df) in enumerate(sends):
        peer = (1 - x if fx else x, 1 - y if fy else y, 1 - c if fc else c)
        cps.append(pltpu.make_async_remote_copy(src_ref=src_ref.at[sf(x, y, c)], dst_ref=out_ref.at[df(x, y, c)],
                                                send_sem=ssem.at[k], recv_sem=rsem.at[k], device_id=peer,
                                                device_id_type=pl.DeviceIdType.MESH))
    return cps


def _exchange(name, src, n_out, local, sends, alias=False):
    nl, nsnd = len(local), len(sends)
    out_shape = jax.ShapeDtypeStruct((n_out,) + src.shape[1:], src.dtype)

    def body(src_ref, out_ref, lsem, ssem, rsem):
        x, y, c = lax.axis_index("x"), lax.axis_index("y"), lax.axis_index("c")
        cps = [pltpu.make_async_copy(src_ref.at[sf(x, y, c)], out_ref.at[df(x, y, c)], lsem.at[j])
               for j, (sf, df) in enumerate(local)]
        cps += _remote_copies(src_ref, out_ref, ssem, rsem, sends)
        for cp in cps:
            cp.start()
        for cp in cps:
            cp.wait()

    return pl.pallas_call(
        body, in_specs=[pl.BlockSpec(memory_space=pl.ANY)], out_specs=pl.BlockSpec(memory_space=pl.ANY), out_shape=out_shape,
        scratch_shapes=[pltpu.SemaphoreType.DMA((max(nl, 1),)), pltpu.SemaphoreType.DMA((nsnd,)), pltpu.SemaphoreType.DMA((nsnd,))],
        input_output_aliases=({0: 0} if alias else {}), name=name)(src)


def _chip(x, y):
    return 2 * x + y


_OTHER_CHIPS = [(1, 0), (0, 1), (1, 1)]


def _all_gather_shards(name, shard):
    src, n_out, sends = _gather_ici_stage(shard)
    return _gather_finish(name, shard, _exchange(name + "_ici", src, n_out, [], sends))


def _gather_ici_stage(shard):
    r, w = shard.shape
    first = [((fx, fy, 0), lambda x, y, c: c, lambda x, y, c: 2 * _chip(x, y) + c) for fx, fy in _OTHER_CHIPS]
    return shard.reshape(2, r // 2, w), 8, first


def _gather_finish(name, shard, g):
    r, w = shard.shape
    second = [((0, 0, 1),
               (lambda x, y, c, fx=fx, fy=fy: 2 * _chip(x ^ fx, y ^ fy) + c),
               (lambda x, y, c, fx=fx, fy=fy: 2 * _chip(x ^ fx, y ^ fy) + c)) for fx, fy in _OTHER_CHIPS]
    g = _exchange(name + "_d2d", g, 8, [], second, alias=True).reshape(4, r, w)
    mine = lax.broadcasted_iota(jnp.int32, (4, 1, 1), 0) == _chip(lax.axis_index("x"), lax.axis_index("y"))
    return jnp.where(mine, shard[None], g)


def _slot_sum(name, terms, sel, tm, out_slots=None, out_slot=None, also_bf16=False):
    rows, w = terms[0][0].shape[1:]

    def imap(slot):
        if isinstance(slot, int):
            return lambda i, s: (slot, i, 0)
        return lambda i, s: (s[slot[1]], i, 0)

    in_specs = [pl.BlockSpec((None, tm, w), imap(sl)) for _, sl in terms]
    if out_slots is None:
        out_specs, out_shape = [pl.BlockSpec((tm, w), lambda i, s: (i, 0))], [jax.ShapeDtypeStruct((rows, w), F32)]
    else:
        out_specs = [pl.BlockSpec((None, tm, w), imap(out_slot))]
        out_shape = [jax.ShapeDtypeStruct((out_slots, rows, w), F32)]
    if also_bf16:
        out_specs.append(pl.BlockSpec((tm, w), lambda i, s: (i, 0)))
        out_shape.append(jax.ShapeDtypeStruct((rows, w), BF16))
    n_in = len(terms)

    def body(sel_ref, *refs):
        acc = refs[0][...].astype(F32)
        for r in refs[1:n_in]:
            acc = acc + r[...].astype(F32)
        refs[n_in][...] = acc
        if also_bf16:
            refs[n_in + 1][...] = acc.astype(BF16)

    grid_spec = pltpu.PrefetchScalarGridSpec(num_scalar_prefetch=1, grid=(rows // tm,), in_specs=in_specs, out_specs=out_specs)
    out = pl.pallas_call(body, grid_spec=grid_spec, out_shape=out_shape, name=name,
                         compiler_params=_compiler_params())(sel, *[a for a, _ in terms])
    return out if also_bf16 else out[0]


def _reduce_scatter(name, parts, sel):
    f, (src, n_out, sends) = _rs_pair_stage(parts)
    p, (src2, n_out2, sends2) = _rs_chip_stage(name, f, _exchange(name + "_d2d", src, n_out, [], sends), sel)
    return _rs_finish(name, p, _exchange(name + "_ici", src2, n_out2, [], sends2), sel)


def _rs_pair_stage(parts):
    _, _, r, w = parts.shape
    f = parts.reshape(2, 4 * r, w)
    return f, (f, 1, [((0, 0, 1), lambda x, y, c: 1 - c, lambda x, y, c: 0)])


def _rs_chip_stage(name, f, got, sel):
    r, w = f.shape[1] // 4, f.shape[2]
    p, p16 = _slot_sum(name + "_add2", [(f, ("sel", 0)), (got, 0)], sel, 384, also_bf16=True)
    sends = [((fx, fy, 0), (lambda x, y, c, fx=fx, fy=fy: _chip(x ^ fx, y ^ fy)), (lambda x, y, c, k=k: k))
             for k, (fx, fy) in enumerate(_OTHER_CHIPS)]
    return p.reshape(4, r, w), (p16.reshape(4, r, w), 3, sends)


def _rs_finish(name, p, got, sel):
    _, r, w = p.shape
    q = _slot_sum(name + "_add4", [(p, ("sel", 1)), (got, 0), (got, 1), (got, 2)], sel, 496, out_slots=2, out_slot=("sel", 0))
    q = _exchange(name + "_pair", q, 2, [], [((0, 0, 1), lambda x, y, c: c, lambda x, y, c: c)], alias=True)
    return q.reshape(2 * r, w)


def _all_gather_all(name, vec, n_slots, flips, slot_fn):
    sends = [(f, lambda x, y, c: 0, slot_fn) for f in flips]
    g = _exchange(name, vec[None], n_slots, [], sends)
    mine = lax.broadcasted_iota(jnp.int32, (n_slots, 1, 1), 0) == slot_fn(*(lax.axis_index(a) for a in MESH_AXES))
    return jnp.where(mine, vec[None], g)


def _nbr_specs(arr, width, cb, nb):
    return [
        (arr, (BLOCK, width), functools.partial(lambda n, cb: (jnp.maximum(n - 1, 0), cb), cb=cb)),
        (arr, (BLOCK, width), functools.partial(lambda n, cb: (n, cb), cb=cb)),
        (arr, (BLOCK, width), functools.partial(lambda n, cb: (jnp.minimum(n + 1, nb - 1), cb), cb=cb)),
    ]


def _head(h):
    return slice(h * HEAD_DIM, (h + 1) * HEAD_DIM)


def _row_group(n_groups, rows_per_group):
    r = lax.broadcasted_iota(jnp.int32, (n_groups * rows_per_group, 1), 0)
    grp = jnp.zeros_like(r)
    for g in range(1, n_groups):
        grp = grp + jnp.where(r >= g * rows_per_group, 1, 0)
    return grp


def _by_group(grp, vals):
    out = vals[-1]
    for g in range(len(vals) - 2, -1, -1):
        out = jnp.where(grp == g, vals[g], out)
    return out


def _attn_fwd(proj, sink):
    lp = proj.shape[0]
    nb = lp // BLOCK
    kv_w = N_KV_HEADS * HEAD_DIM
    specs = _nbr_specs(proj, kv_w, 4, nb) + _nbr_specs(proj, kv_w, 5, nb)
    specs += [(proj, (BLOCK, kv_w), lambda n: (0, 4)), (proj, (BLOCK, kv_w), lambda n: (0, 5))]
    in_specs = [pl.BlockSpec(memory_space=pltpu.SMEM), pl.BlockSpec((BLOCK, D_MODEL), lambda n: (n, 0))]
    in_specs += [pl.BlockSpec(s[1], s[2]) for s in specs]

    def body(sink_ref, q_ref, kp, kc, kn, vp, vc, vn, km, vm, o_ref, lse_ref):
        n = pl.program_id(0)
        qi = lax.broadcasted_iota(jnp.int32, (BLOCK, 3 * BLOCK), 0)
        sj = lax.broadcasted_iota(jnp.int32, (BLOCK, 3 * BLOCK), 1)
        dist = jnp.abs(qi + BLOCK - sj)
        kpos = (n - 1) * BLOCK + sj
        valid = (dist <= WINDOW) & (kpos >= BLOCK) & (kpos < lp)
        distf = dist.astype(F32)
        kb = jnp.concatenate([kp[...], kc[...], kn[...]], axis=0).astype(BF16)
        vb = jnp.concatenate([vp[...], vc[...], vn[...]], axis=0).astype(BF16)
        kmeta = km[PAD:BLOCK, :].astype(BF16)
        vmeta = vm[PAD:BLOCK, :].astype(BF16)
        valid4 = jnp.concatenate([valid] * Q_GROUP, axis=0)
        distf4 = jnp.concatenate([distf] * Q_GROUP, axis=0)
        grp = _row_group(Q_GROUP, BLOCK)
        for kh in range(N_KV_HEADS):
            ksl = slice(kh * HEAD_DIM, (kh + 1) * HEAD_DIM)
            heads = [kh * Q_GROUP + g for g in range(Q_GROUP)]
            slope = _by_group(grp, [SLOPES[h] for h in heads])
            sk = _by_group(grp, [sink_ref[h] for h in heads])
            q4 = (jnp.concatenate([q_ref[:, _head(h)] for h in heads], axis=0) * ATTN_SCALE).astype(BF16)
            s = jnp.where(valid4, _dot_nt(q4, kb[:, ksl]) - slope * distf4, NEG)
            sm = _dot_nt(q4, kmeta[:, ksl])
            m = jnp.maximum(jnp.maximum(jnp.max(s, axis=1, keepdims=True), jnp.max(sm, axis=1, keepdims=True)), sk)
            e = jnp.exp(s - m)
            em = jnp.exp(sm - m)
            den = jnp.sum(e, axis=1, keepdims=True) + jnp.sum(em, axis=1, keepdims=True) + jnp.exp(sk - m)
            o4 = (_dot(e, vb[:, ksl]) + _dot(em, vmeta[:, ksl])) * (1.0 / den)
            lse4 = m + jnp.log(den)
            for g, h in enumerate(heads):
                o_ref[:, _head(h)] = o4[g * BLOCK:(g + 1) * BLOCK].astype(o_ref.dtype)
                lse_ref[:, h:h + 1] = lse4[g * BLOCK:(g + 1) * BLOCK]

    return pl.pallas_call(
        body, grid=(nb,), in_specs=in_specs,
        out_specs=[pl.BlockSpec((BLOCK, D_MODEL), lambda n: (n, 0)), pl.BlockSpec((BLOCK, N_HEADS), lambda n: (n, 0))],
        out_shape=[jax.ShapeDtypeStruct((lp, D_MODEL), BF16), jax.ShapeDtypeStruct((lp, N_HEADS), F32)],
        name="attn_fwd", compiler_params=_compiler_params())(sink, proj, *[s[0] for s in specs])


def _attn_delta(do, o):
    lp = do.shape[0]
    sel = (lax.broadcasted_iota(jnp.int32, (N_HEADS, D_MODEL), 1) // HEAD_DIM
           == lax.broadcasted_iota(jnp.int32, (N_HEADS, D_MODEL), 0)).astype(BF16)

    def body(do_ref, o_ref, sel_ref, d_ref, dt_ref):
        prod = do_ref[...] * o_ref[...].astype(F32)
        hi = prod.astype(BF16)
        lo = (prod - hi.astype(F32)).astype(BF16)
        d_ref[...] = _dot_nt(hi, sel_ref[...]) + _dot_nt(lo, sel_ref[...])
        dt_ref[...] = _dot_nt(sel_ref[...], hi) + _dot_nt(sel_ref[...], lo)

    return pl.pallas_call(
        body, grid=(lp // BLOCK,),
        in_specs=[pl.BlockSpec((BLOCK, D_MODEL), lambda n: (n, 0)), pl.BlockSpec((BLOCK, D_MODEL), lambda n: (n, 0)),
                  pl.BlockSpec((N_HEADS, D_MODEL), lambda n: (0, 0))],
        out_specs=[pl.BlockSpec((BLOCK, N_HEADS), lambda n: (n, 0)), pl.BlockSpec((N_HEADS, BLOCK), lambda n: (0, n))],
        out_shape=[jax.ShapeDtypeStruct((lp, N_HEADS), F32), jax.ShapeDtypeStruct((N_HEADS, lp), F32)],
        name="attn_delta", compiler_params=_compiler_params())(do, o, sel)


def _attn_bwd(proj, sink, o, lse, do):
    lp = proj.shape[0]
    nb = lp // BLOCK
    kv_w = N_KV_HEADS * HEAD_DIM
    delta, delta_t = _attn_delta(do, o)
    lse_t = lse.T
    row_nbrs = lambda arr: [
        (arr, (N_HEADS, BLOCK), lambda n: (0, jnp.maximum(n - 1, 0))), (arr, (N_HEADS, BLOCK), lambda n: (0, n)),
        (arr, (N_HEADS, BLOCK), lambda n: (0, jnp.minimum(n + 1, nb - 1)))]
    specs = (_nbr_specs(proj, D_MODEL, 0, nb) + _nbr_specs(proj, kv_w, 4, nb) + _nbr_specs(proj, kv_w, 5, nb)
             + [(proj, (BLOCK, kv_w), lambda n: (0, 4)), (proj, (BLOCK, kv_w), lambda n: (0, 5))]
             + _nbr_specs(do, D_MODEL, 0, nb) + [(lse, (BLOCK, N_HEADS), lambda n: (n, 0)), (delta, (BLOCK, N_HEADS), lambda n: (n, 0))]
             + row_nbrs(lse_t) + row_nbrs(delta_t))
    in_specs = [pl.BlockSpec(memory_space=pltpu.SMEM)] + [pl.BlockSpec(s[1], s[2]) for s in specs]

    def body(sink_ref, qp, qc, qn, kp, kc, kn, vp, vc, vn, km, vm, dop, doc, don, lc, dc, ltp, ltc, ltn, dtp, dtc, dtn,
             dq_ref, dk_ref, dv_ref, dkm_ref, dvm_ref, dsk_ref):
        n = pl.program_id(0)

        @pl.when(n == 0)
        def _():
            dkm_ref[...] = jnp.zeros_like(dkm_ref)
            dvm_ref[...] = jnp.zeros_like(dvm_ref)
            dsk_ref[...] = jnp.zeros_like(dsk_ref)

        qi = lax.broadcasted_iota(jnp.int32, (BLOCK, 3 * BLOCK), 0)
        sj = lax.broadcasted_iota(jnp.int32, (BLOCK, 3 * BLOCK), 1)
        dist_q = jnp.abs(qi + BLOCK - sj)
        kpos = (n - 1) * BLOCK + sj
        valid_q = (dist_q <= WINDOW) & (kpos >= BLOCK) & (kpos < lp)
        distf_q = dist_q.astype(F32)
        bi = lax.broadcasted_iota(jnp.int32, (BLOCK, 3 * BLOCK), 1)
        kj = lax.broadcasted_iota(jnp.int32, (BLOCK, 3 * BLOCK), 0)
        dist_k = jnp.abs(bi - BLOCK - kj)
        qpos = (n - 1) * BLOCK + bi
        valid_k = (dist_k <= WINDOW) & (qpos >= 0) & (qpos < lp) & (n >= 1)
        distf_k = dist_k.astype(F32)

        kb = jnp.concatenate([kp[...], kc[...], kn[...]], axis=0).astype(BF16)
        vb = jnp.concatenate([vp[...], vc[...], vn[...]], axis=0).astype(BF16)
        kcur = kc[...].astype(BF16)
        vcur = vc[...].astype(BF16)
        kmeta = km[PAD:BLOCK, :].astype(BF16)
        vmeta = vm[PAD:BLOCK, :].astype(BF16)
        lane = lax.broadcasted_iota(jnp.int32, (1, BLOCK), 1)
        dsink = jnp.zeros((1, BLOCK), F32)
        valid_q4 = jnp.concatenate([valid_q] * Q_GROUP, axis=0)
        distf_q4 = jnp.concatenate([distf_q] * Q_GROUP, axis=0)
        valid_k4 = jnp.concatenate([valid_k] * Q_GROUP, axis=1)
        distf_k4 = jnp.concatenate([distf_k] * Q_GROUP, axis=1)
        grp_q = _row_group(Q_GROUP, BLOCK)
        lane_k = lax.broadcasted_iota(jnp.int32, (1, Q_GROUP * 3 * BLOCK), 1)
        grp_k = sum(jnp.where(lane_k >= g * 3 * BLOCK, 1, 0) for g in range(1, Q_GROUP))
        for kh in range(N_KV_HEADS):
            ksl = slice(kh * HEAD_DIM, (kh + 1) * HEAD_DIM)
            heads = [kh * Q_GROUP + g for g in range(Q_GROUP)]
            slopes = [SLOPES[h] for h in heads]
            q4 = (jnp.concatenate([qc[:, _head(h)] for h in heads], axis=0) * ATTN_SCALE).astype(BF16)
            do4 = jnp.concatenate([doc[:, _head(h)] for h in heads], axis=0)
            delta = jnp.concatenate([dc[:, h:h + 1] for h in heads], axis=0)
            lse4 = jnp.concatenate([lc[:, h:h + 1] for h in heads], axis=0)
            s = _dot_nt(q4, kb[:, ksl]) - _by_group(grp_q, slopes) * distf_q4
            p = jnp.exp(jnp.where(valid_q4, s, NEG) - lse4)
            pm = jnp.exp(_dot_nt(q4, kmeta[:, ksl]) - lse4)
            ps = jnp.exp(_by_group(grp_q, [sink_ref[h] for h in heads]) - lse4)
            do4b = do4.astype(BF16)
            ds = p * (_dot_nt(do4b, vb[:, ksl]) - delta)
            dsm = pm * (_dot_nt(do4b, vmeta[:, ksl]) - delta)
            dq4 = ATTN_SCALE * (_dot(ds, kb[:, ksl]) + _dot(dsm, kmeta[:, ksl]))
            dsk4 = ps * delta
            for g, h in enumerate(heads):
                dq_ref[:, _head(h)] = dq4[g * BLOCK:(g + 1) * BLOCK].astype(dq_ref.dtype)
                dsink = dsink + jnp.where(lane == h, -jnp.sum(dsk4[g * BLOCK:(g + 1) * BLOCK]), 0.0)
            dkm_ref[:, ksl] += _dot_tn(dsm, q4)
            dvm_ref[:, ksl] += _dot_tn(pm, do4b)
            band = lambda a, b, c_: jnp.concatenate([r[:, _head(h)] for h in heads for r in (a, b, c_)], axis=0)
            qb4 = (band(qp, qc, qn) * ATTN_SCALE).astype(BF16)
            dob4b = band(dop, doc, don).astype(BF16)
            delta_b = jnp.concatenate([r[h:h + 1, :] for h in heads for r in (dtp, dtc, dtn)], axis=1)
            lse_b = jnp.concatenate([r[h:h + 1, :] for h in heads for r in (ltp, ltc, ltn)], axis=1)
            st = _dot_nt(kcur[:, ksl], qb4) - _by_group(grp_k, slopes) * distf_k4
            pt = jnp.exp(jnp.where(valid_k4, st, NEG) - lse_b)
            dv_ref[:, ksl] = _dot(pt, dob4b)
            dst = pt * (_dot_nt(vcur[:, ksl], dob4b) - delta_b)
            dk_ref[:, ksl] = _dot(dst, qb4)
        dsk_ref[...] += dsink

    blk = lambda w: pl.BlockSpec((BLOCK, w), lambda n: (n, 0))
    fix = lambda shp: pl.BlockSpec(shp, lambda n: (0, 0))
    return pl.pallas_call(
        body, grid=(nb,), in_specs=in_specs,
        out_specs=[blk(D_MODEL), blk(kv_w), blk(kv_w), fix((N_META, kv_w)), fix((N_META, kv_w)), fix((1, BLOCK))],
        out_shape=[jax.ShapeDtypeStruct((lp, D_MODEL), BF16), jax.ShapeDtypeStruct((lp, kv_w), F32),
                   jax.ShapeDtypeStruct((lp, kv_w), F32), jax.ShapeDtypeStruct((N_META, kv_w), F32),
                   jax.ShapeDtypeStruct((N_META, kv_w), F32), jax.ShapeDtypeStruct((1, BLOCK), F32)],
        name="attn_bwd", compiler_params=_compiler_params())(sink, *[s[0] for s in specs])


N_SEG = 8
SSM_TILE = 384


def _to_segments(a):
    lp, w = a.shape
    return a.reshape(N_SEG, lp // N_SEG, w).transpose(1, 0, 2).reshape(lp, w)


def _from_segments(a):
    lp, w = a.shape
    return a.reshape(lp // N_SEG, N_SEG, w).transpose(1, 0, 2).reshape(lp, w)


def _complex_power(ar, ai, n):
    rr, ri = jnp.ones_like(ar), jnp.zeros_like(ai)
    while n:
        if n & 1:
            rr, ri = rr * ar - ri * ai, rr * ai + ri * ar
        ar, ai = ar * ar - ai * ai, 2.0 * ar * ai
        n >>= 1
    return rr, ri


def _segment_starts(finals, a_seg, reverse):
    fr, fi = finals[:, :N_STATE], finals[:, N_STATE:]
    ar, ai = a_seg[:, :N_STATE], a_seg[:, N_STATE:]
    row = lax.broadcasted_iota(jnp.int32, (N_SEG, N_STATE), 0)
    pr = jnp.zeros((1, N_STATE), F32)
    pi = jnp.zeros((1, N_STATE), F32)
    sr = jnp.zeros((N_SEG, N_STATE), F32)
    si = jnp.zeros((N_SEG, N_STATE), F32)
    for s in (range(N_SEG - 1, -1, -1) if reverse else range(N_SEG)):
        sr = jnp.where(row == s, pr, sr)
        si = jnp.where(row == s, pi, si)
        pr, pi = fr[s:s + 1] + ar * pr - ai * pi, fi[s:s + 1] + ar * pi + ai * pr
    return jnp.concatenate([sr, si], axis=1)


def _recurrence(buf_ref, st_ref, a_ref, reverse):
    steps = SSM_TILE // N_SEG
    half = N_STATE // 2
    for c0 in (0, half):
        re = slice(c0, c0 + half)
        im = slice(N_STATE + c0, N_STATE + c0 + half)
        ar = jnp.broadcast_to(a_ref[:, re], (N_SEG, half))
        ai = jnp.broadcast_to(a_ref[:, im], (N_SEG, half))

        def step(k, carry, re=re, im=im, ar=ar, ai=ai):
            xr, xi = carry
            r0 = pl.multiple_of((steps - 1 - k if reverse else k) * N_SEG, N_SEG)
            nr = ar * xr - ai * xi + buf_ref[pl.ds(r0, N_SEG), re]
            ni = ar * xi + ai * xr + buf_ref[pl.ds(r0, N_SEG), im]
            buf_ref[pl.ds(r0, N_SEG), re] = nr
            buf_ref[pl.ds(r0, N_SEG), im] = ni
            return nr, ni

        xr, xi = lax.fori_loop(0, steps, step, (st_ref[:, re], st_ref[:, im]), unroll=2)
        st_ref[:, re] = xr
        st_ref[:, im] = xi


def _copy_in(pairs, sem):
    cps = [pltpu.make_async_copy(src, dst, sem.at[j]) for j, (src, dst) in enumerate(pairs)]
    for cp in cps:
        cp.start()
    for cp in cps:
        cp.wait()


def _ssm_fwd_dir(name, u_seg, wb, wc, a, a_seg, reverse):
    lp = u_seg.shape[0]
    nt = lp // SSM_TILE
    tile = (lambda i: nt - 1 - i) if reverse else (lambda i: i)
    first = tile(0)
    held = lambda p, i: (p * tile(i) + (1 - p) * first, 0)

    def body(u_ref, wb_hbm, wc_hbm, a_ref, aseg_ref, x_ref, y_ref, wb_ref, wc_ref, buf_ref, st_ref, sem):
        p, i = pl.program_id(0), pl.program_id(1)

        @pl.when((p == 0) & (i == 0))
        def _():
            _copy_in([(wb_hbm, wb_ref), (wc_hbm, wc_ref)], sem)
            st_ref[...] = jnp.zeros_like(st_ref)

        @pl.when((p == 1) & (i == 0))
        def _():
            st_ref[...] = _segment_starts(st_ref[...], aseg_ref[...], reverse)

        def states_into(dst_ref):
            for j in range(SUPER):
                part = _dot(u_ref[:, 128 * j:128 * (j + 1)], wb_ref[j])
                dst_ref[:, 512 * j:512 * (j + 1)] = part[:, :512]
                dst_ref[:, N_STATE + 512 * j:N_STATE + 512 * (j + 1)] = part[:, 512:]
            _recurrence(dst_ref, st_ref, a_ref, reverse)

        @pl.when(p == 0)
        def _():
            states_into(buf_ref)

        @pl.when(p == 1)
        def _():
            states_into(x_ref)
            for j in range(SUPER):
                y_ref[:, 128 * j:128 * (j + 1)] = (_dot(x_ref[:, _re(j)], wc_ref[j, :512, :])
                                                   + _dot(x_ref[:, _im(j)], wc_ref[j, 512:, :]))

    fix = lambda shp: pl.BlockSpec(shp, lambda p, i: (0, 0))
    return pl.pallas_call(
        body, grid=(2, nt),
        in_specs=[pl.BlockSpec((SSM_TILE, SSM_WIDTH), lambda p, i: (tile(i), 0)), pl.BlockSpec(memory_space=pl.ANY),
                  pl.BlockSpec(memory_space=pl.ANY), fix((1, 2 * N_STATE)), fix((1, 2 * N_STATE))],
        out_specs=[pl.BlockSpec((SSM_TILE, 2 * N_STATE), held), pl.BlockSpec((SSM_TILE, SSM_WIDTH), held)],
        out_shape=[jax.ShapeDtypeStruct((lp, 2 * N_STATE), F32), jax.ShapeDtypeStruct((lp, SSM_WIDTH), F32)],
        scratch_shapes=[pltpu.VMEM(wb.shape, BF16), pltpu.VMEM(wc.shape, BF16), pltpu.VMEM((SSM_TILE, 2 * N_STATE), F32),
                        pltpu.VMEM((N_SEG, 2 * N_STATE), F32), pltpu.SemaphoreType.DMA((2,))],
        name=name, compiler_params=pltpu.CompilerParams(dimension_semantics=("arbitrary", "arbitrary"),
                                                        vmem_limit_bytes=V7X_VMEM_LIMIT_BYTES))(
        u_seg, wb.astype(BF16), wc.astype(BF16), a, a_seg)


def _ssm_bwd_dir(name, dys_seg, u_seg, x_seg, wb, wc, a_conj, a_seg_conj, fwd_reverse):
    lp = u_seg.shape[0]
    nt = lp // SSM_TILE
    steps = SSM_TILE // N_SEG
    reverse = not fwd_reverse
    tile = (lambda i: nt - 1 - i) if reverse else (lambda i: i)
    first = tile(0)
    held = lambda p, i: (p * tile(i) + (1 - p) * first, 0)
    n_slab = lp // N_SEG
    if fwd_reverse:
        halo = lambda p, i: (p * jnp.minimum((tile(i) + 1) * steps, n_slab - 1), 0)
        edge = lambda p, i: (0, 0)
    else:
        halo = lambda p, i: (p * jnp.maximum(tile(i) * steps - 1, 0), 0)
        edge = lambda p, i: (n_slab - 1, 0)

    def body(dy_ref, u_ref, x_ref, halo_ref, edge_ref, wb_hbm, wc_hbm, a_ref, aseg_ref, du_ref, dwb_ref, dwc_ref, ga_ref,
             wb_ref, wc_ref, buf_ref, st_ref, sem):
        p, i = pl.program_id(0), pl.program_id(1)

        @pl.when((p == 0) & (i == 0))
        def _():
            _copy_in([(wb_hbm, wb_ref), (wc_hbm, wc_ref)], sem)
            st_ref[...] = jnp.zeros_like(st_ref)
            dwb_ref[...] = jnp.zeros_like(dwb_ref)
            dwc_ref[...] = jnp.zeros_like(dwc_ref)
            ga_ref[...] = jnp.zeros_like(ga_ref)

        @pl.when((p == 1) & (i == 0))
        def _():
            st_ref[...] = _segment_starts(st_ref[...], aseg_ref[...], reverse)

        for j in range(SUPER):
            part = _dot_nt(dy_ref[:, 128 * j:128 * (j + 1)], wc_ref[j])
            buf_ref[:, 512 * j:512 * (j + 1)] = part[:, :512]
            buf_ref[:, N_STATE + 512 * j:N_STATE + 512 * (j + 1)] = part[:, 512:]
        _recurrence(buf_ref, st_ref, a_ref, reverse)

        @pl.when(p == 1)
        def _():
            for j in range(SUPER):
                ch = slice(128 * j, 128 * (j + 1))
                du_ref[:, ch] = (_dot_nt(buf_ref[:, _re(j)], wb_ref[j, :, :512]) + _dot_nt(buf_ref[:, _im(j)], wb_ref[j, :, 512:]))
                dwb_ref[ch, :512] += _dot_tn(u_ref[:, ch], buf_ref[:, _re(j)])
                dwb_ref[ch, 512:] += _dot_tn(u_ref[:, ch], buf_ref[:, _im(j)])
                dwc_ref[ch, :512] += _dot_tn(dy_ref[:, ch], x_ref[:, _re(j)])
                dwc_ref[ch, 512:] += _dot_tn(dy_ref[:, ch], x_ref[:, _im(j)])
            row = lax.broadcasted_iota(jnp.int32, (N_SEG, 2 * N_STATE), 0)
            if fwd_reverse:
                wrap = jnp.where(row == N_SEG - 1, 0.0, pltpu.roll(edge_ref[...], N_SEG - 1, axis=0))
                open_slab = jnp.where(tile(i) == nt - 1, wrap, halo_ref[...])
                before = lambda cols: jnp.concatenate([x_ref[N_SEG:, cols], open_slab[:, cols]], axis=0)
            else:
                wrap = jnp.where(row == 0, 0.0, pltpu.roll(edge_ref[...], 1, axis=0))
                open_slab = jnp.where(tile(i) == 0, wrap, halo_ref[...])
                before = lambda cols: jnp.concatenate([open_slab[:, cols], x_ref[:SSM_TILE - N_SEG, cols]], axis=0)
            half = N_STATE // 2
            for c0 in (0, half):
                re = slice(c0, c0 + half)
                im = slice(N_STATE + c0, N_STATE + c0 + half)
                gr, gi = buf_ref[:, re], buf_ref[:, im]
                br, bi = before(re), before(im)
                fold = lambda v: jnp.sum(v.reshape(steps, N_SEG, half), axis=0)
                ga_ref[:, re] += fold(gr * br + gi * bi)
                ga_ref[:, im] += fold(gi * br - gr * bi)

    fix = lambda shp: pl.BlockSpec(shp, lambda p, i: (0, 0))
    row_tile = lambda w: pl.BlockSpec((SSM_TILE, w), lambda p, i: (tile(i), 0))
    return pl.pallas_call(
        body, grid=(2, nt),
        in_specs=[row_tile(SSM_WIDTH), row_tile(SSM_WIDTH), pl.BlockSpec((SSM_TILE, 2 * N_STATE), held),
                  pl.BlockSpec((N_SEG, 2 * N_STATE), halo), pl.BlockSpec((N_SEG, 2 * N_STATE), edge),
                  pl.BlockSpec(memory_space=pl.ANY), pl.BlockSpec(memory_space=pl.ANY), fix((1, 2 * N_STATE)), fix((1, 2 * N_STATE))],
        out_specs=[pl.BlockSpec((SSM_TILE, SSM_WIDTH), held), fix((SSM_WIDTH, 1024)), fix((SSM_WIDTH, 1024)),
                   fix((N_SEG, 2 * N_STATE))],
        out_shape=[jax.ShapeDtypeStruct((lp, SSM_WIDTH), F32), jax.ShapeDtypeStruct((SSM_WIDTH, 1024), F32),
                   jax.ShapeDtypeStruct((SSM_WIDTH, 1024), F32), jax.ShapeDtypeStruct((N_SEG, 2 * N_STATE), F32)],
        scratch_shapes=[pltpu.VMEM(wb.shape, BF16), pltpu.VMEM(wc.shape, BF16), pltpu.VMEM((SSM_TILE, 2 * N_STATE), F32),
                        pltpu.VMEM((N_SEG, 2 * N_STATE), F32), pltpu.SemaphoreType.DMA((2,))],
        name=name, compiler_params=pltpu.CompilerParams(dimension_semantics=("arbitrary", "arbitrary"),
                                                        vmem_limit_bytes=V7X_VMEM_LIMIT_BYTES))(
        dys_seg, u_seg, x_seg, x_seg, x_seg, wb.astype(BF16), wc.astype(BF16), a_conj, a_seg_conj)


def _ssm_prep(lam_re, lam_im, log_dt, b_re, b_im, c_re, c_im):
    dt = jnp.exp(log_dt)[:, None]
    er = jnp.exp(lam_re * dt)
    ar, ai = er * jnp.cos(lam_im * dt), er * jnp.sin(lam_im * dt)
    nr, ni = ar - 1.0, ai
    den = lam_re * lam_re + lam_im * lam_im
    cr, ci = (nr * lam_re + ni * lam_im) / den, (ni * lam_re - nr * lam_im) / den
    bbr = cr[:, :, None] * b_re - ci[:, :, None] * b_im
    bbi = cr[:, :, None] * b_im + ci[:, :, None] * b_re
    eye = jnp.eye(8, dtype=F32)

    def in_map(b):
        b = b.reshape(SUPER, 8, SSM_STATE, SSM_GROUP_CH).transpose(0, 1, 3, 2)
        return (b[:, :, :, None, :] * eye[None, :, None, :, None]).reshape(SUPER, 128, 512)

    def out_map(cm):
        cm = cm.reshape(SUPER, 8, SSM_GROUP_CH, SSM_STATE).transpose(0, 1, 3, 2)
        return (cm[:, :, :, None, :] * eye[None, :, None, :, None]).reshape(SUPER, 512, 128)

    wb = jnp.concatenate([in_map(bbr), in_map(bbi)], axis=2)
    wc = jnp.concatenate([out_map(c_re), -out_map(c_im)], axis=1)
    return ar.reshape(1, N_STATE), ai.reshape(1, N_STATE), wb, wc


def _re(j):
    return slice(512 * j, 512 * (j + 1))


def _im(j):
    return slice(N_STATE + 512 * j, N_STATE + 512 * (j + 1))


def _row_ids(i, tm, width):
    return i * tm + lax.broadcasted_iota(jnp.int32, (tm, width), 0)


_FF_CHUNKS = (slice(0, D_FF // 2), slice(D_FF // 2, D_FF))


def _ffn_fwd(tag, h, gain, wg, wu, wd, next_shard=None):
    def up(i, h_ref, wg_ref, wu_ref, g_ref, n_ref, b_ref, silu_ref, dsilu_ref, act_ref):
        n = _rms_fwd(h_ref[...], g_ref[...]).astype(BF16)
        n_ref[...] = n
        for cols in (slice(0, D_FF),):
            a = _dot(n, wg_ref[:, cols])
            b = _dot(n, wu_ref[:, cols])
            sg = _sigmoid(a)
            silu = a * sg
            b_ref[:, cols] = b.astype(BF16)
            silu_ref[:, cols] = silu.astype(BF16)
            dsilu_ref[:, cols] = (sg * (1.0 + a * (1.0 - sg))).astype(BF16)
            act_ref[:, cols] = (silu * b).astype(BF16)

    outs = [(D_MODEL, BF16)] + [(D_FF, BF16)] * 4
    gathered = None
    if next_shard is None:
        n, b, silu, dsilu, act = _rowk(tag + "_up", up, [_rows(h)], [wg, wu, gain], outs, tm=192, writes_outs=True)
    else:
        n, b, silu, dsilu, act, got = _rowk(tag + "_up_gather", up, [_rows(h)], [wg, wu, gain], outs, tm=192,
                                            writes_outs=True, hosted=_gather_ici_stage(next_shard))
        gathered = _gather_finish("gather_w", next_shard, got)
    out = _rowk(tag + "_down", lambda i, act_ref, h_ref, wd_ref: h_ref[...] + 0.5 * _dot(act_ref[...], wd_ref[...]),
                [_rows(act), _rows(h)], [wd], [(D_MODEL, F32)])[0]
    return out, (h, n, b, silu, dsilu, act), gathered


def _ffn_bwd(tag, dh, saved, gain, wg, wu, wd, pending=None, sel=None):
    h, n, b, silu, dsilu, act = saved
    hosted1 = hosted2 = reduced = None
    if pending is not None:
        f, hosted1 = _rs_pair_stage(pending)
        tag = tag + "_reduce"

    def bwd1(i, dh_ref, b_ref, silu_ref, dsilu_ref, wd_ref, da_ref, db_ref):
        dhb = (0.5 * dh_ref[...]).astype(BF16)
        for cols in _FF_CHUNKS:
            dact = _dot_nt(dhb, wd_ref[cols, :])
            da_ref[:, cols] = (dact * b_ref[:, cols].astype(F32) * dsilu_ref[:, cols].astype(F32)).astype(BF16)
            db_ref[:, cols] = (dact * silu_ref[:, cols].astype(F32)).astype(BF16)

    res = _rowk(tag + "_bwd_act", bwd1, [_rows(dh), _rows(b), _rows(silu), _rows(dsilu)], [wd], [(D_FF, BF16)] * 2,
                writes_outs=True, hosted=hosted1)
    da, db = res[0], res[1]
    if pending is not None:
        p, hosted2 = _rs_chip_stage("reduce_w", f, res[2], sel)

    def bwd2(i, da_ref, db_ref, h_ref, dh_ref, wg_ref, wu_ref, g_ref):
        dn = _dot_nt(da_ref[...], wg_ref[...]) + _dot_nt(db_ref[...], wu_ref[...])
        dx, dg = _rms_bwd(h_ref[...], g_ref[...], dn)
        return dh_ref[...] + dx, dg

    res = _rowk(tag + "_bwd_in", bwd2, [_rows(da), _rows(db), _rows(h), _rows(dh)], [wg, wu, gain],
                [(D_MODEL, F32)], accs=[(1, D_MODEL)], hosted=hosted2)
    dh_in, dgain = res[0], res[1]
    if pending is not None:
        reduced = _rs_finish("reduce_w", p, res[2], sel)
    dwd = _mm_tn("ffn_dwd", act, dh, tk=D_FF // 2, tn=D_MODEL, scale=0.5)
    dwg = _mm_tn("ffn_dwg", n, da, tk=D_MODEL, tn=D_FF // 2)
    dwu = _mm_tn("ffn_dwu", n, db, tk=D_MODEL, tn=D_FF // 2)
    return dh_in, dgain, dwg, dwu, dwd, reduced


def _mixer_fwd(h, lw, ssm):
    lp = h.shape[0]
    def mix_in(i, h_ref, w_ref, g_ref):
        nv = _rms_fwd(h_ref[...], g_ref[...]).astype(BF16)
        return nv, _dot(nv, w_ref[...])

    n, proj = _rowk("mix_in", mix_in, [_rows(h)], [lw["w_in"], lw["mix_norm"]], [(D_MODEL, BF16), (4 * D_MODEL, F32)])
    yattn, lse = _attn_fwd(proj, lw["attn_sink"])
    u_seg = _to_segments(proj[:, 3 * SSM_WIDTH:4 * SSM_WIDTH])
    xs, ydir = [], []
    for d in range(2):
        x_seg, y_seg = _ssm_fwd_dir(f"ssm_fwd{d}", u_seg, ssm[d]["wb"], ssm[d]["wc"], ssm[d]["a"], ssm[d]["a_seg"],
                                    reverse=(d == 1))
        xs.append(x_seg)
        ydir.append(y_seg)

    def ssm_out(i, y0_ref, y1_ref, u_ref, d_ref, wglu_ref):
        ys = y0_ref[...] + y1_ref[...] + d_ref[...] * u_ref[...]
        z = _gelu(ys)
        return ys, z * _sigmoid(_dot(z, wglu_ref[...]))

    ys, yssm_seg = _rowk("ssm_out", ssm_out, [_rows(ydir[0]), _rows(ydir[1]), _rows(u_seg)],
                         [lw["ssm_d"], lw["ssm_w_glu"]], [(SSM_WIDTH, F32), (SSM_WIDTH, BF16)])
    yssm = _from_segments(yssm_seg)

    def merge(i, ys_ref, ya_ref, gs_ref, ga_ref, wbs_ref, wba_ref):
        bs = _dot(ys_ref[...], wbs_ref[...])
        ba = _dot(ya_ref[...], wba_ref[...])
        m = _sigmoid(gs_ref[...]) * bs + _sigmoid(ga_ref[...]) * ba
        return bs, ba, jnp.where(_row_ids(i, ROW_TILE, D_MODEL) >= PAD, m, 0.0)

    bs, ba, merged = _rowk("mix_merge", merge, [_rows(yssm), _rows(yattn), _rows(proj, D_MODEL, 2), _rows(proj, D_MODEL, 3)],
                           [lw["w_branch_ssm"], lw["w_branch_attn"]], [(D_MODEL, BF16)] * 3)
    out = _rowk("mix_out", lambda i, m_ref, h_ref, w_ref: h_ref[...] + _dot(m_ref[...], w_ref[...]),
                [_rows(merged), _rows(h)], [lw["w_out"]], [(D_MODEL, F32)])[0]
    return out, (h, n, proj, yattn, lse, u_seg, xs, ys, yssm, bs, ba, merged)


def _mixer_bwd(dh, saved, lw, ssm):
    h, n, proj, yattn, lse, u_seg, xs, ys, yssm, bs, ba, merged = saved

    def bwd1(i, dh_ref, gs_ref, ga_ref, bs_ref, ba_ref, w_ref):
        dm = _dot_nt(dh_ref[...], w_ref[...])
        dm = jnp.where(_row_ids(i, ROW_TILE, D_MODEL) >= PAD, dm, 0.0)
        sgs = _sigmoid(gs_ref[...])
        sga = _sigmoid(ga_ref[...])
        return (dm * sgs, dm * sga, dm * bs_ref[...].astype(F32) * sgs * (1.0 - sgs),
                dm * ba_ref[...].astype(F32) * sga * (1.0 - sga))

    dbs, dba, dgs, dga = _rowk("mix_bwd_merge", bwd1,
                               [_rows(dh), _rows(proj, D_MODEL, 2), _rows(proj, D_MODEL, 3), _rows(bs), _rows(ba)],
                               [lw["w_out"]], [(D_MODEL, BF16)] * 4)
    dw_out = _mm_tn("mix_dw_out", merged, dh, tk=D_MODEL, tn=D_MODEL)
    dw_bs = _mm_tn("mix_dw_bs", yssm, dbs, tk=SSM_WIDTH, tn=D_MODEL)
    dw_ba = _mm_tn("mix_dw_ba", yattn, dba, tk=D_MODEL, tn=D_MODEL)

    def bwd2(i, dbs_ref, dba_ref, wbs_ref, wba_ref):
        return _dot_nt(dba_ref[...], wba_ref[...]), _dot_nt(dbs_ref[...], wbs_ref[...])

    dyattn, dyssm = _rowk("mix_bwd_branches", bwd2, [_rows(dbs), _rows(dba)], [lw["w_branch_ssm"], lw["w_branch_attn"]],
                          [(D_MODEL, F32), (SSM_WIDTH, F32)])

    def bwd3(i, dyssm_ref, ys_ref, u_ref, wglu_ref):
        ysv = ys_ref[...]
        z = _gelu(ysv)
        sg = _sigmoid(_dot(z, wglu_ref[...]))
        dt = dyssm_ref[...] * z * sg * (1.0 - sg)
        dz = dyssm_ref[...] * sg + _dot_nt(dt, wglu_ref[...])
        dys = dz * _gelu_grad(ysv)
        return dys, z, dt, jnp.sum(dys * u_ref[...], axis=0, keepdims=True)

    dys, z, dt, dd = _rowk("mix_bwd_ssm_out", bwd3, [_rows(_to_segments(dyssm)), _rows(ys), _rows(u_seg)], [lw["ssm_w_glu"]],
                           [(SSM_WIDTH, F32), (SSM_WIDTH, BF16), (SSM_WIDTH, BF16)], accs=[(1, SSM_WIDTH)])
    dw_glu = _mm_tn("mix_dw_glu", z, dt, tk=SSM_WIDTH, tn=SSM_WIDTH)

    dus, ssm_cot = [], []
    for d in range(2):
        du_d, dwb, dwc_t, ga = _ssm_bwd_dir(f"ssm_bwd{d}", dys, u_seg, xs[d], ssm[d]["wb"], ssm[d]["wc"], ssm[d]["a_conj"],
                                            ssm[d]["a_seg_conj"], fwd_reverse=(d == 1))
        dus.append(du_d)
        ga = jnp.sum(ga, axis=0, keepdims=True)
        ssm_cot.append((ga[:, :N_STATE], ga[:, N_STATE:], dwb.reshape(SUPER, 128, 1024),
                        dwc_t.reshape(SUPER, 128, 1024).transpose(0, 2, 1)))

    du_seg = _rowk("ssm_bwd_du", lambda i, a_ref, b_ref, dys_ref, d_ref: a_ref[...] + b_ref[...] + d_ref[...] * dys_ref[...],
                   [_rows(dus[0]), _rows(dus[1]), _rows(dys)], [lw["ssm_d"]], [(SSM_WIDTH, BF16)])[0]
    du = _from_segments(du_seg)

    dq, dk, dv, dkm, dvm, dsink = _attn_bwd(proj, lw["attn_sink"], yattn, lse, dyattn)

    def dproj_fn(i, dq_ref, dk_ref, dv_ref, du_ref, dgs_ref, dga_ref, dkm_ref, dvm_ref):
        first = jnp.where(i == 0, 1.0, 0.0)
        zeros = lambda r: jnp.zeros((r, N_KV_HEADS * HEAD_DIM), F32)
        place = lambda m: jnp.concatenate([zeros(PAD), m[...] * first, zeros(ROW_TILE - BLOCK)], axis=0)
        dp = jnp.concatenate([dq_ref[...].astype(F32), dk_ref[...] + place(dkm_ref), dv_ref[...] + place(dvm_ref),
                              du_ref[...].astype(F32), dgs_ref[...].astype(F32), dga_ref[...].astype(F32)], axis=1)
        return jnp.where(_row_ids(i, ROW_TILE, 4 * D_MODEL) >= PAD, dp, 0.0)

    dproj = _rowk("mix_bwd_dproj", dproj_fn, [_rows(dq), _rows(dk), _rows(dv), _rows(du), _rows(dgs), _rows(dga)],
                  [dkm, dvm], [(4 * D_MODEL, BF16)])[0]

    def bwd_in(i, dp_ref, h_ref, dh_ref, w_ref, g_ref):
        dx, dg = _rms_bwd(h_ref[...], g_ref[...], _dot_nt(dp_ref[...], w_ref[...]))
        return dh_ref[...] + dx, dg

    dh_in, dgain = _rowk("mix_bwd_in", bwd_in, [_rows(dproj), _rows(h), _rows(dh)], [lw["w_in"], lw["mix_norm"]],
                         [(D_MODEL, F32)], accs=[(1, D_MODEL)])
    dw_in = _mm_tn("mix_dw_in", n, dproj, tk=D_MODEL, tn=2 * D_MODEL)
    grads = {"w_out": dw_out, "w_branch_ssm": dw_bs, "w_branch_attn": dw_ba, "ssm_w_glu": dw_glu, "w_in": dw_in,
             "mix_norm": dgain, "ssm_d": dd, "attn_sink": dsink[0, :N_HEADS]}
    return dh_in, grads, ssm_cot


def _loss_head(h, gain, target):
    lp = h.shape[0]

    def fn(i, h_ref, t_ref, g_ref):
        x = h_ref[...]
        y = _rms_fwd(x, g_ref[...])
        live = jnp.where(i == 0, 0.0, 1.0)
        dy = (y - t_ref[...]) * live
        loss = 0.5 * jnp.sum(dy * dy) / D_MODEL
        dx, dg = _rms_bwd(x, g_ref[...], dy * (1.0 / D_MODEL))
        return dx, jnp.full((1, BLOCK), loss, F32), dg

    tgt = (target, (BLOCK, D_MODEL), lambda i: (jnp.maximum(i - 1, 0), 0))
    return _rowk("loss_head", fn, [_rows(h), tgt], [gain], [(D_MODEL, F32)], accs=[(1, BLOCK), (1, D_MODEL)], tm=BLOCK)


def _adamw(name, w, g, m, v, tm):
    def fn(i, w_ref, g_ref, m_ref, v_ref):
        gv = g_ref[...]
        mn = ADAM_B1 * m_ref[...] + (1.0 - ADAM_B1) * gv
        vn = ADAM_B2 * v_ref[...] + (1.0 - ADAM_B2) * (gv * gv)
        m_hat = mn / (1.0 - ADAM_B1 ** ADAM_STEP)
        v_hat = vn / (1.0 - ADAM_B2 ** ADAM_STEP)
        return -ADAM_LR * (m_hat / (jnp.sqrt(v_hat) + ADAM_EPS) + ADAM_WD * w_ref[...]), mn, vn

    wd = w.shape[1]
    return _rowk(name, fn, [_rows(w), _rows(g), _rows(m), _rows(v)], [], [(wd, F32)] * 3, tm=tm)


def _shard_rows(name):
    return {"ffn1_w_gate": 704, "ffn1_w_up": 704, "ffn1_w_down": 704, "ffn2_w_gate": 704, "ffn2_w_up": 704, "ffn2_w_down": 704,
            "w_in": 1024, "ssm_w_glu": 64, "w_branch_ssm": 128, "w_branch_attn": 256, "w_out": 256}[name]


def _full_shape(name):
    return {"ffn1_w_gate": (D_MODEL, D_FF), "ffn1_w_up": (D_MODEL, D_FF), "ffn1_w_down": (D_FF, D_MODEL),
            "ffn2_w_gate": (D_MODEL, D_FF), "ffn2_w_up": (D_MODEL, D_FF), "ffn2_w_down": (D_FF, D_MODEL),
            "w_in": (D_MODEL, 4 * D_MODEL), "ssm_w_glu": (SSM_WIDTH, SSM_WIDTH), "w_branch_ssm": (SSM_WIDTH, D_MODEL),
            "w_branch_attn": (D_MODEL, D_MODEL), "w_out": (D_MODEL, D_MODEL)}[name]


def _unflatten_gathered(gathered):
    out, r0 = {}, 0
    for name in BIG:
        r = _shard_rows(name)
        k, nn = _full_shape(name)
        piece = gathered[:, r0:r0 + r, :]
        if name in COL_SHARDED:
            out[name] = piece.reshape(4, k, nn // 4).transpose(1, 0, 2).reshape(k, nn)
        else:
            out[name] = piece.reshape(k, nn)
        r0 += r
    return out


def _flatten_full(grads):
    per_shard = []
    for s in range(4):
        pieces = []
        for name in BIG:
            k, nn = _full_shape(name)
            g = grads[name]
            piece = g[:, s * (nn // 4):(s + 1) * (nn // 4)] if name in COL_SHARDED else g[s * (k // 4):(s + 1) * (k // 4), :]
            pieces.append(piece.reshape(-1, 1024))
        per_shard.append(jnp.concatenate(pieces, axis=0))
    f = jnp.stack(per_shard)
    return f.reshape(4, 2, f.shape[1] // 2, 1024).transpose(1, 0, 2, 3)


def _shard_2d(a):
    return a.reshape(-1, a.shape[-1])


def kernel(x, meta_tokens, ffn1_norm, ffn1_w_gate, ffn1_w_up, ffn1_w_down, mix_norm, w_in, ssm_lam_re, ssm_lam_im, ssm_log_dt, ssm_b_re, ssm_b_im, ssm_c_re, ssm_c_im, ssm_d, ssm_w_glu, attn_sink, w_branch_ssm, w_branch_attn, w_out, ffn2_norm, ffn2_w_gate, ffn2_w_up, ffn2_w_down, final_norm, loss_target, m_meta_tokens, m_ffn1_norm, m_ffn1_w_gate, m_ffn1_w_up, m_ffn1_w_down, m_mix_norm, m_w_in, m_ssm_lam_re, m_ssm_lam_im, m_ssm_log_dt, m_ssm_b_re, m_ssm_b_im, m_ssm_c_re, m_ssm_c_im, m_ssm_d, m_ssm_w_glu, m_attn_sink, m_w_branch_ssm, m_w_branch_attn, m_w_out, m_ffn2_norm, m_ffn2_w_gate, m_ffn2_w_up, m_ffn2_w_down, m_final_norm, v_meta_tokens, v_ffn1_norm, v_ffn1_w_gate, v_ffn1_w_up, v_ffn1_w_down, v_mix_norm, v_w_in, v_ssm_lam_re, v_ssm_lam_im, v_ssm_log_dt, v_ssm_b_re, v_ssm_b_im, v_ssm_c_re, v_ssm_c_im, v_ssm_d, v_ssm_w_glu, v_attn_sink, v_w_branch_ssm, v_w_branch_attn, v_w_out, v_ffn2_norm, v_ffn2_w_gate, v_ffn2_w_up, v_ffn2_w_down, v_final_norm):
    args = dict(locals())
    w = {k: args[k] for k in WEIGHTS}
    mom = {k: args["m_" + k] for k in WEIGHTS}
    var = {k: args["v_" + k] for k in WEIGHTS}
    depth = ffn1_norm.shape[0]
    seq = x.shape[1]
    xi, yi, ci = lax.axis_index("x"), lax.axis_index("y"), lax.axis_index("c")
    chip = 2 * xi + yi
    sel = jnp.stack([ci, chip]).astype(jnp.int32)

    same_core = [(fx, fy, 0) for fx, fy in _OTHER_CHIPS]
    meta_all = _all_gather_all("gather_meta", meta_tokens, 4, same_core, lambda x_, y_, c_: _chip(x_, y_))
    meta_full = meta_all.transpose(1, 0, 2).reshape(N_META, D_MODEL)
    flat_w = [jnp.concatenate([w[name][l].reshape(-1, 1024) for name in BIG], axis=0).astype(BF16) for l in range(depth)]

    def layer_weights(l, gathered):
        lw = _unflatten_gathered(gathered)
        for name in ("ffn1_norm", "mix_norm", "ffn2_norm"):
            lw[name] = w[name][l].reshape(1, D_MODEL)
        lw["ssm_d"] = ssm_d[l].reshape(1, SSM_WIDTH)
        lw["attn_sink"] = attn_sink[l]
        return lw

    layer_w = [layer_weights(0, _all_gather_shards("gather_w", flat_w[0]))]

    ssm_params = ("ssm_lam_re", "ssm_lam_im", "ssm_log_dt", "ssm_b_re", "ssm_b_im", "ssm_c_re", "ssm_c_im")
    ssm, ssm_vjp = [], []
    for l in range(depth):
        dirs, vjps = [], []
        for d in range(2):
            prm = tuple(w[k][l, d] for k in ssm_params)
            (ar, ai, wb, wc), pull = jax.vjp(_ssm_prep, *prm)
            a_seg = _complex_power(ar, ai, (seq + BLOCK) // N_SEG)
            conj = lambda v: jnp.concatenate([v[0], -v[1]], axis=1)
            pack = lambda v: jnp.concatenate([v[0], v[1]], axis=1)
            dirs.append({"wb": wb, "wc": wc, "a": pack((ar, ai)), "a_seg": pack(a_seg),
                         "a_conj": conj((ar, ai)), "a_seg_conj": conj(a_seg)})
            vjps.append(pull)
        ssm.append(dirs)
        ssm_vjp.append(vjps)

    h = jnp.concatenate([jnp.zeros((PAD, D_MODEL), F32), meta_full, x[0]], axis=0)
    saved = []
    for l in range(depth):
        lw = layer_w[l]
        h, s1, gathered = _ffn_fwd("ffn", h, lw["ffn1_norm"], lw["ffn1_w_gate"], lw["ffn1_w_up"], lw["ffn1_w_down"],
                                   next_shard=flat_w[l + 1] if l + 1 < depth else None)
        if gathered is not None:
            layer_w.append(layer_weights(l + 1, gathered))
        h, s2 = _mixer_fwd(h, lw, ssm[l])
        h, s3, _ = _ffn_fwd("ffn", h, lw["ffn2_norm"], lw["ffn2_w_gate"], lw["ffn2_w_up"], lw["ffn2_w_down"])
        saved.append((s1, s2, s3))
    dh, loss_part, d_final = _loss_head(h, final_norm.reshape(1, D_MODEL), loss_target[0])
    loss = lax.psum(loss_part[0, 0], MESH_AXES)

    small_g = {k: [None] * depth for k in SMALL if k not in ("meta_tokens", "final_norm")}
    big_g = {k: [None] * depth for k in BIG}
    def keep_shard(l, reduced):
        r0 = 0
        for name in BIG:
            r = _shard_rows(name)
            big_g[name][l] = reduced[r0:r0 + r].reshape(w[name].shape[1:])
            r0 += r

    pending = None
    for l in reversed(range(depth)):
        lw = layer_w[l]
        s1, s2, s3 = saved[l]
        full = {}
        dh, dg, full["ffn2_w_gate"], full["ffn2_w_up"], full["ffn2_w_down"], reduced = _ffn_bwd(
            "ffn", dh, s3, lw["ffn2_norm"], lw["ffn2_w_gate"], lw["ffn2_w_up"], lw["ffn2_w_down"], pending, sel)
        if pending is not None:
            keep_shard(l + 1, reduced)
        small_g["ffn2_norm"][l] = dg[0]
        dh, mg, ssm_cot = _mixer_bwd(dh, s2, lw, ssm[l])
        for k in ("w_out", "w_branch_ssm", "w_branch_attn", "ssm_w_glu", "w_in"):
            full[k] = mg[k]
        small_g["mix_norm"][l] = mg["mix_norm"][0]
        small_g["ssm_d"][l] = mg["ssm_d"][0]
        small_g["attn_sink"][l] = mg["attn_sink"]
        per_dir = []
        for d in range(2):
            per_dir.append(ssm_vjp[l][d](ssm_cot[d]))
        for j, k in enumerate(ssm_params):
            small_g[k][l] = jnp.stack([per_dir[0][j], per_dir[1][j]])
        dh, dg, full["ffn1_w_gate"], full["ffn1_w_up"], full["ffn1_w_down"], _ = _ffn_bwd(
            "ffn", dh, s1, lw["ffn1_norm"], lw["ffn1_w_gate"], lw["ffn1_w_up"], lw["ffn1_w_down"])
        small_g["ffn1_norm"][l] = dg[0]
        pending = _flatten_full(full)
    keep_shard(0, _reduce_scatter("reduce_w", pending, sel))

    grad_x = dh[BLOCK:][None]
    small_list = [dh[PAD:BLOCK].reshape(-1)]
    for k in SMALL[1:]:
        small_list.append(d_final.reshape(-1) if k == "final_norm" else jnp.stack(small_g[k]).reshape(-1))
    small_vec = jnp.concatenate(small_list)
    n_small = small_vec.shape[0]
    rows_small = -(-n_small // (64 * 1024)) * 64
    small_vec = jnp.pad(small_vec, (0, rows_small * 1024 - n_small)).reshape(rows_small, 1024)
    everyone = [(fx, fy, fc) for fx in (0, 1) for fy in (0, 1) for fc in (0, 1)][1:]
    small_all = _all_gather_all("gather_small", small_vec, 8, everyone, lambda x_, y_, c_: 4 * x_ + 2 * y_ + c_)
    small_sum = _slot_sum("sum_small", [(small_all, k) for k in range(8)], sel, 64).reshape(-1)

    grads, deltas, new_m, new_v = {}, {}, {}, {}
    off = 0
    flat_w, flat_m, flat_v, flat_g = [], [], [], []
    for k in SMALL:
        size = (N_META * D_MODEL) if k == "meta_tokens" else int(np.prod(w[k].shape))
        g = small_sum[off:off + size]
        off += size
        if k == "meta_tokens":
            g = lax.dynamic_slice(g.reshape(N_META, D_MODEL), (0, chip * (D_MODEL // 4)), (N_META, D_MODEL // 4))
            grads[k] = g
            deltas[k], new_m[k], new_v[k] = _adamw("adamw_meta", w[k], g, mom[k], var[k], N_META)
        else:
            grads[k] = g.reshape(w[k].shape)
            flat_g.append(g)
            flat_w.append(w[k].reshape(-1))
            flat_m.append(mom[k].reshape(-1))
            flat_v.append(var[k].reshape(-1))
    n_flat = sum(a.shape[0] for a in flat_g)
    rows_flat = -(-n_flat // (64 * 1024)) * 64
    pack = lambda parts, fill: jnp.pad(jnp.concatenate(parts), (0, rows_flat * 1024 - n_flat),
                                       constant_values=fill).reshape(rows_flat, 1024)
    sd, sm_, sv = _adamw("adamw_small", pack(flat_w, 0.0), pack(flat_g, 0.0), pack(flat_m, 0.0), pack(flat_v, 1.0), 64)
    off = 0
    for k in SMALL:
        if k == "meta_tokens":
            continue
        size = int(np.prod(w[k].shape))
        for dst, src in ((deltas, sd), (new_m, sm_), (new_v, sv)):
            dst[k] = src.reshape(-1)[off:off + size].reshape(w[k].shape)
        off += size
    for k in BIG:
        g = jnp.stack(big_g[k])
        grads[k] = g
        rows_k = _shard_2d(g).shape[0]
        tm = 512 if rows_k % 512 == 0 else rows_k // depth
        d_, m_, v_ = _adamw("adamw_" + k, _shard_2d(w[k]), _shard_2d(g), _shard_2d(mom[k]), _shard_2d(var[k]), tm)
        deltas[k], new_m[k], new_v[k] = d_.reshape(g.shape), m_.reshape(g.shape), v_.reshape(g.shape)

    return (loss, grad_x, *[grads[k] for k in WEIGHTS], *[deltas[k] for k in WEIGHTS],
            *[new_m[k] for k in WEIGHTS], *[new_v[k] for k in WEIGHTS])
```

```python
import functools
import math

import numpy as np
import jax
import jax.numpy as jnp
from jax import lax
from jax.experimental import pallas as pl
from jax.experimental.pallas import tpu as pltpu

F32 = jnp.float32
BF16 = jnp.bfloat16

D_MODEL = 1024
N_META = 16
N_HEADS = 16
N_KV_HEADS = 4
HEAD_DIM = 64
Q_GROUP = N_HEADS // N_KV_HEADS
WINDOW = 128
BLOCK = 128
PAD = BLOCK - N_META
SSM_WIDTH = 512
SSM_GROUP_CH = 16
SSM_GROUPS = 32
SSM_STATE = 64
N_STATE = SSM_GROUPS * SSM_STATE
SUPER = 4
D_FF = 2816
EPS = 1e-6
NEG = -1e30
ATTN_SCALE = HEAD_DIM ** -0.5
SLOPES = [float(2.0 ** (-8.0 * (h + 1) / N_HEADS)) for h in range(N_HEADS)]

ADAM_LR = 0.001
ADAM_B1 = 0.9
ADAM_B2 = 0.999
ADAM_EPS = 1e-08
ADAM_WD = 0.01
ADAM_STEP = 10

V7X_VMEM_LIMIT_BYTES = 52 * 1024 * 1024
ROW_TILE = 384
LONG_ROW_TILE = 1376
MESH_AXES = ("x", "y", "c")

BIG = ["ffn1_w_gate", "ffn1_w_up", "ffn1_w_down", "ffn2_w_gate", "ffn2_w_up", "ffn2_w_down",
       "w_in", "ssm_w_glu", "w_branch_ssm", "w_branch_attn", "w_out"]
COL_SHARDED = {"ffn1_w_gate", "ffn1_w_up", "ffn2_w_gate", "ffn2_w_up", "w_in", "w_branch_ssm"}
SMALL = ["meta_tokens", "ffn1_norm", "mix_norm", "ffn2_norm", "final_norm", "ssm_lam_re", "ssm_lam_im", "ssm_log_dt",
         "ssm_b_re", "ssm_b_im", "ssm_c_re", "ssm_c_im", "ssm_d", "attn_sink"]
WEIGHTS = ["meta_tokens", "ffn1_norm", "ffn1_w_gate", "ffn1_w_up", "ffn1_w_down", "mix_norm", "w_in", "ssm_lam_re",
           "ssm_lam_im", "ssm_log_dt", "ssm_b_re", "ssm_b_im", "ssm_c_re", "ssm_c_im", "ssm_d", "ssm_w_glu", "attn_sink",
           "w_branch_ssm", "w_branch_attn", "w_out", "ffn2_norm", "ffn2_w_gate", "ffn2_w_up", "ffn2_w_down", "final_norm"]


def _dot(a, b):
    return lax.dot_general(a.astype(BF16), b.astype(BF16), (((1,), (0,)), ((), ())), preferred_element_type=F32)


def _dot_nt(a, b):
    return lax.dot_general(a.astype(BF16), b.astype(BF16), (((1,), (1,)), ((), ())), preferred_element_type=F32)


def _dot_tn(a, b):
    return lax.dot_general(a.astype(BF16), b.astype(BF16), (((0,), (0,)), ((), ())), preferred_element_type=F32)


def _sigmoid(x):
    return 0.5 * jnp.tanh(0.5 * x) + 0.5


_GELU_C = math.sqrt(2.0 / math.pi)


def _gelu(x):
    return 0.5 * x * (1.0 + jnp.tanh(_GELU_C * (x + 0.044715 * x * x * x)))


def _gelu_grad(x):
    th = jnp.tanh(_GELU_C * (x + 0.044715 * x * x * x))
    return 0.5 * (1.0 + th) + 0.5 * x * (1.0 - th * th) * _GELU_C * (1.0 + 3.0 * 0.044715 * x * x)


def _rms_fwd(x, g):
    r = lax.rsqrt(jnp.mean(x * x, axis=-1, keepdims=True) + EPS)
    return x * r * g


def _rms_bwd(x, g, dn):
    r = lax.rsqrt(jnp.mean(x * x, axis=-1, keepdims=True) + EPS)
    xh = x * r
    t = dn * g
    dx = r * (t - xh * jnp.mean(t * xh, axis=-1, keepdims=True))
    return dx, jnp.sum(dn * xh, axis=0, keepdims=True)


def _compiler_params():
    return pltpu.CompilerParams(dimension_semantics=("arbitrary",), vmem_limit_bytes=V7X_VMEM_LIMIT_BYTES)


def _rows(arr, width=None, cb=0):
    return (arr, arr.shape[1] if width is None else width, cb)


def _rowk(name, fn, rows, fulls, outs, accs=(), tm=ROW_TILE, smem=(), n_rows=None, hosted=None, writes_outs=False):
    n = rows[0][0].shape[0] if n_rows is None else n_rows
    assert n % tm == 0, (name, n, tm)
    in_specs, args = [], []
    for s in smem:
        in_specs.append(pl.BlockSpec(memory_space=pltpu.SMEM))
        args.append(s)
    for r in rows:
        if callable(r[2]):
            in_specs.append(pl.BlockSpec(r[1], r[2]))
        else:
            in_specs.append(pl.BlockSpec((tm, r[1]), functools.partial(lambda i, cb: (i, cb), cb=r[2])))
        args.append(r[0])
    for f in fulls:
        in_specs.append(pl.BlockSpec(memory_space=pl.ANY))
        args.append(f)
    nh = 0 if hosted is None else 1
    if nh:
        in_specs.append(pl.BlockSpec(memory_space=pl.ANY))
        args.append(hosted[0])
    out_specs, out_shape = [], []
    for w, dt in outs:
        out_specs.append(pl.BlockSpec((tm, w), lambda i: (i, 0)))
        out_shape.append(jax.ShapeDtypeStruct((n, w), dt))
    for shp in accs:
        out_specs.append(pl.BlockSpec(shp, functools.partial(lambda i, nd: (0,) * nd, nd=len(shp))))
        out_shape.append(jax.ShapeDtypeStruct(shp, F32))
    if nh:
        out_specs.append(pl.BlockSpec(memory_space=pl.ANY))
        out_shape.append(jax.ShapeDtypeStruct((hosted[1],) + hosted[0].shape[1:], hosted[0].dtype))
    ns, nr, nf, no, na = len(smem), len(rows), len(fulls), len(outs), len(accs)
    scratch = [pltpu.VMEM(f.shape, f.dtype) for f in fulls]
    if nf:
        scratch.append(pltpu.SemaphoreType.DMA((nf,)))
    if nh:
        scratch += [pltpu.SemaphoreType.DMA((len(hosted[2]),)), pltpu.SemaphoreType.DMA((len(hosted[2]),))]
    steps = n // tm

    def body(*refs):
        i = pl.program_id(0)
        refs = list(refs)
        take = lambda k: [refs.pop(0) for _ in range(k)]
        sm, rr, fh, hsrc, oo, aa, hout, fv = take(ns), take(nr), take(nf), take(nh), take(no), take(na), take(nh), take(nf)
        if nf:
            sem = refs.pop(0)

            @pl.when(i == 0)
            def _():
                cps = [pltpu.make_async_copy(fh[j], fv[j], sem.at[j]) for j in range(nf)]
                for cp in cps:
                    cp.start()
                for cp in cps:
                    cp.wait()
        if nh:
            @pl.when(i == 0)
            def _():
                for cp in _remote_copies(hsrc[0], hout[0], refs[0], refs[1], hosted[2]):
                    cp.start()
        if writes_outs:
            res = fn(i, *sm, *rr, *fv, *oo)
            res = (None,) * no + (tuple(res) if isinstance(res, (tuple, list)) else ())
        else:
            res = fn(i, *sm, *rr, *fv)
            res = tuple(res) if isinstance(res, (tuple, list)) else (res,)
            for o, v in zip(oo, res[:no]):
                o[...] = v.astype(o.dtype)
        if na:
            @pl.when(i == 0)
            def _():
                for a in aa:
                    a[...] = jnp.zeros_like(a)
            for a, v in zip(aa, res[no:]):
                a[...] += v
        if nh:
            @pl.when(i == steps - 1)
            def _():
                for cp in _remote_copies(hsrc[0], hout[0], refs[0], refs[1], hosted[2]):
                    cp.wait()

    return pl.pallas_call(body, grid=(steps,), in_specs=in_specs, out_specs=out_specs, out_shape=out_shape,
                          scratch_shapes=scratch, name=name, compiler_params=_compiler_params())(*args)


def _mm_tn(name, x, y, *, xw=None, xcb=0, tk, tn, scale=1.0, tm=None):
    m = x.shape[0]
    if tm is None:
        tm = LONG_ROW_TILE if m % LONG_ROW_TILE == 0 else ROW_TILE
    k = x.shape[1] if xw is None else xw
    nn = y.shape[1]
    assert m % tm == 0 and k % tk == 0 and nn % tn == 0, (name, m, k, nn)
    kb0 = (xcb * k) // tk

    def body(x_ref, y_ref, o_ref):
        @pl.when(pl.program_id(2) == 0)
        def _():
            o_ref[...] = jnp.zeros_like(o_ref)
        yv = y_ref[...]
        if scale != 1.0:
            yv = yv * scale
        o_ref[...] += _dot_tn(x_ref[...], yv)

    return pl.pallas_call(
        body, grid=(k // tk, nn // tn, m // tm),
        in_specs=[pl.BlockSpec((tm, tk), lambda a, b, i: (i, kb0 + a)), pl.BlockSpec((tm, tn), lambda a, b, i: (i, b))],
        out_specs=pl.BlockSpec((tk, tn), lambda a, b, i: (a, b)), out_shape=jax.ShapeDtypeStruct((k, nn), F32), name=name,
        compiler_params=pltpu.CompilerParams(dimension_semantics=("arbitrary", "arbitrary", "arbitrary"),
                                             vmem_limit_bytes=V7X_VMEM_LIMIT_BYTES))(x, y)


def _remote_copies(src_ref, out_ref, ssem, rsem, sends):
    x, y, c = lax.axis_index("x"), lax.axis_index("y"), lax.axis_index("c")
    cps = []
    for k, ((fx, fy, fc), sf, df) in enumerate(sends):
        peer = (1 - x if fx else x, 1 - y if fy else y, 1 - c if fc else c)
        cps.append(pltpu.make_async_remote_copy(src_ref=src_ref.at[sf(x, y, c)], dst_ref=out_ref.at[df(x, y, c)],
                                                send_sem=ssem.at[k], recv_sem=rsem.at[k], device_id=peer,
                                                device_id_type=pl.DeviceIdType.MESH))
    return cps


def _exchange(name, src, n_out, local, sends, alias=False):
    nl, nsnd = len(local), len(sends)
    out_shape = jax.ShapeDtypeStruct((n_out,) + src.shape[1:], src.dtype)

    def body(src_ref, out_ref, lsem, ssem, rsem):
        x, y, c = lax.axis_index("x"), lax.axis_index("y"), lax.axis_index("c")
        cps = [pltpu.make_async_copy(src_ref.at[sf(x, y, c)], out_ref.at[df(x, y, c)], lsem.at[j])
               for j, (sf, df) in enumerate(local)]
        cps += _remote_copies(src_ref, out_ref, ssem, rsem, sends)
        for cp in cps:
            cp.start()
        for cp in cps:
            cp.wait()

    return pl.pallas_call(
        body, in_specs=[pl.BlockSpec(memory_space=pl.ANY)], out_specs=pl.BlockSpec(memory_space=pl.ANY), out_shape=out_shape,
        scratch_shapes=[pltpu.SemaphoreType.DMA((max(nl, 1),)), pltpu.SemaphoreType.DMA((nsnd,)), pltpu.SemaphoreType.DMA((nsnd,))],
        input_output_aliases=({0: 0} if alias else {}), name=name)(src)


def _chip(x, y):
    return 2 * x + y


_OTHER_CHIPS = [(1, 0), (0, 1), (1, 1)]


def _all_gather_shards(name, shard):
    src, n_out, sends = _gather_ici_stage(shard)
    return _gather_finish(name, shard, _exchange(name + "_ici", src, n_out, [], sends))


def _gather_ici_stage(shard):
    r, w = shard.shape
    first = [((fx, fy, 0), lambda x, y, c: c, lambda x, y, c: 2 * _chip(x, y) + c) for fx, fy in _OTHER_CHIPS]
    return shard.reshape(2, r // 2, w), 8, first


def _gather_finish(name, shard, g):
    r, w = shard.shape
    second = [((0, 0, 1),
               (lambda x, y, c, fx=fx, fy=fy: 2 * _chip(x ^ fx, y ^ fy) + c),
               (lambda x, y, c, fx=fx, fy=fy: 2 * _chip(x ^ fx, y ^ fy) + c)) for fx, fy in _OTHER_CHIPS]
    g = _exchange(name + "_d2d", g, 8, [], second, alias=True).reshape(4, r, w)
    mine = lax.broadcasted_iota(jnp.int32, (4, 1, 1), 0) == _chip(lax.axis_index("x"), lax.axis_index("y"))
    return jnp.where(mine, shard[None], g)


def _slot_sum(name, terms, sel, tm, out_slots=None, out_slot=None, also_bf16=False):
    rows, w = terms[0][0].shape[1:]

    def imap(slot):
        if isinstance(slot, int):
            return lambda i, s: (slot, i, 0)
        return lambda i, s: (s[slot[1]], i, 0)

    in_specs = [pl.BlockSpec((None, tm, w), imap(sl)) for _, sl in terms]
    if out_slots is None:
        out_specs, out_shape = [pl.BlockSpec((tm, w), lambda i, s: (i, 0))], [jax.ShapeDtypeStruct((rows, w), F32)]
    else:
        out_specs = [pl.BlockSpec((None, tm, w), imap(out_slot))]
        out_shape = [jax.ShapeDtypeStruct((out_slots, rows, w), F32)]
    if also_bf16:
        out_specs.append(pl.BlockSpec((tm, w), lambda i, s: (i, 0)))
        out_shape.append(jax.ShapeDtypeStruct((rows, w), BF16))
    n_in = len(terms)

    def body(sel_ref, *refs):
        acc = refs[0][...].astype(F32)
        for r in refs[1:n_in]:
            acc = acc + r[...].astype(F32)
        refs[n_in][...] = acc
        if also_bf16:
            refs[n_in + 1][...] = acc.astype(BF16)

    grid_spec = pltpu.PrefetchScalarGridSpec(num_scalar_prefetch=1, grid=(rows // tm,), in_specs=in_specs, out_specs=out_specs)
    out = pl.pallas_call(body, grid_spec=grid_spec, out_shape=out_shape, name=name,
                         compiler_params=_compiler_params())(sel, *[a for a, _ in terms])
    return out if also_bf16 else out[0]


def _reduce_scatter(name, parts, sel):
    f, (src, n_out, sends) = _rs_pair_stage(parts)
    p, (src2, n_out2, sends2) = _rs_chip_stage(name, f, _exchange(name + "_d2d", src, n_out, [], sends), sel)
    return _rs_finish(name, p, _exchange(name + "_ici", src2, n_out2, [], sends2), sel)


def _rs_pair_stage(parts):
    _, _, r, w = parts.shape
    f = parts.reshape(2, 4 * r, w)
    return f, (f, 1, [((0, 0, 1), lambda x, y, c: 1 - c, lambda x, y, c: 0)])


def _rs_chip_stage(name, f, got, sel):
    r, w = f.shape[1] // 4, f.shape[2]
    p, p16 = _slot_sum(name + "_add2", [(f, ("sel", 0)), (got, 0)], sel, 384, also_bf16=True)
    sends = [((fx, fy, 0), (lambda x, y, c, fx=fx, fy=fy: _chip(x ^ fx, y ^ fy)), (lambda x, y, c, k=k: k))
             for k, (fx, fy) in enumerate(_OTHER_CHIPS)]
    return p.reshape(4, r, w), (p16.reshape(4, r, w), 3, sends)


def _rs_finish(name, p, got, sel):
    _, r, w = p.shape
    q = _slot_sum(name + "_add4", [(p, ("sel", 1)), (got, 0), (got, 1), (got, 2)], sel, 496, out_slots=2, out_slot=("sel", 0))
    q = _exchange(name + "_pair", q, 2, [], [((0, 0, 1), lambda x, y, c: c, lambda x, y, c: c)], alias=True)
    return q.reshape(2 * r, w)


def _all_gather_all(name, vec, n_slots, flips, slot_fn):
    sends = [(f, lambda x, y, c: 0, slot_fn) for f in flips]
    g = _exchange(name, vec[None], n_slots, [], sends)
    mine = lax.broadcasted_iota(jnp.int32, (n_slots, 1, 1), 0) == slot_fn(*(lax.axis_index(a) for a in MESH_AXES))
    return jnp.where(mine, vec[None], g)


def _nbr_specs(arr, width, cb, nb):
    return [
        (arr, (BLOCK, width), functools.partial(lambda n, cb: (jnp.maximum(n - 1, 0), cb), cb=cb)),
        (arr, (BLOCK, width), functools.partial(lambda n, cb: (n, cb), cb=cb)),
        (arr, (BLOCK, width), functools.partial(lambda n, cb: (jnp.minimum(n + 1, nb - 1), cb), cb=cb)),
    ]


def _head(h):
    return slice(h * HEAD_DIM, (h + 1) * HEAD_DIM)


def _row_group(n_groups, rows_per_group):
    r = lax.broadcasted_iota(jnp.int32, (n_groups * rows_per_group, 1), 0)
    grp = jnp.zeros_like(r)
    for g in range(1, n_groups):
        grp = grp + jnp.where(r >= g * rows_per_group, 1, 0)
    return grp


def _by_group(grp, vals):
    out = vals[-1]
    for g in range(len(vals) - 2, -1, -1):
        out = jnp.where(grp == g, vals[g], out)
    return out


def _attn_fwd(proj, sink):
    lp = proj.shape[0]
    nb = lp // BLOCK
    kv_w = N_KV_HEADS * HEAD_DIM
    specs = _nbr_specs(proj, kv_w, 4, nb) + _nbr_specs(proj, kv_w, 5, nb)
    specs += [(proj, (BLOCK, kv_w), lambda n: (0, 4)), (proj, (BLOCK, kv_w), lambda n: (0, 5))]
    in_specs = [pl.BlockSpec(memory_space=pltpu.SMEM), pl.BlockSpec((BLOCK, D_MODEL), lambda n: (n, 0))]
    in_specs += [pl.BlockSpec(s[1], s[2]) for s in specs]

    def body(sink_ref, q_ref, kp, kc, kn, vp, vc, vn, km, vm, o_ref, lse_ref):
        n = pl.program_id(0)
        qi = lax.broadcasted_iota(jnp.int32, (BLOCK, 3 * BLOCK), 0)
        sj = lax.broadcasted_iota(jnp.int32, (BLOCK, 3 * BLOCK), 1)
        dist = jnp.abs(qi + BLOCK - sj)
        kpos = (n - 1) * BLOCK + sj
        valid = (dist <= WINDOW) & (kpos >= BLOCK) & (kpos < lp)
        distf = dist.astype(F32)
        kb = jnp.concatenate([kp[...], kc[...], kn[...]], axis=0).astype(BF16)
        vb = jnp.concatenate([vp[...], vc[...], vn[...]], axis=0).astype(BF16)
        kmeta = km[PAD:BLOCK, :].astype(BF16)
        vmeta = vm[PAD:BLOCK, :].astype(BF16)
        valid4 = jnp.concatenate([valid] * Q_GROUP, axis=0)
        distf4 = jnp.concatenate([distf] * Q_GROUP, axis=0)
        grp = _row_group(Q_GROUP, BLOCK)
        for kh in range(N_KV_HEADS):
            ksl = slice(kh * HEAD_DIM, (kh + 1) * HEAD_DIM)
            heads = [kh * Q_GROUP + g for g in range(Q_GROUP)]
            slope = _by_group(grp, [SLOPES[h] for h in heads])
            sk = _by_group(grp, [sink_ref[h] for h in heads])
            q4 = (jnp.concatenate([q_ref[:, _head(h)] for h in heads], axis=0) * ATTN_SCALE).astype(BF16)
            s = jnp.where(valid4, _dot_nt(q4, kb[:, ksl]) - slope * distf4, NEG)
            sm = _dot_nt(q4, kmeta[:, ksl])
            m = jnp.maximum(jnp.maximum(jnp.max(s, axis=1, keepdims=True), jnp.max(sm, axis=1, keepdims=True)), sk)
            e = jnp.exp(s - m)
            em = jnp.exp(sm - m)
            den = jnp.sum(e, axis=1, keepdims=True) + jnp.sum(em, axis=1, keepdims=True) + jnp.exp(sk - m)
            o4 = (_dot(e, vb[:, ksl]) + _dot(em, vmeta[:, ksl])) * (1.0 / den)
            lse4 = m + jnp.log(den)
            for g, h in enumerate(heads):
                o_ref[:, _head(h)] = o4[g * BLOCK:(g + 1) * BLOCK].astype(o_ref.dtype)
                lse_ref[:, h:h + 1] = lse4[g * BLOCK:(g + 1) * BLOCK]

    return pl.pallas_call(
        body, grid=(nb,), in_specs=in_specs,
        out_specs=[pl.BlockSpec((BLOCK, D_MODEL), lambda n: (n, 0)), pl.BlockSpec((BLOCK, N_HEADS), lambda n: (n, 0))],
        out_shape=[jax.ShapeDtypeStruct((lp, D_MODEL), BF16), jax.ShapeDtypeStruct((lp, N_HEADS), F32)],
        name="attn_fwd", compiler_params=_compiler_params())(sink, proj, *[s[0] for s in specs])


def _attn_delta(do, o):
    lp = do.shape[0]
    sel = (lax.broadcasted_iota(jnp.int32, (N_HEADS, D_MODEL), 1) // HEAD_DIM
           == lax.broadcasted_iota(jnp.int32, (N_HEADS, D_MODEL), 0)).astype(BF16)

    def body(do_ref, o_ref, sel_ref, d_ref, dt_ref):
        prod = do_ref[...] * o_ref[...].astype(F32)
        hi = prod.astype(BF16)
        lo = (prod - hi.astype(F32)).astype(BF16)
        d_ref[...] = _dot_nt(hi, sel_ref[...]) + _dot_nt(lo, sel_ref[...])
        dt_ref[...] = _dot_nt(sel_ref[...], hi) + _dot_nt(sel_ref[...], lo)

    return pl.pallas_call(
        body, grid=(lp // BLOCK,),
        in_specs=[pl.BlockSpec((BLOCK, D_MODEL), lambda n: (n, 0)), pl.BlockSpec((BLOCK, D_MODEL), lambda n: (n, 0)),
                  pl.BlockSpec((N_HEADS, D_MODEL), lambda n: (0, 0))],
        out_specs=[pl.BlockSpec((BLOCK, N_HEADS), lambda n: (n, 0)), pl.BlockSpec((N_HEADS, BLOCK), lambda n: (0, n))],
        out_shape=[jax.ShapeDtypeStruct((lp, N_HEADS), F32), jax.ShapeDtypeStruct((N_HEADS, lp), F32)],
        name="attn_delta", compiler_params=_compiler_params())(do, o, sel)


def _attn_bwd(proj, sink, o, lse, do):
    lp = proj.shape[0]
    nb = lp // BLOCK
    kv_w = N_KV_HEADS * HEAD_DIM
    delta, delta_t = _attn_delta(do, o)
    lse_t = lse.T
    row_nbrs = lambda arr: [
        (arr, (N_HEADS, BLOCK), lambda n: (0, jnp.maximum(n - 1, 0))), (arr, (N_HEADS, BLOCK), lambda n: (0, n)),
        (arr, (N_HEADS, BLOCK), lambda n: (0, jnp.minimum(n + 1, nb - 1)))]
    specs = (_nbr_specs(proj, D_MODEL, 0, nb) + _nbr_specs(proj, kv_w, 4, nb) + _nbr_specs(proj, kv_w, 5, nb)
             + [(proj, (BLOCK, kv_w), lambda n: (0, 4)), (proj, (BLOCK, kv_w), lambda n: (0, 5))]
             + _nbr_specs(do, D_MODEL, 0, nb) + [(lse, (BLOCK, N_HEADS), lambda n: (n, 0)), (delta, (BLOCK, N_HEADS), lambda n: (n, 0))]
             + row_nbrs(lse_t) + row_nbrs(delta_t))
    in_specs = [pl.BlockSpec(memory_space=pltpu.SMEM)] + [pl.BlockSpec(s[1], s[2]) for s in specs]

    def body(sink_ref, qp, qc, qn, kp, kc, kn, vp, vc, vn, km, vm, dop, doc, don, lc, dc, ltp, ltc, ltn, dtp, dtc, dtn,
             dq_ref, dk_ref, dv_ref, dkm_ref, dvm_ref, dsk_ref):
        n = pl.program_id(0)

        @pl.when(n == 0)
        def _():
            dkm_ref[...] = jnp.zeros_like(dkm_ref)
            dvm_ref[...] = jnp.zeros_like(dvm_ref)
            dsk_ref[...] = jnp.zeros_like(dsk_ref)

        qi = lax.broadcasted_iota(jnp.int32, (BLOCK, 3 * BLOCK), 0)
        sj = lax.broadcasted_iota(jnp.int32, (BLOCK, 3 * BLOCK), 1)
        dist_q = jnp.abs(qi + BLOCK - sj)
        kpos = (n - 1) * BLOCK + sj
        valid_q = (dist_q <= WINDOW) & (kpos >= BLOCK) & (kpos < lp)
        distf_q = dist_q.astype(F32)
        bi = lax.broadcasted_iota(jnp.int32, (BLOCK, 3 * BLOCK), 1)
        kj = lax.broadcasted_iota(jnp.int32, (BLOCK, 3 * BLOCK), 0)
        dist_k = jnp.abs(bi - BLOCK - kj)
        qpos = (n - 1) * BLOCK + bi
        valid_k = (dist_k <= WINDOW) & (qpos >= 0) & (qpos < lp) & (n >= 1)
        distf_k = dist_k.astype(F32)

        kb = jnp.concatenate([kp[...], kc[...], kn[...]], axis=0).astype(BF16)
        vb = jnp.concatenate([vp[...], vc[...], vn[...]], axis=0).astype(BF16)
        kcur = kc[...].astype(BF16)
        vcur = vc[...].astype(BF16)
        kmeta = km[PAD:BLOCK, :].astype(BF16)
        vmeta = vm[PAD:BLOCK, :].astype(BF16)
        lane = lax.broadcasted_iota(jnp.int32, (1, BLOCK), 1)
        dsink = jnp.zeros((1, BLOCK), F32)
        valid_q4 = jnp.concatenate([valid_q] * Q_GROUP, axis=0)
        distf_q4 = jnp.concatenate([distf_q] * Q_GROUP, axis=0)
        valid_k4 = jnp.concatenate([valid_k] * Q_GROUP, axis=1)
        distf_k4 = jnp.concatenate([distf_k] * Q_GROUP, axis=1)
        grp_q = _row_group(Q_GROUP, BLOCK)
        lane_k = lax.broadcasted_iota(jnp.int32, (1, Q_GROUP * 3 * BLOCK), 1)
        grp_k = sum(jnp.where(lane_k >= g * 3 * BLOCK, 1, 0) for g in range(1, Q_GROUP))
        for kh in range(N_KV_HEADS):
            ksl = slice(kh * HEAD_DIM, (kh + 1) * HEAD_DIM)
            heads = [kh * Q_GROUP + g for g in range(Q_GROUP)]
            slopes = [SLOPES[h] for h in heads]
            q4 = (jnp.concatenate([qc[:, _head(h)] for h in heads], axis=0) * ATTN_SCALE).astype(BF16)
            do4 = jnp.concatenate([doc[:, _head(h)] for h in heads], axis=0)
            delta = jnp.concatenate([dc[:, h:h + 1] for h in heads], axis=0)
            lse4 = jnp.concatenate([lc[:, h:h + 1] for h in heads], axis=0)
            s = _dot_nt(q4, kb[:, ksl]) - _by_group(grp_q, slopes) * distf_q4
            p = jnp.exp(jnp.where(valid_q4, s, NEG) - lse4)
            pm = jnp.exp(_dot_nt(q4, kmeta[:, ksl]) - lse4)
            ps = jnp.exp(_by_group(grp_q, [sink_ref[h] for h in heads]) - lse4)
            do4b = do4.astype(BF16)
            ds = p * (_dot_nt(do4b, vb[:, ksl]) - delta)
            dsm = pm * (_dot_nt(do4b, vmeta[:, ksl]) - delta)
            dq4 = ATTN_SCALE * (_dot(ds, kb[:, ksl]) + _dot(dsm, kmeta[:, ksl]))
            dsk4 = ps * delta
            for g, h in enumerate(heads):
                dq_ref[:, _head(h)] = dq4[g * BLOCK:(g + 1) * BLOCK].astype(dq_ref.dtype)
                dsink = dsink + jnp.where(lane == h, -jnp.sum(dsk4[g * BLOCK:(g + 1) * BLOCK]), 0.0)
            dkm_ref[:, ksl] += _dot_tn(dsm, q4)
            dvm_ref[:, ksl] += _dot_tn(pm, do4b)
            band = lambda a, b, c_: jnp.concatenate([r[:, _head(h)] for h in heads for r in (a, b, c_)], axis=0)
            qb4 = (band(qp, qc, qn) * ATTN_SCALE).astype(BF16)
            dob4b = band(dop, doc, don).astype(BF16)
            delta_b = jnp.concatenate([r[h:h + 1, :] for h in heads for r in (dtp, dtc, dtn)], axis=1)
            lse_b = jnp.concatenate([r[h:h + 1, :] for h in heads for r in (ltp, ltc, ltn)], axis=1)
            st = _dot_nt(kcur[:, ksl], qb4) - _by_group(grp_k, slopes) * distf_k4
            pt = jnp.exp(jnp.where(valid_k4, st, NEG) - lse_b)
            dv_ref[:, ksl] = _dot(pt, dob4b)
            dst = pt * (_dot_nt(vcur[:, ksl], dob4b) - delta_b)
            dk_ref[:, ksl] = _dot(dst, qb4)
        dsk_ref[...] += dsink

    blk = lambda w: pl.BlockSpec((BLOCK, w), lambda n: (n, 0))
    fix = lambda shp: pl.BlockSpec(shp, lambda n: (0, 0))
    return pl.pallas_call(
        body, grid=(nb,), in_specs=in_specs,
        out_specs=[blk(D_MODEL), blk(kv_w), blk(kv_w), fix((N_META, kv_w)), fix((N_META, kv_w)), fix((1, BLOCK))],
        out_shape=[jax.ShapeDtypeStruct((lp, D_MODEL), BF16), jax.ShapeDtypeStruct((lp, kv_w), F32),
                   jax.ShapeDtypeStruct((lp, kv_w), F32), jax.ShapeDtypeStruct((N_META, kv_w), F32),
                   jax.ShapeDtypeStruct((N_META, kv_w), F32), jax.ShapeDtypeStruct((1, BLOCK), F32)],
        name="attn_bwd", compiler_params=_compiler_params())(sink, *[s[0] for s in specs])


N_SEG = 8
SSM_TILE = 384


def _to_segments(a):
    lp, w = a.shape
    return a.reshape(N_SEG, lp // N_SEG, w).transpose(1, 0, 2).reshape(lp, w)


def _from_segments(a):
    lp, w = a.shape
    return a.reshape(lp // N_SEG, N_SEG, w).transpose(1, 0, 2).reshape(lp, w)


def _complex_power(ar, ai, n):
    rr, ri = jnp.ones_like(ar), jnp.zeros_like(ai)
    while n:
        if n & 1:
            rr, ri = rr * ar - ri * ai, rr * ai + ri * ar
        ar, ai = ar * ar - ai * ai, 2.0 * ar * ai
        n >>= 1
    return rr, ri


def _segment_starts(finals, a_seg, reverse):
    fr, fi = finals[:, :N_STATE], finals[:, N_STATE:]
    ar, ai = a_seg[:, :N_STATE], a_seg[:, N_STATE:]
    row = lax.broadcasted_iota(jnp.int32, (N_SEG, N_STATE), 0)
    pr = jnp.zeros((1, N_STATE), F32)
    pi = jnp.zeros((1, N_STATE), F32)
    sr = jnp.zeros((N_SEG, N_STATE), F32)
    si = jnp.zeros((N_SEG, N_STATE), F32)
    for s in (range(N_SEG - 1, -1, -1) if reverse else range(N_SEG)):
        sr = jnp.where(row == s, pr, sr)
        si = jnp.where(row == s, pi, si)
        pr, pi = fr[s:s + 1] + ar * pr - ai * pi, fi[s:s + 1] + ar * pi + ai * pr
    return jnp.concatenate([sr, si], axis=1)


def _recurrence(buf_ref, st_ref, a_ref, reverse):
    steps = SSM_TILE // N_SEG
    half = N_STATE // 2
    for c0 in (0, half):
        re = slice(c0, c0 + half)
        im = slice(N_STATE + c0, N_STATE + c0 + half)
        ar = jnp.broadcast_to(a_ref[:, re], (N_SEG, half))
        ai = jnp.broadcast_to(a_ref[:, im], (N_SEG, half))

        def step(k, carry, re=re, im=im, ar=ar, ai=ai):
            xr, xi = carry
            r0 = pl.multiple_of((steps - 1 - k if reverse else k) * N_SEG, N_SEG)
            nr = ar * xr - ai * xi + buf_ref[pl.ds(r0, N_SEG), re]
            ni = ar * xi + ai * xr + buf_ref[pl.ds(r0, N_SEG), im]
            buf_ref[pl.ds(r0, N_SEG), re] = nr
            buf_ref[pl.ds(r0, N_SEG), im] = ni
            return nr, ni

        xr, xi = lax.fori_loop(0, steps, step, (st_ref[:, re], st_ref[:, im]), unroll=2)
        st_ref[:, re] = xr
        st_ref[:, im] = xi


def _copy_in(pairs, sem):
    cps = [pltpu.make_async_copy(src, dst, sem.at[j]) for j, (src, dst) in enumerate(pairs)]
    for cp in cps:
        cp.start()
    for cp in cps:
        cp.wait()


def _ssm_fwd_dir(name, u_seg, wb, wc, a, a_seg, reverse):
    lp = u_seg.shape[0]
    nt = lp // SSM_TILE
    tile = (lambda i: nt - 1 - i) if reverse else (lambda i: i)
    first = tile(0)
    held = lambda p, i: (p * tile(i) + (1 - p) * first, 0)

    def body(u_ref, wb_hbm, wc_hbm, a_ref, aseg_ref, x_ref, y_ref, wb_ref, wc_ref, buf_ref, st_ref, sem):
        p, i = pl.program_id(0), pl.program_id(1)

        @pl.when((p == 0) & (i == 0))
        def _():
            _copy_in([(wb_hbm, wb_ref), (wc_hbm, wc_ref)], sem)
            st_ref[...] = jnp.zeros_like(st_ref)

        @pl.when((p == 1) & (i == 0))
        def _():
            st_ref[...] = _segment_starts(st_ref[...], aseg_ref[...], reverse)

        def states_into(dst_ref):
            for j in range(SUPER):
                part = _dot(u_ref[:, 128 * j:128 * (j + 1)], wb_ref[j])
                dst_ref[:, 512 * j:512 * (j + 1)] = part[:, :512]
                dst_ref[:, N_STATE + 512 * j:N_STATE + 512 * (j + 1)] = part[:, 512:]
            _recurrence(dst_ref, st_ref, a_ref, reverse)

        @pl.when(p == 0)
        def _():
            states_into(buf_ref)

        @pl.when(p == 1)
        def _():
            states_into(x_ref)
            for j in range(SUPER):
                y_ref[:, 128 * j:128 * (j + 1)] = (_dot(x_ref[:, _re(j)], wc_ref[j, :512, :])
                                                   + _dot(x_ref[:, _im(j)], wc_ref[j, 512:, :]))

    fix = lambda shp: pl.BlockSpec(shp, lambda p, i: (0, 0))
    return pl.pallas_call(
        body, grid=(2, nt),
        in_specs=[pl.BlockSpec((SSM_TILE, SSM_WIDTH), lambda p, i: (tile(i), 0)), pl.BlockSpec(memory_space=pl.ANY),
                  pl.BlockSpec(memory_space=pl.ANY), fix((1, 2 * N_STATE)), fix((1, 2 * N_STATE))],
        out_specs=[pl.BlockSpec((SSM_TILE, 2 * N_STATE), held), pl.BlockSpec((SSM_TILE, SSM_WIDTH), held)],
        out_shape=[jax.ShapeDtypeStruct((lp, 2 * N_STATE), F32), jax.ShapeDtypeStruct((lp, SSM_WIDTH), F32)],
        scratch_shapes=[pltpu.VMEM(wb.shape, BF16), pltpu.VMEM(wc.shape, BF16), pltpu.VMEM((SSM_TILE, 2 * N_STATE), F32),
                        pltpu.VMEM((N_SEG, 2 * N_STATE), F32), pltpu.SemaphoreType.DMA((2,))],
        name=name, compiler_params=pltpu.CompilerParams(dimension_semantics=("arbitrary", "arbitrary"),
                                                        vmem_limit_bytes=V7X_VMEM_LIMIT_BYTES))(
        u_seg, wb.astype(BF16), wc.astype(BF16), a, a_seg)


def _ssm_bwd_dir(name, dys_seg, u_seg, x_seg, wb, wc, a_conj, a_seg_conj, fwd_reverse):
    lp = u_seg.shape[0]
    nt = lp // SSM_TILE
    steps = SSM_TILE // N_SEG
    reverse = not fwd_reverse
    tile = (lambda i: nt - 1 - i) if reverse else (lambda i: i)
    first = tile(0)
    held = lambda p, i: (p * tile(i) + (1 - p) * first, 0)
    n_slab = lp // N_SEG
    if fwd_reverse:
        halo = lambda p, i: (p * jnp.minimum((tile(i) + 1) * steps, n_slab - 1), 0)
        edge = lambda p, i: (0, 0)
    else:
        halo = lambda p, i: (p * jnp.maximum(tile(i) * steps - 1, 0), 0)
        edge = lambda p, i: (n_slab - 1, 0)

    def body(dy_ref, u_ref, x_ref, halo_ref, edge_ref, wb_hbm, wc_hbm, a_ref, aseg_ref, du_ref, dwb_ref, dwc_ref, ga_ref,
             wb_ref, wc_ref, buf_ref, st_ref, sem):
        p, i = pl.program_id(0), pl.program_id(1)

        @pl.when((p == 0) & (i == 0))
        def _():
            _copy_in([(wb_hbm, wb_ref), (wc_hbm, wc_ref)], sem)
            st_ref[...] = jnp.zeros_like(st_ref)
            dwb_ref[...] = jnp.zeros_like(dwb_ref)
            dwc_ref[...] = jnp.zeros_like(dwc_ref)
            ga_ref[...] = jnp.zeros_like(ga_ref)

        @pl.when((p == 1) & (i == 0))
        def _():
            st_ref[...] = _segment_starts(st_ref[...], aseg_ref[...], reverse)

        for j in range(SUPER):
            part = _dot_nt(dy_ref[:, 128 * j:128 * (j + 1)], wc_ref[j])
            buf_ref[:, 512 * j:512 * (j + 1)] = part[:, :512]
            buf_ref[:, N_STATE + 512 * j:N_STATE + 512 * (j + 1)] = part[:, 512:]
        _recurrence(buf_ref, st_ref, a_ref, reverse)

        @pl.when(p == 1)
        def _():
            for j in range(SUPER):
                ch = slice(128 * j, 128 * (j + 1))
                du_ref[:, ch] = (_dot_nt(buf_ref[:, _re(j)], wb_ref[j, :, :512]) + _dot_nt(buf_ref[:, _im(j)], wb_ref[j, :, 512:]))
                dwb_ref[ch, :512] += _dot_tn(u_ref[:, ch], buf_ref[:, _re(j)])
                dwb_ref[ch, 512:] += _dot_tn(u_ref[:, ch], buf_ref[:, _im(j)])
                dwc_ref[ch, :512] += _dot_tn(dy_ref[:, ch], x_ref[:, _re(j)])
                dwc_ref[ch, 512:] += _dot_tn(dy_ref[:, ch], x_ref[:, _im(j)])
            row = lax.broadcasted_iota(jnp.int32, (N_SEG, 2 * N_STATE), 0)
            if fwd_reverse:
                wrap = jnp.where(row == N_SEG - 1, 0.0, pltpu.roll(edge_ref[...], N_SEG - 1, axis=0))
                open_slab = jnp.where(tile(i) == nt - 1, wrap, halo_ref[...])
                before = lambda cols: jnp.concatenate([x_ref[N_SEG:, cols], open_slab[:, cols]], axis=0)
            else:
                wrap = jnp.where(row == 0, 0.0, pltpu.roll(edge_ref[...], 1, axis=0))
                open_slab = jnp.where(tile(i) == 0, wrap, halo_ref[...])
                before = lambda cols: jnp.concatenate([open_slab[:, cols], x_ref[:SSM_TILE - N_SEG, cols]], axis=0)
            half = N_STATE // 2
            for c0 in (0, half):
                re = slice(c0, c0 + half)
                im = slice(N_STATE + c0, N_STATE + c0 + half)
                gr, gi = buf_ref[:, re], buf_ref[:, im]
                br, bi = before(re), before(im)
                fold = lambda v: jnp.sum(v.reshape(steps, N_SEG, half), axis=0)
                ga_ref[:, re] += fold(gr * br + gi * bi)
                ga_ref[:, im] += fold(gi * br - gr * bi)

    fix = lambda shp: pl.BlockSpec(shp, lambda p, i: (0, 0))
    row_tile = lambda w: pl.BlockSpec((SSM_TILE, w), lambda p, i: (tile(i), 0))
    return pl.pallas_call(
        body, grid=(2, nt),
        in_specs=[row_tile(SSM_WIDTH), row_tile(SSM_WIDTH), pl.BlockSpec((SSM_TILE, 2 * N_STATE), held),
                  pl.BlockSpec((N_SEG, 2 * N_STATE), halo), pl.BlockSpec((N_SEG, 2 * N_STATE), edge),
                  pl.BlockSpec(memory_space=pl.ANY), pl.BlockSpec(memory_space=pl.ANY), fix((1, 2 * N_STATE)), fix((1, 2 * N_STATE))],
        out_specs=[pl.BlockSpec((SSM_TILE, SSM_WIDTH), held), fix((SSM_WIDTH, 1024)), fix((SSM_WIDTH, 1024)),
                   fix((N_SEG, 2 * N_STATE))],
        out_shape=[jax.ShapeDtypeStruct((lp, SSM_WIDTH), F32), jax.ShapeDtypeStruct((SSM_WIDTH, 1024), F32),
                   jax.ShapeDtypeStruct((SSM_WIDTH, 1024), F32), jax.ShapeDtypeStruct((N_SEG, 2 * N_STATE), F32)],
        scratch_shapes=[pltpu.VMEM(wb.shape, BF16), pltpu.VMEM(wc.shape, BF16), pltpu.VMEM((SSM_TILE, 2 * N_STATE), F32),
                        pltpu.VMEM((N_SEG, 2 * N_STATE), F32), pltpu.SemaphoreType.DMA((2,))],
        name=name, compiler_params=pltpu.CompilerParams(dimension_semantics=("arbitrary", "arbitrary"),
                                                        vmem_limit_bytes=V7X_VMEM_LIMIT_BYTES))(
        dys_seg, u_seg, x_seg, x_seg, x_seg, wb.astype(BF16), wc.astype(BF16), a_conj, a_seg_conj)


def _ssm_prep(lam_re, lam_im, log_dt, b_re, b_im, c_re, c_im):
    dt = jnp.exp(log_dt)[:, None]
    er = jnp.exp(lam_re * dt)
    ar, ai = er * jnp.cos(lam_im * dt), er * jnp.sin(lam_im * dt)
    nr, ni = ar - 1.0, ai
    den = lam_re * lam_re + lam_im * lam_im
    cr, ci = (nr * lam_re + ni * lam_im) / den, (ni * lam_re - nr * lam_im) / den
    bbr = cr[:, :, None] * b_re - ci[:, :, None] * b_im
    bbi = cr[:, :, None] * b_im + ci[:, :, None] * b_re
    eye = jnp.eye(8, dtype=F32)

    def in_map(b):
        b = b.reshape(SUPER, 8, SSM_STATE, SSM_GROUP_CH).transpose(0, 1, 3, 2)
        return (b[:, :, :, None, :] * eye[None, :, None, :, None]).reshape(SUPER, 128, 512)

    def out_map(cm):
        cm = cm.reshape(SUPER, 8, SSM_GROUP_CH, SSM_STATE).transpose(0, 1, 3, 2)
        return (cm[:, :, :, None, :] * eye[None, :, None, :, None]).reshape(SUPER, 512, 128)

    wb = jnp.concatenate([in_map(bbr), in_map(bbi)], axis=2)
    wc = jnp.concatenate([out_map(c_re), -out_map(c_im)], axis=1)
    return ar.reshape(1, N_STATE), ai.reshape(1, N_STATE), wb, wc


def _re(j):
    return slice(512 * j, 512 * (j + 1))


def _im(j):
    return slice(N_STATE + 512 * j, N_STATE + 512 * (j + 1))


def _row_ids(i, tm, width):
    return i * tm + lax.broadcasted_iota(jnp.int32, (tm, width), 0)


_FF_CHUNKS = (slice(0, D_FF // 2), slice(D_FF // 2, D_FF))


def _ffn_fwd(tag, h, gain, wg, wu, wd, next_shard=None):
    def up(i, h_ref, wg_ref, wu_ref, g_ref, n_ref, b_ref, silu_ref, dsilu_ref, act_ref):
        n = _rms_fwd(h_ref[...], g_ref[...]).astype(BF16)
        n_ref[...] = n
        for cols in (slice(0, D_FF),):
            a = _dot(n, wg_ref[:, cols])
            b = _dot(n, wu_ref[:, cols])
            sg = _sigmoid(a)
            silu = a * sg
            b_ref[:, cols] = b.astype(BF16)
            silu_ref[:, cols] = silu.astype(BF16)
            dsilu_ref[:, cols] = (sg * (1.0 + a * (1.0 - sg))).astype(BF16)
            act_ref[:, cols] = (silu * b).astype(BF16)

    outs = [(D_MODEL, BF16)] + [(D_FF, BF16)] * 4
    gathered = None
    if next_shard is None:
        n, b, silu, dsilu, act = _rowk(tag + "_up", up, [_rows(h)], [wg, wu, gain], outs, tm=192, writes_outs=True)
    else:
        n, b, silu, dsilu, act, got = _rowk(tag + "_up_gather", up, [_rows(h)], [wg, wu, gain], outs, tm=192,
                                            writes_outs=True, hosted=_gather_ici_stage(next_shard))
        gathered = _gather_finish("gather_w", next_shard, got)
    out = _rowk(tag + "_down", lambda i, act_ref, h_ref, wd_ref: h_ref[...] + 0.5 * _dot(act_ref[...], wd_ref[...]),
                [_rows(act), _rows(h)], [wd], [(D_MODEL, F32)])[0]
    return out, (h, n, b, silu, dsilu, act), gathered


def _ffn_bwd(tag, dh, saved, gain, wg, wu, wd, pending=None, sel=None):
    h, n, b, silu, dsilu, act = saved
    hosted1 = hosted2 = reduced = None
    if pending is not None:
        f, hosted1 = _rs_pair_stage(pending)
        tag = tag + "_reduce"

    def bwd1(i, dh_ref, b_ref, silu_ref, dsilu_ref, wd_ref, da_ref, db_ref):
        dhb = (0.5 * dh_ref[...]).astype(BF16)
        for cols in _FF_CHUNKS:
            dact = _dot_nt(dhb, wd_ref[cols, :])
            da_ref[:, cols] = (dact * b_ref[:, cols].astype(F32) * dsilu_ref[:, cols].astype(F32)).astype(BF16)
            db_ref[:, cols] = (dact * silu_ref[:, cols].astype(F32)).astype(BF16)

    res = _rowk(tag + "_bwd_act", bwd1, [_rows(dh), _rows(b), _rows(silu), _rows(dsilu)], [wd], [(D_FF, BF16)] * 2,
                writes_outs=True, hosted=hosted1)
    da, db = res[0], res[1]
    if pending is not None:
        p, hosted2 = _rs_chip_stage("reduce_w", f, res[2], sel)

    def bwd2(i, da_ref, db_ref, h_ref, dh_ref, wg_ref, wu_ref, g_ref):
        dn = _dot_nt(da_ref[...], wg_ref[...]) + _dot_nt(db_ref[...], wu_ref[...])
        dx, dg = _rms_bwd(h_ref[...], g_ref[...], dn)
        return dh_ref[...] + dx, dg

    res = _rowk(tag + "_bwd_in", bwd2, [_rows(da), _rows(db), _rows(h), _rows(dh)], [wg, wu, gain],
                [(D_MODEL, F32)], accs=[(1, D_MODEL)], hosted=hosted2)
    dh_in, dgain = res[0], res[1]
    if pending is not None:
        reduced = _rs_finish("reduce_w", p, res[2], sel)
    dwd = _mm_tn("ffn_dwd", act, dh, tk=D_FF // 2, tn=D_MODEL, scale=0.5)
    dwg = _mm_tn("ffn_dwg", n, da, tk=D_MODEL, tn=D_FF // 2)
    dwu = _mm_tn("ffn_dwu", n, db, tk=D_MODEL, tn=D_FF // 2)
    return dh_in, dgain, dwg, dwu, dwd, reduced


def _mixer_fwd(h, lw, ssm):
    lp = h.shape[0]
    def mix_in(i, h_ref, w_ref, g_ref):
        nv = _rms_fwd(h_ref[...], g_ref[...]).astype(BF16)
        return nv, _dot(nv, w_ref[...])

    n, proj = _rowk("mix_in", mix_in, [_rows(h)], [lw["w_in"], lw["mix_norm"]], [(D_MODEL, BF16), (4 * D_MODEL, F32)])
    yattn, lse = _attn_fwd(proj, lw["attn_sink"])
    u_seg = _to_segments(proj[:, 3 * SSM_WIDTH:4 * SSM_WIDTH])
    xs, ydir = [], []
    for d in range(2):
        x_seg, y_seg = _ssm_fwd_dir(f"ssm_fwd{d}", u_seg, ssm[d]["wb"], ssm[d]["wc"], ssm[d]["a"], ssm[d]["a_seg"],
                                    reverse=(d == 1))
        xs.append(x_seg)
        ydir.append(y_seg)

    def ssm_out(i, y0_ref, y1_ref, u_ref, d_ref, wglu_ref):
        ys = y0_ref[...] + y1_ref[...] + d_ref[...] * u_ref[...]
        z = _gelu(ys)
        return ys, z * _sigmoid(_dot(z, wglu_ref[...]))

    ys, yssm_seg = _rowk("ssm_out", ssm_out, [_rows(ydir[0]), _rows(ydir[1]), _rows(u_seg)],
                         [lw["ssm_d"], lw["ssm_w_glu"]], [(SSM_WIDTH, F32), (SSM_WIDTH, BF16)])
    yssm = _from_segments(yssm_seg)

    def merge(i, ys_ref, ya_ref, gs_ref, ga_ref, wbs_ref, wba_ref):
        bs = _dot(ys_ref[...], wbs_ref[...])
        ba = _dot(ya_ref[...], wba_ref[...])
        m = _sigmoid(gs_ref[...]) * bs + _sigmoid(ga_ref[...]) * ba
        return bs, ba, jnp.where(_row_ids(i, ROW_TILE, D_MODEL) >= PAD, m, 0.0)

    bs, ba, merged = _rowk("mix_merge", merge, [_rows(yssm), _rows(yattn), _rows(proj, D_MODEL, 2), _rows(proj, D_MODEL, 3)],
                           [lw["w_branch_ssm"], lw["w_branch_attn"]], [(D_MODEL, BF16)] * 3)
    out = _rowk("mix_out", lambda i, m_ref, h_ref, w_ref: h_ref[...] + _dot(m_ref[...], w_ref[...]),
                [_rows(merged), _rows(h)], [lw["w_out"]], [(D_MODEL, F32)])[0]
    return out, (h, n, proj, yattn, lse, u_seg, xs, ys, yssm, bs, ba, merged)


def _mixer_bwd(dh, saved, lw, ssm):
    h, n, proj, yattn, lse, u_seg, xs, ys, yssm, bs, ba, merged = saved

    def bwd1(i, dh_ref, gs_ref, ga_ref, bs_ref, ba_ref, w_ref):
        dm = _dot_nt(dh_ref[...], w_ref[...])
        dm = jnp.where(_row_ids(i, ROW_TILE, D_MODEL) >= PAD, dm, 0.0)
        sgs = _sigmoid(gs_ref[...])
        sga = _sigmoid(ga_ref[...])
        return (dm * sgs, dm * sga, dm * bs_ref[...].astype(F32) * sgs * (1.0 - sgs),
                dm * ba_ref[...].astype(F32) * sga * (1.0 - sga))

    dbs, dba, dgs, dga = _rowk("mix_bwd_merge", bwd1,
                               [_rows(dh), _rows(proj, D_MODEL, 2), _rows(proj, D_MODEL, 3), _rows(bs), _rows(ba)],
                               [lw["w_out"]], [(D_MODEL, BF16)] * 4)
    dw_out = _mm_tn("mix_dw_out", merged, dh, tk=D_MODEL, tn=D_MODEL)
    dw_bs = _mm_tn("mix_dw_bs", yssm, dbs, tk=SSM_WIDTH, tn=D_MODEL)
    dw_ba = _mm_tn("mix_dw_ba", yattn, dba, tk=D_MODEL, tn=D_MODEL)

    def bwd2(i, dbs_ref, dba_ref, wbs_ref, wba_ref):
        return _dot_nt(dba_ref[...], wba_ref[...]), _dot_nt(dbs_ref[...], wbs_ref[...])

    dyattn, dyssm = _rowk("mix_bwd_branches", bwd2, [_rows(dbs), _rows(dba)], [lw["w_branch_ssm"], lw["w_branch_attn"]],
                          [(D_MODEL, F32), (SSM_WIDTH, F32)])

    def bwd3(i, dyssm_ref, ys_ref, u_ref, wglu_ref):
        ysv = ys_ref[...]
        z = _gelu(ysv)
        sg = _sigmoid(_dot(z, wglu_ref[...]))
        dt = dyssm_ref[...] * z * sg * (1.0 - sg)
        dz = dyssm_ref[...] * sg + _dot_nt(dt, wglu_ref[...])
        dys = dz * _gelu_grad(ysv)
        return dys, z, dt, jnp.sum(dys * u_ref[...], axis=0, keepdims=True)

    dys, z, dt, dd = _rowk("mix_bwd_ssm_out", bwd3, [_rows(_to_segments(dyssm)), _rows(ys), _rows(u_seg)], [lw["ssm_w_glu"]],
                           [(SSM_WIDTH, F32), (SSM_WIDTH, BF16), (SSM_WIDTH, BF16)], accs=[(1, SSM_WIDTH)])
    dw_glu = _mm_tn("mix_dw_glu", z, dt, tk=SSM_WIDTH, tn=SSM_WIDTH)

    dus, ssm_cot = [], []
    for d in range(2):
        du_d, dwb, dwc_t, ga = _ssm_bwd_dir(f"ssm_bwd{d}", dys, u_seg, xs[d], ssm[d]["wb"], ssm[d]["wc"], ssm[d]["a_conj"],
                                            ssm[d]["a_seg_conj"], fwd_reverse=(d == 1))
        dus.append(du_d)
        ga = jnp.sum(ga, axis=0, keepdims=True)
        ssm_cot.append((ga[:, :N_STATE], ga[:, N_STATE:], dwb.reshape(SUPER, 128, 1024),
                        dwc_t.reshape(SUPER, 128, 1024).transpose(0, 2, 1)))

    du_seg = _rowk("ssm_bwd_du", lambda i, a_ref, b_ref, dys_ref, d_ref: a_ref[...] + b_ref[...] + d_ref[...] * dys_ref[...],
                   [_rows(dus[0]), _rows(dus[1]), _rows(dys)], [lw["ssm_d"]], [(SSM_WIDTH, BF16)])[0]
    du = _from_segments(du_seg)

    dq, dk, dv, dkm, dvm, dsink = _attn_bwd(proj, lw["attn_sink"], yattn, lse, dyattn)

    def dproj_fn(i, dq_ref, dk_ref, dv_ref, du_ref, dgs_ref, dga_ref, dkm_ref, dvm_ref):
        first = jnp.where(i == 0, 1.0, 0.0)
        zeros = lambda r: jnp.zeros((r, N_KV_HEADS * HEAD_DIM), F32)
        place = lambda m: jnp.concatenate([zeros(PAD), m[...] * first, zeros(ROW_TILE - BLOCK)], axis=0)
        dp = jnp.concatenate([dq_ref[...].astype(F32), dk_ref[...] + place(dkm_ref), dv_ref[...] + place(dvm_ref),
                              du_ref[...].astype(F32), dgs_ref[...].astype(F32), dga_ref[...].astype(F32)], axis=1)
        return jnp.where(_row_ids(i, ROW_TILE, 4 * D_MODEL) >= PAD, dp, 0.0)

    dproj = _rowk("mix_bwd_dproj", dproj_fn, [_rows(dq), _rows(dk), _rows(dv), _rows(du), _rows(dgs), _rows(dga)],
                  [dkm, dvm], [(4 * D_MODEL, BF16)])[0]

    def bwd_in(i, dp_ref, h_ref, dh_ref, w_ref, g_ref):
        dx, dg = _rms_bwd(h_ref[...], g_ref[...], _dot_nt(dp_ref[...], w_ref[...]))
        return dh_ref[...] + dx, dg

    dh_in, dgain = _rowk("mix_bwd_in", bwd_in, [_rows(dproj), _rows(h), _rows(dh)], [lw["w_in"], lw["mix_norm"]],
                         [(D_MODEL, F32)], accs=[(1, D_MODEL)])
    dw_in = _mm_tn("mix_dw_in", n, dproj, tk=D_MODEL, tn=2 * D_MODEL)
    grads = {"w_out": dw_out, "w_branch_ssm": dw_bs, "w_branch_attn": dw_ba, "ssm_w_glu": dw_glu, "w_in": dw_in,
             "mix_norm": dgain, "ssm_d": dd, "attn_sink": dsink[0, :N_HEADS]}
    return dh_in, grads, ssm_cot


def _loss_head(h, gain, target):
    lp = h.shape[0]

    def fn(i, h_ref, t_ref, g_ref):
        x = h_ref[...]
        y = _rms_fwd(x, g_ref[...])
        live = jnp.where(i == 0, 0.0, 1.0)
        dy = (y - t_ref[...]) * live
        loss = 0.5 * jnp.sum(dy * dy) / D_MODEL
        dx, dg = _rms_bwd(x, g_ref[...], dy * (1.0 / D_MODEL))
        return dx, jnp.full((1, BLOCK), loss, F32), dg

    tgt = (target, (BLOCK, D_MODEL), lambda i: (jnp.maximum(i - 1, 0), 0))
    return _rowk("loss_head", fn, [_rows(h), tgt], [gain], [(D_MODEL, F32)], accs=[(1, BLOCK), (1, D_MODEL)], tm=BLOCK)


def _adamw(name, w, g, m, v, tm):
    def fn(i, w_ref, g_ref, m_ref, v_ref):
        gv = g_ref[...]
        mn = ADAM_B1 * m_ref[...] + (1.0 - ADAM_B1) * gv
        vn = ADAM_B2 * v_ref[...] + (1.0 - ADAM_B2) * (gv * gv)
        m_hat = mn / (1.0 - ADAM_B1 ** ADAM_STEP)
        v_hat = vn / (1.0 - ADAM_B2 ** ADAM_STEP)
        return -ADAM_LR * (m_hat / (jnp.sqrt(v_hat) + ADAM_EPS) + ADAM_WD * w_ref[...]), mn, vn

    wd = w.shape[1]
    return _rowk(name, fn, [_rows(w), _rows(g), _rows(m), _rows(v)], [], [(wd, F32)] * 3, tm=tm)


def _shard_rows(name):
    return {"ffn1_w_gate": 704, "ffn1_w_up": 704, "ffn1_w_down": 704, "ffn2_w_gate": 704, "ffn2_w_up": 704, "ffn2_w_down": 704,
            "w_in": 1024, "ssm_w_glu": 64, "w_branch_ssm": 128, "w_branch_attn": 256, "w_out": 256}[name]


def _full_shape(name):
    return {"ffn1_w_gate": (D_MODEL, D_FF), "ffn1_w_up": (D_MODEL, D_FF), "ffn1_w_down": (D_FF, D_MODEL),
            "ffn2_w_gate": (D_MODEL, D_FF), "ffn2_w_up": (D_MODEL, D_FF), "ffn2_w_down": (D_FF, D_MODEL),
            "w_in": (D_MODEL, 4 * D_MODEL), "ssm_w_glu": (SSM_WIDTH, SSM_WIDTH), "w_branch_ssm": (SSM_WIDTH, D_MODEL),
            "w_branch_attn": (D_MODEL, D_MODEL), "w_out": (D_MODEL, D_MODEL)}[name]


def _unflatten_gathered(gathered):
    out, r0 = {}, 0
    for name in BIG:
        r = _shard_rows(name)
        k, nn = _full_shape(name)
        piece = gathered[:, r0:r0 + r, :]
        if name in COL_SHARDED:
            out[name] = piece.reshape(4, k, nn // 4).transpose(1, 0, 2).reshape(k, nn)
        else:
            out[name] = piece.reshape(k, nn)
        r0 += r
    return out


def _flatten_full(grads):
    per_shard = []
    for s in range(4):
        pieces = []
        for name in BIG:
            k, nn = _full_shape(name)
            g = grads[name]
            piece = g[:, s * (nn // 4):(s + 1) * (nn // 4)] if name in COL_SHARDED else g[s * (k // 4):(s + 1) * (k // 4), :]
            pieces.append(piece.reshape(-1, 1024))
        per_shard.append(jnp.concatenate(pieces, axis=0))
    f = jnp.stack(per_shard)
    return f.reshape(4, 2, f.shape[1] // 2, 1024).transpose(1, 0, 2, 3)


def _shard_2d(a):
    return a.reshape(-1, a.shape[-1])


def kernel(x, meta_tokens, ffn1_norm, ffn1_w_gate, ffn1_w_up, ffn1_w_down, mix_norm, w_in, ssm_lam_re, ssm_lam_im, ssm_log_dt, ssm_b_re, ssm_b_im, ssm_c_re, ssm_c_im, ssm_d, ssm_w_glu, attn_sink, w_branch_ssm, w_branch_attn, w_out, ffn2_norm, ffn2_w_gate, ffn2_w_up, ffn2_w_down, final_norm, loss_target, m_meta_tokens, m_ffn1_norm, m_ffn1_w_gate, m_ffn1_w_up, m_ffn1_w_down, m_mix_norm, m_w_in, m_ssm_lam_re, m_ssm_lam_im, m_ssm_log_dt, m_ssm_b_re, m_ssm_b_im, m_ssm_c_re, m_ssm_c_im, m_ssm_d, m_ssm_w_glu, m_attn_sink, m_w_branch_ssm, m_w_branch_attn, m_w_out, m_ffn2_norm, m_ffn2_w_gate, m_ffn2_w_up, m_ffn2_w_down, m_final_norm, v_meta_tokens, v_ffn1_norm, v_ffn1_w_gate, v_ffn1_w_up, v_ffn1_w_down, v_mix_norm, v_w_in, v_ssm_lam_re, v_ssm_lam_im, v_ssm_log_dt, v_ssm_b_re, v_ssm_b_im, v_ssm_c_re, v_ssm_c_im, v_ssm_d, v_ssm_w_glu, v_attn_sink, v_w_branch_ssm, v_w_branch_attn, v_w_out, v_ffn2_norm, v_ffn2_w_gate, v_ffn2_w_up, v_ffn2_w_down, v_final_norm):
    args = dict(locals())
    w = {k: args[k] for k in WEIGHTS}
    mom = {k: args["m_" + k] for k in WEIGHTS}
    var = {k: args["v_" + k] for k in WEIGHTS}
    depth = ffn1_norm.shape[0]
    seq = x.shape[1]
    xi, yi, ci = lax.axis_index("x"), lax.axis_index("y"), lax.axis_index("c")
    chip = 2 * xi + yi
    me = 4 * xi + 2 * yi + ci
    sel = jnp.stack([ci, chip]).astype(jnp.int32)

    same_core = [(fx, fy, 0) for fx, fy in _OTHER_CHIPS]
    meta_all = _all_gather_all("gather_meta", meta_tokens, 4, same_core, lambda x_, y_, c_: _chip(x_, y_))
    meta_full = meta_all.transpose(1, 0, 2).reshape(N_META, D_MODEL)
    flat_w = [jnp.concatenate([w[name][l].reshape(-1, 1024) for name in BIG], axis=0).astype(BF16) for l in range(depth)]

    def layer_weights(l, gathered):
        lw = _unflatten_gathered(gathered)
        for name in ("ffn1_norm", "mix_norm", "ffn2_norm"):
            lw[name] = w[name][l].reshape(1, D_MODEL)
        lw["ssm_d"] = ssm_d[l].reshape(1, SSM_WIDTH)
        lw["attn_sink"] = attn_sink[l]
        return lw

    layer_w = [layer_weights(0, _all_gather_shards("gather_w", flat_w[0]))]

    ssm_params = ("ssm_lam_re", "ssm_lam_im", "ssm_log_dt", "ssm_b_re", "ssm_b_im", "ssm_c_re", "ssm_c_im")
    ssm, ssm_vjp = [], []
    for l in range(depth):
        dirs, vjps = [], []
        for d in range(2):
            prm = tuple(w[k][l, d] for k in ssm_params)
            (ar, ai, wb, wc), pull = jax.vjp(_ssm_prep, *prm)
            a_seg = _complex_power(ar, ai, (seq + BLOCK) // N_SEG)
            conj = lambda v: jnp.concatenate([v[0], -v[1]], axis=1)
            pack = lambda v: jnp.concatenate([v[0], v[1]], axis=1)
            dirs.append({"wb": wb, "wc": wc, "a": pack((ar, ai)), "a_seg": pack(a_seg),
                         "a_conj": conj((ar, ai)), "a_seg_conj": conj(a_seg)})
            vjps.append(pull)
        ssm.append(dirs)
        ssm_vjp.append(vjps)

    h = jnp.concatenate([jnp.zeros((PAD, D_MODEL), F32), meta_full, x[0]], axis=0)
    saved = []
    for l in range(depth):
        lw = layer_w[l]
        h, s1, gathered = _ffn_fwd("ffn", h, lw["ffn1_norm"], lw["ffn1_w_gate"], lw["ffn1_w_up"], lw["ffn1_w_down"],
                                   next_shard=flat_w[l + 1] if l + 1 < depth else None)
        if gathered is not None:
            layer_w.append(layer_weights(l + 1, gathered))
        h, s2 = _mixer_fwd(h, lw, ssm[l])
        h, s3, _ = _ffn_fwd("ffn", h, lw["ffn2_norm"], lw["ffn2_w_gate"], lw["ffn2_w_up"], lw["ffn2_w_down"])
        saved.append((s1, s2, s3))
    dh, loss_part, d_final = _loss_head(h, final_norm.reshape(1, D_MODEL), loss_target[0])
    loss = lax.psum(loss_part[0, 0], MESH_AXES)

    small_g = {k: [None] * depth for k in SMALL if k not in ("meta_tokens", "final_norm")}
    big_g = {k: [None] * depth for k in BIG}
    def keep_shard(l, reduced):
        r0 = 0
        for name in BIG:
            r = _shard_rows(name)
            big_g[name][l] = reduced[r0:r0 + r].reshape(w[name].shape[1:])
            r0 += r

    pending = None
    for l in reversed(range(depth)):
        lw = layer_w[l]
        s1, s2, s3 = saved[l]
        full = {}
        dh, dg, full["ffn2_w_gate"], full["ffn2_w_up"], full["ffn2_w_down"], reduced = _ffn_bwd(
            "ffn", dh, s3, lw["ffn2_norm"], lw["ffn2_w_gate"], lw["ffn2_w_up"], lw["ffn2_w_down"], pending, sel)
        if pending is not None:
            keep_shard(l + 1, reduced)
        small_g["ffn2_norm"][l] = dg[0]
        dh, mg, ssm_cot = _mixer_bwd(dh, s2, lw, ssm[l])
        for k in ("w_out", "w_branch_ssm", "w_branch_attn", "ssm_w_glu", "w_in"):
            full[k] = mg[k]
        small_g["mix_norm"][l] = mg["mix_norm"][0]
        small_g["ssm_d"][l] = mg["ssm_d"][0]
        small_g["attn_sink"][l] = mg["attn_sink"]
        per_dir = []
        for d in range(2):
            per_dir.append(ssm_vjp[l][d](ssm_cot[d]))
        for j, k in enumerate(ssm_params):
            small_g[k][l] = jnp.stack([per_dir[0][j], per_dir[1][j]])
        dh, dg, full["ffn1_w_gate"], full["ffn1_w_up"], full["ffn1_w_down"], _ = _ffn_bwd(
            "ffn", dh, s1, lw["ffn1_norm"], lw["ffn1_w_gate"], lw["ffn1_w_up"], lw["ffn1_w_down"])
        small_g["ffn1_norm"][l] = dg[0]
        pending = _flatten_full(full)
    keep_shard(0, _reduce_scatter("reduce_w", pending, sel))

    grad_x = dh[BLOCK:][None]
    small_list = [dh[PAD:BLOCK].reshape(-1)]
    for k in SMALL[1:]:
        small_list.append(d_final.reshape(-1) if k == "final_norm" else jnp.stack(small_g[k]).reshape(-1))
    small_vec = jnp.concatenate(small_list)
    n_small = small_vec.shape[0]
    rows_small = -(-n_small // (64 * 1024)) * 64
    small_vec = jnp.pad(small_vec, (0, rows_small * 1024 - n_small)).reshape(rows_small, 1024)
    everyone = [(fx, fy, fc) for fx in (0, 1) for fy in (0, 1) for fc in (0, 1)][1:]
    index = lambda x_, y_, c_: 4 * x_ + 2 * y_ + c_
    eighths = small_vec.reshape(8, rows_small // 8, 1024)
    sends = [(f, (lambda x_, y_, c_, f=f: index(x_ ^ f[0], y_ ^ f[1], c_ ^ f[2])), index) for f in everyone]
    got = _exchange("scatter_small", eighths, 8, [], sends)
    mine = lax.broadcasted_iota(jnp.int32, (8, 1, 1), 0) == me
    got = jnp.where(mine, lax.dynamic_index_in_dim(eighths, me, 0, keepdims=True), got)
    part = _slot_sum("sum_small", [(got, k) for k in range(8)], sel, rows_small // 8)
    small_sum = _all_gather_all("gather_small", part, 8, everyone, index).reshape(-1)

    grads, deltas, new_m, new_v = {}, {}, {}, {}
    off = 0
    for k in SMALL:
        size = (N_META * D_MODEL) if k == "meta_tokens" else int(np.prod(w[k].shape))
        g = small_sum[off:off + size]
        off += size
        if k == "meta_tokens":
            g = lax.dynamic_slice(g.reshape(N_META, D_MODEL), (0, chip * (D_MODEL // 4)), (N_META, D_MODEL // 4))
        else:
            g = g.reshape(w[k].shape)
        grads[k] = g
        as_2d = (lambda a: a.reshape(-1, a.shape[-1])) if g.ndim >= 2 else (lambda a: a.reshape(1, -1))
        rows_k = as_2d(g).shape[0]
        d_, m_, v_ = _adamw("adamw_" + k, as_2d(w[k]), as_2d(g), as_2d(mom[k]), as_2d(var[k]), min(rows_k, 2048))
        deltas[k], new_m[k], new_v[k] = d_.reshape(g.shape), m_.reshape(g.shape), v_.reshape(g.shape)
    for k in BIG:
        g = jnp.stack(big_g[k])
        grads[k] = g
        rows_k = _shard_2d(g).shape[0]
        tm = 512 if rows_k % 512 == 0 else rows_k // depth
        d_, m_, v_ = _adamw("adamw_" + k, _shard_2d(w[k]), _shard_2d(g), _shard_2d(mom[k]), _shard_2d(var[k]), tm)
        deltas[k], new_m[k], new_v[k] = d_.reshape(g.shape), m_.reshape(g.shape), v_.reshape(g.shape)

    return (loss, grad_x, *[grads[k] for k in WEIGHTS], *[deltas[k] for k in WEIGHTS],
            *[new_m[k] for k in WEIGHTS], *[new_v[k] for k in WEIGHTS])
```

```python
import functools
import math

import numpy as np
import jax
import jax.numpy as jnp
from jax import lax
from jax.experimental import pallas as pl
from jax.experimental.pallas import tpu as pltpu

F32 = jnp.float32
BF16 = jnp.bfloat16

D_MODEL = 1024
N_META = 16
N_HEADS = 16
N_KV_HEADS = 4
HEAD_DIM = 64
Q_GROUP = N_HEADS // N_KV_HEADS
WINDOW = 128
BLOCK = 128
PAD = BLOCK - N_META
SSM_WIDTH = 512
SSM_GROUP_CH = 16
SSM_GROUPS = 32
SSM_STATE = 64
N_STATE = SSM_GROUPS * SSM_STATE
SUPER = 4
D_FF = 2816
EPS = 1e-6
NEG = -1e30
ATTN_SCALE = HEAD_DIM ** -0.5
SLOPES = [float(2.0 ** (-8.0 * (h + 1) / N_HEADS)) for h in range(N_HEADS)]

ADAM_LR = 0.001
ADAM_B1 = 0.9
ADAM_B2 = 0.999
ADAM_EPS = 1e-08
ADAM_WD = 0.01
ADAM_STEP = 10

V7X_VMEM_LIMIT_BYTES = 52 * 1024 * 1024
ROW_TILE = 384
LONG_ROW_TILE = 2064
MESH_AXES = ("x", "y", "c")

BIG = ["ffn1_w_gate", "ffn1_w_up", "ffn1_w_down", "ffn2_w_gate", "ffn2_w_up", "ffn2_w_down",
       "w_in", "ssm_w_glu", "w_branch_ssm", "w_branch_attn", "w_out"]
COL_SHARDED = {"ffn1_w_gate", "ffn1_w_up", "ffn2_w_gate", "ffn2_w_up", "w_in", "w_branch_ssm"}
SMALL = ["meta_tokens", "ffn1_norm", "mix_norm", "ffn2_norm", "final_norm", "ssm_lam_re", "ssm_lam_im", "ssm_log_dt",
         "ssm_b_re", "ssm_b_im", "ssm_c_re", "ssm_c_im", "ssm_d", "attn_sink"]
WEIGHTS = ["meta_tokens", "ffn1_norm", "ffn1_w_gate", "ffn1_w_up", "ffn1_w_down", "mix_norm", "w_in", "ssm_lam_re",
           "ssm_lam_im", "ssm_log_dt", "ssm_b_re", "ssm_b_im", "ssm_c_re", "ssm_c_im", "ssm_d", "ssm_w_glu", "attn_sink",
           "w_branch_ssm", "w_branch_attn", "w_out", "ffn2_norm", "ffn2_w_gate", "ffn2_w_up", "ffn2_w_down", "final_norm"]


def _dot(a, b):
    return lax.dot_general(a.astype(BF16), b.astype(BF16), (((1,), (0,)), ((), ())), preferred_element_type=F32)


def _dot_nt(a, b):
    return lax.dot_general(a.astype(BF16), b.astype(BF16), (((1,), (1,)), ((), ())), preferred_element_type=F32)


def _dot_tn(a, b):
    return lax.dot_general(a.astype(BF16), b.astype(BF16), (((0,), (0,)), ((), ())), preferred_element_type=F32)


def _sigmoid(x):
    return 0.5 * jnp.tanh(0.5 * x) + 0.5


_GELU_C = math.sqrt(2.0 / math.pi)


def _gelu(x):
    return 0.5 * x * (1.0 + jnp.tanh(_GELU_C * (x + 0.044715 * x * x * x)))


def _gelu_grad(x):
    th = jnp.tanh(_GELU_C * (x + 0.044715 * x * x * x))
    return 0.5 * (1.0 + th) + 0.5 * x * (1.0 - th * th) * _GELU_C * (1.0 + 3.0 * 0.044715 * x * x)


def _rms_fwd(x, g):
    r = lax.rsqrt(jnp.mean(x * x, axis=-1, keepdims=True) + EPS)
    return x * r * g


def _rms_bwd(x, g, dn):
    r = lax.rsqrt(jnp.mean(x * x, axis=-1, keepdims=True) + EPS)
    xh = x * r
    t = dn * g
    dx = r * (t - xh * jnp.mean(t * xh, axis=-1, keepdims=True))
    return dx, jnp.sum(dn * xh, axis=0, keepdims=True)


def _compiler_params():
    return pltpu.CompilerParams(dimension_semantics=("arbitrary",), vmem_limit_bytes=V7X_VMEM_LIMIT_BYTES)


def _rows(arr, width=None, cb=0):
    return (arr, arr.shape[1] if width is None else width, cb)


def _rowk(name, fn, rows, fulls, outs, accs=(), tm=ROW_TILE, smem=(), n_rows=None, hosted=None, writes_outs=False):
    n = rows[0][0].shape[0] if n_rows is None else n_rows
    assert n % tm == 0, (name, n, tm)
    in_specs, args = [], []
    for s in smem:
        in_specs.append(pl.BlockSpec(memory_space=pltpu.SMEM))
        args.append(s)
    for r in rows:
        if callable(r[2]):
            in_specs.append(pl.BlockSpec(r[1], r[2]))
        else:
            in_specs.append(pl.BlockSpec((tm, r[1]), functools.partial(lambda i, cb: (i, cb), cb=r[2])))
        args.append(r[0])
    for f in fulls:
        in_specs.append(pl.BlockSpec(memory_space=pl.ANY))
        args.append(f)
    nh = 0 if hosted is None else 1
    if nh:
        in_specs.append(pl.BlockSpec(memory_space=pl.ANY))
        args.append(hosted[0])
    out_specs, out_shape = [], []
    for w, dt in outs:
        out_specs.append(pl.BlockSpec((tm, w), lambda i: (i, 0)))
        out_shape.append(jax.ShapeDtypeStruct((n, w), dt))
    for shp in accs:
        out_specs.append(pl.BlockSpec(shp, functools.partial(lambda i, nd: (0,) * nd, nd=len(shp))))
        out_shape.append(jax.ShapeDtypeStruct(shp, F32))
    if nh:
        out_specs.append(pl.BlockSpec(memory_space=pl.ANY))
        out_shape.append(jax.ShapeDtypeStruct((hosted[1],) + hosted[0].shape[1:], hosted[0].dtype))
    ns, nr, nf, no, na = len(smem), len(rows), len(fulls), len(outs), len(accs)
    scratch = [pltpu.VMEM(f.shape, f.dtype) for f in fulls]
    if nf:
        scratch.append(pltpu.SemaphoreType.DMA((nf,)))
    if nh:
        scratch += [pltpu.SemaphoreType.DMA((len(hosted[2]),)), pltpu.SemaphoreType.DMA((len(hosted[2]),))]
    steps = n // tm

    def body(*refs):
        i = pl.program_id(0)
        refs = list(refs)
        take = lambda k: [refs.pop(0) for _ in range(k)]
        sm, rr, fh, hsrc, oo, aa, hout, fv = take(ns), take(nr), take(nf), take(nh), take(no), take(na), take(nh), take(nf)
        if nf:
            sem = refs.pop(0)

            @pl.when(i == 0)
            def _():
                cps = [pltpu.make_async_copy(fh[j], fv[j], sem.at[j]) for j in range(nf)]
                for cp in cps:
                    cp.start()
                for cp in cps:
                    cp.wait()
        if nh:
            @pl.when(i == 0)
            def _():
                for cp in _remote_copies(hsrc[0], hout[0], refs[0], refs[1], hosted[2]):
                    cp.start()
        if writes_outs:
            res = fn(i, *sm, *rr, *fv, *oo)
            res = (None,) * no + (tuple(res) if isinstance(res, (tuple, list)) else ())
        else:
            res = fn(i, *sm, *rr, *fv)
            res = tuple(res) if isinstance(res, (tuple, list)) else (res,)
            for o, v in zip(oo, res[:no]):
                o[...] = v.astype(o.dtype)
        if na:
            @pl.when(i == 0)
            def _():
                for a in aa:
                    a[...] = jnp.zeros_like(a)
            for a, v in zip(aa, res[no:]):
                a[...] += v
        if nh:
            @pl.when(i == steps - 1)
            def _():
                for cp in _remote_copies(hsrc[0], hout[0], refs[0], refs[1], hosted[2]):
                    cp.wait()

    return pl.pallas_call(body, grid=(steps,), in_specs=in_specs, out_specs=out_specs, out_shape=out_shape,
                          scratch_shapes=scratch, name=name, compiler_params=_compiler_params())(*args)


def _mm_tn(name, x, y, *, xw=None, xcb=0, tk, tn, scale=1.0, tm=None):
    m = x.shape[0]
    if tm is None:
        tm = LONG_ROW_TILE if m % LONG_ROW_TILE == 0 else ROW_TILE
    k = x.shape[1] if xw is None else xw
    nn = y.shape[1]
    assert m % tm == 0 and k % tk == 0 and nn % tn == 0, (name, m, k, nn)
    kb0 = (xcb * k) // tk

    def body(x_ref, y_ref, o_ref):
        @pl.when(pl.program_id(2) == 0)
        def _():
            o_ref[...] = jnp.zeros_like(o_ref)
        yv = y_ref[...]
        if scale != 1.0:
            yv = yv * scale
        o_ref[...] += _dot_tn(x_ref[...], yv)

    return pl.pallas_call(
        body, grid=(k // tk, nn // tn, m // tm),
        in_specs=[pl.BlockSpec((tm, tk), lambda a, b, i: (i, kb0 + a)), pl.BlockSpec((tm, tn), lambda a, b, i: (i, b))],
        out_specs=pl.BlockSpec((tk, tn), lambda a, b, i: (a, b)), out_shape=jax.ShapeDtypeStruct((k, nn), F32), name=name,
        compiler_params=pltpu.CompilerParams(dimension_semantics=("arbitrary", "arbitrary", "arbitrary"),
                                             vmem_limit_bytes=V7X_VMEM_LIMIT_BYTES))(x, y)


def _remote_copies(src_ref, out_ref, ssem, rsem, sends):
    x, y, c = lax.axis_index("x"), lax.axis_index("y"), lax.axis_index("c")
    cps = []
    for k, ((fx, fy, fc), sf, df) in enumerate(sends):
        peer = (1 - x if fx else x, 1 - y if fy else y, 1 - c if fc else c)
        cps.append(pltpu.make_async_remote_copy(src_ref=src_ref.at[sf(x, y, c)], dst_ref=out_ref.at[df(x, y, c)],
                                                send_sem=ssem.at[k], recv_sem=rsem.at[k], device_id=peer,
                                                device_id_type=pl.DeviceIdType.MESH))
    return cps


def _exchange(name, src, n_out, local, sends, alias=False):
    nl, nsnd = len(local), len(sends)
    out_shape = jax.ShapeDtypeStruct((n_out,) + src.shape[1:], src.dtype)

    def body(src_ref, out_ref, lsem, ssem, rsem):
        x, y, c = lax.axis_index("x"), lax.axis_index("y"), lax.axis_index("c")
        cps = [pltpu.make_async_copy(src_ref.at[sf(x, y, c)], out_ref.at[df(x, y, c)], lsem.at[j])
               for j, (sf, df) in enumerate(local)]
        cps += _remote_copies(src_ref, out_ref, ssem, rsem, sends)
        for cp in cps:
            cp.start()
        for cp in cps:
            cp.wait()

    return pl.pallas_call(
        body, in_specs=[pl.BlockSpec(memory_space=pl.ANY)], out_specs=pl.BlockSpec(memory_space=pl.ANY), out_shape=out_shape,
        scratch_shapes=[pltpu.SemaphoreType.DMA((max(nl, 1),)), pltpu.SemaphoreType.DMA((nsnd,)), pltpu.SemaphoreType.DMA((nsnd,))],
        input_output_aliases=({0: 0} if alias else {}), name=name)(src)


def _chip(x, y):
    return 2 * x + y


_OTHER_CHIPS = [(1, 0), (0, 1), (1, 1)]


def _all_gather_shards(name, shard):
    src, n_out, sends = _gather_ici_stage(shard)
    return _gather_finish(name, shard, _exchange(name + "_ici", src, n_out, [], sends))


def _gather_ici_stage(shard):
    r, w = shard.shape
    first = [((fx, fy, 0), lambda x, y, c: c, lambda x, y, c: 2 * _chip(x, y) + c) for fx, fy in _OTHER_CHIPS]
    return shard.reshape(2, r // 2, w), 8, first


def _gather_finish(name, shard, g):
    r, w = shard.shape
    second = [((0, 0, 1),
               (lambda x, y, c, fx=fx, fy=fy: 2 * _chip(x ^ fx, y ^ fy) + c),
               (lambda x, y, c, fx=fx, fy=fy: 2 * _chip(x ^ fx, y ^ fy) + c)) for fx, fy in _OTHER_CHIPS]
    g = _exchange(name + "_d2d", g, 8, [], second, alias=True).reshape(4, r, w)
    mine = lax.broadcasted_iota(jnp.int32, (4, 1, 1), 0) == _chip(lax.axis_index("x"), lax.axis_index("y"))
    return jnp.where(mine, shard[None], g)


def _slot_sum(name, terms, sel, tm, out_slots=None, out_slot=None, also_bf16=False):
    rows, w = terms[0][0].shape[1:]

    def imap(slot):
        if isinstance(slot, int):
            return lambda i, s: (slot, i, 0)
        return lambda i, s: (s[slot[1]], i, 0)

    in_specs = [pl.BlockSpec((None, tm, w), imap(sl)) for _, sl in terms]
    if out_slots is None:
        out_specs, out_shape = [pl.BlockSpec((tm, w), lambda i, s: (i, 0))], [jax.ShapeDtypeStruct((rows, w), F32)]
    else:
        out_specs = [pl.BlockSpec((None, tm, w), imap(out_slot))]
        out_shape = [jax.ShapeDtypeStruct((out_slots, rows, w), F32)]
    if also_bf16:
        out_specs.append(pl.BlockSpec((tm, w), lambda i, s: (i, 0)))
        out_shape.append(jax.ShapeDtypeStruct((rows, w), BF16))
    n_in = len(terms)

    def body(sel_ref, *refs):
        acc = refs[0][...].astype(F32)
        for r in refs[1:n_in]:
            acc = acc + r[...].astype(F32)
        refs[n_in][...] = acc
        if also_bf16:
            refs[n_in + 1][...] = acc.astype(BF16)

    grid_spec = pltpu.PrefetchScalarGridSpec(num_scalar_prefetch=1, grid=(rows // tm,), in_specs=in_specs, out_specs=out_specs)
    out = pl.pallas_call(body, grid_spec=grid_spec, out_shape=out_shape, name=name,
                         compiler_params=_compiler_params())(sel, *[a for a, _ in terms])
    return out if also_bf16 else out[0]


def _reduce_scatter(name, parts, sel):
    f, (src, n_out, sends) = _rs_pair_stage(parts)
    p, (src2, n_out2, sends2) = _rs_chip_stage(name, f, _exchange(name + "_d2d", src, n_out, [], sends), sel)
    return _rs_finish(name, p, _exchange(name + "_ici", src2, n_out2, [], sends2), sel)


def _rs_pair_stage(parts):
    _, _, r, w = parts.shape
    f = parts.reshape(2, 4 * r, w)
    return f, (f, 1, [((0, 0, 1), lambda x, y, c: 1 - c, lambda x, y, c: 0)])


def _rs_chip_stage(name, f, got, sel):
    r, w = f.shape[1] // 4, f.shape[2]
    p, p16 = _slot_sum(name + "_add2", [(f, ("sel", 0)), (got, 0)], sel, 384, also_bf16=True)
    sends = [((fx, fy, 0), (lambda x, y, c, fx=fx, fy=fy: _chip(x ^ fx, y ^ fy)), (lambda x, y, c, k=k: k))
             for k, (fx, fy) in enumerate(_OTHER_CHIPS)]
    return p.reshape(4, r, w), (p16.reshape(4, r, w), 3, sends)


def _rs_finish(name, p, got, sel):
    _, r, w = p.shape
    q = _slot_sum(name + "_add4", [(p, ("sel", 1)), (got, 0), (got, 1), (got, 2)], sel, 496, out_slots=2, out_slot=("sel", 0))
    q = _exchange(name + "_pair", q, 2, [], [((0, 0, 1), lambda x, y, c: c, lambda x, y, c: c)], alias=True)
    return q.reshape(2 * r, w)


def _all_gather_all(name, vec, n_slots, flips, slot_fn):
    sends = [(f, lambda x, y, c: 0, slot_fn) for f in flips]
    g = _exchange(name, vec[None], n_slots, [], sends)
    mine = lax.broadcasted_iota(jnp.int32, (n_slots, 1, 1), 0) == slot_fn(*(lax.axis_index(a) for a in MESH_AXES))
    return jnp.where(mine, vec[None], g)


def _nbr_specs(arr, width, cb, nb):
    return [
        (arr, (BLOCK, width), functools.partial(lambda n, cb: (jnp.maximum(n - 1, 0), cb), cb=cb)),
        (arr, (BLOCK, width), functools.partial(lambda n, cb: (n, cb), cb=cb)),
        (arr, (BLOCK, width), functools.partial(lambda n, cb: (jnp.minimum(n + 1, nb - 1), cb), cb=cb)),
    ]


def _head(h):
    return slice(h * HEAD_DIM, (h + 1) * HEAD_DIM)


def _row_group(n_groups, rows_per_group):
    r = lax.broadcasted_iota(jnp.int32, (n_groups * rows_per_group, 1), 0)
    grp = jnp.zeros_like(r)
    for g in range(1, n_groups):
        grp = grp + jnp.where(r >= g * rows_per_group, 1, 0)
    return grp


def _by_group(grp, vals):
    out = vals[-1]
    for g in range(len(vals) - 2, -1, -1):
        out = jnp.where(grp == g, vals[g], out)
    return out


def _attn_fwd(proj, sink):
    lp = proj.shape[0]
    nb = lp // BLOCK
    kv_w = N_KV_HEADS * HEAD_DIM
    specs = _nbr_specs(proj, kv_w, 4, nb) + _nbr_specs(proj, kv_w, 5, nb)
    specs += [(proj, (BLOCK, kv_w), lambda n: (0, 4)), (proj, (BLOCK, kv_w), lambda n: (0, 5))]
    in_specs = [pl.BlockSpec(memory_space=pltpu.SMEM), pl.BlockSpec((BLOCK, D_MODEL), lambda n: (n, 0))]
    in_specs += [pl.BlockSpec(s[1], s[2]) for s in specs]

    def body(sink_ref, q_ref, kp, kc, kn, vp, vc, vn, km, vm, o_ref, lse_ref):
        n = pl.program_id(0)
        qi = lax.broadcasted_iota(jnp.int32, (BLOCK, 3 * BLOCK), 0)
        sj = lax.broadcasted_iota(jnp.int32, (BLOCK, 3 * BLOCK), 1)
        dist = jnp.abs(qi + BLOCK - sj)
        kpos = (n - 1) * BLOCK + sj
        valid = (dist <= WINDOW) & (kpos >= BLOCK) & (kpos < lp)
        distf = dist.astype(F32)
        kb = jnp.concatenate([kp[...], kc[...], kn[...]], axis=0).astype(BF16)
        vb = jnp.concatenate([vp[...], vc[...], vn[...]], axis=0).astype(BF16)
        kmeta = km[PAD:BLOCK, :].astype(BF16)
        vmeta = vm[PAD:BLOCK, :].astype(BF16)
        valid4 = jnp.concatenate([valid] * Q_GROUP, axis=0)
        distf4 = jnp.concatenate([distf] * Q_GROUP, axis=0)
        grp = _row_group(Q_GROUP, BLOCK)
        for kh in range(N_KV_HEADS):
            ksl = slice(kh * HEAD_DIM, (kh + 1) * HEAD_DIM)
            heads = [kh * Q_GROUP + g for g in range(Q_GROUP)]
            slope = _by_group(grp, [SLOPES[h] for h in heads])
            sk = _by_group(grp, [sink_ref[h] for h in heads])
            q4 = (jnp.concatenate([q_ref[:, _head(h)] for h in heads], axis=0) * ATTN_SCALE).astype(BF16)
            s = jnp.where(valid4, _dot_nt(q4, kb[:, ksl]) - slope * distf4, NEG)
            sm = _dot_nt(q4, kmeta[:, ksl])
            m = jnp.maximum(jnp.maximum(jnp.max(s, axis=1, keepdims=True), jnp.max(sm, axis=1, keepdims=True)), sk)
            e = jnp.exp(s - m)
            em = jnp.exp(sm - m)
            den = jnp.sum(e, axis=1, keepdims=True) + jnp.sum(em, axis=1, keepdims=True) + jnp.exp(sk - m)
            o4 = (_dot(e, vb[:, ksl]) + _dot(em, vmeta[:, ksl])) * (1.0 / den)
            lse4 = m + jnp.log(den)
            for g, h in enumerate(heads):
                o_ref[:, _head(h)] = o4[g * BLOCK:(g + 1) * BLOCK].astype(o_ref.dtype)
                lse_ref[:, h:h + 1] = lse4[g * BLOCK:(g + 1) * BLOCK]

    return pl.pallas_call(
        body, grid=(nb,), in_specs=in_specs,
        out_specs=[pl.BlockSpec((BLOCK, D_MODEL), lambda n: (n, 0)), pl.BlockSpec((BLOCK, N_HEADS), lambda n: (n, 0))],
        out_shape=[jax.ShapeDtypeStruct((lp, D_MODEL), BF16), jax.ShapeDtypeStruct((lp, N_HEADS), F32)],
        name="attn_fwd", compiler_params=_compiler_params())(sink, proj, *[s[0] for s in specs])


def _attn_delta(do, o):
    lp = do.shape[0]
    sel = (lax.broadcasted_iota(jnp.int32, (N_HEADS, D_MODEL), 1) // HEAD_DIM
           == lax.broadcasted_iota(jnp.int32, (N_HEADS, D_MODEL), 0)).astype(BF16)

    def body(do_ref, o_ref, sel_ref, d_ref, dt_ref):
        prod = do_ref[...] * o_ref[...].astype(F32)
        hi = prod.astype(BF16)
        lo = (prod - hi.astype(F32)).astype(BF16)
        d_ref[...] = _dot_nt(hi, sel_ref[...]) + _dot_nt(lo, sel_ref[...])
        dt_ref[...] = _dot_nt(sel_ref[...], hi) + _dot_nt(sel_ref[...], lo)

    return pl.pallas_call(
        body, grid=(lp // BLOCK,),
        in_specs=[pl.BlockSpec((BLOCK, D_MODEL), lambda n: (n, 0)), pl.BlockSpec((BLOCK, D_MODEL), lambda n: (n, 0)),
                  pl.BlockSpec((N_HEADS, D_MODEL), lambda n: (0, 0))],
        out_specs=[pl.BlockSpec((BLOCK, N_HEADS), lambda n: (n, 0)), pl.BlockSpec((N_HEADS, BLOCK), lambda n: (0, n))],
        out_shape=[jax.ShapeDtypeStruct((lp, N_HEADS), F32), jax.ShapeDtypeStruct((N_HEADS, lp), F32)],
        name="attn_delta", compiler_params=_compiler_params())(do, o, sel)


def _attn_bwd(proj, sink, o, lse, do):
    lp = proj.shape[0]
    nb = lp // BLOCK
    kv_w = N_KV_HEADS * HEAD_DIM
    delta, delta_t = _attn_delta(do, o)
    lse_t = lse.T
    row_nbrs = lambda arr: [
        (arr, (N_HEADS, BLOCK), lambda n: (0, jnp.maximum(n - 1, 0))), (arr, (N_HEADS, BLOCK), lambda n: (0, n)),
        (arr, (N_HEADS, BLOCK), lambda n: (0, jnp.minimum(n + 1, nb - 1)))]
    specs = (_nbr_specs(proj, D_MODEL, 0, nb) + _nbr_specs(proj, kv_w, 4, nb) + _nbr_specs(proj, kv_w, 5, nb)
             + [(proj, (BLOCK, kv_w), lambda n: (0, 4)), (proj, (BLOCK, kv_w), lambda n: (0, 5))]
             + _nbr_specs(do, D_MODEL, 0, nb) + [(lse, (BLOCK, N_HEADS), lambda n: (n, 0)), (delta, (BLOCK, N_HEADS), lambda n: (n, 0))]
             + row_nbrs(lse_t) + row_nbrs(delta_t))
    in_specs = [pl.BlockSpec(memory_space=pltpu.SMEM)] + [pl.BlockSpec(s[1], s[2]) for s in specs]

    def body(sink_ref, qp, qc, qn, kp, kc, kn, vp, vc, vn, km, vm, dop, doc, don, lc, dc, ltp, ltc, ltn, dtp, dtc, dtn,
             dq_ref, dk_ref, dv_ref, dkm_ref, dvm_ref, dsk_ref):
        n = pl.program_id(0)

        @pl.when(n == 0)
        def _():
            dkm_ref[...] = jnp.zeros_like(dkm_ref)
            dvm_ref[...] = jnp.zeros_like(dvm_ref)
            dsk_ref[...] = jnp.zeros_like(dsk_ref)

        qi = lax.broadcasted_iota(jnp.int32, (BLOCK, 3 * BLOCK), 0)
        sj = lax.broadcasted_iota(jnp.int32, (BLOCK, 3 * BLOCK), 1)
        dist_q = jnp.abs(qi + BLOCK - sj)
        kpos = (n - 1) * BLOCK + sj
        valid_q = (dist_q <= WINDOW) & (kpos >= BLOCK) & (kpos < lp)
        distf_q = dist_q.astype(F32)
        bi = lax.broadcasted_iota(jnp.int32, (BLOCK, 3 * BLOCK), 1)
        kj = lax.broadcasted_iota(jnp.int32, (BLOCK, 3 * BLOCK), 0)
        dist_k = jnp.abs(bi - BLOCK - kj)
        qpos = (n - 1) * BLOCK + bi
        valid_k = (dist_k <= WINDOW) & (qpos >= 0) & (qpos < lp) & (n >= 1)
        distf_k = dist_k.astype(F32)

        kb = jnp.concatenate([kp[...], kc[...], kn[...]], axis=0).astype(BF16)
        vb = jnp.concatenate([vp[...], vc[...], vn[...]], axis=0).astype(BF16)
        kcur = kc[...].astype(BF16)
        vcur = vc[...].astype(BF16)
        kmeta = km[PAD:BLOCK, :].astype(BF16)
        vmeta = vm[PAD:BLOCK, :].astype(BF16)
        lane = lax.broadcasted_iota(jnp.int32, (1, BLOCK), 1)
        dsink = jnp.zeros((1, BLOCK), F32)
        valid_q4 = jnp.concatenate([valid_q] * Q_GROUP, axis=0)
        distf_q4 = jnp.concatenate([distf_q] * Q_GROUP, axis=0)
        valid_k4 = jnp.concatenate([valid_k] * Q_GROUP, axis=1)
        distf_k4 = jnp.concatenate([distf_k] * Q_GROUP, axis=1)
        grp_q = _row_group(Q_GROUP, BLOCK)
        lane_k = lax.broadcasted_iota(jnp.int32, (1, Q_GROUP * 3 * BLOCK), 1)
        grp_k = sum(jnp.where(lane_k >= g * 3 * BLOCK, 1, 0) for g in range(1, Q_GROUP))
        for kh in range(N_KV_HEADS):
            ksl = slice(kh * HEAD_DIM, (kh + 1) * HEAD_DIM)
            heads = [kh * Q_GROUP + g for g in range(Q_GROUP)]
            slopes = [SLOPES[h] for h in heads]
            q4 = (jnp.concatenate([qc[:, _head(h)] for h in heads], axis=0) * ATTN_SCALE).astype(BF16)
            do4 = jnp.concatenate([doc[:, _head(h)] for h in heads], axis=0)
            delta = jnp.concatenate([dc[:, h:h + 1] for h in heads], axis=0)
            lse4 = jnp.concatenate([lc[:, h:h + 1] for h in heads], axis=0)
            s = _dot_nt(q4, kb[:, ksl]) - _by_group(grp_q, slopes) * distf_q4
            p = jnp.exp(jnp.where(valid_q4, s, NEG) - lse4)
            pm = jnp.exp(_dot_nt(q4, kmeta[:, ksl]) - lse4)
            ps = jnp.exp(_by_group(grp_q, [sink_ref[h] for h in heads]) - lse4)
            do4b = do4.astype(BF16)
            ds = p * (_dot_nt(do4b, vb[:, ksl]) - delta)
            dsm = pm * (_dot_nt(do4b, vmeta[:, ksl]) - delta)
            dq4 = ATTN_SCALE * (_dot(ds, kb[:, ksl]) + _dot(dsm, kmeta[:, ksl]))
            dsk4 = ps * delta
            for g, h in enumerate(heads):
                dq_ref[:, _head(h)] = dq4[g * BLOCK:(g + 1) * BLOCK].astype(dq_ref.dtype)
                dsink = dsink + jnp.where(lane == h, -jnp.sum(dsk4[g * BLOCK:(g + 1) * BLOCK]), 0.0)
            dkm_ref[:, ksl] += _dot_tn(dsm, q4)
            dvm_ref[:, ksl] += _dot_tn(pm, do4b)
            band = lambda a, b, c_: jnp.concatenate([r[:, _head(h)] for h in heads for r in (a, b, c_)], axis=0)
            qb4 = (band(qp, qc, qn) * ATTN_SCALE).astype(BF16)
            dob4b = band(dop, doc, don).astype(BF16)
            delta_b = jnp.concatenate([r[h:h + 1, :] for h in heads for r in (dtp, dtc, dtn)], axis=1)
            lse_b = jnp.concatenate([r[h:h + 1, :] for h in heads for r in (ltp, ltc, ltn)], axis=1)
            st = _dot_nt(kcur[:, ksl], qb4) - _by_group(grp_k, slopes) * distf_k4
            pt = jnp.exp(jnp.where(valid_k4, st, NEG) - lse_b)
            dv_ref[:, ksl] = _dot(pt, dob4b)
            dst = pt * (_dot_nt(vcur[:, ksl], dob4b) - delta_b)
            dk_ref[:, ksl] = _dot(dst, qb4)
        dsk_ref[...] += dsink

    blk = lambda w: pl.BlockSpec((BLOCK, w), lambda n: (n, 0))
    fix = lambda shp: pl.BlockSpec(shp, lambda n: (0, 0))
    return pl.pallas_call(
        body, grid=(nb,), in_specs=in_specs,
        out_specs=[blk(D_MODEL), blk(kv_w), blk(kv_w), fix((N_META, kv_w)), fix((N_META, kv_w)), fix((1, BLOCK))],
        out_shape=[jax.ShapeDtypeStruct((lp, D_MODEL), BF16), jax.ShapeDtypeStruct((lp, kv_w), F32),
                   jax.ShapeDtypeStruct((lp, kv_w), F32), jax.ShapeDtypeStruct((N_META, kv_w), F32),
                   jax.ShapeDtypeStruct((N_META, kv_w), F32), jax.ShapeDtypeStruct((1, BLOCK), F32)],
        name="attn_bwd", compiler_params=_compiler_params())(sink, *[s[0] for s in specs])


N_SEG = 8


def _ssm_tile(lp, long_tile):
    return 688 if long_tile and lp % 688 == 0 else 384


def _to_segments(a):
    lp, w = a.shape
    return a.reshape(N_SEG, lp // N_SEG, w).transpose(1, 0, 2).reshape(lp, w)


def _from_segments(a):
    lp, w = a.shape
    return a.reshape(lp // N_SEG, N_SEG, w).transpose(1, 0, 2).reshape(lp, w)


def _complex_power(ar, ai, n):
    rr, ri = jnp.ones_like(ar), jnp.zeros_like(ai)
    while n:
        if n & 1:
            rr, ri = rr * ar - ri * ai, rr * ai + ri * ar
        ar, ai = ar * ar - ai * ai, 2.0 * ar * ai
        n >>= 1
    return rr, ri


def _segment_starts(finals, a_seg, reverse):
    fr, fi = finals[:, :N_STATE], finals[:, N_STATE:]
    ar, ai = a_seg[:, :N_STATE], a_seg[:, N_STATE:]
    row = lax.broadcasted_iota(jnp.int32, (N_SEG, N_STATE), 0)
    pr = jnp.zeros((1, N_STATE), F32)
    pi = jnp.zeros((1, N_STATE), F32)
    sr = jnp.zeros((N_SEG, N_STATE), F32)
    si = jnp.zeros((N_SEG, N_STATE), F32)
    for s in (range(N_SEG - 1, -1, -1) if reverse else range(N_SEG)):
        sr = jnp.where(row == s, pr, sr)
        si = jnp.where(row == s, pi, si)
        pr, pi = fr[s:s + 1] + ar * pr - ai * pi, fi[s:s + 1] + ar * pi + ai * pr
    return jnp.concatenate([sr, si], axis=1)


def _recurrence(buf_ref, st_ref, a_ref, reverse):
    steps = buf_ref.shape[0] // N_SEG
    half = N_STATE // 2
    for c0 in (0, half):
        re = slice(c0, c0 + half)
        im = slice(N_STATE + c0, N_STATE + c0 + half)
        ar = jnp.broadcast_to(a_ref[:, re], (N_SEG, half))
        ai = jnp.broadcast_to(a_ref[:, im], (N_SEG, half))

        def step(k, carry, re=re, im=im, ar=ar, ai=ai):
            xr, xi = carry
            r0 = pl.multiple_of((steps - 1 - k if reverse else k) * N_SEG, N_SEG)
            nr = ar * xr - ai * xi + buf_ref[pl.ds(r0, N_SEG), re]
            ni = ar * xi + ai * xr + buf_ref[pl.ds(r0, N_SEG), im]
            buf_ref[pl.ds(r0, N_SEG), re] = nr
            buf_ref[pl.ds(r0, N_SEG), im] = ni
            return nr, ni

        xr, xi = lax.fori_loop(0, steps, step, (st_ref[:, re], st_ref[:, im]), unroll=2)
        st_ref[:, re] = xr
        st_ref[:, im] = xi


def _copy_in(pairs, sem):
    cps = [pltpu.make_async_copy(src, dst, sem.at[j]) for j, (src, dst) in enumerate(pairs)]
    for cp in cps:
        cp.start()
    for cp in cps:
        cp.wait()


def _ssm_fwd_dir(name, u_seg, wb, wc, a, a_seg, reverse):
    lp = u_seg.shape[0]
    tile_rows = _ssm_tile(lp, True)
    nt = lp // tile_rows
    tile = (lambda i: nt - 1 - i) if reverse else (lambda i: i)
    first = tile(0)
    held = lambda p, i: (p * tile(i) + (1 - p) * first, 0)

    def body(u_ref, wb_hbm, wc_hbm, a_ref, aseg_ref, x_ref, y_ref, wb_ref, wc_ref, buf_ref, st_ref, sem):
        p, i = pl.program_id(0), pl.program_id(1)

        @pl.when((p == 0) & (i == 0))
        def _():
            _copy_in([(wb_hbm, wb_ref), (wc_hbm, wc_ref)], sem)
            st_ref[...] = jnp.zeros_like(st_ref)

        @pl.when((p == 1) & (i == 0))
        def _():
            st_ref[...] = _segment_starts(st_ref[...], aseg_ref[...], reverse)

        def states_into(dst_ref):
            for j in range(SUPER):
                part = _dot(u_ref[:, 128 * j:128 * (j + 1)], wb_ref[j])
                dst_ref[:, 512 * j:512 * (j + 1)] = part[:, :512]
                dst_ref[:, N_STATE + 512 * j:N_STATE + 512 * (j + 1)] = part[:, 512:]
            _recurrence(dst_ref, st_ref, a_ref, reverse)

        @pl.when(p == 0)
        def _():
            states_into(buf_ref)

        @pl.when(p == 1)
        def _():
            states_into(x_ref)
            for j in range(SUPER):
                y_ref[:, 128 * j:128 * (j + 1)] = (_dot(x_ref[:, _re(j)], wc_ref[j, :512, :])
                                                   + _dot(x_ref[:, _im(j)], wc_ref[j, 512:, :]))

    fix = lambda shp: pl.BlockSpec(shp, lambda p, i: (0, 0))
    return pl.pallas_call(
        body, grid=(2, nt),
        in_specs=[pl.BlockSpec((tile_rows, SSM_WIDTH), lambda p, i: (tile(i), 0)), pl.BlockSpec(memory_space=pl.ANY),
                  pl.BlockSpec(memory_space=pl.ANY), fix((1, 2 * N_STATE)), fix((1, 2 * N_STATE))],
        out_specs=[pl.BlockSpec((tile_rows, 2 * N_STATE), held), pl.BlockSpec((tile_rows, SSM_WIDTH), held)],
        out_shape=[jax.ShapeDtypeStruct((lp, 2 * N_STATE), F32), jax.ShapeDtypeStruct((lp, SSM_WIDTH), F32)],
        scratch_shapes=[pltpu.VMEM(wb.shape, BF16), pltpu.VMEM(wc.shape, BF16), pltpu.VMEM((tile_rows, 2 * N_STATE), F32),
                        pltpu.VMEM((N_SEG, 2 * N_STATE), F32), pltpu.SemaphoreType.DMA((2,))],
        name=name, compiler_params=pltpu.CompilerParams(dimension_semantics=("arbitrary", "arbitrary"),
                                                        vmem_limit_bytes=V7X_VMEM_LIMIT_BYTES))(
        u_seg, wb.astype(BF16), wc.astype(BF16), a, a_seg)


def _ssm_bwd_dir(name, dys_seg, u_seg, x_seg, wb, wc, a_conj, a_seg_conj, fwd_reverse):
    lp = u_seg.shape[0]
    tile_rows = _ssm_tile(lp, False)
    nt = lp // tile_rows
    steps = tile_rows // N_SEG
    reverse = not fwd_reverse
    tile = (lambda i: nt - 1 - i) if reverse else (lambda i: i)
    first = tile(0)
    held = lambda p, i: (p * tile(i) + (1 - p) * first, 0)
    n_slab = lp // N_SEG
    if fwd_reverse:
        halo = lambda p, i: (p * jnp.minimum((tile(i) + 1) * steps, n_slab - 1), 0)
        edge = lambda p, i: (0, 0)
    else:
        halo = lambda p, i: (p * jnp.maximum(tile(i) * steps - 1, 0), 0)
        edge = lambda p, i: (n_slab - 1, 0)

    def body(dy_ref, u_ref, x_ref, halo_ref, edge_ref, wb_hbm, wc_hbm, a_ref, aseg_ref, du_ref, dwb_ref, dwc_ref, ga_ref,
             wb_ref, wc_ref, buf_ref, st_ref, sem):
        p, i = pl.program_id(0), pl.program_id(1)

        @pl.when((p == 0) & (i == 0))
        def _():
            _copy_in([(wb_hbm, wb_ref), (wc_hbm, wc_ref)], sem)
            st_ref[...] = jnp.zeros_like(st_ref)
            dwb_ref[...] = jnp.zeros_like(dwb_ref)
            dwc_ref[...] = jnp.zeros_like(dwc_ref)
            ga_ref[...] = jnp.zeros_like(ga_ref)

        @pl.when((p == 1) & (i == 0))
        def _():
            st_ref[...] = _segment_starts(st_ref[...], aseg_ref[...], reverse)

        for j in range(SUPER):
            part = _dot_nt(dy_ref[:, 128 * j:128 * (j + 1)], wc_ref[j])
            buf_ref[:, 512 * j:512 * (j + 1)] = part[:, :512]
            buf_ref[:, N_STATE + 512 * j:N_STATE + 512 * (j + 1)] = part[:, 512:]
        _recurrence(buf_ref, st_ref, a_ref, reverse)

        @pl.when(p == 1)
        def _():
            for j in range(SUPER):
                ch = slice(128 * j, 128 * (j + 1))
                du_ref[:, ch] = (_dot_nt(buf_ref[:, _re(j)], wb_ref[j, :, :512]) + _dot_nt(buf_ref[:, _im(j)], wb_ref[j, :, 512:]))
                dwb_ref[ch, :512] += _dot_tn(u_ref[:, ch], buf_ref[:, _re(j)])
                dwb_ref[ch, 512:] += _dot_tn(u_ref[:, ch], buf_ref[:, _im(j)])
                dwc_ref[ch, :512] += _dot_tn(dy_ref[:, ch], x_ref[:, _re(j)])
                dwc_ref[ch, 512:] += _dot_tn(dy_ref[:, ch], x_ref[:, _im(j)])
            row = lax.broadcasted_iota(jnp.int32, (N_SEG, 2 * N_STATE), 0)
            if fwd_reverse:
                wrap = jnp.where(row == N_SEG - 1, 0.0, pltpu.roll(edge_ref[...], N_SEG - 1, axis=0))
                open_slab = jnp.where(tile(i) == nt - 1, wrap, halo_ref[...])
                before = lambda cols: jnp.concatenate([x_ref[N_SEG:, cols], open_slab[:, cols]], axis=0)
            else:
                wrap = jnp.where(row == 0, 0.0, pltpu.roll(edge_ref[...], 1, axis=0))
                open_slab = jnp.where(tile(i) == 0, wrap, halo_ref[...])
                before = lambda cols: jnp.concatenate([open_slab[:, cols], x_ref[:tile_rows - N_SEG, cols]], axis=0)
            half = N_STATE // 2
            for c0 in (0, half):
                re = slice(c0, c0 + half)
                im = slice(N_STATE + c0, N_STATE + c0 + half)
                gr, gi = buf_ref[:, re], buf_ref[:, im]
                br, bi = before(re), before(im)
                fold = lambda v: jnp.sum(v.reshape(steps, N_SEG, half), axis=0)
                ga_ref[:, re] += fold(gr * br + gi * bi)
                ga_ref[:, im] += fold(gi * br - gr * bi)

    fix = lambda shp: pl.BlockSpec(shp, lambda p, i: (0, 0))
    row_tile = lambda w: pl.BlockSpec((tile_rows, w), lambda p, i: (tile(i), 0))
    return pl.pallas_call(
        body, grid=(2, nt),
        in_specs=[row_tile(SSM_WIDTH), row_tile(SSM_WIDTH), pl.BlockSpec((tile_rows, 2 * N_STATE), held),
                  pl.BlockSpec((N_SEG, 2 * N_STATE), halo), pl.BlockSpec((N_SEG, 2 * N_STATE), edge),
                  pl.BlockSpec(memory_space=pl.ANY), pl.BlockSpec(memory_space=pl.ANY), fix((1, 2 * N_STATE)), fix((1, 2 * N_STATE))],
        out_specs=[pl.BlockSpec((tile_rows, SSM_WIDTH), held), fix((SSM_WIDTH, 1024)), fix((SSM_WIDTH, 1024)),
                   fix((N_SEG, 2 * N_STATE))],
        out_shape=[jax.ShapeDtypeStruct((lp, SSM_WIDTH), F32), jax.ShapeDtypeStruct((SSM_WIDTH, 1024), F32),
                   jax.ShapeDtypeStruct((SSM_WIDTH, 1024), F32), jax.ShapeDtypeStruct((N_SEG, 2 * N_STATE), F32)],
        scratch_shapes=[pltpu.VMEM(wb.shape, BF16), pltpu.VMEM(wc.shape, BF16), pltpu.VMEM((tile_rows, 2 * N_STATE), F32),
                        pltpu.VMEM((N_SEG, 2 * N_STATE), F32), pltpu.SemaphoreType.DMA((2,))],
        name=name, compiler_params=pltpu.CompilerParams(dimension_semantics=("arbitrary", "arbitrary"),
                                                        vmem_limit_bytes=V7X_VMEM_LIMIT_BYTES))(
        dys_seg, u_seg, x_seg, x_seg, x_seg, wb.astype(BF16), wc.astype(BF16), a_conj, a_seg_conj)


def _ssm_prep(lam_re, lam_im, log_dt, b_re, b_im, c_re, c_im):
    dt = jnp.exp(log_dt)[:, None]
    er = jnp.exp(lam_re * dt)
    ar, ai = er * jnp.cos(lam_im * dt), er * jnp.sin(lam_im * dt)
    nr, ni = ar - 1.0, ai
    den = lam_re * lam_re + lam_im * lam_im
    cr, ci = (nr * lam_re + ni * lam_im) / den, (ni * lam_re - nr * lam_im) / den
    bbr = cr[:, :, None] * b_re - ci[:, :, None] * b_im
    bbi = cr[:, :, None] * b_im + ci[:, :, None] * b_re
    eye = jnp.eye(8, dtype=F32)

    def in_map(b):
        b = b.reshape(SUPER, 8, SSM_STATE, SSM_GROUP_CH).transpose(0, 1, 3, 2)
        return (b[:, :, :, None, :] * eye[None, :, None, :, None]).reshape(SUPER, 128, 512)

    def out_map(cm):
        cm = cm.reshape(SUPER, 8, SSM_GROUP_CH, SSM_STATE).transpose(0, 1, 3, 2)
        return (cm[:, :, :, None, :] * eye[None, :, None, :, None]).reshape(SUPER, 512, 128)

    wb = jnp.concatenate([in_map(bbr), in_map(bbi)], axis=2)
    wc = jnp.concatenate([out_map(c_re), -out_map(c_im)], axis=1)
    return ar.reshape(1, N_STATE), ai.reshape(1, N_STATE), wb, wc


def _re(j):
    return slice(512 * j, 512 * (j + 1))


def _im(j):
    return slice(N_STATE + 512 * j, N_STATE + 512 * (j + 1))


def _row_ids(i, tm, width):
    return i * tm + lax.broadcasted_iota(jnp.int32, (tm, width), 0)


_FF_CHUNKS = (slice(0, D_FF // 2), slice(D_FF // 2, D_FF))


def _ffn_fwd(tag, h, gain, wg, wu, wd, next_shard=None):
    def up(i, h_ref, wg_ref, wu_ref, g_ref, n_ref, b_ref, silu_ref, dsilu_ref, act_ref):
        n = _rms_fwd(h_ref[...], g_ref[...]).astype(BF16)
        n_ref[...] = n
        for cols in (slice(0, D_FF),):
            a = _dot(n, wg_ref[:, cols])
            b = _dot(n, wu_ref[:, cols])
            sg = _sigmoid(a)
            silu = a * sg
            b_ref[:, cols] = b.astype(BF16)
            silu_ref[:, cols] = silu.astype(BF16)
            dsilu_ref[:, cols] = (sg * (1.0 + a * (1.0 - sg))).astype(BF16)
            act_ref[:, cols] = (silu * b).astype(BF16)

    outs = [(D_MODEL, BF16)] + [(D_FF, BF16)] * 4
    gathered = None
    if next_shard is None:
        n, b, silu, dsilu, act = _rowk(tag + "_up", up, [_rows(h)], [wg, wu, gain], outs, tm=192, writes_outs=True)
    else:
        n, b, silu, dsilu, act, got = _rowk(tag + "_up_gather", up, [_rows(h)], [wg, wu, gain], outs, tm=192,
                                            writes_outs=True, hosted=_gather_ici_stage(next_shard))
        gathered = _gather_finish("gather_w", next_shard, got)
    out = _rowk(tag + "_down", lambda i, act_ref, h_ref, wd_ref: h_ref[...] + 0.5 * _dot(act_ref[...], wd_ref[...]),
                [_rows(act), _rows(h)], [wd], [(D_MODEL, F32)])[0]
    return out, (h, n, b, silu, dsilu, act), gathered


def _ffn_bwd(tag, dh, saved, gain, wg, wu, wd, pending=None, sel=None):
    h, n, b, silu, dsilu, act = saved
    hosted1 = hosted2 = reduced = None
    if pending is not None:
        f, hosted1 = _rs_pair_stage(pending)
        tag = tag + "_reduce"

    def bwd1(i, dh_ref, b_ref, silu_ref, dsilu_ref, wd_ref, da_ref, db_ref):
        dhb = (0.5 * dh_ref[...]).astype(BF16)
        for cols in _FF_CHUNKS:
            dact = _dot_nt(dhb, wd_ref[cols, :])
            da_ref[:, cols] = (dact * b_ref[:, cols].astype(F32) * dsilu_ref[:, cols].astype(F32)).astype(BF16)
            db_ref[:, cols] = (dact * silu_ref[:, cols].astype(F32)).astype(BF16)

    res = _rowk(tag + "_bwd_act", bwd1, [_rows(dh), _rows(b), _rows(silu), _rows(dsilu)], [wd], [(D_FF, BF16)] * 2,
                writes_outs=True, hosted=hosted1)
    da, db = res[0], res[1]
    if pending is not None:
        p, hosted2 = _rs_chip_stage("reduce_w", f, res[2], sel)

    def bwd2(i, da_ref, db_ref, h_ref, dh_ref, wg_ref, wu_ref, g_ref):
        dn = _dot_nt(da_ref[...], wg_ref[...]) + _dot_nt(db_ref[...], wu_ref[...])
        dx, dg = _rms_bwd(h_ref[...], g_ref[...], dn)
        return dh_ref[...] + dx, dg

    res = _rowk(tag + "_bwd_in", bwd2, [_rows(da), _rows(db), _rows(h), _rows(dh)], [wg, wu, gain],
                [(D_MODEL, F32)], accs=[(1, D_MODEL)], hosted=hosted2)
    dh_in, dgain = res[0], res[1]
    if pending is not None:
        reduced = _rs_finish("reduce_w", p, res[2], sel)
    dwd = _mm_tn("ffn_dwd", act, dh, tk=D_FF // 2, tn=D_MODEL, scale=0.5)
    dwg = _mm_tn("ffn_dwg", n, da, tk=D_MODEL, tn=D_FF // 2)
    dwu = _mm_tn("ffn_dwu", n, db, tk=D_MODEL, tn=D_FF // 2)
    return dh_in, dgain, dwg, dwu, dwd, reduced


def _mixer_fwd(h, lw, ssm):
    lp = h.shape[0]
    def mix_in(i, h_ref, w_ref, g_ref):
        nv = _rms_fwd(h_ref[...], g_ref[...]).astype(BF16)
        return nv, _dot(nv, w_ref[...])

    n, proj = _rowk("mix_in", mix_in, [_rows(h)], [lw["w_in"], lw["mix_norm"]], [(D_MODEL, BF16), (4 * D_MODEL, F32)])
    yattn, lse = _attn_fwd(proj, lw["attn_sink"])
    u_seg = _to_segments(proj[:, 3 * SSM_WIDTH:4 * SSM_WIDTH])
    xs, ydir = [], []
    for d in range(2):
        x_seg, y_seg = _ssm_fwd_dir(f"ssm_fwd{d}", u_seg, ssm[d]["wb"], ssm[d]["wc"], ssm[d]["a"], ssm[d]["a_seg"],
                                    reverse=(d == 1))
        xs.append(x_seg)
        ydir.append(y_seg)

    def ssm_out(i, y0_ref, y1_ref, u_ref, d_ref, wglu_ref):
        ys = y0_ref[...] + y1_ref[...] + d_ref[...] * u_ref[...]
        z = _gelu(ys)
        return ys, z * _sigmoid(_dot(z, wglu_ref[...]))

    ys, yssm_seg = _rowk("ssm_out", ssm_out, [_rows(ydir[0]), _rows(ydir[1]), _rows(u_seg)],
                         [lw["ssm_d"], lw["ssm_w_glu"]], [(SSM_WIDTH, F32), (SSM_WIDTH, BF16)])
    yssm = _from_segments(yssm_seg)

    def merge(i, ys_ref, ya_ref, gs_ref, ga_ref, wbs_ref, wba_ref):
        bs = _dot(ys_ref[...], wbs_ref[...])
        ba = _dot(ya_ref[...], wba_ref[...])
        m = _sigmoid(gs_ref[...]) * bs + _sigmoid(ga_ref[...]) * ba
        return bs, ba, jnp.where(_row_ids(i, ROW_TILE, D_MODEL) >= PAD, m, 0.0)

    bs, ba, merged = _rowk("mix_merge", merge, [_rows(yssm), _rows(yattn), _rows(proj, D_MODEL, 2), _rows(proj, D_MODEL, 3)],
                           [lw["w_branch_ssm"], lw["w_branch_attn"]], [(D_MODEL, BF16)] * 3)
    out = _rowk("mix_out", lambda i, m_ref, h_ref, w_ref: h_ref[...] + _dot(m_ref[...], w_ref[...]),
                [_rows(merged), _rows(h)], [lw["w_out"]], [(D_MODEL, F32)])[0]
    return out, (h, n, proj, yattn, lse, u_seg, xs, ys, yssm, bs, ba, merged)


def _mixer_bwd(dh, saved, lw, ssm):
    h, n, proj, yattn, lse, u_seg, xs, ys, yssm, bs, ba, merged = saved

    def bwd1(i, dh_ref, gs_ref, ga_ref, bs_ref, ba_ref, w_ref):
        dm = _dot_nt(dh_ref[...], w_ref[...])
        dm = jnp.where(_row_ids(i, ROW_TILE, D_MODEL) >= PAD, dm, 0.0)
        sgs = _sigmoid(gs_ref[...])
        sga = _sigmoid(ga_ref[...])
        return (dm * sgs, dm * sga, dm * bs_ref[...].astype(F32) * sgs * (1.0 - sgs),
                dm * ba_ref[...].astype(F32) * sga * (1.0 - sga))

    dbs, dba, dgs, dga = _rowk("mix_bwd_merge", bwd1,
                               [_rows(dh), _rows(proj, D_MODEL, 2), _rows(proj, D_MODEL, 3), _rows(bs), _rows(ba)],
                               [lw["w_out"]], [(D_MODEL, BF16)] * 4)
    dw_out = _mm_tn("mix_dw_out", merged, dh, tk=D_MODEL, tn=D_MODEL)
    dw_bs = _mm_tn("mix_dw_bs", yssm, dbs, tk=SSM_WIDTH, tn=D_MODEL)
    dw_ba = _mm_tn("mix_dw_ba", yattn, dba, tk=D_MODEL, tn=D_MODEL)

    def bwd2(i, dbs_ref, dba_ref, wbs_ref, wba_ref):
        return _dot_nt(dba_ref[...], wba_ref[...]), _dot_nt(dbs_ref[...], wbs_ref[...])

    dyattn, dyssm = _rowk("mix_bwd_branches", bwd2, [_rows(dbs), _rows(dba)], [lw["w_branch_ssm"], lw["w_branch_attn"]],
                          [(D_MODEL, F32), (SSM_WIDTH, F32)])

    def bwd3(i, dyssm_ref, ys_ref, u_ref, wglu_ref):
        ysv = ys_ref[...]
        z = _gelu(ysv)
        sg = _sigmoid(_dot(z, wglu_ref[...]))
        dt = dyssm_ref[...] * z * sg * (1.0 - sg)
        dz = dyssm_ref[...] * sg + _dot_nt(dt, wglu_ref[...])
        dys = dz * _gelu_grad(ysv)
        return dys, z, dt, jnp.sum(dys * u_ref[...], axis=0, keepdims=True)

    dys, z, dt, dd = _rowk("mix_bwd_ssm_out", bwd3, [_rows(_to_segments(dyssm)), _rows(ys), _rows(u_seg)], [lw["ssm_w_glu"]],
                           [(SSM_WIDTH, F32), (SSM_WIDTH, BF16), (SSM_WIDTH, BF16)], accs=[(1, SSM_WIDTH)])
    dw_glu = _mm_tn("mix_dw_glu", z, dt, tk=SSM_WIDTH, tn=SSM_WIDTH)

    dus, ssm_cot = [], []
    for d in range(2):
        du_d, dwb, dwc_t, ga = _ssm_bwd_dir(f"ssm_bwd{d}", dys, u_seg, xs[d], ssm[d]["wb"], ssm[d]["wc"], ssm[d]["a_conj"],
                                            ssm[d]["a_seg_conj"], fwd_reverse=(d == 1))
        dus.append(du_d)
        ga = jnp.sum(ga, axis=0, keepdims=True)
        ssm_cot.append((ga[:, :N_STATE], ga[:, N_STATE:], dwb.reshape(SUPER, 128, 1024),
                        dwc_t.reshape(SUPER, 128, 1024).transpose(0, 2, 1)))

    du_seg = _rowk("ssm_bwd_du", lambda i, a_ref, b_ref, dys_ref, d_ref: a_ref[...] + b_ref[...] + d_ref[...] * dys_ref[...],
                   [_rows(dus[0]), _rows(dus[1]), _rows(dys)], [lw["ssm_d"]], [(SSM_WIDTH, BF16)])[0]
    du = _from_segments(du_seg)

    dq, dk, dv, dkm, dvm, dsink = _attn_bwd(proj, lw["attn_sink"], yattn, lse, dyattn)

    def dproj_fn(i, dq_ref, dk_ref, dv_ref, du_ref, dgs_ref, dga_ref, dkm_ref, dvm_ref):
        first = jnp.where(i == 0, 1.0, 0.0)
        zeros = lambda r: jnp.zeros((r, N_KV_HEADS * HEAD_DIM), F32)
        place = lambda m: jnp.concatenate([zeros(PAD), m[...] * first, zeros(ROW_TILE - BLOCK)], axis=0)
        dp = jnp.concatenate([dq_ref[...].astype(F32), dk_ref[...] + place(dkm_ref), dv_ref[...] + place(dvm_ref),
                              du_ref[...].astype(F32), dgs_ref[...].astype(F32), dga_ref[...].astype(F32)], axis=1)
        return jnp.where(_row_ids(i, ROW_TILE, 4 * D_MODEL) >= PAD, dp, 0.0)

    dproj = _rowk("mix_bwd_dproj", dproj_fn, [_rows(dq), _rows(dk), _rows(dv), _rows(du), _rows(dgs), _rows(dga)],
                  [dkm, dvm], [(4 * D_MODEL, BF16)])[0]

    def bwd_in(i, dp_ref, h_ref, dh_ref, w_ref, g_ref):
        dx, dg = _rms_bwd(h_ref[...], g_ref[...], _dot_nt(dp_ref[...], w_ref[...]))
        return dh_ref[...] + dx, dg

    dh_in, dgain = _rowk("mix_bwd_in", bwd_in, [_rows(dproj), _rows(h), _rows(dh)], [lw["w_in"], lw["mix_norm"]],
                         [(D_MODEL, F32)], accs=[(1, D_MODEL)])
    dw_in = _mm_tn("mix_dw_in", n, dproj, tk=D_MODEL, tn=2 * D_MODEL)
    grads = {"w_out": dw_out, "w_branch_ssm": dw_bs, "w_branch_attn": dw_ba, "ssm_w_glu": dw_glu, "w_in": dw_in,
             "mix_norm": dgain, "ssm_d": dd, "attn_sink": dsink[0, :N_HEADS]}
    return dh_in, grads, ssm_cot


def _loss_head(h, gain, target):
    lp = h.shape[0]

    def fn(i, h_ref, t_ref, g_ref):
        x = h_ref[...]
        y = _rms_fwd(x, g_ref[...])
        live = jnp.where(i == 0, 0.0, 1.0)
        dy = (y - t_ref[...]) * live
        loss = 0.5 * jnp.sum(dy * dy) / D_MODEL
        dx, dg = _rms_bwd(x, g_ref[...], dy * (1.0 / D_MODEL))
        return dx, jnp.full((1, BLOCK), loss, F32), dg

    tgt = (target, (BLOCK, D_MODEL), lambda i: (jnp.maximum(i - 1, 0), 0))
    return _rowk("loss_head", fn, [_rows(h), tgt], [gain], [(D_MODEL, F32)], accs=[(1, BLOCK), (1, D_MODEL)], tm=BLOCK)


def _adamw(name, w, g, m, v, tm):
    def fn(i, w_ref, g_ref, m_ref, v_ref):
        gv = g_ref[...]
        mn = ADAM_B1 * m_ref[...] + (1.0 - ADAM_B1) * gv
        vn = ADAM_B2 * v_ref[...] + (1.0 - ADAM_B2) * (gv * gv)
        m_hat = mn / (1.0 - ADAM_B1 ** ADAM_STEP)
        v_hat = vn / (1.0 - ADAM_B2 ** ADAM_STEP)
        return -ADAM_LR * (m_hat / (jnp.sqrt(v_hat) + ADAM_EPS) + ADAM_WD * w_ref[...]), mn, vn

    wd = w.shape[1]
    return _rowk(name, fn, [_rows(w), _rows(g), _rows(m), _rows(v)], [], [(wd, F32)] * 3, tm=tm)


def _shard_rows(name):
    return {"ffn1_w_gate": 704, "ffn1_w_up": 704, "ffn1_w_down": 704, "ffn2_w_gate": 704, "ffn2_w_up": 704, "ffn2_w_down": 704,
            "w_in": 1024, "ssm_w_glu": 64, "w_branch_ssm": 128, "w_branch_attn": 256, "w_out": 256}[name]


def _full_shape(name):
    return {"ffn1_w_gate": (D_MODEL, D_FF), "ffn1_w_up": (D_MODEL, D_FF), "ffn1_w_down": (D_FF, D_MODEL),
            "ffn2_w_gate": (D_MODEL, D_FF), "ffn2_w_up": (D_MODEL, D_FF), "ffn2_w_down": (D_FF, D_MODEL),
            "w_in": (D_MODEL, 4 * D_MODEL), "ssm_w_glu": (SSM_WIDTH, SSM_WIDTH), "w_branch_ssm": (SSM_WIDTH, D_MODEL),
            "w_branch_attn": (D_MODEL, D_MODEL), "w_out": (D_MODEL, D_MODEL)}[name]


def _unflatten_gathered(gathered):
    out, r0 = {}, 0
    for name in BIG:
        r = _shard_rows(name)
        k, nn = _full_shape(name)
        piece = gathered[:, r0:r0 + r, :]
        if name in COL_SHARDED:
            out[name] = piece.reshape(4, k, nn // 4).transpose(1, 0, 2).reshape(k, nn)
        else:
            out[name] = piece.reshape(k, nn)
        r0 += r
    return out


def _flatten_full(grads):
    per_shard = []
    for s in range(4):
        pieces = []
        for name in BIG:
            k, nn = _full_shape(name)
            g = grads[name]
            piece = g[:, s * (nn // 4):(s + 1) * (nn // 4)] if name in COL_SHARDED else g[s * (k // 4):(s + 1) * (k // 4), :]
            pieces.append(piece.reshape(-1, 1024))
        per_shard.append(jnp.concatenate(pieces, axis=0))
    f = jnp.stack(per_shard)
    return f.reshape(4, 2, f.shape[1] // 2, 1024).transpose(1, 0, 2, 3)


def _shard_2d(a):
    return a.reshape(-1, a.shape[-1])


def kernel(x, meta_tokens, ffn1_norm, ffn1_w_gate, ffn1_w_up, ffn1_w_down, mix_norm, w_in, ssm_lam_re, ssm_lam_im, ssm_log_dt, ssm_b_re, ssm_b_im, ssm_c_re, ssm_c_im, ssm_d, ssm_w_glu, attn_sink, w_branch_ssm, w_branch_attn, w_out, ffn2_norm, ffn2_w_gate, ffn2_w_up, ffn2_w_down, final_norm, loss_target, m_meta_tokens, m_ffn1_norm, m_ffn1_w_gate, m_ffn1_w_up, m_ffn1_w_down, m_mix_norm, m_w_in, m_ssm_lam_re, m_ssm_lam_im, m_ssm_log_dt, m_ssm_b_re, m_ssm_b_im, m_ssm_c_re, m_ssm_c_im, m_ssm_d, m_ssm_w_glu, m_attn_sink, m_w_branch_ssm, m_w_branch_attn, m_w_out, m_ffn2_norm, m_ffn2_w_gate, m_ffn2_w_up, m_ffn2_w_down, m_final_norm, v_meta_tokens, v_ffn1_norm, v_ffn1_w_gate, v_ffn1_w_up, v_ffn1_w_down, v_mix_norm, v_w_in, v_ssm_lam_re, v_ssm_lam_im, v_ssm_log_dt, v_ssm_b_re, v_ssm_b_im, v_ssm_c_re, v_ssm_c_im, v_ssm_d, v_ssm_w_glu, v_attn_sink, v_w_branch_ssm, v_w_branch_attn, v_w_out, v_ffn2_norm, v_ffn2_w_gate, v_ffn2_w_up, v_ffn2_w_down, v_final_norm):
    args = dict(locals())
    w = {k: args[k] for k in WEIGHTS}
    mom = {k: args["m_" + k] for k in WEIGHTS}
    var = {k: args["v_" + k] for k in WEIGHTS}
    depth = ffn1_norm.shape[0]
    seq = x.shape[1]
    xi, yi, ci = lax.axis_index("x"), lax.axis_index("y"), lax.axis_index("c")
    chip = 2 * xi + yi
    me = 4 * xi + 2 * yi + ci
    sel = jnp.stack([ci, chip]).astype(jnp.int32)

    same_core = [(fx, fy, 0) for fx, fy in _OTHER_CHIPS]
    meta_all = _all_gather_all("gather_meta", meta_tokens, 4, same_core, lambda x_, y_, c_: _chip(x_, y_))
    meta_full = meta_all.transpose(1, 0, 2).reshape(N_META, D_MODEL)
    flat_w = [jnp.concatenate([w[name][l].reshape(-1, 1024) for name in BIG], axis=0).astype(BF16) for l in range(depth)]

    def layer_weights(l, gathered):
        lw = _unflatten_gathered(gathered)
        for name in ("ffn1_norm", "mix_norm", "ffn2_norm"):
            lw[name] = w[name][l].reshape(1, D_MODEL)
        lw["ssm_d"] = ssm_d[l].reshape(1, SSM_WIDTH)
        lw["attn_sink"] = attn_sink[l]
        return lw

    layer_w = [layer_weights(0, _all_gather_shards("gather_w", flat_w[0]))]

    ssm_params = ("ssm_lam_re", "ssm_lam_im", "ssm_log_dt", "ssm_b_re", "ssm_b_im", "ssm_c_re", "ssm_c_im")
    ssm, ssm_vjp = [], []
    for l in range(depth):
        dirs, vjps = [], []
        for d in range(2):
            prm = tuple(w[k][l, d] for k in ssm_params)
            (ar, ai, wb, wc), pull = jax.vjp(_ssm_prep, *prm)
            a_seg = _complex_power(ar, ai, (seq + BLOCK) // N_SEG)
            conj = lambda v: jnp.concatenate([v[0], -v[1]], axis=1)
            pack = lambda v: jnp.concatenate([v[0], v[1]], axis=1)
            dirs.append({"wb": wb, "wc": wc, "a": pack((ar, ai)), "a_seg": pack(a_seg),
                         "a_conj": conj((ar, ai)), "a_seg_conj": conj(a_seg)})
            vjps.append(pull)
        ssm.append(dirs)
        ssm_vjp.append(vjps)

    h = jnp.concatenate([jnp.zeros((PAD, D_MODEL), F32), meta_full, x[0]], axis=0)
    saved = []
    for l in range(depth):
        lw = layer_w[l]
        h, s1, gathered = _ffn_fwd("ffn", h, lw["ffn1_norm"], lw["ffn1_w_gate"], lw["ffn1_w_up"], lw["ffn1_w_down"],
                                   next_shard=flat_w[l + 1] if l + 1 < depth else None)
        if gathered is not None:
            layer_w.append(layer_weights(l + 1, gathered))
        h, s2 = _mixer_fwd(h, lw, ssm[l])
        h, s3, _ = _ffn_fwd("ffn", h, lw["ffn2_norm"], lw["ffn2_w_gate"], lw["ffn2_w_up"], lw["ffn2_w_down"])
        saved.append((s1, s2, s3))
    dh, loss_part, d_final = _loss_head(h, final_norm.reshape(1, D_MODEL), loss_target[0])
    loss = lax.psum(loss_part[0, 0], MESH_AXES)

    small_g = {k: [None] * depth for k in SMALL if k not in ("meta_tokens", "final_norm")}
    big_g = {k: [None] * depth for k in BIG}
    def keep_shard(l, reduced):
        r0 = 0
        for name in BIG:
            r = _shard_rows(name)
            big_g[name][l] = reduced[r0:r0 + r].reshape(w[name].shape[1:])
            r0 += r

    pending = None
    for l in reversed(range(depth)):
        lw = layer_w[l]
        s1, s2, s3 = saved[l]
        full = {}
        dh, dg, full["ffn2_w_gate"], full["ffn2_w_up"], full["ffn2_w_down"], reduced = _ffn_bwd(
            "ffn", dh, s3, lw["ffn2_norm"], lw["ffn2_w_gate"], lw["ffn2_w_up"], lw["ffn2_w_down"], pending, sel)
        if pending is not None:
            keep_shard(l + 1, reduced)
        small_g["ffn2_norm"][l] = dg[0]
        dh, mg, ssm_cot = _mixer_bwd(dh, s2, lw, ssm[l])
        for k in ("w_out", "w_branch_ssm", "w_branch_attn", "ssm_w_glu", "w_in"):
            full[k] = mg[k]
        small_g["mix_norm"][l] = mg["mix_norm"][0]
        small_g["ssm_d"][l] = mg["ssm_d"][0]
        small_g["attn_sink"][l] = mg["attn_sink"]
        per_dir = []
        for d in range(2):
            per_dir.append(ssm_vjp[l][d](ssm_cot[d]))
        for j, k in enumerate(ssm_params):
            small_g[k][l] = jnp.stack([per_dir[0][j], per_dir[1][j]])
        dh, dg, full["ffn1_w_gate"], full["ffn1_w_up"], full["ffn1_w_down"], _ = _ffn_bwd(
            "ffn", dh, s1, lw["ffn1_norm"], lw["ffn1_w_gate"], lw["ffn1_w_up"], lw["ffn1_w_down"])
        small_g["ffn1_norm"][l] = dg[0]
        pending = _flatten_full(full)
    keep_shard(0, _reduce_scatter("reduce_w", pending, sel))

    grad_x = dh[BLOCK:][None]
    small_list = [dh[PAD:BLOCK].reshape(-1)]
    for k in SMALL[1:]:
        small_list.append(d_final.reshape(-1) if k == "final_norm" else jnp.stack(small_g[k]).reshape(-1))
    small_vec = jnp.concatenate(small_list)
    n_small = small_vec.shape[0]
    rows_small = -(-n_small // (64 * 1024)) * 64
    small_vec = jnp.pad(small_vec, (0, rows_small * 1024 - n_small)).reshape(rows_small, 1024)
    everyone = [(fx, fy, fc) for fx in (0, 1) for fy in (0, 1) for fc in (0, 1)][1:]
    index = lambda x_, y_, c_: 4 * x_ + 2 * y_ + c_
    eighths = small_vec.reshape(8, rows_small // 8, 1024)
    sends = [(f, (lambda x_, y_, c_, f=f: index(x_ ^ f[0], y_ ^ f[1], c_ ^ f[2])), index) for f in everyone]
    got = _exchange("scatter_small", eighths, 8, [], sends)
    mine = lax.broadcasted_iota(jnp.int32, (8, 1, 1), 0) == me
    got = jnp.where(mine, lax.dynamic_index_in_dim(eighths, me, 0, keepdims=True), got)
    part = _slot_sum("sum_small", [(got, k) for k in range(8)], sel, rows_small // 8)
    small_sum = _all_gather_all("gather_small", part, 8, everyone, index).reshape(-1)

    grads, deltas, new_m, new_v = {}, {}, {}, {}
    off = 0
    for k in SMALL:
        size = (N_META * D_MODEL) if k == "meta_tokens" else int(np.prod(w[k].shape))
        g = small_sum[off:off + size]
        off += size
        if k == "meta_tokens":
            g = lax.dynamic_slice(g.reshape(N_META, D_MODEL), (0, chip * (D_MODEL // 4)), (N_META, D_MODEL // 4))
        else:
            g = g.reshape(w[k].shape)
        grads[k] = g
        as_2d = (lambda a: a.reshape(-1, a.shape[-1])) if g.ndim >= 2 else (lambda a: a.reshape(1, -1))
        rows_k = as_2d(g).shape[0]
        d_, m_, v_ = _adamw("adamw_" + k, as_2d(w[k]), as_2d(g), as_2d(mom[k]), as_2d(var[k]), min(rows_k, 2048))
        deltas[k], new_m[k], new_v[k] = d_.reshape(g.shape), m_.reshape(g.shape), v_.reshape(g.shape)
    for k in BIG:
        g = jnp.stack(big_g[k])
        grads[k] = g
        rows_k = _shard_2d(g).shape[0]
        tm = 512 if rows_k % 512 == 0 else rows_k // depth
        d_, m_, v_ = _adamw("adamw_" + k, _shard_2d(w[k]), _shard_2d(g), _shard_2d(mom[k]), _shard_2d(var[k]), tm)
        deltas[k], new_m[k], new_v[k] = d_.reshape(g.shape), m_.reshape(g.shape), v_.reshape(g.shape)

    return (loss, grad_x, *[grads[k] for k in WEIGHTS], *[deltas[k] for k in WEIGHTS],
            *[new_m[k] for k in WEIGHTS], *[new_v[k] for k in WEIGHTS])
```

```python
import functools
import math

import numpy as np
import jax
import jax.numpy as jnp
from jax import lax
from jax.experimental import pallas as pl
from jax.experimental.pallas import tpu as pltpu

F32 = jnp.float32
BF16 = jnp.bfloat16

D_MODEL = 1024
N_META = 16
N_HEADS = 16
N_KV_HEADS = 4
HEAD_DIM = 64
Q_GROUP = N_HEADS // N_KV_HEADS
WINDOW = 128
BLOCK = 128
PAD = BLOCK - N_META
SSM_WIDTH = 512
SSM_GROUP_CH = 16
SSM_GROUPS = 32
SSM_STATE = 64
N_STATE = SSM_GROUPS * SSM_STATE
SUPER = 4
D_FF = 2816
EPS = 1e-6
NEG = -1e30
ATTN_SCALE = HEAD_DIM ** -0.5
SLOPES = [float(2.0 ** (-8.0 * (h + 1) / N_HEADS)) for h in range(N_HEADS)]

ADAM_LR = 0.001
ADAM_B1 = 0.9
ADAM_B2 = 0.999
ADAM_EPS = 1e-08
ADAM_WD = 0.01
ADAM_STEP = 10

V7X_VMEM_LIMIT_BYTES = 52 * 1024 * 1024
ROW_TILE = 384
LONG_ROW_TILE = 2064
MESH_AXES = ("x", "y", "c")

BIG = ["ffn1_w_gate", "ffn1_w_up", "ffn1_w_down", "ffn2_w_gate", "ffn2_w_up", "ffn2_w_down",
       "w_in", "ssm_w_glu", "w_branch_ssm", "w_branch_attn", "w_out"]
COL_SHARDED = {"ffn1_w_gate", "ffn1_w_up", "ffn2_w_gate", "ffn2_w_up", "w_in", "w_branch_ssm"}
SMALL = ["meta_tokens", "ffn1_norm", "mix_norm", "ffn2_norm", "final_norm", "ssm_lam_re", "ssm_lam_im", "ssm_log_dt",
         "ssm_b_re", "ssm_b_im", "ssm_c_re", "ssm_c_im", "ssm_d", "attn_sink"]
WEIGHTS = ["meta_tokens", "ffn1_norm", "ffn1_w_gate", "ffn1_w_up", "ffn1_w_down", "mix_norm", "w_in", "ssm_lam_re",
           "ssm_lam_im", "ssm_log_dt", "ssm_b_re", "ssm_b_im", "ssm_c_re", "ssm_c_im", "ssm_d", "ssm_w_glu", "attn_sink",
           "w_branch_ssm", "w_branch_attn", "w_out", "ffn2_norm", "ffn2_w_gate", "ffn2_w_up", "ffn2_w_down", "final_norm"]


def _dot(a, b):
    return lax.dot_general(a.astype(BF16), b.astype(BF16), (((1,), (0,)), ((), ())), preferred_element_type=F32)


def _dot_nt(a, b):
    return lax.dot_general(a.astype(BF16), b.astype(BF16), (((1,), (1,)), ((), ())), preferred_element_type=F32)


def _dot_tn(a, b):
    return lax.dot_general(a.astype(BF16), b.astype(BF16), (((0,), (0,)), ((), ())), preferred_element_type=F32)


def _sigmoid(x):
    return 0.5 * jnp.tanh(0.5 * x) + 0.5


_GELU_C = math.sqrt(2.0 / math.pi)


def _gelu(x):
    return 0.5 * x * (1.0 + jnp.tanh(_GELU_C * (x + 0.044715 * x * x * x)))


def _gelu_grad(x):
    th = jnp.tanh(_GELU_C * (x + 0.044715 * x * x * x))
    return 0.5 * (1.0 + th) + 0.5 * x * (1.0 - th * th) * _GELU_C * (1.0 + 3.0 * 0.044715 * x * x)


def _rms_fwd(x, g):
    r = lax.rsqrt(jnp.mean(x * x, axis=-1, keepdims=True) + EPS)
    return x * r * g


def _rms_bwd(x, g, dn):
    r = lax.rsqrt(jnp.mean(x * x, axis=-1, keepdims=True) + EPS)
    xh = x * r
    t = dn * g
    dx = r * (t - xh * jnp.mean(t * xh, axis=-1, keepdims=True))
    return dx, jnp.sum(dn * xh, axis=0, keepdims=True)


def _compiler_params():
    return pltpu.CompilerParams(dimension_semantics=("arbitrary",), vmem_limit_bytes=V7X_VMEM_LIMIT_BYTES)


def _rows(arr, width=None, cb=0):
    return (arr, arr.shape[1] if width is None else width, cb)


def _rowk(name, fn, rows, fulls, outs, accs=(), tm=ROW_TILE, smem=(), n_rows=None, hosted=None, writes_outs=False):
    n = rows[0][0].shape[0] if n_rows is None else n_rows
    assert n % tm == 0, (name, n, tm)
    in_specs, args = [], []
    for s in smem:
        in_specs.append(pl.BlockSpec(memory_space=pltpu.SMEM))
        args.append(s)
    for r in rows:
        if callable(r[2]):
            in_specs.append(pl.BlockSpec(r[1], r[2]))
        else:
            in_specs.append(pl.BlockSpec((tm, r[1]), functools.partial(lambda i, cb: (i, cb), cb=r[2])))
        args.append(r[0])
    for f in fulls:
        in_specs.append(pl.BlockSpec(memory_space=pl.ANY))
        args.append(f)
    nh = 0 if hosted is None else 1
    if nh:
        in_specs.append(pl.BlockSpec(memory_space=pl.ANY))
        args.append(hosted[0])
    out_specs, out_shape = [], []
    for w, dt in outs:
        out_specs.append(pl.BlockSpec((tm, w), lambda i: (i, 0)))
        out_shape.append(jax.ShapeDtypeStruct((n, w), dt))
    for shp in accs:
        out_specs.append(pl.BlockSpec(shp, functools.partial(lambda i, nd: (0,) * nd, nd=len(shp))))
        out_shape.append(jax.ShapeDtypeStruct(shp, F32))
    if nh:
        out_specs.append(pl.BlockSpec(memory_space=pl.ANY))
        out_shape.append(jax.ShapeDtypeStruct((hosted[1],) + hosted[0].shape[1:], hosted[0].dtype))
    ns, nr, nf, no, na = len(smem), len(rows), len(fulls), len(outs), len(accs)
    scratch = [pltpu.VMEM(f.shape, f.dtype) for f in fulls]
    if nf:
        scratch.append(pltpu.SemaphoreType.DMA((nf,)))
    if nh:
        scratch += [pltpu.SemaphoreType.DMA((len(hosted[2]),)), pltpu.SemaphoreType.DMA((len(hosted[2]),))]
    steps = n // tm

    def body(*refs):
        i = pl.program_id(0)
        refs = list(refs)
        take = lambda k: [refs.pop(0) for _ in range(k)]
        sm, rr, fh, hsrc, oo, aa, hout, fv = take(ns), take(nr), take(nf), take(nh), take(no), take(na), take(nh), take(nf)
        if nf:
            sem = refs.pop(0)

            @pl.when(i == 0)
            def _():
                cps = [pltpu.make_async_copy(fh[j], fv[j], sem.at[j]) for j in range(nf)]
                for cp in cps:
                    cp.start()
                for cp in cps:
                    cp.wait()
        if nh:
            @pl.when(i == 0)
            def _():
                for cp in _remote_copies(hsrc[0], hout[0], refs[0], refs[1], hosted[2]):
                    cp.start()
        if writes_outs:
            res = fn(i, *sm, *rr, *fv, *oo)
            res = (None,) * no + (tuple(res) if isinstance(res, (tuple, list)) else ())
        else:
            res = fn(i, *sm, *rr, *fv)
            res = tuple(res) if isinstance(res, (tuple, list)) else (res,)
            for o, v in zip(oo, res[:no]):
                o[...] = v.astype(o.dtype)
        if na:
            @pl.when(i == 0)
            def _():
                for a in aa:
                    a[...] = jnp.zeros_like(a)
            for a, v in zip(aa, res[no:]):
                a[...] += v
        if nh:
            @pl.when(i == steps - 1)
            def _():
                for cp in _remote_copies(hsrc[0], hout[0], refs[0], refs[1], hosted[2]):
                    cp.wait()

    return pl.pallas_call(body, grid=(steps,), in_specs=in_specs, out_specs=out_specs, out_shape=out_shape,
                          scratch_shapes=scratch, name=name, compiler_params=_compiler_params())(*args)


def _mm_tn(name, x, y, *, xw=None, xcb=0, tk, tn, scale=1.0, tm=None):
    m = x.shape[0]
    if tm is None:
        tm = LONG_ROW_TILE if m % LONG_ROW_TILE == 0 else ROW_TILE
    k = x.shape[1] if xw is None else xw
    nn = y.shape[1]
    assert m % tm == 0 and k % tk == 0 and nn % tn == 0, (name, m, k, nn)
    kb0 = (xcb * k) // tk

    def body(x_ref, y_ref, o_ref):
        @pl.when(pl.program_id(2) == 0)
        def _():
            o_ref[...] = jnp.zeros_like(o_ref)
        yv = y_ref[...]
        if scale != 1.0:
            yv = yv * scale
        o_ref[...] += _dot_tn(x_ref[...], yv)

    return pl.pallas_call(
        body, grid=(k // tk, nn // tn, m // tm),
        in_specs=[pl.BlockSpec((tm, tk), lambda a, b, i: (i, kb0 + a)), pl.BlockSpec((tm, tn), lambda a, b, i: (i, b))],
        out_specs=pl.BlockSpec((tk, tn), lambda a, b, i: (a, b)), out_shape=jax.ShapeDtypeStruct((k, nn), F32), name=name,
        compiler_params=pltpu.CompilerParams(dimension_semantics=("arbitrary", "arbitrary", "arbitrary"),
                                             vmem_limit_bytes=V7X_VMEM_LIMIT_BYTES))(x, y)


def _remote_copies(src_ref, out_ref, ssem, rsem, sends):
    x, y, c = lax.axis_index("x"), lax.axis_index("y"), lax.axis_index("c")
    cps = []
    for k, ((fx, fy, fc), sf, df) in enumerate(sends):
        peer = (1 - x if fx else x, 1 - y if fy else y, 1 - c if fc else c)
        cps.append(pltpu.make_async_remote_copy(src_ref=src_ref.at[sf(x, y, c)], dst_ref=out_ref.at[df(x, y, c)],
                                                send_sem=ssem.at[k], recv_sem=rsem.at[k], device_id=peer,
                                                device_id_type=pl.DeviceIdType.MESH))
    return cps


def _exchange(name, src, n_out, local, sends, alias=False):
    nl, nsnd = len(local), len(sends)
    out_shape = jax.ShapeDtypeStruct((n_out,) + src.shape[1:], src.dtype)

    def body(src_ref, out_ref, lsem, ssem, rsem):
        x, y, c = lax.axis_index("x"), lax.axis_index("y"), lax.axis_index("c")
        cps = [pltpu.make_async_copy(src_ref.at[sf(x, y, c)], out_ref.at[df(x, y, c)], lsem.at[j])
               for j, (sf, df) in enumerate(local)]
        cps += _remote_copies(src_ref, out_ref, ssem, rsem, sends)
        for cp in cps:
            cp.start()
        for cp in cps:
            cp.wait()

    return pl.pallas_call(
        body, in_specs=[pl.BlockSpec(memory_space=pl.ANY)], out_specs=pl.BlockSpec(memory_space=pl.ANY), out_shape=out_shape,
        scratch_shapes=[pltpu.SemaphoreType.DMA((max(nl, 1),)), pltpu.SemaphoreType.DMA((nsnd,)), pltpu.SemaphoreType.DMA((nsnd,))],
        input_output_aliases=({0: 0} if alias else {}), name=name)(src)


def _chip(x, y):
    return 2 * x + y


_OTHER_CHIPS = [(1, 0), (0, 1), (1, 1)]


def _all_gather_shards(name, shard):
    src, n_out, sends = _gather_ici_stage(shard)
    return _gather_finish(name, shard, _exchange(name + "_ici", src, n_out, [], sends))


def _gather_ici_stage(shard):
    r, w = shard.shape
    first = [((fx, fy, 0), lambda x, y, c: c, lambda x, y, c: 2 * _chip(x, y) + c) for fx, fy in _OTHER_CHIPS]
    return shard.reshape(2, r // 2, w), 8, first


def _gather_finish(name, shard, g):
    r, w = shard.shape
    second = [((0, 0, 1),
               (lambda x, y, c, fx=fx, fy=fy: 2 * _chip(x ^ fx, y ^ fy) + c),
               (lambda x, y, c, fx=fx, fy=fy: 2 * _chip(x ^ fx, y ^ fy) + c)) for fx, fy in _OTHER_CHIPS]
    g = _exchange(name + "_d2d", g, 8, [], second, alias=True).reshape(4, r, w)
    mine = lax.broadcasted_iota(jnp.int32, (4, 1, 1), 0) == _chip(lax.axis_index("x"), lax.axis_index("y"))
    return jnp.where(mine, shard[None], g)


def _row_tile(rows, cap=512):
    return max(t for t in range(16, cap + 1, 16) if rows % t == 0)


def _slot_sum(name, terms, sel, tm, out_slots=None, out_slot=None, also_bf16=False):
    rows, w = terms[0][0].shape[1:]

    def imap(slot):
        if isinstance(slot, int):
            return lambda i, s: (slot, i, 0)
        return lambda i, s: (s[slot[1]], i, 0)

    in_specs = [pl.BlockSpec((None, tm, w), imap(sl)) for _, sl in terms]
    if out_slots is None:
        out_specs, out_shape = [pl.BlockSpec((tm, w), lambda i, s: (i, 0))], [jax.ShapeDtypeStruct((rows, w), F32)]
    else:
        out_specs = [pl.BlockSpec((None, tm, w), imap(out_slot))]
        out_shape = [jax.ShapeDtypeStruct((out_slots, rows, w), F32)]
    if also_bf16:
        out_specs.append(pl.BlockSpec((tm, w), lambda i, s: (i, 0)))
        out_shape.append(jax.ShapeDtypeStruct((rows, w), BF16))
    n_in = len(terms)

    def body(sel_ref, *refs):
        acc = refs[0][...].astype(F32)
        for r in refs[1:n_in]:
            acc = acc + r[...].astype(F32)
        refs[n_in][...] = acc
        if also_bf16:
            refs[n_in + 1][...] = acc.astype(BF16)

    grid_spec = pltpu.PrefetchScalarGridSpec(num_scalar_prefetch=1, grid=(rows // tm,), in_specs=in_specs, out_specs=out_specs)
    out = pl.pallas_call(body, grid_spec=grid_spec, out_shape=out_shape, name=name,
                         compiler_params=_compiler_params())(sel, *[a for a, _ in terms])
    return out if also_bf16 else out[0]


def _reduce_scatter(name, parts, sel):
    f, (src, n_out, sends) = _rs_pair_stage(parts)
    p, (src2, n_out2, sends2) = _rs_chip_stage(name, f, _exchange(name + "_d2d", src, n_out, [], sends), sel)
    return _rs_finish(name, p, _exchange(name + "_ici", src2, n_out2, [], sends2), sel)


def _rs_pair_stage(parts):
    _, _, r, w = parts.shape
    f = parts.reshape(2, 4 * r, w)
    return f, (f, 1, [((0, 0, 1), lambda x, y, c: 1 - c, lambda x, y, c: 0)])


def _rs_chip_stage(name, f, got, sel):
    r, w = f.shape[1] // 4, f.shape[2]
    p, p16 = _slot_sum(name + "_add2", [(f, ("sel", 0)), (got, 0)], sel, _row_tile(4 * r), also_bf16=True)
    sends = [((fx, fy, 0), (lambda x, y, c, fx=fx, fy=fy: _chip(x ^ fx, y ^ fy)), (lambda x, y, c, k=k: k))
             for k, (fx, fy) in enumerate(_OTHER_CHIPS)]
    return p.reshape(4, r, w), (p16.reshape(4, r, w), 3, sends)


def _rs_finish(name, p, got, sel):
    _, r, w = p.shape
    q = _slot_sum(name + "_add4", [(p, ("sel", 1)), (got, 0), (got, 1), (got, 2)], sel, _row_tile(r), out_slots=2,
                  out_slot=("sel", 0))
    q = _exchange(name + "_pair", q, 2, [], [((0, 0, 1), lambda x, y, c: c, lambda x, y, c: c)], alias=True)
    return q.reshape(2 * r, w)


def _all_gather_all(name, vec, n_slots, flips, slot_fn):
    sends = [(f, lambda x, y, c: 0, slot_fn) for f in flips]
    g = _exchange(name, vec[None], n_slots, [], sends)
    mine = lax.broadcasted_iota(jnp.int32, (n_slots, 1, 1), 0) == slot_fn(*(lax.axis_index(a) for a in MESH_AXES))
    return jnp.where(mine, vec[None], g)


def _nbr_specs(arr, width, cb, nb):
    return [
        (arr, (BLOCK, width), functools.partial(lambda n, cb: (jnp.maximum(n - 1, 0), cb), cb=cb)),
        (arr, (BLOCK, width), functools.partial(lambda n, cb: (n, cb), cb=cb)),
        (arr, (BLOCK, width), functools.partial(lambda n, cb: (jnp.minimum(n + 1, nb - 1), cb), cb=cb)),
    ]


def _head(h):
    return slice(h * HEAD_DIM, (h + 1) * HEAD_DIM)


def _row_group(n_groups, rows_per_group):
    r = lax.broadcasted_iota(jnp.int32, (n_groups * rows_per_group, 1), 0)
    grp = jnp.zeros_like(r)
    for g in range(1, n_groups):
        grp = grp + jnp.where(r >= g * rows_per_group, 1, 0)
    return grp


def _by_group(grp, vals):
    out = vals[-1]
    for g in range(len(vals) - 2, -1, -1):
        out = jnp.where(grp == g, vals[g], out)
    return out


def _attn_fwd(proj, sink):
    lp = proj.shape[0]
    nb = lp // BLOCK
    kv_w = N_KV_HEADS * HEAD_DIM
    specs = _nbr_specs(proj, kv_w, 4, nb) + _nbr_specs(proj, kv_w, 5, nb)
    specs += [(proj, (BLOCK, kv_w), lambda n: (0, 4)), (proj, (BLOCK, kv_w), lambda n: (0, 5))]
    in_specs = [pl.BlockSpec(memory_space=pltpu.SMEM), pl.BlockSpec((BLOCK, D_MODEL), lambda n: (n, 0))]
    in_specs += [pl.BlockSpec(s[1], s[2]) for s in specs]

    def body(sink_ref, q_ref, kp, kc, kn, vp, vc, vn, km, vm, o_ref, lse_ref):
        n = pl.program_id(0)
        qi = lax.broadcasted_iota(jnp.int32, (BLOCK, 3 * BLOCK), 0)
        sj = lax.broadcasted_iota(jnp.int32, (BLOCK, 3 * BLOCK), 1)
        dist = jnp.abs(qi + BLOCK - sj)
        kpos = (n - 1) * BLOCK + sj
        valid = (dist <= WINDOW) & (kpos >= BLOCK) & (kpos < lp)
        distf = dist.astype(F32)
        kb = jnp.concatenate([kp[...], kc[...], kn[...]], axis=0).astype(BF16)
        vb = jnp.concatenate([vp[...], vc[...], vn[...]], axis=0).astype(BF16)
        kmeta = km[PAD:BLOCK, :].astype(BF16)
        vmeta = vm[PAD:BLOCK, :].astype(BF16)
        valid4 = jnp.concatenate([valid] * Q_GROUP, axis=0)
        distf4 = jnp.concatenate([distf] * Q_GROUP, axis=0)
        grp = _row_group(Q_GROUP, BLOCK)
        for kh in range(N_KV_HEADS):
            ksl = slice(kh * HEAD_DIM, (kh + 1) * HEAD_DIM)
            heads = [kh * Q_GROUP + g for g in range(Q_GROUP)]
            slope = _by_group(grp, [SLOPES[h] for h in heads])
            sk = _by_group(grp, [sink_ref[h] for h in heads])
            q4 = (jnp.concatenate([q_ref[:, _head(h)] for h in heads], axis=0) * ATTN_SCALE).astype(BF16)
            s = jnp.where(valid4, _dot_nt(q4, kb[:, ksl]) - slope * distf4, NEG)
            sm = _dot_nt(q4, kmeta[:, ksl])
            m = jnp.maximum(jnp.maximum(jnp.max(s, axis=1, keepdims=True), jnp.max(sm, axis=1, keepdims=True)), sk)
            e = jnp.exp(s - m)
            em = jnp.exp(sm - m)
            den = jnp.sum(e, axis=1, keepdims=True) + jnp.sum(em, axis=1, keepdims=True) + jnp.exp(sk - m)
            o4 = (_dot(e, vb[:, ksl]) + _dot(em, vmeta[:, ksl])) * (1.0 / den)
            lse4 = m + jnp.log(den)
            for g, h in enumerate(heads):
                o_ref[:, _head(h)] = o4[g * BLOCK:(g + 1) * BLOCK].astype(o_ref.dtype)
                lse_ref[:, h:h + 1] = lse4[g * BLOCK:(g + 1) * BLOCK]

    return pl.pallas_call(
        body, grid=(nb,), in_specs=in_specs,
        out_specs=[pl.BlockSpec((BLOCK, D_MODEL), lambda n: (n, 0)), pl.BlockSpec((BLOCK, N_HEADS), lambda n: (n, 0))],
        out_shape=[jax.ShapeDtypeStruct((lp, D_MODEL), BF16), jax.ShapeDtypeStruct((lp, N_HEADS), F32)],
        name="attn_fwd", compiler_params=_compiler_params())(sink, proj, *[s[0] for s in specs])


def _attn_delta(do, o):
    lp = do.shape[0]
    sel = (lax.broadcasted_iota(jnp.int32, (N_HEADS, D_MODEL), 1) // HEAD_DIM
           == lax.broadcasted_iota(jnp.int32, (N_HEADS, D_MODEL), 0)).astype(BF16)

    def body(do_ref, o_ref, sel_ref, d_ref, dt_ref):
        prod = do_ref[...] * o_ref[...].astype(F32)
        hi = prod.astype(BF16)
        lo = (prod - hi.astype(F32)).astype(BF16)
        d_ref[...] = _dot_nt(hi, sel_ref[...]) + _dot_nt(lo, sel_ref[...])
        dt_ref[...] = _dot_nt(sel_ref[...], hi) + _dot_nt(sel_ref[...], lo)

    return pl.pallas_call(
        body, grid=(lp // BLOCK,),
        in_specs=[pl.BlockSpec((BLOCK, D_MODEL), lambda n: (n, 0)), pl.BlockSpec((BLOCK, D_MODEL), lambda n: (n, 0)),
                  pl.BlockSpec((N_HEADS, D_MODEL), lambda n: (0, 0))],
        out_specs=[pl.BlockSpec((BLOCK, N_HEADS), lambda n: (n, 0)), pl.BlockSpec((N_HEADS, BLOCK), lambda n: (0, n))],
        out_shape=[jax.ShapeDtypeStruct((lp, N_HEADS), F32), jax.ShapeDtypeStruct((N_HEADS, lp), F32)],
        name="attn_delta", compiler_params=_compiler_params())(do, o, sel)


def _attn_bwd(proj, sink, o, lse, do):
    lp = proj.shape[0]
    nb = lp // BLOCK
    kv_w = N_KV_HEADS * HEAD_DIM
    delta, delta_t = _attn_delta(do, o)
    lse_t = lse.T
    row_nbrs = lambda arr: [
        (arr, (N_HEADS, BLOCK), lambda n: (0, jnp.maximum(n - 1, 0))), (arr, (N_HEADS, BLOCK), lambda n: (0, n)),
        (arr, (N_HEADS, BLOCK), lambda n: (0, jnp.minimum(n + 1, nb - 1)))]
    specs = (_nbr_specs(proj, D_MODEL, 0, nb) + _nbr_specs(proj, kv_w, 4, nb) + _nbr_specs(proj, kv_w, 5, nb)
             + [(proj, (BLOCK, kv_w), lambda n: (0, 4)), (proj, (BLOCK, kv_w), lambda n: (0, 5))]
             + _nbr_specs(do, D_MODEL, 0, nb) + [(lse, (BLOCK, N_HEADS), lambda n: (n, 0)), (delta, (BLOCK, N_HEADS), lambda n: (n, 0))]
             + row_nbrs(lse_t) + row_nbrs(delta_t))
    in_specs = [pl.BlockSpec(memory_space=pltpu.SMEM)] + [pl.BlockSpec(s[1], s[2]) for s in specs]

    def body(sink_ref, qp, qc, qn, kp, kc, kn, vp, vc, vn, km, vm, dop, doc, don, lc, dc, ltp, ltc, ltn, dtp, dtc, dtn,
             dq_ref, dk_ref, dv_ref, dkm_ref, dvm_ref, dsk_ref):
        n = pl.program_id(0)

        @pl.when(n == 0)
        def _():
            dkm_ref[...] = jnp.zeros_like(dkm_ref)
            dvm_ref[...] = jnp.zeros_like(dvm_ref)
            dsk_ref[...] = jnp.zeros_like(dsk_ref)

        qi = lax.broadcasted_iota(jnp.int32, (BLOCK, 3 * BLOCK), 0)
        sj = lax.broadcasted_iota(jnp.int32, (BLOCK, 3 * BLOCK), 1)
        dist_q = jnp.abs(qi + BLOCK - sj)
        kpos = (n - 1) * BLOCK + sj
        valid_q = (dist_q <= WINDOW) & (kpos >= BLOCK) & (kpos < lp)
        distf_q = dist_q.astype(F32)
        bi = lax.broadcasted_iota(jnp.int32, (BLOCK, 3 * BLOCK), 1)
        kj = lax.broadcasted_iota(jnp.int32, (BLOCK, 3 * BLOCK), 0)
        dist_k = jnp.abs(bi - BLOCK - kj)
        qpos = (n - 1) * BLOCK + bi
        valid_k = (dist_k <= WINDOW) & (qpos >= 0) & (qpos < lp) & (n >= 1)
        distf_k = dist_k.astype(F32)

        kb = jnp.concatenate([kp[...], kc[...], kn[...]], axis=0).astype(BF16)
        vb = jnp.concatenate([vp[...], vc[...], vn[...]], axis=0).astype(BF16)
        kcur = kc[...].astype(BF16)
        vcur = vc[...].astype(BF16)
        kmeta = km[PAD:BLOCK, :].astype(BF16)
        vmeta = vm[PAD:BLOCK, :].astype(BF16)
        lane = lax.broadcasted_iota(jnp.int32, (1, BLOCK), 1)
        dsink = jnp.zeros((1, BLOCK), F32)
        valid_q4 = jnp.concatenate([valid_q] * Q_GROUP, axis=0)
        distf_q4 = jnp.concatenate([distf_q] * Q_GROUP, axis=0)
        valid_k4 = jnp.concatenate([valid_k] * Q_GROUP, axis=1)
        distf_k4 = jnp.concatenate([distf_k] * Q_GROUP, axis=1)
        grp_q = _row_group(Q_GROUP, BLOCK)
        lane_k = lax.broadcasted_iota(jnp.int32, (1, Q_GROUP * 3 * BLOCK), 1)
        grp_k = sum(jnp.where(lane_k >= g * 3 * BLOCK, 1, 0) for g in range(1, Q_GROUP))
        for kh in range(N_KV_HEADS):
            ksl = slice(kh * HEAD_DIM, (kh + 1) * HEAD_DIM)
            heads = [kh * Q_GROUP + g for g in range(Q_GROUP)]
            slopes = [SLOPES[h] for h in heads]
            q4 = (jnp.concatenate([qc[:, _head(h)] for h in heads], axis=0) * ATTN_SCALE).astype(BF16)
            do4 = jnp.concatenate([doc[:, _head(h)] for h in heads], axis=0)
            delta = jnp.concatenate([dc[:, h:h + 1] for h in heads], axis=0)
            lse4 = jnp.concatenate([lc[:, h:h + 1] for h in heads], axis=0)
            s = _dot_nt(q4, kb[:, ksl]) - _by_group(grp_q, slopes) * distf_q4
            p = jnp.exp(jnp.where(valid_q4, s, NEG) - lse4)
            pm = jnp.exp(_dot_nt(q4, kmeta[:, ksl]) - lse4)
            ps = jnp.exp(_by_group(grp_q, [sink_ref[h] for h in heads]) - lse4)
            do4b = do4.astype(BF16)
            ds = p * (_dot_nt(do4b, vb[:, ksl]) - delta)
            dsm = pm * (_dot_nt(do4b, vmeta[:, ksl]) - delta)
            dq4 = ATTN_SCALE * (_dot(ds, kb[:, ksl]) + _dot(dsm, kmeta[:, ksl]))
            dsk4 = ps * delta
            for g, h in enumerate(heads):
                dq_ref[:, _head(h)] = dq4[g * BLOCK:(g + 1) * BLOCK].astype(dq_ref.dtype)
                dsink = dsink + jnp.where(lane == h, -jnp.sum(dsk4[g * BLOCK:(g + 1) * BLOCK]), 0.0)
            dkm_ref[:, ksl] += _dot_tn(dsm, q4)
            dvm_ref[:, ksl] += _dot_tn(pm, do4b)
            band = lambda a, b, c_: jnp.concatenate([r[:, _head(h)] for h in heads for r in (a, b, c_)], axis=0)
            qb4 = (band(qp, qc, qn) * ATTN_SCALE).astype(BF16)
            dob4b = band(dop, doc, don).astype(BF16)
            delta_b = jnp.concatenate([r[h:h + 1, :] for h in heads for r in (dtp, dtc, dtn)], axis=1)
            lse_b = jnp.concatenate([r[h:h + 1, :] for h in heads for r in (ltp, ltc, ltn)], axis=1)
            st = _dot_nt(kcur[:, ksl], qb4) - _by_group(grp_k, slopes) * distf_k4
            pt = jnp.exp(jnp.where(valid_k4, st, NEG) - lse_b)
            dv_ref[:, ksl] = _dot(pt, dob4b)
            dst = pt * (_dot_nt(vcur[:, ksl], dob4b) - delta_b)
            dk_ref[:, ksl] = _dot(dst, qb4)
        dsk_ref[...] += dsink

    blk = lambda w: pl.BlockSpec((BLOCK, w), lambda n: (n, 0))
    fix = lambda shp: pl.BlockSpec(shp, lambda n: (0, 0))
    return pl.pallas_call(
        body, grid=(nb,), in_specs=in_specs,
        out_specs=[blk(D_MODEL), blk(kv_w), blk(kv_w), fix((N_META, kv_w)), fix((N_META, kv_w)), fix((1, BLOCK))],
        out_shape=[jax.ShapeDtypeStruct((lp, D_MODEL), BF16), jax.ShapeDtypeStruct((lp, kv_w), F32),
                   jax.ShapeDtypeStruct((lp, kv_w), F32), jax.ShapeDtypeStruct((N_META, kv_w), F32),
                   jax.ShapeDtypeStruct((N_META, kv_w), F32), jax.ShapeDtypeStruct((1, BLOCK), F32)],
        name="attn_bwd", compiler_params=_compiler_params())(sink, *[s[0] for s in specs])


N_SEG = 8


def _ssm_tile(lp, long_tile):
    return 688 if long_tile and lp % 688 == 0 else 384


def _to_segments(a):
    lp, w = a.shape
    return a.reshape(N_SEG, lp // N_SEG, w).transpose(1, 0, 2).reshape(lp, w)


def _from_segments(a):
    lp, w = a.shape
    return a.reshape(lp // N_SEG, N_SEG, w).transpose(1, 0, 2).reshape(lp, w)


def _complex_power(ar, ai, n):
    rr, ri = jnp.ones_like(ar), jnp.zeros_like(ai)
    while n:
        if n & 1:
            rr, ri = rr * ar - ri * ai, rr * ai + ri * ar
        ar, ai = ar * ar - ai * ai, 2.0 * ar * ai
        n >>= 1
    return rr, ri


def _segment_starts(finals, a_seg, reverse):
    fr, fi = finals[:, :N_STATE], finals[:, N_STATE:]
    ar, ai = a_seg[:, :N_STATE], a_seg[:, N_STATE:]
    row = lax.broadcasted_iota(jnp.int32, (N_SEG, N_STATE), 0)
    pr = jnp.zeros((1, N_STATE), F32)
    pi = jnp.zeros((1, N_STATE), F32)
    sr = jnp.zeros((N_SEG, N_STATE), F32)
    si = jnp.zeros((N_SEG, N_STATE), F32)
    for s in (range(N_SEG - 1, -1, -1) if reverse else range(N_SEG)):
        sr = jnp.where(row == s, pr, sr)
        si = jnp.where(row == s, pi, si)
        pr, pi = fr[s:s + 1] + ar * pr - ai * pi, fi[s:s + 1] + ar * pi + ai * pr
    return jnp.concatenate([sr, si], axis=1)


def _recurrence(buf_ref, st_ref, a_ref, reverse):
    steps = buf_ref.shape[0] // N_SEG
    half = N_STATE // 2
    for c0 in (0, half):
        re = slice(c0, c0 + half)
        im = slice(N_STATE + c0, N_STATE + c0 + half)
        ar = jnp.broadcast_to(a_ref[:, re], (N_SEG, half))
        ai = jnp.broadcast_to(a_ref[:, im], (N_SEG, half))

        def step(k, carry, re=re, im=im, ar=ar, ai=ai):
            xr, xi = carry
            r0 = pl.multiple_of((steps - 1 - k if reverse else k) * N_SEG, N_SEG)
            nr = ar * xr - ai * xi + buf_ref[pl.ds(r0, N_SEG), re]
            ni = ar * xi + ai * xr + buf_ref[pl.ds(r0, N_SEG), im]
            buf_ref[pl.ds(r0, N_SEG), re] = nr
            buf_ref[pl.ds(r0, N_SEG), im] = ni
            return nr, ni

        xr, xi = lax.fori_loop(0, steps, step, (st_ref[:, re], st_ref[:, im]), unroll=2)
        st_ref[:, re] = xr
        st_ref[:, im] = xi


def _copy_in(pairs, sem):
    cps = [pltpu.make_async_copy(src, dst, sem.at[j]) for j, (src, dst) in enumerate(pairs)]
    for cp in cps:
        cp.start()
    for cp in cps:
        cp.wait()


def _ssm_fwd_dir(name, u_seg, wb, wc, a, a_seg, reverse):
    lp = u_seg.shape[0]
    tile_rows = _ssm_tile(lp, True)
    nt = lp // tile_rows
    tile = (lambda i: nt - 1 - i) if reverse else (lambda i: i)
    first = tile(0)
    held = lambda p, i: (p * tile(i) + (1 - p) * first, 0)

    def body(u_ref, wb_hbm, wc_hbm, a_ref, aseg_ref, x_ref, y_ref, wb_ref, wc_ref, buf_ref, st_ref, sem):
        p, i = pl.program_id(0), pl.program_id(1)

        @pl.when((p == 0) & (i == 0))
        def _():
            _copy_in([(wb_hbm, wb_ref), (wc_hbm, wc_ref)], sem)
            st_ref[...] = jnp.zeros_like(st_ref)

        @pl.when((p == 1) & (i == 0))
        def _():
            st_ref[...] = _segment_starts(st_ref[...], aseg_ref[...], reverse)

        def states_into(dst_ref):
            for j in range(SUPER):
                part = _dot(u_ref[:, 128 * j:128 * (j + 1)], wb_ref[j])
                dst_ref[:, 512 * j:512 * (j + 1)] = part[:, :512]
                dst_ref[:, N_STATE + 512 * j:N_STATE + 512 * (j + 1)] = part[:, 512:]
            _recurrence(dst_ref, st_ref, a_ref, reverse)

        @pl.when(p == 0)
        def _():
            states_into(buf_ref)

        @pl.when(p == 1)
        def _():
            states_into(x_ref)
            for j in range(SUPER):
                y_ref[:, 128 * j:128 * (j + 1)] = (_dot(x_ref[:, _re(j)], wc_ref[j, :512, :])
                                                   + _dot(x_ref[:, _im(j)], wc_ref[j, 512:, :]))

    fix = lambda shp: pl.BlockSpec(shp, lambda p, i: (0, 0))
    return pl.pallas_call(
        body, grid=(2, nt),
        in_specs=[pl.BlockSpec((tile_rows, SSM_WIDTH), lambda p, i: (tile(i), 0)), pl.BlockSpec(memory_space=pl.ANY),
                  pl.BlockSpec(memory_space=pl.ANY), fix((1, 2 * N_STATE)), fix((1, 2 * N_STATE))],
        out_specs=[pl.BlockSpec((tile_rows, 2 * N_STATE), held), pl.BlockSpec((tile_rows, SSM_WIDTH), held)],
        out_shape=[jax.ShapeDtypeStruct((lp, 2 * N_STATE), F32), jax.ShapeDtypeStruct((lp, SSM_WIDTH), F32)],
        scratch_shapes=[pltpu.VMEM(wb.shape, BF16), pltpu.VMEM(wc.shape, BF16), pltpu.VMEM((tile_rows, 2 * N_STATE), F32),
                        pltpu.VMEM((N_SEG, 2 * N_STATE), F32), pltpu.SemaphoreType.DMA((2,))],
        name=name, compiler_params=pltpu.CompilerParams(dimension_semantics=("arbitrary", "arbitrary"),
                                                        vmem_limit_bytes=V7X_VMEM_LIMIT_BYTES))(
        u_seg, wb.astype(BF16), wc.astype(BF16), a, a_seg)


def _ssm_bwd_dir(name, dys_seg, u_seg, x_seg, wb, wc, a_conj, a_seg_conj, fwd_reverse):
    lp = u_seg.shape[0]
    tile_rows = _ssm_tile(lp, False)
    nt = lp // tile_rows
    steps = tile_rows // N_SEG
    reverse = not fwd_reverse
    tile = (lambda i: nt - 1 - i) if reverse else (lambda i: i)
    first = tile(0)
    held = lambda p, i: (p * tile(i) + (1 - p) * first, 0)
    n_slab = lp // N_SEG
    if fwd_reverse:
        halo = lambda p, i: (p * jnp.minimum((tile(i) + 1) * steps, n_slab - 1), 0)
        edge = lambda p, i: (0, 0)
    else:
        halo = lambda p, i: (p * jnp.maximum(tile(i) * steps - 1, 0), 0)
        edge = lambda p, i: (n_slab - 1, 0)

    def body(dy_ref, u_ref, x_ref, halo_ref, edge_ref, wb_hbm, wc_hbm, a_ref, aseg_ref, du_ref, dwb_ref, dwc_ref, ga_ref,
             wb_ref, wc_ref, buf_ref, st_ref, sem):
        p, i = pl.program_id(0), pl.program_id(1)

        @pl.when((p == 0) & (i == 0))
        def _():
            _copy_in([(wb_hbm, wb_ref), (wc_hbm, wc_ref)], sem)
            st_ref[...] = jnp.zeros_like(st_ref)
            dwb_ref[...] = jnp.zeros_like(dwb_ref)
            dwc_ref[...] = jnp.zeros_like(dwc_ref)
            ga_ref[...] = jnp.zeros_like(ga_ref)

        @pl.when((p == 1) & (i == 0))
        def _():
            st_ref[...] = _segment_starts(st_ref[...], aseg_ref[...], reverse)

        for j in range(SUPER):
            part = _dot_nt(dy_ref[:, 128 * j:128 * (j + 1)], wc_ref[j])
            buf_ref[:, 512 * j:512 * (j + 1)] = part[:, :512]
            buf_ref[:, N_STATE + 512 * j:N_STATE + 512 * (j + 1)] = part[:, 512:]
        _recurrence(buf_ref, st_ref, a_ref, reverse)

        @pl.when(p == 1)
        def _():
            for j in range(SUPER):
                ch = slice(128 * j, 128 * (j + 1))
                du_ref[:, ch] = (_dot_nt(buf_ref[:, _re(j)], wb_ref[j, :, :512]) + _dot_nt(buf_ref[:, _im(j)], wb_ref[j, :, 512:]))
                dwb_ref[ch, :512] += _dot_tn(u_ref[:, ch], buf_ref[:, _re(j)])
                dwb_ref[ch, 512:] += _dot_tn(u_ref[:, ch], buf_ref[:, _im(j)])
                dwc_ref[ch, :512] += _dot_tn(dy_ref[:, ch], x_ref[:, _re(j)])
                dwc_ref[ch, 512:] += _dot_tn(dy_ref[:, ch], x_ref[:, _im(j)])
            row = lax.broadcasted_iota(jnp.int32, (N_SEG, 2 * N_STATE), 0)
            if fwd_reverse:
                wrap = jnp.where(row == N_SEG - 1, 0.0, pltpu.roll(edge_ref[...], N_SEG - 1, axis=0))
                open_slab = jnp.where(tile(i) == nt - 1, wrap, halo_ref[...])
                before = lambda cols: jnp.concatenate([x_ref[N_SEG:, cols], open_slab[:, cols]], axis=0)
            else:
                wrap = jnp.where(row == 0, 0.0, pltpu.roll(edge_ref[...], 1, axis=0))
                open_slab = jnp.where(tile(i) == 0, wrap, halo_ref[...])
                before = lambda cols: jnp.concatenate([open_slab[:, cols], x_ref[:tile_rows - N_SEG, cols]], axis=0)
            half = N_STATE // 2
            for c0 in (0, half):
                re = slice(c0, c0 + half)
                im = slice(N_STATE + c0, N_STATE + c0 + half)
                gr, gi = buf_ref[:, re], buf_ref[:, im]
                br, bi = before(re), before(im)
                fold = lambda v: jnp.sum(v.reshape(steps, N_SEG, half), axis=0)
                ga_ref[:, re] += fold(gr * br + gi * bi)
                ga_ref[:, im] += fold(gi * br - gr * bi)

    fix = lambda shp: pl.BlockSpec(shp, lambda p, i: (0, 0))
    row_tile = lambda w: pl.BlockSpec((tile_rows, w), lambda p, i: (tile(i), 0))
    return pl.pallas_call(
        body, grid=(2, nt),
        in_specs=[row_tile(SSM_WIDTH), row_tile(SSM_WIDTH), pl.BlockSpec((tile_rows, 2 * N_STATE), held),
                  pl.BlockSpec((N_SEG, 2 * N_STATE), halo), pl.BlockSpec((N_SEG, 2 * N_STATE), edge),
                  pl.BlockSpec(memory_space=pl.ANY), pl.BlockSpec(memory_space=pl.ANY), fix((1, 2 * N_STATE)), fix((1, 2 * N_STATE))],
        out_specs=[pl.BlockSpec((tile_rows, SSM_WIDTH), held), fix((SSM_WIDTH, 1024)), fix((SSM_WIDTH, 1024)),
                   fix((N_SEG, 2 * N_STATE))],
        out_shape=[jax.ShapeDtypeStruct((lp, SSM_WIDTH), F32), jax.ShapeDtypeStruct((SSM_WIDTH, 1024), F32),
                   jax.ShapeDtypeStruct((SSM_WIDTH, 1024), F32), jax.ShapeDtypeStruct((N_SEG, 2 * N_STATE), F32)],
        scratch_shapes=[pltpu.VMEM(wb.shape, BF16), pltpu.VMEM(wc.shape, BF16), pltpu.VMEM((tile_rows, 2 * N_STATE), F32),
                        pltpu.VMEM((N_SEG, 2 * N_STATE), F32), pltpu.SemaphoreType.DMA((2,))],
        name=name, compiler_params=pltpu.CompilerParams(dimension_semantics=("arbitrary", "arbitrary"),
                                                        vmem_limit_bytes=V7X_VMEM_LIMIT_BYTES))(
        dys_seg, u_seg, x_seg, x_seg, x_seg, wb.astype(BF16), wc.astype(BF16), a_conj, a_seg_conj)


def _ssm_prep(lam_re, lam_im, log_dt, b_re, b_im, c_re, c_im):
    dt = jnp.exp(log_dt)[:, None]
    er = jnp.exp(lam_re * dt)
    ar, ai = er * jnp.cos(lam_im * dt), er * jnp.sin(lam_im * dt)
    nr, ni = ar - 1.0, ai
    den = lam_re * lam_re + lam_im * lam_im
    cr, ci = (nr * lam_re + ni * lam_im) / den, (ni * lam_re - nr * lam_im) / den
    bbr = cr[:, :, None] * b_re - ci[:, :, None] * b_im
    bbi = cr[:, :, None] * b_im + ci[:, :, None] * b_re
    eye = jnp.eye(8, dtype=F32)

    def in_map(b):
        b = b.reshape(SUPER, 8, SSM_STATE, SSM_GROUP_CH).transpose(0, 1, 3, 2)
        return (b[:, :, :, None, :] * eye[None, :, None, :, None]).reshape(SUPER, 128, 512)

    def out_map(cm):
        cm = cm.reshape(SUPER, 8, SSM_GROUP_CH, SSM_STATE).transpose(0, 1, 3, 2)
        return (cm[:, :, :, None, :] * eye[None, :, None, :, None]).reshape(SUPER, 512, 128)

    wb = jnp.concatenate([in_map(bbr), in_map(bbi)], axis=2)
    wc = jnp.concatenate([out_map(c_re), -out_map(c_im)], axis=1)
    return ar.reshape(1, N_STATE), ai.reshape(1, N_STATE), wb, wc


def _re(j):
    return slice(512 * j, 512 * (j + 1))


def _im(j):
    return slice(N_STATE + 512 * j, N_STATE + 512 * (j + 1))


def _row_ids(i, tm, width):
    return i * tm + lax.broadcasted_iota(jnp.int32, (tm, width), 0)


_FF_CHUNKS = (slice(0, D_FF // 2), slice(D_FF // 2, D_FF))


def _ffn_fwd(tag, h, gain, wg, wu, wd, next_shard=None):
    def up(i, h_ref, wg_ref, wu_ref, g_ref, n_ref, b_ref, silu_ref, dsilu_ref, act_ref):
        n = _rms_fwd(h_ref[...], g_ref[...]).astype(BF16)
        n_ref[...] = n
        for cols in (slice(0, D_FF),):
            a = _dot(n, wg_ref[:, cols])
            b = _dot(n, wu_ref[:, cols])
            sg = _sigmoid(a)
            silu = a * sg
            b_ref[:, cols] = b.astype(BF16)
            silu_ref[:, cols] = silu.astype(BF16)
            dsilu_ref[:, cols] = (sg * (1.0 + a * (1.0 - sg))).astype(BF16)
            act_ref[:, cols] = (silu * b).astype(BF16)

    outs = [(D_MODEL, BF16)] + [(D_FF, BF16)] * 4
    gathered = None
    if next_shard is None:
        n, b, silu, dsilu, act = _rowk(tag + "_up", up, [_rows(h)], [wg, wu, gain], outs, tm=192, writes_outs=True)
    else:
        n, b, silu, dsilu, act, got = _rowk(tag + "_up_gather", up, [_rows(h)], [wg, wu, gain], outs, tm=192,
                                            writes_outs=True, hosted=_gather_ici_stage(next_shard))
        gathered = _gather_finish("gather_w", next_shard, got)
    out = _rowk(tag + "_down", lambda i, act_ref, h_ref, wd_ref: h_ref[...] + 0.5 * _dot(act_ref[...], wd_ref[...]),
                [_rows(act), _rows(h)], [wd], [(D_MODEL, F32)])[0]
    return out, (h, n, b, silu, dsilu, act), gathered


def _ffn_bwd(tag, dh, saved, gain, wg, wu, wd, pending=None, sel=None):
    h, n, b, silu, dsilu, act = saved
    hosted1 = hosted2 = reduced = None
    if pending is not None:
        f, hosted1 = _rs_pair_stage(pending)
        tag = tag + "_reduce"

    def bwd1(i, dh_ref, b_ref, silu_ref, dsilu_ref, wd_ref, da_ref, db_ref):
        dhb = (0.5 * dh_ref[...]).astype(BF16)
        for cols in _FF_CHUNKS:
            dact = _dot_nt(dhb, wd_ref[cols, :])
            da_ref[:, cols] = (dact * b_ref[:, cols].astype(F32) * dsilu_ref[:, cols].astype(F32)).astype(BF16)
            db_ref[:, cols] = (dact * silu_ref[:, cols].astype(F32)).astype(BF16)

    res = _rowk(tag + "_bwd_act", bwd1, [_rows(dh), _rows(b), _rows(silu), _rows(dsilu)], [wd], [(D_FF, BF16)] * 2,
                writes_outs=True, hosted=hosted1)
    da, db = res[0], res[1]
    if pending is not None:
        p, hosted2 = _rs_chip_stage("reduce_w", f, res[2], sel)

    def bwd2(i, da_ref, db_ref, h_ref, dh_ref, wg_ref, wu_ref, g_ref):
        dn = _dot_nt(da_ref[...], wg_ref[...]) + _dot_nt(db_ref[...], wu_ref[...])
        dx, dg = _rms_bwd(h_ref[...], g_ref[...], dn)
        return dh_ref[...] + dx, dg

    res = _rowk(tag + "_bwd_in", bwd2, [_rows(da), _rows(db), _rows(h), _rows(dh)], [wg, wu, gain],
                [(D_MODEL, F32)], accs=[(1, D_MODEL)], hosted=hosted2)
    dh_in, dgain = res[0], res[1]
    if pending is not None:
        reduced = _rs_finish("reduce_w", p, res[2], sel)
    dwd = _mm_tn("ffn_dwd", act, dh, tk=D_FF // 2, tn=D_MODEL, scale=0.5)
    dwg = _mm_tn("ffn_dwg", n, da, tk=D_MODEL, tn=D_FF // 2)
    dwu = _mm_tn("ffn_dwu", n, db, tk=D_MODEL, tn=D_FF // 2)
    return dh_in, dgain, dwg, dwu, dwd, reduced


def _mixer_fwd(h, lw, ssm):
    lp = h.shape[0]
    def mix_in(i, h_ref, w_ref, g_ref):
        nv = _rms_fwd(h_ref[...], g_ref[...]).astype(BF16)
        return nv, _dot(nv, w_ref[...])

    n, proj = _rowk("mix_in", mix_in, [_rows(h)], [lw["w_in"], lw["mix_norm"]], [(D_MODEL, BF16), (4 * D_MODEL, F32)])
    yattn, lse = _attn_fwd(proj, lw["attn_sink"])
    u_seg = _to_segments(proj[:, 3 * SSM_WIDTH:4 * SSM_WIDTH])
    xs, ydir = [], []
    for d in range(2):
        x_seg, y_seg = _ssm_fwd_dir(f"ssm_fwd{d}", u_seg, ssm[d]["wb"], ssm[d]["wc"], ssm[d]["a"], ssm[d]["a_seg"],
                                    reverse=(d == 1))
        xs.append(x_seg)
        ydir.append(y_seg)

    def ssm_out(i, y0_ref, y1_ref, u_ref, d_ref, wglu_ref):
        ys = y0_ref[...] + y1_ref[...] + d_ref[...] * u_ref[...]
        z = _gelu(ys)
        return ys, z * _sigmoid(_dot(z, wglu_ref[...]))

    ys, yssm_seg = _rowk("ssm_out", ssm_out, [_rows(ydir[0]), _rows(ydir[1]), _rows(u_seg)],
                         [lw["ssm_d"], lw["ssm_w_glu"]], [(SSM_WIDTH, F32), (SSM_WIDTH, BF16)])
    yssm = _from_segments(yssm_seg)

    def merge(i, ys_ref, ya_ref, gs_ref, ga_ref, wbs_ref, wba_ref):
        bs = _dot(ys_ref[...], wbs_ref[...])
        ba = _dot(ya_ref[...], wba_ref[...])
        m = _sigmoid(gs_ref[...]) * bs + _sigmoid(ga_ref[...]) * ba
        return bs, ba, jnp.where(_row_ids(i, ROW_TILE, D_MODEL) >= PAD, m, 0.0)

    bs, ba, merged = _rowk("mix_merge", merge, [_rows(yssm), _rows(yattn), _rows(proj, D_MODEL, 2), _rows(proj, D_MODEL, 3)],
                           [lw["w_branch_ssm"], lw["w_branch_attn"]], [(D_MODEL, BF16)] * 3)
    out = _rowk("mix_out", lambda i, m_ref, h_ref, w_ref: h_ref[...] + _dot(m_ref[...], w_ref[...]),
                [_rows(merged), _rows(h)], [lw["w_out"]], [(D_MODEL, F32)])[0]
    return out, (h, n, proj, yattn, lse, u_seg, xs, ys, yssm, bs, ba, merged)


def _mixer_bwd(dh, saved, lw, ssm):
    h, n, proj, yattn, lse, u_seg, xs, ys, yssm, bs, ba, merged = saved

    def bwd1(i, dh_ref, gs_ref, ga_ref, bs_ref, ba_ref, w_ref):
        dm = _dot_nt(dh_ref[...], w_ref[...])
        dm = jnp.where(_row_ids(i, ROW_TILE, D_MODEL) >= PAD, dm, 0.0)
        sgs = _sigmoid(gs_ref[...])
        sga = _sigmoid(ga_ref[...])
        return (dm * sgs, dm * sga, dm * bs_ref[...].astype(F32) * sgs * (1.0 - sgs),
                dm * ba_ref[...].astype(F32) * sga * (1.0 - sga))

    dbs, dba, dgs, dga = _rowk("mix_bwd_merge", bwd1,
                               [_rows(dh), _rows(proj, D_MODEL, 2), _rows(proj, D_MODEL, 3), _rows(bs), _rows(ba)],
                               [lw["w_out"]], [(D_MODEL, BF16)] * 4)
    dw_out = _mm_tn("mix_dw_out", merged, dh, tk=D_MODEL, tn=D_MODEL)
    dw_bs = _mm_tn("mix_dw_bs", yssm, dbs, tk=SSM_WIDTH, tn=D_MODEL)
    dw_ba = _mm_tn("mix_dw_ba", yattn, dba, tk=D_MODEL, tn=D_MODEL)

    def bwd2(i, dbs_ref, dba_ref, wbs_ref, wba_ref):
        return _dot_nt(dba_ref[...], wba_ref[...]), _dot_nt(dbs_ref[...], wbs_ref[...])

    dyattn, dyssm = _rowk("mix_bwd_branches", bwd2, [_rows(dbs), _rows(dba)], [lw["w_branch_ssm"], lw["w_branch_attn"]],
                          [(D_MODEL, F32), (SSM_WIDTH, F32)])

    def bwd3(i, dyssm_ref, ys_ref, u_ref, wglu_ref):
        ysv = ys_ref[...]
        z = _gelu(ysv)
        sg = _sigmoid(_dot(z, wglu_ref[...]))
        dt = dyssm_ref[...] * z * sg * (1.0 - sg)
        dz = dyssm_ref[...] * sg + _dot_nt(dt, wglu_ref[...])
        dys = dz * _gelu_grad(ysv)
        return dys, z, dt, jnp.sum(dys * u_ref[...], axis=0, keepdims=True)

    dys, z, dt, dd = _rowk("mix_bwd_ssm_out", bwd3, [_rows(_to_segments(dyssm)), _rows(ys), _rows(u_seg)], [lw["ssm_w_glu"]],
                           [(SSM_WIDTH, F32), (SSM_WIDTH, BF16), (SSM_WIDTH, BF16)], accs=[(1, SSM_WIDTH)])
    dw_glu = _mm_tn("mix_dw_glu", z, dt, tk=SSM_WIDTH, tn=SSM_WIDTH)

    dus, ssm_cot = [], []
    for d in range(2):
        du_d, dwb, dwc_t, ga = _ssm_bwd_dir(f"ssm_bwd{d}", dys, u_seg, xs[d], ssm[d]["wb"], ssm[d]["wc"], ssm[d]["a_conj"],
                                            ssm[d]["a_seg_conj"], fwd_reverse=(d == 1))
        dus.append(du_d)
        ga = jnp.sum(ga, axis=0, keepdims=True)
        ssm_cot.append((ga[:, :N_STATE], ga[:, N_STATE:], dwb.reshape(SUPER, 128, 1024),
                        dwc_t.reshape(SUPER, 128, 1024).transpose(0, 2, 1)))

    du_seg = _rowk("ssm_bwd_du", lambda i, a_ref, b_ref, dys_ref, d_ref: a_ref[...] + b_ref[...] + d_ref[...] * dys_ref[...],
                   [_rows(dus[0]), _rows(dus[1]), _rows(dys)], [lw["ssm_d"]], [(SSM_WIDTH, BF16)])[0]
    du = _from_segments(du_seg)

    dq, dk, dv, dkm, dvm, dsink = _attn_bwd(proj, lw["attn_sink"], yattn, lse, dyattn)

    def bwd_in(i, dq_ref, dk_ref, dv_ref, du_ref, dgs_ref, dga_ref, h_ref, dh_ref, dkm_ref, dvm_ref, w_ref, g_ref):
        first = jnp.where(i == 0, 1.0, 0.0)
        zeros = lambda r: jnp.zeros((r, N_KV_HEADS * HEAD_DIM), F32)
        place = lambda m: jnp.concatenate([zeros(PAD), m[...] * first, zeros(ROW_TILE - BLOCK)], axis=0)
        dp = jnp.concatenate([dq_ref[...].astype(F32), dk_ref[...] + place(dkm_ref), dv_ref[...] + place(dvm_ref),
                              du_ref[...].astype(F32), dgs_ref[...].astype(F32), dga_ref[...].astype(F32)], axis=1)
        dp = jnp.where(_row_ids(i, ROW_TILE, 4 * D_MODEL) >= PAD, dp, 0.0).astype(BF16)
        dx, dg = _rms_bwd(h_ref[...], g_ref[...], _dot_nt(dp, w_ref[...]))
        return dp, dh_ref[...] + dx, dg

    dproj, dh_in, dgain = _rowk("mix_bwd_in", bwd_in,
                                [_rows(dq), _rows(dk), _rows(dv), _rows(du), _rows(dgs), _rows(dga), _rows(h), _rows(dh)],
                                [dkm, dvm, lw["w_in"], lw["mix_norm"]], [(4 * D_MODEL, BF16), (D_MODEL, F32)],
                                accs=[(1, D_MODEL)])
    dw_in = _mm_tn("mix_dw_in", n, dproj, tk=D_MODEL, tn=2 * D_MODEL)
    grads = {"w_out": dw_out, "w_branch_ssm": dw_bs, "w_branch_attn": dw_ba, "ssm_w_glu": dw_glu, "w_in": dw_in,
             "mix_norm": dgain, "ssm_d": dd, "attn_sink": dsink[0, :N_HEADS]}
    return dh_in, grads, ssm_cot


def _loss_head(h, gain, target):
    lp = h.shape[0]

    def fn(i, h_ref, t_ref, g_ref):
        x = h_ref[...]
        y = _rms_fwd(x, g_ref[...])
        live = jnp.where(i == 0, 0.0, 1.0)
        dy = (y - t_ref[...]) * live
        loss = 0.5 * jnp.sum(dy * dy) / D_MODEL
        dx, dg = _rms_bwd(x, g_ref[...], dy * (1.0 / D_MODEL))
        return dx, jnp.full((1, BLOCK), loss, F32), dg

    tgt = (target, (BLOCK, D_MODEL), lambda i: (jnp.maximum(i - 1, 0), 0))
    return _rowk("loss_head", fn, [_rows(h), tgt], [gain], [(D_MODEL, F32)], accs=[(1, BLOCK), (1, D_MODEL)], tm=BLOCK)


def _adamw(name, w, g, m, v, tm):
    def fn(i, w_ref, g_ref, m_ref, v_ref):
        gv = g_ref[...]
        mn = ADAM_B1 * m_ref[...] + (1.0 - ADAM_B1) * gv
        vn = ADAM_B2 * v_ref[...] + (1.0 - ADAM_B2) * (gv * gv)
        m_hat = mn / (1.0 - ADAM_B1 ** ADAM_STEP)
        v_hat = vn / (1.0 - ADAM_B2 ** ADAM_STEP)
        return -ADAM_LR * (m_hat / (jnp.sqrt(v_hat) + ADAM_EPS) + ADAM_WD * w_ref[...]), mn, vn

    wd = w.shape[1]
    return _rowk(name, fn, [_rows(w), _rows(g), _rows(m), _rows(v)], [], [(wd, F32)] * 3, tm=tm)


def _shard_rows(name):
    return {"ffn1_w_gate": 704, "ffn1_w_up": 704, "ffn1_w_down": 704, "ffn2_w_gate": 704, "ffn2_w_up": 704, "ffn2_w_down": 704,
            "w_in": 1024, "ssm_w_glu": 64, "w_branch_ssm": 128, "w_branch_attn": 256, "w_out": 256}[name]


def _full_shape(name):
    return {"ffn1_w_gate": (D_MODEL, D_FF), "ffn1_w_up": (D_MODEL, D_FF), "ffn1_w_down": (D_FF, D_MODEL),
            "ffn2_w_gate": (D_MODEL, D_FF), "ffn2_w_up": (D_MODEL, D_FF), "ffn2_w_down": (D_FF, D_MODEL),
            "w_in": (D_MODEL, 4 * D_MODEL), "ssm_w_glu": (SSM_WIDTH, SSM_WIDTH), "w_branch_ssm": (SSM_WIDTH, D_MODEL),
            "w_branch_attn": (D_MODEL, D_MODEL), "w_out": (D_MODEL, D_MODEL)}[name]


def _unflatten_gathered(gathered):
    out, r0 = {}, 0
    for name in BIG:
        r = _shard_rows(name)
        k, nn = _full_shape(name)
        piece = gathered[:, r0:r0 + r, :]
        if name in COL_SHARDED:
            out[name] = piece.reshape(4, k, nn // 4).transpose(1, 0, 2).reshape(k, nn)
        else:
            out[name] = piece.reshape(k, nn)
        r0 += r
    return out


FIRST_FFN = ["ffn1_w_gate", "ffn1_w_up", "ffn1_w_down"]
REST_OF_LAYER = [n for n in BIG if n not in FIRST_FFN]


def _flatten_full(grads, names):
    per_shard = []
    for s in range(4):
        pieces = []
        for name in names:
            k, nn = _full_shape(name)
            g = grads[name]
            piece = g[:, s * (nn // 4):(s + 1) * (nn // 4)] if name in COL_SHARDED else g[s * (k // 4):(s + 1) * (k // 4), :]
            pieces.append(piece.reshape(-1, 1024))
        per_shard.append(jnp.concatenate(pieces, axis=0))
    f = jnp.stack(per_shard)
    return f.reshape(4, 2, f.shape[1] // 2, 1024).transpose(1, 0, 2, 3)


def _shard_2d(a):
    return a.reshape(-1, a.shape[-1])


def kernel(x, meta_tokens, ffn1_norm, ffn1_w_gate, ffn1_w_up, ffn1_w_down, mix_norm, w_in, ssm_lam_re, ssm_lam_im, ssm_log_dt, ssm_b_re, ssm_b_im, ssm_c_re, ssm_c_im, ssm_d, ssm_w_glu, attn_sink, w_branch_ssm, w_branch_attn, w_out, ffn2_norm, ffn2_w_gate, ffn2_w_up, ffn2_w_down, final_norm, loss_target, m_meta_tokens, m_ffn1_norm, m_ffn1_w_gate, m_ffn1_w_up, m_ffn1_w_down, m_mix_norm, m_w_in, m_ssm_lam_re, m_ssm_lam_im, m_ssm_log_dt, m_ssm_b_re, m_ssm_b_im, m_ssm_c_re, m_ssm_c_im, m_ssm_d, m_ssm_w_glu, m_attn_sink, m_w_branch_ssm, m_w_branch_attn, m_w_out, m_ffn2_norm, m_ffn2_w_gate, m_ffn2_w_up, m_ffn2_w_down, m_final_norm, v_meta_tokens, v_ffn1_norm, v_ffn1_w_gate, v_ffn1_w_up, v_ffn1_w_down, v_mix_norm, v_w_in, v_ssm_lam_re, v_ssm_lam_im, v_ssm_log_dt, v_ssm_b_re, v_ssm_b_im, v_ssm_c_re, v_ssm_c_im, v_ssm_d, v_ssm_w_glu, v_attn_sink, v_w_branch_ssm, v_w_branch_attn, v_w_out, v_ffn2_norm, v_ffn2_w_gate, v_ffn2_w_up, v_ffn2_w_down, v_final_norm):
    args = dict(locals())
    w = {k: args[k] for k in WEIGHTS}
    mom = {k: args["m_" + k] for k in WEIGHTS}
    var = {k: args["v_" + k] for k in WEIGHTS}
    depth = ffn1_norm.shape[0]
    seq = x.shape[1]
    xi, yi, ci = lax.axis_index("x"), lax.axis_index("y"), lax.axis_index("c")
    chip = 2 * xi + yi
    me = 4 * xi + 2 * yi + ci
    sel = jnp.stack([ci, chip]).astype(jnp.int32)

    same_core = [(fx, fy, 0) for fx, fy in _OTHER_CHIPS]
    meta_all = _all_gather_all("gather_meta", meta_tokens, 4, same_core, lambda x_, y_, c_: _chip(x_, y_))
    meta_full = meta_all.transpose(1, 0, 2).reshape(N_META, D_MODEL)
    flat_w = [jnp.concatenate([w[name][l].reshape(-1, 1024) for name in BIG], axis=0).astype(BF16) for l in range(depth)]

    def layer_weights(l, gathered):
        lw = _unflatten_gathered(gathered)
        for name in ("ffn1_norm", "mix_norm", "ffn2_norm"):
            lw[name] = w[name][l].reshape(1, D_MODEL)
        lw["ssm_d"] = ssm_d[l].reshape(1, SSM_WIDTH)
        lw["attn_sink"] = attn_sink[l]
        return lw

    layer_w = [layer_weights(0, _all_gather_shards("gather_w", flat_w[0]))]

    ssm_params = ("ssm_lam_re", "ssm_lam_im", "ssm_log_dt", "ssm_b_re", "ssm_b_im", "ssm_c_re", "ssm_c_im")
    ssm, ssm_vjp = [], []
    for l in range(depth):
        dirs, vjps = [], []
        for d in range(2):
            prm = tuple(w[k][l, d] for k in ssm_params)
            (ar, ai, wb, wc), pull = jax.vjp(_ssm_prep, *prm)
            a_seg = _complex_power(ar, ai, (seq + BLOCK) // N_SEG)
            conj = lambda v: jnp.concatenate([v[0], -v[1]], axis=1)
            pack = lambda v: jnp.concatenate([v[0], v[1]], axis=1)
            dirs.append({"wb": wb, "wc": wc, "a": pack((ar, ai)), "a_seg": pack(a_seg),
                         "a_conj": conj((ar, ai)), "a_seg_conj": conj(a_seg)})
            vjps.append(pull)
        ssm.append(dirs)
        ssm_vjp.append(vjps)

    h = jnp.concatenate([jnp.zeros((PAD, D_MODEL), F32), meta_full, x[0]], axis=0)
    saved = []
    for l in range(depth):
        lw = layer_w[l]
        h, s1, gathered = _ffn_fwd("ffn", h, lw["ffn1_norm"], lw["ffn1_w_gate"], lw["ffn1_w_up"], lw["ffn1_w_down"],
                                   next_shard=flat_w[l + 1] if l + 1 < depth else None)
        if gathered is not None:
            layer_w.append(layer_weights(l + 1, gathered))
        h, s2 = _mixer_fwd(h, lw, ssm[l])
        h, s3, _ = _ffn_fwd("ffn", h, lw["ffn2_norm"], lw["ffn2_w_gate"], lw["ffn2_w_up"], lw["ffn2_w_down"])
        saved.append((s1, s2, s3))
    dh, loss_part, d_final = _loss_head(h, final_norm.reshape(1, D_MODEL), loss_target[0])
    loss = lax.psum(loss_part[0, 0], MESH_AXES)

    small_g = {k: [None] * depth for k in SMALL if k not in ("meta_tokens", "final_norm")}
    big_g = {k: [None] * depth for k in BIG}
    def keep_shard(l, reduced, names):
        r0 = 0
        for name in names:
            r = _shard_rows(name)
            big_g[name][l] = reduced[r0:r0 + r].reshape(w[name].shape[1:])
            r0 += r

    pending = None
    for l in reversed(range(depth)):
        lw = layer_w[l]
        s1, s2, s3 = saved[l]
        full = {}
        dh, dg, full["ffn2_w_gate"], full["ffn2_w_up"], full["ffn2_w_down"], reduced = _ffn_bwd(
            "ffn", dh, s3, lw["ffn2_norm"], lw["ffn2_w_gate"], lw["ffn2_w_up"], lw["ffn2_w_down"], pending, sel)
        if pending is not None:
            keep_shard(l + 1, reduced, FIRST_FFN)
        small_g["ffn2_norm"][l] = dg[0]
        dh, mg, ssm_cot = _mixer_bwd(dh, s2, lw, ssm[l])
        for k in ("w_out", "w_branch_ssm", "w_branch_attn", "ssm_w_glu", "w_in"):
            full[k] = mg[k]
        small_g["mix_norm"][l] = mg["mix_norm"][0]
        small_g["ssm_d"][l] = mg["ssm_d"][0]
        small_g["attn_sink"][l] = mg["attn_sink"]
        per_dir = []
        for d in range(2):
            per_dir.append(ssm_vjp[l][d](ssm_cot[d]))
        for j, k in enumerate(ssm_params):
            small_g[k][l] = jnp.stack([per_dir[0][j], per_dir[1][j]])
        dh, dg, full["ffn1_w_gate"], full["ffn1_w_up"], full["ffn1_w_down"], reduced = _ffn_bwd(
            "ffn", dh, s1, lw["ffn1_norm"], lw["ffn1_w_gate"], lw["ffn1_w_up"], lw["ffn1_w_down"],
            _flatten_full(full, REST_OF_LAYER), sel)
        keep_shard(l, reduced, REST_OF_LAYER)
        small_g["ffn1_norm"][l] = dg[0]
        pending = _flatten_full(full, FIRST_FFN)
    keep_shard(0, _reduce_scatter("reduce_w", pending, sel), FIRST_FFN)

    grad_x = dh[BLOCK:][None]
    small_list = [dh[PAD:BLOCK].reshape(-1)]
    for k in SMALL[1:]:
        small_list.append(d_final.reshape(-1) if k == "final_norm" else jnp.stack(small_g[k]).reshape(-1))
    small_vec = jnp.concatenate(small_list)
    n_small = small_vec.shape[0]
    rows_small = -(-n_small // (64 * 1024)) * 64
    small_vec = jnp.pad(small_vec, (0, rows_small * 1024 - n_small)).reshape(rows_small, 1024)
    everyone = [(fx, fy, fc) for fx in (0, 1) for fy in (0, 1) for fc in (0, 1)][1:]
    index = lambda x_, y_, c_: 4 * x_ + 2 * y_ + c_
    eighths = small_vec.reshape(8, rows_small // 8, 1024)
    sends = [(f, (lambda x_, y_, c_, f=f: index(x_ ^ f[0], y_ ^ f[1], c_ ^ f[2])), index) for f in everyone]
    got = _exchange("scatter_small", eighths, 8, [], sends)
    mine = lax.broadcasted_iota(jnp.int32, (8, 1, 1), 0) == me
    got = jnp.where(mine, lax.dynamic_index_in_dim(eighths, me, 0, keepdims=True), got)
    part = _slot_sum("sum_small", [(got, k) for k in range(8)], sel, rows_small // 8)
    small_sum = _all_gather_all("gather_small", part, 8, everyone, index).reshape(-1)

    grads, deltas, new_m, new_v = {}, {}, {}, {}
    off = 0
    for k in SMALL:
        size = (N_META * D_MODEL) if k == "meta_tokens" else int(np.prod(w[k].shape))
        g = small_sum[off:off + size]
        off += size
        if k == "meta_tokens":
            g = lax.dynamic_slice(g.reshape(N_META, D_MODEL), (0, chip * (D_MODEL // 4)), (N_META, D_MODEL // 4))
        else:
            g = g.reshape(w[k].shape)
        grads[k] = g
        as_2d = (lambda a: a.reshape(-1, a.shape[-1])) if g.ndim >= 2 else (lambda a: a.reshape(1, -1))
        rows_k = as_2d(g).shape[0]
        d_, m_, v_ = _adamw("adamw_" + k, as_2d(w[k]), as_2d(g), as_2d(mom[k]), as_2d(var[k]), min(rows_k, 2048))
        deltas[k], new_m[k], new_v[k] = d_.reshape(g.shape), m_.reshape(g.shape), v_.reshape(g.shape)
    for k in BIG:
        g = jnp.stack(big_g[k])
        grads[k] = g
        rows_k = _shard_2d(g).shape[0]
        tm = 512 if rows_k % 512 == 0 else rows_k // depth
        d_, m_, v_ = _adamw("adamw_" + k, _shard_2d(w[k]), _shard_2d(g), _shard_2d(mom[k]), _shard_2d(var[k]), tm)
        deltas[k], new_m[k], new_v[k] = d_.reshape(g.shape), m_.reshape(g.shape), v_.reshape(g.shape)

    return (loss, grad_x, *[grads[k] for k in WEIGHTS], *[deltas[k] for k in WEIGHTS],
            *[new_m[k] for k in WEIGHTS], *[new_v[k] for k in WEIGHTS])
```

```python
import functools
import math

import numpy as np
import jax
import jax.numpy as jnp
from jax import lax
from jax.experimental import pallas as pl
from jax.experimental.pallas import tpu as pltpu

F32 = jnp.float32
BF16 = jnp.bfloat16

D_MODEL = 1024
N_META = 16
N_HEADS = 16
N_KV_HEADS = 4
HEAD_DIM = 64
Q_GROUP = N_HEADS // N_KV_HEADS
WINDOW = 128
BLOCK = 128
PAD = BLOCK - N_META
SSM_WIDTH = 512
SSM_GROUP_CH = 16
SSM_GROUPS = 32
SSM_STATE = 64
N_STATE = SSM_GROUPS * SSM_STATE
SUPER = 4
D_FF = 2816
EPS = 1e-6
NEG = -1e30
ATTN_SCALE = HEAD_DIM ** -0.5
SLOPES = [float(2.0 ** (-8.0 * (h + 1) / N_HEADS)) for h in range(N_HEADS)]

ADAM_LR = 0.001
ADAM_B1 = 0.9
ADAM_B2 = 0.999
ADAM_EPS = 1e-08
ADAM_WD = 0.01
ADAM_STEP = 10

V7X_VMEM_LIMIT_BYTES = 56 * 1024 * 1024
ROW_TILE = 384
LONG_ROW_TILE = 2064
MESH_AXES = ("x", "y", "c")

BIG = ["ffn1_w_gate", "ffn1_w_up", "ffn1_w_down", "ffn2_w_gate", "ffn2_w_up", "ffn2_w_down",
       "w_in", "ssm_w_glu", "w_branch_ssm", "w_branch_attn", "w_out"]
COL_SHARDED = {"ffn1_w_gate", "ffn1_w_up", "ffn2_w_gate", "ffn2_w_up", "w_in", "w_branch_ssm"}
SMALL = ["meta_tokens", "ffn1_norm", "mix_norm", "ffn2_norm", "final_norm", "ssm_lam_re", "ssm_lam_im", "ssm_log_dt",
         "ssm_b_re", "ssm_b_im", "ssm_c_re", "ssm_c_im", "ssm_d", "attn_sink"]
WEIGHTS = ["meta_tokens", "ffn1_norm", "ffn1_w_gate", "ffn1_w_up", "ffn1_w_down", "mix_norm", "w_in", "ssm_lam_re",
           "ssm_lam_im", "ssm_log_dt", "ssm_b_re", "ssm_b_im", "ssm_c_re", "ssm_c_im", "ssm_d", "ssm_w_glu", "attn_sink",
           "w_branch_ssm", "w_branch_attn", "w_out", "ffn2_norm", "ffn2_w_gate", "ffn2_w_up", "ffn2_w_down", "final_norm"]


def _dot(a, b):
    return lax.dot_general(a.astype(BF16), b.astype(BF16), (((1,), (0,)), ((), ())), preferred_element_type=F32)


def _dot_nt(a, b):
    return lax.dot_general(a.astype(BF16), b.astype(BF16), (((1,), (1,)), ((), ())), preferred_element_type=F32)


def _dot_tn(a, b):
    return lax.dot_general(a.astype(BF16), b.astype(BF16), (((0,), (0,)), ((), ())), preferred_element_type=F32)


def _sigmoid(x):
    return 0.5 * jnp.tanh(0.5 * x) + 0.5


_GELU_C = math.sqrt(2.0 / math.pi)


def _gelu(x):
    return 0.5 * x * (1.0 + jnp.tanh(_GELU_C * (x + 0.044715 * x * x * x)))


def _gelu_grad(x):
    th = jnp.tanh(_GELU_C * (x + 0.044715 * x * x * x))
    return 0.5 * (1.0 + th) + 0.5 * x * (1.0 - th * th) * _GELU_C * (1.0 + 3.0 * 0.044715 * x * x)


def _rms_fwd(x, g):
    r = lax.rsqrt(jnp.mean(x * x, axis=-1, keepdims=True) + EPS)
    return x * r * g


def _rms_bwd(x, g, dn):
    r = lax.rsqrt(jnp.mean(x * x, axis=-1, keepdims=True) + EPS)
    xh = x * r
    t = dn * g
    dx = r * (t - xh * jnp.mean(t * xh, axis=-1, keepdims=True))
    return dx, jnp.sum(dn * xh, axis=0, keepdims=True)


def _compiler_params():
    return pltpu.CompilerParams(dimension_semantics=("arbitrary",), vmem_limit_bytes=V7X_VMEM_LIMIT_BYTES)


def _rows(arr, width=None, cb=0):
    return (arr, arr.shape[1] if width is None else width, cb)


def _rowk(name, fn, rows, fulls, outs, accs=(), tm=ROW_TILE, smem=(), n_rows=None, hosted=None, writes_outs=False):
    n = rows[0][0].shape[0] if n_rows is None else n_rows
    assert n % tm == 0, (name, n, tm)
    in_specs, args = [], []
    for s in smem:
        in_specs.append(pl.BlockSpec(memory_space=pltpu.SMEM))
        args.append(s)
    for r in rows:
        if callable(r[2]):
            in_specs.append(pl.BlockSpec(r[1], r[2]))
        else:
            in_specs.append(pl.BlockSpec((tm, r[1]), functools.partial(lambda i, cb: (i, cb), cb=r[2])))
        args.append(r[0])
    for f in fulls:
        in_specs.append(pl.BlockSpec(memory_space=pl.ANY))
        args.append(f)
    nh = 0 if hosted is None else 1
    if nh:
        in_specs.append(pl.BlockSpec(memory_space=pl.ANY))
        args.append(hosted[0])
    out_specs, out_shape = [], []
    for w, dt in outs:
        out_specs.append(pl.BlockSpec((tm, w), lambda i: (i, 0)))
        out_shape.append(jax.ShapeDtypeStruct((n, w), dt))
    for shp in accs:
        out_specs.append(pl.BlockSpec(shp, functools.partial(lambda i, nd: (0,) * nd, nd=len(shp))))
        out_shape.append(jax.ShapeDtypeStruct(shp, F32))
    if nh:
        out_specs.append(pl.BlockSpec(memory_space=pl.ANY))
        out_shape.append(jax.ShapeDtypeStruct((hosted[1],) + hosted[0].shape[1:], hosted[0].dtype))
    ns, nr, nf, no, na = len(smem), len(rows), len(fulls), len(outs), len(accs)
    scratch = [pltpu.VMEM(f.shape, f.dtype) for f in fulls]
    if nf:
        scratch.append(pltpu.SemaphoreType.DMA((nf,)))
    if nh:
        scratch += [pltpu.SemaphoreType.DMA((len(hosted[2]),)), pltpu.SemaphoreType.DMA((len(hosted[2]),))]
    steps = n // tm

    def body(*refs):
        i = pl.program_id(0)
        refs = list(refs)
        take = lambda k: [refs.pop(0) for _ in range(k)]
        sm, rr, fh, hsrc, oo, aa, hout, fv = take(ns), take(nr), take(nf), take(nh), take(no), take(na), take(nh), take(nf)
        if nf:
            sem = refs.pop(0)

            @pl.when(i == 0)
            def _():
                cps = [pltpu.make_async_copy(fh[j], fv[j], sem.at[j]) for j in range(nf)]
                for cp in cps:
                    cp.start()
                for cp in cps:
                    cp.wait()
        if nh:
            @pl.when(i == 0)
            def _():
                for cp in _remote_copies(hsrc[0], hout[0], refs[0], refs[1], hosted[2]):
                    cp.start()
        if writes_outs:
            res = fn(i, *sm, *rr, *fv, *oo)
            res = (None,) * no + (tuple(res) if isinstance(res, (tuple, list)) else ())
        else:
            res = fn(i, *sm, *rr, *fv)
            res = tuple(res) if isinstance(res, (tuple, list)) else (res,)
            for o, v in zip(oo, res[:no]):
                o[...] = v.astype(o.dtype)
        if na:
            @pl.when(i == 0)
            def _():
                for a in aa:
                    a[...] = jnp.zeros_like(a)
            for a, v in zip(aa, res[no:]):
                a[...] += v
        if nh:
            @pl.when(i == steps - 1)
            def _():
                for cp in _remote_copies(hsrc[0], hout[0], refs[0], refs[1], hosted[2]):
                    cp.wait()

    return pl.pallas_call(body, grid=(steps,), in_specs=in_specs, out_specs=out_specs, out_shape=out_shape,
                          scratch_shapes=scratch, name=name, compiler_params=_compiler_params())(*args)


def _mm_tn(name, x, y, *, xw=None, xcb=0, tk, tn, scale=1.0, tm=None):
    m = x.shape[0]
    if tm is None:
        tm = LONG_ROW_TILE if m % LONG_ROW_TILE == 0 else ROW_TILE
    k = x.shape[1] if xw is None else xw
    nn = y.shape[1]
    assert m % tm == 0 and k % tk == 0 and nn % tn == 0, (name, m, k, nn)
    kb0 = (xcb * k) // tk

    def body(x_ref, y_ref, o_ref):
        @pl.when(pl.program_id(2) == 0)
        def _():
            o_ref[...] = jnp.zeros_like(o_ref)
        yv = y_ref[...]
        if scale != 1.0:
            yv = yv * scale
        o_ref[...] += _dot_tn(x_ref[...], yv)

    return pl.pallas_call(
        body, grid=(k // tk, nn // tn, m // tm),
        in_specs=[pl.BlockSpec((tm, tk), lambda a, b, i: (i, kb0 + a)), pl.BlockSpec((tm, tn), lambda a, b, i: (i, b))],
        out_specs=pl.BlockSpec((tk, tn), lambda a, b, i: (a, b)), out_shape=jax.ShapeDtypeStruct((k, nn), F32), name=name,
        compiler_params=pltpu.CompilerParams(dimension_semantics=("arbitrary", "arbitrary", "arbitrary"),
                                             vmem_limit_bytes=V7X_VMEM_LIMIT_BYTES))(x, y)


def _remote_copies(src_ref, out_ref, ssem, rsem, sends):
    x, y, c = lax.axis_index("x"), lax.axis_index("y"), lax.axis_index("c")
    cps = []
    for k, ((fx, fy, fc), sf, df) in enumerate(sends):
        peer = (1 - x if fx else x, 1 - y if fy else y, 1 - c if fc else c)
        cps.append(pltpu.make_async_remote_copy(src_ref=src_ref.at[sf(x, y, c)], dst_ref=out_ref.at[df(x, y, c)],
                                                send_sem=ssem.at[k], recv_sem=rsem.at[k], device_id=peer,
                                                device_id_type=pl.DeviceIdType.MESH))
    return cps


def _exchange(name, src, n_out, local, sends, alias=False):
    nl, nsnd = len(local), len(sends)
    out_shape = jax.ShapeDtypeStruct((n_out,) + src.shape[1:], src.dtype)

    def body(src_ref, out_ref, lsem, ssem, rsem):
        x, y, c = lax.axis_index("x"), lax.axis_index("y"), lax.axis_index("c")
        cps = [pltpu.make_async_copy(src_ref.at[sf(x, y, c)], out_ref.at[df(x, y, c)], lsem.at[j])
               for j, (sf, df) in enumerate(local)]
        cps += _remote_copies(src_ref, out_ref, ssem, rsem, sends)
        for cp in cps:
            cp.start()
        for cp in cps:
            cp.wait()

    return pl.pallas_call(
        body, in_specs=[pl.BlockSpec(memory_space=pl.ANY)], out_specs=pl.BlockSpec(memory_space=pl.ANY), out_shape=out_shape,
        scratch_shapes=[pltpu.SemaphoreType.DMA((max(nl, 1),)), pltpu.SemaphoreType.DMA((nsnd,)), pltpu.SemaphoreType.DMA((nsnd,))],
        input_output_aliases=({0: 0} if alias else {}), name=name)(src)


def _chip(x, y):
    return 2 * x + y


_OTHER_CHIPS = [(1, 0), (0, 1), (1, 1)]


def _all_gather_shards(name, shard):
    src, n_out, sends = _gather_ici_stage(shard)
    return _gather_finish(name, shard, _exchange(name + "_ici", src, n_out, [], sends))


def _gather_ici_stage(shard):
    r, w = shard.shape
    first = [((fx, fy, 0), lambda x, y, c: c, lambda x, y, c: 2 * _chip(x, y) + c) for fx, fy in _OTHER_CHIPS]
    return shard.reshape(2, r // 2, w), 8, first


def _gather_finish(name, shard, g):
    r, w = shard.shape
    second = [((0, 0, 1),
               (lambda x, y, c, fx=fx, fy=fy: 2 * _chip(x ^ fx, y ^ fy) + c),
               (lambda x, y, c, fx=fx, fy=fy: 2 * _chip(x ^ fx, y ^ fy) + c)) for fx, fy in _OTHER_CHIPS]
    g = _exchange(name + "_d2d", g, 8, [], second, alias=True).reshape(4, r, w)
    mine = lax.broadcasted_iota(jnp.int32, (4, 1, 1), 0) == _chip(lax.axis_index("x"), lax.axis_index("y"))
    return jnp.where(mine, shard[None], g)


def _row_tile(rows, cap=512):
    return max(t for t in range(16, cap + 1, 16) if rows % t == 0)


def _slot_sum(name, terms, sel, tm, out_slots=None, out_slot=None, also_bf16=False):
    rows, w = terms[0][0].shape[1:]

    def imap(slot):
        if isinstance(slot, int):
            return lambda i, s: (slot, i, 0)
        return lambda i, s: (s[slot[1]], i, 0)

    in_specs = [pl.BlockSpec((None, tm, w), imap(sl)) for _, sl in terms]
    if out_slots is None:
        out_specs, out_shape = [pl.BlockSpec((tm, w), lambda i, s: (i, 0))], [jax.ShapeDtypeStruct((rows, w), F32)]
    else:
        out_specs = [pl.BlockSpec((None, tm, w), imap(out_slot))]
        out_shape = [jax.ShapeDtypeStruct((out_slots, rows, w), F32)]
    if also_bf16:
        out_specs.append(pl.BlockSpec((tm, w), lambda i, s: (i, 0)))
        out_shape.append(jax.ShapeDtypeStruct((rows, w), BF16))
    n_in = len(terms)

    def body(sel_ref, *refs):
        acc = refs[0][...].astype(F32)
        for r in refs[1:n_in]:
            acc = acc + r[...].astype(F32)
        refs[n_in][...] = acc
        if also_bf16:
            refs[n_in + 1][...] = acc.astype(BF16)

    grid_spec = pltpu.PrefetchScalarGridSpec(num_scalar_prefetch=1, grid=(rows // tm,), in_specs=in_specs, out_specs=out_specs)
    out = pl.pallas_call(body, grid_spec=grid_spec, out_shape=out_shape, name=name,
                         compiler_params=_compiler_params())(sel, *[a for a, _ in terms])
    return out if also_bf16 else out[0]


def _reduce_scatter(name, parts, sel):
    f, (src, n_out, sends) = _rs_pair_stage(parts)
    p, (src2, n_out2, sends2) = _rs_chip_stage(name, f, _exchange(name + "_d2d", src, n_out, [], sends), sel)
    return _rs_finish(name, p, _exchange(name + "_ici", src2, n_out2, [], sends2), sel)


def _rs_pair_stage(parts):
    _, _, r, w = parts.shape
    f = parts.reshape(2, 4 * r, w)
    return f, (f, 1, [((0, 0, 1), lambda x, y, c: 1 - c, lambda x, y, c: 0)])


def _rs_chip_stage(name, f, got, sel):
    r, w = f.shape[1] // 4, f.shape[2]
    p, p16 = _slot_sum(name + "_add2", [(f, ("sel", 0)), (got, 0)], sel, _row_tile(4 * r), also_bf16=True)
    sends = [((fx, fy, 0), (lambda x, y, c, fx=fx, fy=fy: _chip(x ^ fx, y ^ fy)), (lambda x, y, c, k=k: k))
             for k, (fx, fy) in enumerate(_OTHER_CHIPS)]
    return p.reshape(4, r, w), (p16.reshape(4, r, w), 3, sends)


def _rs_finish(name, p, got, sel):
    _, r, w = p.shape
    q = _slot_sum(name + "_add4", [(p, ("sel", 1)), (got, 0), (got, 1), (got, 2)], sel, _row_tile(r), out_slots=2,
                  out_slot=("sel", 0))
    q = _exchange(name + "_pair", q, 2, [], [((0, 0, 1), lambda x, y, c: c, lambda x, y, c: c)], alias=True)
    return q.reshape(2 * r, w)


def _all_gather_all(name, vec, n_slots, flips, slot_fn):
    sends = [(f, lambda x, y, c: 0, slot_fn) for f in flips]
    g = _exchange(name, vec[None], n_slots, [], sends)
    mine = lax.broadcasted_iota(jnp.int32, (n_slots, 1, 1), 0) == slot_fn(*(lax.axis_index(a) for a in MESH_AXES))
    return jnp.where(mine, vec[None], g)


def _nbr_specs(arr, width, cb, nb):
    return [
        (arr, (BLOCK, width), functools.partial(lambda n, cb: (jnp.maximum(n - 1, 0), cb), cb=cb)),
        (arr, (BLOCK, width), functools.partial(lambda n, cb: (n, cb), cb=cb)),
        (arr, (BLOCK, width), functools.partial(lambda n, cb: (jnp.minimum(n + 1, nb - 1), cb), cb=cb)),
    ]


def _head(h):
    return slice(h * HEAD_DIM, (h + 1) * HEAD_DIM)


def _row_group(n_groups, rows_per_group):
    r = lax.broadcasted_iota(jnp.int32, (n_groups * rows_per_group, 1), 0)
    grp = jnp.zeros_like(r)
    for g in range(1, n_groups):
        grp = grp + jnp.where(r >= g * rows_per_group, 1, 0)
    return grp


def _by_group(grp, vals):
    out = vals[-1]
    for g in range(len(vals) - 2, -1, -1):
        out = jnp.where(grp == g, vals[g], out)
    return out


def _attn_fwd(proj, sink):
    lp = proj.shape[0]
    nb = lp // BLOCK
    kv_w = N_KV_HEADS * HEAD_DIM
    specs = _nbr_specs(proj, kv_w, 4, nb) + _nbr_specs(proj, kv_w, 5, nb)
    specs += [(proj, (BLOCK, kv_w), lambda n: (0, 4)), (proj, (BLOCK, kv_w), lambda n: (0, 5))]
    in_specs = [pl.BlockSpec(memory_space=pltpu.SMEM), pl.BlockSpec((BLOCK, D_MODEL), lambda n: (n, 0))]
    in_specs += [pl.BlockSpec(s[1], s[2]) for s in specs]

    def body(sink_ref, q_ref, kp, kc, kn, vp, vc, vn, km, vm, o_ref, lse_ref):
        n = pl.program_id(0)
        qi = lax.broadcasted_iota(jnp.int32, (BLOCK, 3 * BLOCK), 0)
        sj = lax.broadcasted_iota(jnp.int32, (BLOCK, 3 * BLOCK), 1)
        dist = jnp.abs(qi + BLOCK - sj)
        kpos = (n - 1) * BLOCK + sj
        valid = (dist <= WINDOW) & (kpos >= BLOCK) & (kpos < lp)
        distf = dist.astype(F32)
        kb = jnp.concatenate([kp[...], kc[...], kn[...]], axis=0).astype(BF16)
        vb = jnp.concatenate([vp[...], vc[...], vn[...]], axis=0).astype(BF16)
        kmeta = km[PAD:BLOCK, :].astype(BF16)
        vmeta = vm[PAD:BLOCK, :].astype(BF16)
        valid4 = jnp.concatenate([valid] * Q_GROUP, axis=0)
        distf4 = jnp.concatenate([distf] * Q_GROUP, axis=0)
        grp = _row_group(Q_GROUP, BLOCK)
        for kh in range(N_KV_HEADS):
            ksl = slice(kh * HEAD_DIM, (kh + 1) * HEAD_DIM)
            heads = [kh * Q_GROUP + g for g in range(Q_GROUP)]
            slope = _by_group(grp, [SLOPES[h] for h in heads])
            sk = _by_group(grp, [sink_ref[h] for h in heads])
            q4 = (jnp.concatenate([q_ref[:, _head(h)] for h in heads], axis=0) * ATTN_SCALE).astype(BF16)
            s = jnp.where(valid4, _dot_nt(q4, kb[:, ksl]) - slope * distf4, NEG)
            sm = _dot_nt(q4, kmeta[:, ksl])
            m = jnp.maximum(jnp.maximum(jnp.max(s, axis=1, keepdims=True), jnp.max(sm, axis=1, keepdims=True)), sk)
            e = jnp.exp(s - m)
            em = jnp.exp(sm - m)
            den = jnp.sum(e, axis=1, keepdims=True) + jnp.sum(em, axis=1, keepdims=True) + jnp.exp(sk - m)
            o4 = (_dot(e, vb[:, ksl]) + _dot(em, vmeta[:, ksl])) * (1.0 / den)
            lse4 = m + jnp.log(den)
            for g, h in enumerate(heads):
                o_ref[:, _head(h)] = o4[g * BLOCK:(g + 1) * BLOCK].astype(o_ref.dtype)
                lse_ref[:, h:h + 1] = lse4[g * BLOCK:(g + 1) * BLOCK]

    return pl.pallas_call(
        body, grid=(nb,), in_specs=in_specs,
        out_specs=[pl.BlockSpec((BLOCK, D_MODEL), lambda n: (n, 0)), pl.BlockSpec((BLOCK, N_HEADS), lambda n: (n, 0))],
        out_shape=[jax.ShapeDtypeStruct((lp, D_MODEL), BF16), jax.ShapeDtypeStruct((lp, N_HEADS), F32)],
        name="attn_fwd", compiler_params=_compiler_params())(sink, proj, *[s[0] for s in specs])


def _attn_delta(do, o):
    lp = do.shape[0]
    sel = (lax.broadcasted_iota(jnp.int32, (N_HEADS, D_MODEL), 1) // HEAD_DIM
           == lax.broadcasted_iota(jnp.int32, (N_HEADS, D_MODEL), 0)).astype(BF16)

    def body(do_ref, o_ref, sel_ref, d_ref, dt_ref):
        prod = do_ref[...] * o_ref[...].astype(F32)
        hi = prod.astype(BF16)
        lo = (prod - hi.astype(F32)).astype(BF16)
        d_ref[...] = _dot_nt(hi, sel_ref[...]) + _dot_nt(lo, sel_ref[...])
        dt_ref[...] = _dot_nt(sel_ref[...], hi) + _dot_nt(sel_ref[...], lo)

    return pl.pallas_call(
        body, grid=(lp // BLOCK,),
        in_specs=[pl.BlockSpec((BLOCK, D_MODEL), lambda n: (n, 0)), pl.BlockSpec((BLOCK, D_MODEL), lambda n: (n, 0)),
                  pl.BlockSpec((N_HEADS, D_MODEL), lambda n: (0, 0))],
        out_specs=[pl.BlockSpec((BLOCK, N_HEADS), lambda n: (n, 0)), pl.BlockSpec((N_HEADS, BLOCK), lambda n: (0, n))],
        out_shape=[jax.ShapeDtypeStruct((lp, N_HEADS), F32), jax.ShapeDtypeStruct((N_HEADS, lp), F32)],
        name="attn_delta", compiler_params=_compiler_params())(do, o, sel)


def _attn_bwd(proj, sink, o, lse, do):
    lp = proj.shape[0]
    nb = lp // BLOCK
    kv_w = N_KV_HEADS * HEAD_DIM
    delta, delta_t = _attn_delta(do, o)
    lse_t = lse.T
    row_nbrs = lambda arr: [
        (arr, (N_HEADS, BLOCK), lambda n: (0, jnp.maximum(n - 1, 0))), (arr, (N_HEADS, BLOCK), lambda n: (0, n)),
        (arr, (N_HEADS, BLOCK), lambda n: (0, jnp.minimum(n + 1, nb - 1)))]
    specs = (_nbr_specs(proj, D_MODEL, 0, nb) + _nbr_specs(proj, kv_w, 4, nb) + _nbr_specs(proj, kv_w, 5, nb)
             + [(proj, (BLOCK, kv_w), lambda n: (0, 4)), (proj, (BLOCK, kv_w), lambda n: (0, 5))]
             + _nbr_specs(do, D_MODEL, 0, nb) + [(lse, (BLOCK, N_HEADS), lambda n: (n, 0)), (delta, (BLOCK, N_HEADS), lambda n: (n, 0))]
             + row_nbrs(lse_t) + row_nbrs(delta_t))
    in_specs = [pl.BlockSpec(memory_space=pltpu.SMEM)] + [pl.BlockSpec(s[1], s[2]) for s in specs]

    def body(sink_ref, qp, qc, qn, kp, kc, kn, vp, vc, vn, km, vm, dop, doc, don, lc, dc, ltp, ltc, ltn, dtp, dtc, dtn,
             dq_ref, dk_ref, dv_ref, dkm_ref, dvm_ref, dsk_ref):
        n = pl.program_id(0)

        @pl.when(n == 0)
        def _():
            dkm_ref[...] = jnp.zeros_like(dkm_ref)
            dvm_ref[...] = jnp.zeros_like(dvm_ref)
            dsk_ref[...] = jnp.zeros_like(dsk_ref)

        qi = lax.broadcasted_iota(jnp.int32, (BLOCK, 3 * BLOCK), 0)
        sj = lax.broadcasted_iota(jnp.int32, (BLOCK, 3 * BLOCK), 1)
        dist_q = jnp.abs(qi + BLOCK - sj)
        kpos = (n - 1) * BLOCK + sj
        valid_q = (dist_q <= WINDOW) & (kpos >= BLOCK) & (kpos < lp)
        distf_q = dist_q.astype(F32)
        bi = lax.broadcasted_iota(jnp.int32, (BLOCK, 3 * BLOCK), 1)
        kj = lax.broadcasted_iota(jnp.int32, (BLOCK, 3 * BLOCK), 0)
        dist_k = jnp.abs(bi - BLOCK - kj)
        qpos = (n - 1) * BLOCK + bi
        valid_k = (dist_k <= WINDOW) & (qpos >= 0) & (qpos < lp) & (n >= 1)
        distf_k = dist_k.astype(F32)

        kb = jnp.concatenate([kp[...], kc[...], kn[...]], axis=0).astype(BF16)
        vb = jnp.concatenate([vp[...], vc[...], vn[...]], axis=0).astype(BF16)
        kcur = kc[...].astype(BF16)
        vcur = vc[...].astype(BF16)
        kmeta = km[PAD:BLOCK, :].astype(BF16)
        vmeta = vm[PAD:BLOCK, :].astype(BF16)
        lane = lax.broadcasted_iota(jnp.int32, (1, BLOCK), 1)
        dsink = jnp.zeros((1, BLOCK), F32)
        valid_q4 = jnp.concatenate([valid_q] * Q_GROUP, axis=0)
        distf_q4 = jnp.concatenate([distf_q] * Q_GROUP, axis=0)
        valid_k4 = jnp.concatenate([valid_k] * Q_GROUP, axis=1)
        distf_k4 = jnp.concatenate([distf_k] * Q_GROUP, axis=1)
        grp_q = _row_group(Q_GROUP, BLOCK)
        lane_k = lax.broadcasted_iota(jnp.int32, (1, Q_GROUP * 3 * BLOCK), 1)
        grp_k = sum(jnp.where(lane_k >= g * 3 * BLOCK, 1, 0) for g in range(1, Q_GROUP))
        for kh in range(N_KV_HEADS):
            ksl = slice(kh * HEAD_DIM, (kh + 1) * HEAD_DIM)
            heads = [kh * Q_GROUP + g for g in range(Q_GROUP)]
            slopes = [SLOPES[h] for h in heads]
            q4 = (jnp.concatenate([qc[:, _head(h)] for h in heads], axis=0) * ATTN_SCALE).astype(BF16)
            do4 = jnp.concatenate([doc[:, _head(h)] for h in heads], axis=0)
            delta = jnp.concatenate([dc[:, h:h + 1] for h in heads], axis=0)
            lse4 = jnp.concatenate([lc[:, h:h + 1] for h in heads], axis=0)
            s = _dot_nt(q4, kb[:, ksl]) - _by_group(grp_q, slopes) * distf_q4
            p = jnp.exp(jnp.where(valid_q4, s, NEG) - lse4)
            pm = jnp.exp(_dot_nt(q4, kmeta[:, ksl]) - lse4)
            ps = jnp.exp(_by_group(grp_q, [sink_ref[h] for h in heads]) - lse4)
            do4b = do4.astype(BF16)
            ds = p * (_dot_nt(do4b, vb[:, ksl]) - delta)
            dsm = pm * (_dot_nt(do4b, vmeta[:, ksl]) - delta)
            dq4 = ATTN_SCALE * (_dot(ds, kb[:, ksl]) + _dot(dsm, kmeta[:, ksl]))
            dsk4 = ps * delta
            for g, h in enumerate(heads):
                dq_ref[:, _head(h)] = dq4[g * BLOCK:(g + 1) * BLOCK].astype(dq_ref.dtype)
                dsink = dsink + jnp.where(lane == h, -jnp.sum(dsk4[g * BLOCK:(g + 1) * BLOCK]), 0.0)
            dkm_ref[:, ksl] += _dot_tn(dsm, q4)
            dvm_ref[:, ksl] += _dot_tn(pm, do4b)
            band = lambda a, b, c_: jnp.concatenate([r[:, _head(h)] for h in heads for r in (a, b, c_)], axis=0)
            qb4 = (band(qp, qc, qn) * ATTN_SCALE).astype(BF16)
            dob4b = band(dop, doc, don).astype(BF16)
            delta_b = jnp.concatenate([r[h:h + 1, :] for h in heads for r in (dtp, dtc, dtn)], axis=1)
            lse_b = jnp.concatenate([r[h:h + 1, :] for h in heads for r in (ltp, ltc, ltn)], axis=1)
            st = _dot_nt(kcur[:, ksl], qb4) - _by_group(grp_k, slopes) * distf_k4
            pt = jnp.exp(jnp.where(valid_k4, st, NEG) - lse_b)
            dv_ref[:, ksl] = _dot(pt, dob4b)
            dst = pt * (_dot_nt(vcur[:, ksl], dob4b) - delta_b)
            dk_ref[:, ksl] = _dot(dst, qb4)
        dsk_ref[...] += dsink

    blk = lambda w: pl.BlockSpec((BLOCK, w), lambda n: (n, 0))
    fix = lambda shp: pl.BlockSpec(shp, lambda n: (0, 0))
    return pl.pallas_call(
        body, grid=(nb,), in_specs=in_specs,
        out_specs=[blk(D_MODEL), blk(kv_w), blk(kv_w), fix((N_META, kv_w)), fix((N_META, kv_w)), fix((1, BLOCK))],
        out_shape=[jax.ShapeDtypeStruct((lp, D_MODEL), BF16), jax.ShapeDtypeStruct((lp, kv_w), F32),
                   jax.ShapeDtypeStruct((lp, kv_w), F32), jax.ShapeDtypeStruct((N_META, kv_w), F32),
                   jax.ShapeDtypeStruct((N_META, kv_w), F32), jax.ShapeDtypeStruct((1, BLOCK), F32)],
        name="attn_bwd", compiler_params=_compiler_params())(sink, *[s[0] for s in specs])


N_SEG = 8


def _ssm_tile(lp, long_tile):
    return 688 if lp % 688 == 0 else 384


def _to_segments(a):
    lp, w = a.shape
    return a.reshape(N_SEG, lp // N_SEG, w).transpose(1, 0, 2).reshape(lp, w)


def _from_segments(a):
    lp, w = a.shape
    return a.reshape(lp // N_SEG, N_SEG, w).transpose(1, 0, 2).reshape(lp, w)


def _complex_power(ar, ai, n):
    rr, ri = jnp.ones_like(ar), jnp.zeros_like(ai)
    while n:
        if n & 1:
            rr, ri = rr * ar - ri * ai, rr * ai + ri * ar
        ar, ai = ar * ar - ai * ai, 2.0 * ar * ai
        n >>= 1
    return rr, ri


def _segment_starts(finals, a_seg, reverse):
    fr, fi = finals[:, :N_STATE], finals[:, N_STATE:]
    ar, ai = a_seg[:, :N_STATE], a_seg[:, N_STATE:]
    row = lax.broadcasted_iota(jnp.int32, (N_SEG, N_STATE), 0)
    pr = jnp.zeros((1, N_STATE), F32)
    pi = jnp.zeros((1, N_STATE), F32)
    sr = jnp.zeros((N_SEG, N_STATE), F32)
    si = jnp.zeros((N_SEG, N_STATE), F32)
    for s in (range(N_SEG - 1, -1, -1) if reverse else range(N_SEG)):
        sr = jnp.where(row == s, pr, sr)
        si = jnp.where(row == s, pi, si)
        pr, pi = fr[s:s + 1] + ar * pr - ai * pi, fi[s:s + 1] + ar * pi + ai * pr
    return jnp.concatenate([sr, si], axis=1)


def _recurrence(buf_ref, st_ref, a_ref, reverse):
    steps = buf_ref.shape[0] // N_SEG
    half = N_STATE // 2
    for c0 in (0, half):
        re = slice(c0, c0 + half)
        im = slice(N_STATE + c0, N_STATE + c0 + half)
        ar = jnp.broadcast_to(a_ref[:, re], (N_SEG, half))
        ai = jnp.broadcast_to(a_ref[:, im], (N_SEG, half))

        def step(k, carry, re=re, im=im, ar=ar, ai=ai):
            xr, xi = carry
            r0 = pl.multiple_of((steps - 1 - k if reverse else k) * N_SEG, N_SEG)
            nr = ar * xr - ai * xi + buf_ref[pl.ds(r0, N_SEG), re]
            ni = ar * xi + ai * xr + buf_ref[pl.ds(r0, N_SEG), im]
            buf_ref[pl.ds(r0, N_SEG), re] = nr
            buf_ref[pl.ds(r0, N_SEG), im] = ni
            return nr, ni

        xr, xi = lax.fori_loop(0, steps, step, (st_ref[:, re], st_ref[:, im]), unroll=2)
        st_ref[:, re] = xr
        st_ref[:, im] = xi


def _copy_in(pairs, sem):
    cps = [pltpu.make_async_copy(src, dst, sem.at[j]) for j, (src, dst) in enumerate(pairs)]
    for cp in cps:
        cp.start()
    for cp in cps:
        cp.wait()


def _ssm_fwd_dir(name, u_seg, wb, wc, a, a_seg, reverse):
    lp = u_seg.shape[0]
    tile_rows = _ssm_tile(lp, True)
    nt = lp // tile_rows
    tile = (lambda i: nt - 1 - i) if reverse else (lambda i: i)
    first = tile(0)
    held = lambda p, i: (p * tile(i) + (1 - p) * first, 0)

    def body(u_ref, wb_hbm, wc_hbm, a_ref, aseg_ref, x_ref, y_ref, wb_ref, wc_ref, buf_ref, st_ref, sem):
        p, i = pl.program_id(0), pl.program_id(1)

        @pl.when((p == 0) & (i == 0))
        def _():
            _copy_in([(wb_hbm, wb_ref), (wc_hbm, wc_ref)], sem)
            st_ref[...] = jnp.zeros_like(st_ref)

        @pl.when((p == 1) & (i == 0))
        def _():
            st_ref[...] = _segment_starts(st_ref[...], aseg_ref[...], reverse)

        def states_into(dst_ref):
            for j in range(SUPER):
                part = _dot(u_ref[:, 128 * j:128 * (j + 1)], wb_ref[j])
                dst_ref[:, 512 * j:512 * (j + 1)] = part[:, :512]
                dst_ref[:, N_STATE + 512 * j:N_STATE + 512 * (j + 1)] = part[:, 512:]
            _recurrence(dst_ref, st_ref, a_ref, reverse)

        @pl.when(p == 0)
        def _():
            states_into(buf_ref)

        @pl.when(p == 1)
        def _():
            states_into(x_ref)
            for j in range(SUPER):
                y_ref[:, 128 * j:128 * (j + 1)] = (_dot(x_ref[:, _re(j)], wc_ref[j, :512, :])
                                                   + _dot(x_ref[:, _im(j)], wc_ref[j, 512:, :]))

    fix = lambda shp: pl.BlockSpec(shp, lambda p, i: (0, 0))
    return pl.pallas_call(
        body, grid=(2, nt),
        in_specs=[pl.BlockSpec((tile_rows, SSM_WIDTH), lambda p, i: (tile(i), 0)), pl.BlockSpec(memory_space=pl.ANY),
                  pl.BlockSpec(memory_space=pl.ANY), fix((1, 2 * N_STATE)), fix((1, 2 * N_STATE))],
        out_specs=[pl.BlockSpec((tile_rows, 2 * N_STATE), held), pl.BlockSpec((tile_rows, SSM_WIDTH), held)],
        out_shape=[jax.ShapeDtypeStruct((lp, 2 * N_STATE), F32), jax.ShapeDtypeStruct((lp, SSM_WIDTH), F32)],
        scratch_shapes=[pltpu.VMEM(wb.shape, BF16), pltpu.VMEM(wc.shape, BF16), pltpu.VMEM((tile_rows, 2 * N_STATE), F32),
                        pltpu.VMEM((N_SEG, 2 * N_STATE), F32), pltpu.SemaphoreType.DMA((2,))],
        name=name, compiler_params=pltpu.CompilerParams(dimension_semantics=("arbitrary", "arbitrary"),
                                                        vmem_limit_bytes=V7X_VMEM_LIMIT_BYTES))(
        u_seg, wb.astype(BF16), wc.astype(BF16), a, a_seg)


def _ssm_bwd_dir(name, dys_seg, u_seg, x_seg, wb, wc, a_conj, a_seg_conj, fwd_reverse):
    lp = u_seg.shape[0]
    tile_rows = _ssm_tile(lp, False)
    nt = lp // tile_rows
    steps = tile_rows // N_SEG
    reverse = not fwd_reverse
    tile = (lambda i: nt - 1 - i) if reverse else (lambda i: i)
    first = tile(0)
    held = lambda p, i: (p * tile(i) + (1 - p) * first, 0)
    n_slab = lp // N_SEG
    if fwd_reverse:
        halo = lambda p, i: (p * jnp.minimum((tile(i) + 1) * steps, n_slab - 1), 0)
        edge = lambda p, i: (0, 0)
    else:
        halo = lambda p, i: (p * jnp.maximum(tile(i) * steps - 1, 0), 0)
        edge = lambda p, i: (n_slab - 1, 0)

    def body(dy_ref, u_ref, x_ref, halo_ref, edge_ref, wb_hbm, wc_hbm, a_ref, aseg_ref, du_ref, dwb_ref, dwc_ref, ga_ref,
             wb_ref, wc_ref, buf_ref, st_ref, sem):
        p, i = pl.program_id(0), pl.program_id(1)

        @pl.when((p == 0) & (i == 0))
        def _():
            _copy_in([(wb_hbm, wb_ref), (wc_hbm, wc_ref)], sem)
            st_ref[...] = jnp.zeros_like(st_ref)
            dwb_ref[...] = jnp.zeros_like(dwb_ref)
            dwc_ref[...] = jnp.zeros_like(dwc_ref)
            ga_ref[...] = jnp.zeros_like(ga_ref)

        @pl.when((p == 1) & (i == 0))
        def _():
            st_ref[...] = _segment_starts(st_ref[...], aseg_ref[...], reverse)

        for j in range(SUPER):
            part = _dot_nt(dy_ref[:, 128 * j:128 * (j + 1)], wc_ref[j])
            buf_ref[:, 512 * j:512 * (j + 1)] = part[:, :512]
            buf_ref[:, N_STATE + 512 * j:N_STATE + 512 * (j + 1)] = part[:, 512:]
        _recurrence(buf_ref, st_ref, a_ref, reverse)

        @pl.when(p == 1)
        def _():
            for j in range(SUPER):
                ch = slice(128 * j, 128 * (j + 1))
                du_ref[:, ch] = (_dot_nt(buf_ref[:, _re(j)], wb_ref[j, :, :512]) + _dot_nt(buf_ref[:, _im(j)], wb_ref[j, :, 512:]))
                dwb_ref[ch, :512] += _dot_tn(u_ref[:, ch], buf_ref[:, _re(j)])
                dwb_ref[ch, 512:] += _dot_tn(u_ref[:, ch], buf_ref[:, _im(j)])
                dwc_ref[ch, :512] += _dot_tn(dy_ref[:, ch], x_ref[:, _re(j)])
                dwc_ref[ch, 512:] += _dot_tn(dy_ref[:, ch], x_ref[:, _im(j)])
            row = lax.broadcasted_iota(jnp.int32, (N_SEG, 2 * N_STATE), 0)
            if fwd_reverse:
                wrap = jnp.where(row == N_SEG - 1, 0.0, pltpu.roll(edge_ref[...], N_SEG - 1, axis=0))
                open_slab = jnp.where(tile(i) == nt - 1, wrap, halo_ref[...])
                before = lambda cols: jnp.concatenate([x_ref[N_SEG:, cols], open_slab[:, cols]], axis=0)
            else:
                wrap = jnp.where(row == 0, 0.0, pltpu.roll(edge_ref[...], 1, axis=0))
                open_slab = jnp.where(tile(i) == 0, wrap, halo_ref[...])
                before = lambda cols: jnp.concatenate([open_slab[:, cols], x_ref[:tile_rows - N_SEG, cols]], axis=0)
            half = N_STATE // 2
            for c0 in (0, half):
                re = slice(c0, c0 + half)
                im = slice(N_STATE + c0, N_STATE + c0 + half)
                gr, gi = buf_ref[:, re], buf_ref[:, im]
                br, bi = before(re), before(im)
                fold = lambda v: jnp.sum(v.reshape(steps, N_SEG, half), axis=0)
                ga_ref[:, re] += fold(gr * br + gi * bi)
                ga_ref[:, im] += fold(gi * br - gr * bi)

    fix = lambda shp: pl.BlockSpec(shp, lambda p, i: (0, 0))
    row_tile = lambda w: pl.BlockSpec((tile_rows, w), lambda p, i: (tile(i), 0))
    return pl.pallas_call(
        body, grid=(2, nt),
        in_specs=[row_tile(SSM_WIDTH), row_tile(SSM_WIDTH), pl.BlockSpec((tile_rows, 2 * N_STATE), held),
                  pl.BlockSpec((N_SEG, 2 * N_STATE), halo), pl.BlockSpec((N_SEG, 2 * N_STATE), edge),
                  pl.BlockSpec(memory_space=pl.ANY), pl.BlockSpec(memory_space=pl.ANY), fix((1, 2 * N_STATE)), fix((1, 2 * N_STATE))],
        out_specs=[pl.BlockSpec((tile_rows, SSM_WIDTH), held), fix((SSM_WIDTH, 1024)), fix((SSM_WIDTH, 1024)),
                   fix((N_SEG, 2 * N_STATE))],
        out_shape=[jax.ShapeDtypeStruct((lp, SSM_WIDTH), F32), jax.ShapeDtypeStruct((SSM_WIDTH, 1024), F32),
                   jax.ShapeDtypeStruct((SSM_WIDTH, 1024), F32), jax.ShapeDtypeStruct((N_SEG, 2 * N_STATE), F32)],
        scratch_shapes=[pltpu.VMEM(wb.shape, BF16), pltpu.VMEM(wc.shape, BF16), pltpu.VMEM((tile_rows, 2 * N_STATE), F32),
                        pltpu.VMEM((N_SEG, 2 * N_STATE), F32), pltpu.SemaphoreType.DMA((2,))],
        name=name, compiler_params=pltpu.CompilerParams(dimension_semantics=("arbitrary", "arbitrary"),
                                                        vmem_limit_bytes=V7X_VMEM_LIMIT_BYTES))(
        dys_seg, u_seg, x_seg, x_seg, x_seg, wb.astype(BF16), wc.astype(BF16), a_conj, a_seg_conj)


def _ssm_prep(lam_re, lam_im, log_dt, b_re, b_im, c_re, c_im):
    dt = jnp.exp(log_dt)[:, None]
    er = jnp.exp(lam_re * dt)
    ar, ai = er * jnp.cos(lam_im * dt), er * jnp.sin(lam_im * dt)
    nr, ni = ar - 1.0, ai
    den = lam_re * lam_re + lam_im * lam_im
    cr, ci = (nr * lam_re + ni * lam_im) / den, (ni * lam_re - nr * lam_im) / den
    bbr = cr[:, :, None] * b_re - ci[:, :, None] * b_im
    bbi = cr[:, :, None] * b_im + ci[:, :, None] * b_re
    eye = jnp.eye(8, dtype=F32)

    def in_map(b):
        b = b.reshape(SUPER, 8, SSM_STATE, SSM_GROUP_CH).transpose(0, 1, 3, 2)
        return (b[:, :, :, None, :] * eye[None, :, None, :, None]).reshape(SUPER, 128, 512)

    def out_map(cm):
        cm = cm.reshape(SUPER, 8, SSM_GROUP_CH, SSM_STATE).transpose(0, 1, 3, 2)
        return (cm[:, :, :, None, :] * eye[None, :, None, :, None]).reshape(SUPER, 512, 128)

    wb = jnp.concatenate([in_map(bbr), in_map(bbi)], axis=2)
    wc = jnp.concatenate([out_map(c_re), -out_map(c_im)], axis=1)
    return ar.reshape(1, N_STATE), ai.reshape(1, N_STATE), wb, wc


def _re(j):
    return slice(512 * j, 512 * (j + 1))


def _im(j):
    return slice(N_STATE + 512 * j, N_STATE + 512 * (j + 1))


def _row_ids(i, tm, width):
    return i * tm + lax.broadcasted_iota(jnp.int32, (tm, width), 0)


_FF_CHUNKS = (slice(0, D_FF // 2), slice(D_FF // 2, D_FF))


def _ffn_fwd(tag, h, gain, wg, wu, wd, next_shard=None):
    def up(i, h_ref, wg_ref, wu_ref, g_ref, n_ref, b_ref, silu_ref, dsilu_ref, act_ref):
        n = _rms_fwd(h_ref[...], g_ref[...]).astype(BF16)
        n_ref[...] = n
        for cols in (slice(0, D_FF),):
            a = _dot(n, wg_ref[:, cols])
            b = _dot(n, wu_ref[:, cols])
            sg = _sigmoid(a)
            silu = a * sg
            b_ref[:, cols] = b.astype(BF16)
            silu_ref[:, cols] = silu.astype(BF16)
            dsilu_ref[:, cols] = (sg * (1.0 + a * (1.0 - sg))).astype(BF16)
            act_ref[:, cols] = (silu * b).astype(BF16)

    outs = [(D_MODEL, BF16)] + [(D_FF, BF16)] * 4
    gathered = None
    if next_shard is None:
        n, b, silu, dsilu, act = _rowk(tag + "_up", up, [_rows(h)], [wg, wu, gain], outs, tm=192, writes_outs=True)
    else:
        n, b, silu, dsilu, act, got = _rowk(tag + "_up_gather", up, [_rows(h)], [wg, wu, gain], outs, tm=192,
                                            writes_outs=True, hosted=_gather_ici_stage(next_shard))
        gathered = _gather_finish("gather_w", next_shard, got)
    out = _rowk(tag + "_down", lambda i, act_ref, h_ref, wd_ref: h_ref[...] + 0.5 * _dot(act_ref[...], wd_ref[...]),
                [_rows(act), _rows(h)], [wd], [(D_MODEL, F32)])[0]
    return out, (h, n, b, silu, dsilu, act), gathered


def _ffn_bwd(tag, dh, saved, gain, wg, wu, wd, pending=None, sel=None):
    h, n, b, silu, dsilu, act = saved
    hosted1 = hosted2 = reduced = None
    if pending is not None:
        f, hosted1 = _rs_pair_stage(pending)
        tag = tag + "_reduce"

    def bwd1(i, dh_ref, b_ref, silu_ref, dsilu_ref, wd_ref, da_ref, db_ref):
        dhb = (0.5 * dh_ref[...]).astype(BF16)
        for cols in _FF_CHUNKS:
            dact = _dot_nt(dhb, wd_ref[cols, :])
            da_ref[:, cols] = (dact * b_ref[:, cols].astype(F32) * dsilu_ref[:, cols].astype(F32)).astype(BF16)
            db_ref[:, cols] = (dact * silu_ref[:, cols].astype(F32)).astype(BF16)

    res = _rowk(tag + "_bwd_act", bwd1, [_rows(dh), _rows(b), _rows(silu), _rows(dsilu)], [wd], [(D_FF, BF16)] * 2,
                writes_outs=True, hosted=hosted1)
    da, db = res[0], res[1]
    if pending is not None:
        p, hosted2 = _rs_chip_stage("reduce_w", f, res[2], sel)

    def bwd2(i, da_ref, db_ref, h_ref, dh_ref, wg_ref, wu_ref, g_ref):
        dn = _dot_nt(da_ref[...], wg_ref[...]) + _dot_nt(db_ref[...], wu_ref[...])
        dx, dg = _rms_bwd(h_ref[...], g_ref[...], dn)
        return dh_ref[...] + dx, dg

    res = _rowk(tag + "_bwd_in", bwd2, [_rows(da), _rows(db), _rows(h), _rows(dh)], [wg, wu, gain],
                [(D_MODEL, F32)], accs=[(1, D_MODEL)], hosted=hosted2)
    dh_in, dgain = res[0], res[1]
    if pending is not None:
        reduced = _rs_finish("reduce_w", p, res[2], sel)
    dwd = _mm_tn("ffn_dwd", act, dh, tk=D_FF // 2, tn=D_MODEL, scale=0.5)
    dwg = _mm_tn("ffn_dwg", n, da, tk=D_MODEL, tn=D_FF // 2)
    dwu = _mm_tn("ffn_dwu", n, db, tk=D_MODEL, tn=D_FF // 2)
    return dh_in, dgain, dwg, dwu, dwd, reduced


def _mixer_fwd(h, lw, ssm):
    lp = h.shape[0]
    def mix_in(i, h_ref, w_ref, g_ref):
        nv = _rms_fwd(h_ref[...], g_ref[...]).astype(BF16)
        return nv, _dot(nv, w_ref[...])

    n, proj = _rowk("mix_in", mix_in, [_rows(h)], [lw["w_in"], lw["mix_norm"]], [(D_MODEL, BF16), (4 * D_MODEL, F32)])
    yattn, lse = _attn_fwd(proj, lw["attn_sink"])
    u_seg = _to_segments(proj[:, 3 * SSM_WIDTH:4 * SSM_WIDTH])
    xs, ydir = [], []
    for d in range(2):
        x_seg, y_seg = _ssm_fwd_dir(f"ssm_fwd{d}", u_seg, ssm[d]["wb"], ssm[d]["wc"], ssm[d]["a"], ssm[d]["a_seg"],
                                    reverse=(d == 1))
        xs.append(x_seg)
        ydir.append(y_seg)

    def ssm_out(i, y0_ref, y1_ref, u_ref, d_ref, wglu_ref):
        ys = y0_ref[...] + y1_ref[...] + d_ref[...] * u_ref[...]
        z = _gelu(ys)
        return ys, z * _sigmoid(_dot(z, wglu_ref[...]))

    ys, yssm_seg = _rowk("ssm_out", ssm_out, [_rows(ydir[0]), _rows(ydir[1]), _rows(u_seg)],
                         [lw["ssm_d"], lw["ssm_w_glu"]], [(SSM_WIDTH, F32), (SSM_WIDTH, BF16)])
    yssm = _from_segments(yssm_seg)

    def merge(i, ys_ref, ya_ref, gs_ref, ga_ref, wbs_ref, wba_ref):
        bs = _dot(ys_ref[...], wbs_ref[...])
        ba = _dot(ya_ref[...], wba_ref[...])
        m = _sigmoid(gs_ref[...]) * bs + _sigmoid(ga_ref[...]) * ba
        return bs, ba, jnp.where(_row_ids(i, ROW_TILE, D_MODEL) >= PAD, m, 0.0)

    bs, ba, merged = _rowk("mix_merge", merge, [_rows(yssm), _rows(yattn), _rows(proj, D_MODEL, 2), _rows(proj, D_MODEL, 3)],
                           [lw["w_branch_ssm"], lw["w_branch_attn"]], [(D_MODEL, BF16)] * 3)
    out = _rowk("mix_out", lambda i, m_ref, h_ref, w_ref: h_ref[...] + _dot(m_ref[...], w_ref[...]),
                [_rows(merged), _rows(h)], [lw["w_out"]], [(D_MODEL, F32)])[0]
    return out, (h, n, proj, yattn, lse, u_seg, xs, ys, yssm, bs, ba, merged)


def _mixer_bwd(dh, saved, lw, ssm):
    h, n, proj, yattn, lse, u_seg, xs, ys, yssm, bs, ba, merged = saved

    def bwd1(i, dh_ref, gs_ref, ga_ref, bs_ref, ba_ref, w_ref):
        dm = _dot_nt(dh_ref[...], w_ref[...])
        dm = jnp.where(_row_ids(i, ROW_TILE, D_MODEL) >= PAD, dm, 0.0)
        sgs = _sigmoid(gs_ref[...])
        sga = _sigmoid(ga_ref[...])
        return (dm * sgs, dm * sga, dm * bs_ref[...].astype(F32) * sgs * (1.0 - sgs),
                dm * ba_ref[...].astype(F32) * sga * (1.0 - sga))

    dbs, dba, dgs, dga = _rowk("mix_bwd_merge", bwd1,
                               [_rows(dh), _rows(proj, D_MODEL, 2), _rows(proj, D_MODEL, 3), _rows(bs), _rows(ba)],
                               [lw["w_out"]], [(D_MODEL, BF16)] * 4)
    dw_out = _mm_tn("mix_dw_out", merged, dh, tk=D_MODEL, tn=D_MODEL)
    dw_bs = _mm_tn("mix_dw_bs", yssm, dbs, tk=SSM_WIDTH, tn=D_MODEL)
    dw_ba = _mm_tn("mix_dw_ba", yattn, dba, tk=D_MODEL, tn=D_MODEL)

    def bwd2(i, dbs_ref, dba_ref, wbs_ref, wba_ref):
        return _dot_nt(dba_ref[...], wba_ref[...]), _dot_nt(dbs_ref[...], wbs_ref[...])

    dyattn, dyssm = _rowk("mix_bwd_branches", bwd2, [_rows(dbs), _rows(dba)], [lw["w_branch_ssm"], lw["w_branch_attn"]],
                          [(D_MODEL, F32), (SSM_WIDTH, F32)])

    def bwd3(i, dyssm_ref, ys_ref, u_ref, wglu_ref):
        ysv = ys_ref[...]
        z = _gelu(ysv)
        sg = _sigmoid(_dot(z, wglu_ref[...]))
        dt = dyssm_ref[...] * z * sg * (1.0 - sg)
        dz = dyssm_ref[...] * sg + _dot_nt(dt, wglu_ref[...])
        dys = dz * _gelu_grad(ysv)
        return dys, z, dt, jnp.sum(dys * u_ref[...], axis=0, keepdims=True)

    dys, z, dt, dd = _rowk("mix_bwd_ssm_out", bwd3, [_rows(_to_segments(dyssm)), _rows(ys), _rows(u_seg)], [lw["ssm_w_glu"]],
                           [(SSM_WIDTH, F32), (SSM_WIDTH, BF16), (SSM_WIDTH, BF16)], accs=[(1, SSM_WIDTH)])
    dw_glu = _mm_tn("mix_dw_glu", z, dt, tk=SSM_WIDTH, tn=SSM_WIDTH)

    dus, ssm_cot = [], []
    for d in range(2):
        du_d, dwb, dwc_t, ga = _ssm_bwd_dir(f"ssm_bwd{d}", dys, u_seg, xs[d], ssm[d]["wb"], ssm[d]["wc"], ssm[d]["a_conj"],
                                            ssm[d]["a_seg_conj"], fwd_reverse=(d == 1))
        dus.append(du_d)
        ga = jnp.sum(ga, axis=0, keepdims=True)
        ssm_cot.append((ga[:, :N_STATE], ga[:, N_STATE:], dwb.reshape(SUPER, 128, 1024),
                        dwc_t.reshape(SUPER, 128, 1024).transpose(0, 2, 1)))

    du_seg = _rowk("ssm_bwd_du", lambda i, a_ref, b_ref, dys_ref, d_ref: a_ref[...] + b_ref[...] + d_ref[...] * dys_ref[...],
                   [_rows(dus[0]), _rows(dus[1]), _rows(dys)], [lw["ssm_d"]], [(SSM_WIDTH, BF16)])[0]
    du = _from_segments(du_seg)

    dq, dk, dv, dkm, dvm, dsink = _attn_bwd(proj, lw["attn_sink"], yattn, lse, dyattn)

    def bwd_in(i, dq_ref, dk_ref, dv_ref, du_ref, dgs_ref, dga_ref, h_ref, dh_ref, dkm_ref, dvm_ref, w_ref, g_ref):
        first = jnp.where(i == 0, 1.0, 0.0)
        zeros = lambda r: jnp.zeros((r, N_KV_HEADS * HEAD_DIM), F32)
        place = lambda m: jnp.concatenate([zeros(PAD), m[...] * first, zeros(ROW_TILE - BLOCK)], axis=0)
        dp = jnp.concatenate([dq_ref[...].astype(F32), dk_ref[...] + place(dkm_ref), dv_ref[...] + place(dvm_ref),
                              du_ref[...].astype(F32), dgs_ref[...].astype(F32), dga_ref[...].astype(F32)], axis=1)
        dp = jnp.where(_row_ids(i, ROW_TILE, 4 * D_MODEL) >= PAD, dp, 0.0).astype(BF16)
        dx, dg = _rms_bwd(h_ref[...], g_ref[...], _dot_nt(dp, w_ref[...]))
        return dp, dh_ref[...] + dx, dg

    dproj, dh_in, dgain = _rowk("mix_bwd_in", bwd_in,
                                [_rows(dq), _rows(dk), _rows(dv), _rows(du), _rows(dgs), _rows(dga), _rows(h), _rows(dh)],
                                [dkm, dvm, lw["w_in"], lw["mix_norm"]], [(4 * D_MODEL, BF16), (D_MODEL, F32)],
                                accs=[(1, D_MODEL)])
    dw_in = _mm_tn("mix_dw_in", n, dproj, tk=D_MODEL, tn=2 * D_MODEL)
    grads = {"w_out": dw_out, "w_branch_ssm": dw_bs, "w_branch_attn": dw_ba, "ssm_w_glu": dw_glu, "w_in": dw_in,
             "mix_norm": dgain, "ssm_d": dd, "attn_sink": dsink[0, :N_HEADS]}
    return dh_in, grads, ssm_cot


def _loss_head(h, gain, target):
    lp = h.shape[0]

    def fn(i, h_ref, t_ref, g_ref):
        x = h_ref[...]
        y = _rms_fwd(x, g_ref[...])
        live = jnp.where(i == 0, 0.0, 1.0)
        dy = (y - t_ref[...]) * live
        loss = 0.5 * jnp.sum(dy * dy) / D_MODEL
        dx, dg = _rms_bwd(x, g_ref[...], dy * (1.0 / D_MODEL))
        return dx, jnp.full((1, BLOCK), loss, F32), dg

    tgt = (target, (BLOCK, D_MODEL), lambda i: (jnp.maximum(i - 1, 0), 0))
    return _rowk("loss_head", fn, [_rows(h), tgt], [gain], [(D_MODEL, F32)], accs=[(1, BLOCK), (1, D_MODEL)], tm=BLOCK)


def _adamw(name, w, g, m, v, tm):
    def fn(i, w_ref, g_ref, m_ref, v_ref):
        gv = g_ref[...]
        mn = ADAM_B1 * m_ref[...] + (1.0 - ADAM_B1) * gv
        vn = ADAM_B2 * v_ref[...] + (1.0 - ADAM_B2) * (gv * gv)
        m_hat = mn / (1.0 - ADAM_B1 ** ADAM_STEP)
        v_hat = vn / (1.0 - ADAM_B2 ** ADAM_STEP)
        return -ADAM_LR * (m_hat / (jnp.sqrt(v_hat) + ADAM_EPS) + ADAM_WD * w_ref[...]), mn, vn

    wd = w.shape[1]
    return _rowk(name, fn, [_rows(w), _rows(g), _rows(m), _rows(v)], [], [(wd, F32)] * 3, tm=tm)


def _shard_rows(name):
    return {"ffn1_w_gate": 704, "ffn1_w_up": 704, "ffn1_w_down": 704, "ffn2_w_gate": 704, "ffn2_w_up": 704, "ffn2_w_down": 704,
            "w_in": 1024, "ssm_w_glu": 64, "w_branch_ssm": 128, "w_branch_attn": 256, "w_out": 256}[name]


def _full_shape(name):
    return {"ffn1_w_gate": (D_MODEL, D_FF), "ffn1_w_up": (D_MODEL, D_FF), "ffn1_w_down": (D_FF, D_MODEL),
            "ffn2_w_gate": (D_MODEL, D_FF), "ffn2_w_up": (D_MODEL, D_FF), "ffn2_w_down": (D_FF, D_MODEL),
            "w_in": (D_MODEL, 4 * D_MODEL), "ssm_w_glu": (SSM_WIDTH, SSM_WIDTH), "w_branch_ssm": (SSM_WIDTH, D_MODEL),
            "w_branch_attn": (D_MODEL, D_MODEL), "w_out": (D_MODEL, D_MODEL)}[name]


def _unflatten_gathered(gathered):
    out, r0 = {}, 0
    for name in BIG:
        r = _shard_rows(name)
        k, nn = _full_shape(name)
        piece = gathered[:, r0:r0 + r, :]
        if name in COL_SHARDED:
            out[name] = piece.reshape(4, k, nn // 4).transpose(1, 0, 2).reshape(k, nn)
        else:
            out[name] = piece.reshape(k, nn)
        r0 += r
    return out


FIRST_FFN = ["ffn1_w_gate", "ffn1_w_up", "ffn1_w_down"]
REST_OF_LAYER = [n for n in BIG if n not in FIRST_FFN]


def _flatten_full(grads, names):
    per_shard = []
    for s in range(4):
        pieces = []
        for name in names:
            k, nn = _full_shape(name)
            g = grads[name]
            piece = g[:, s * (nn // 4):(s + 1) * (nn // 4)] if name in COL_SHARDED else g[s * (k // 4):(s + 1) * (k // 4), :]
            pieces.append(piece.reshape(-1, 1024))
        per_shard.append(jnp.concatenate(pieces, axis=0))
    f = jnp.stack(per_shard)
    return f.reshape(4, 2, f.shape[1] // 2, 1024).transpose(1, 0, 2, 3)


def _shard_2d(a):
    return a.reshape(-1, a.shape[-1])


def kernel(x, meta_tokens, ffn1_norm, ffn1_w_gate, ffn1_w_up, ffn1_w_down, mix_norm, w_in, ssm_lam_re, ssm_lam_im, ssm_log_dt, ssm_b_re, ssm_b_im, ssm_c_re, ssm_c_im, ssm_d, ssm_w_glu, attn_sink, w_branch_ssm, w_branch_attn, w_out, ffn2_norm, ffn2_w_gate, ffn2_w_up, ffn2_w_down, final_norm, loss_target, m_meta_tokens, m_ffn1_norm, m_ffn1_w_gate, m_ffn1_w_up, m_ffn1_w_down, m_mix_norm, m_w_in, m_ssm_lam_re, m_ssm_lam_im, m_ssm_log_dt, m_ssm_b_re, m_ssm_b_im, m_ssm_c_re, m_ssm_c_im, m_ssm_d, m_ssm_w_glu, m_attn_sink, m_w_branch_ssm, m_w_branch_attn, m_w_out, m_ffn2_norm, m_ffn2_w_gate, m_ffn2_w_up, m_ffn2_w_down, m_final_norm, v_meta_tokens, v_ffn1_norm, v_ffn1_w_gate, v_ffn1_w_up, v_ffn1_w_down, v_mix_norm, v_w_in, v_ssm_lam_re, v_ssm_lam_im, v_ssm_log_dt, v_ssm_b_re, v_ssm_b_im, v_ssm_c_re, v_ssm_c_im, v_ssm_d, v_ssm_w_glu, v_attn_sink, v_w_branch_ssm, v_w_branch_attn, v_w_out, v_ffn2_norm, v_ffn2_w_gate, v_ffn2_w_up, v_ffn2_w_down, v_final_norm):
    args = dict(locals())
    w = {k: args[k] for k in WEIGHTS}
    mom = {k: args["m_" + k] for k in WEIGHTS}
    var = {k: args["v_" + k] for k in WEIGHTS}
    depth = ffn1_norm.shape[0]
    seq = x.shape[1]
    xi, yi, ci = lax.axis_index("x"), lax.axis_index("y"), lax.axis_index("c")
    chip = 2 * xi + yi
    me = 4 * xi + 2 * yi + ci
    sel = jnp.stack([ci, chip]).astype(jnp.int32)

    same_core = [(fx, fy, 0) for fx, fy in _OTHER_CHIPS]
    meta_all = _all_gather_all("gather_meta", meta_tokens, 4, same_core, lambda x_, y_, c_: _chip(x_, y_))
    meta_full = meta_all.transpose(1, 0, 2).reshape(N_META, D_MODEL)
    flat_w = [jnp.concatenate([w[name][l].reshape(-1, 1024) for name in BIG], axis=0).astype(BF16) for l in range(depth)]

    def layer_weights(l, gathered):
        lw = _unflatten_gathered(gathered)
        for name in ("ffn1_norm", "mix_norm", "ffn2_norm"):
            lw[name] = w[name][l].reshape(1, D_MODEL)
        lw["ssm_d"] = ssm_d[l].reshape(1, SSM_WIDTH)
        lw["attn_sink"] = attn_sink[l]
        return lw

    layer_w = [layer_weights(0, _all_gather_shards("gather_w", flat_w[0]))]

    ssm_params = ("ssm_lam_re", "ssm_lam_im", "ssm_log_dt", "ssm_b_re", "ssm_b_im", "ssm_c_re", "ssm_c_im")
    ssm, ssm_vjp = [], []
    for l in range(depth):
        dirs, vjps = [], []
        for d in range(2):
            prm = tuple(w[k][l, d] for k in ssm_params)
            (ar, ai, wb, wc), pull = jax.vjp(_ssm_prep, *prm)
            a_seg = _complex_power(ar, ai, (seq + BLOCK) // N_SEG)
            conj = lambda v: jnp.concatenate([v[0], -v[1]], axis=1)
            pack = lambda v: jnp.concatenate([v[0], v[1]], axis=1)
            dirs.append({"wb": wb, "wc": wc, "a": pack((ar, ai)), "a_seg": pack(a_seg),
                         "a_conj": conj((ar, ai)), "a_seg_conj": conj(a_seg)})
            vjps.append(pull)
        ssm.append(dirs)
        ssm_vjp.append(vjps)

    h = jnp.concatenate([jnp.zeros((PAD, D_MODEL), F32), meta_full, x[0]], axis=0)
    saved = []
    for l in range(depth):
        lw = layer_w[l]
        h, s1, gathered = _ffn_fwd("ffn", h, lw["ffn1_norm"], lw["ffn1_w_gate"], lw["ffn1_w_up"], lw["ffn1_w_down"],
                                   next_shard=flat_w[l + 1] if l + 1 < depth else None)
        if gathered is not None:
            layer_w.append(layer_weights(l + 1, gathered))
        h, s2 = _mixer_fwd(h, lw, ssm[l])
        h, s3, _ = _ffn_fwd("ffn", h, lw["ffn2_norm"], lw["ffn2_w_gate"], lw["ffn2_w_up"], lw["ffn2_w_down"])
        saved.append((s1, s2, s3))
    dh, loss_part, d_final = _loss_head(h, final_norm.reshape(1, D_MODEL), loss_target[0])
    loss = lax.psum(loss_part[0, 0], MESH_AXES)

    small_g = {k: [None] * depth for k in SMALL if k not in ("meta_tokens", "final_norm")}
    big_g = {k: [None] * depth for k in BIG}
    def keep_shard(l, reduced, names):
        r0 = 0
        for name in names:
            r = _shard_rows(name)
            big_g[name][l] = reduced[r0:r0 + r].reshape(w[name].shape[1:])
            r0 += r

    pending = None
    for l in reversed(range(depth)):
        lw = layer_w[l]
        s1, s2, s3 = saved[l]
        full = {}
        dh, dg, full["ffn2_w_gate"], full["ffn2_w_up"], full["ffn2_w_down"], reduced = _ffn_bwd(
            "ffn", dh, s3, lw["ffn2_norm"], lw["ffn2_w_gate"], lw["ffn2_w_up"], lw["ffn2_w_down"], pending, sel)
        if pending is not None:
            keep_shard(l + 1, reduced, FIRST_FFN)
        small_g["ffn2_norm"][l] = dg[0]
        dh, mg, ssm_cot = _mixer_bwd(dh, s2, lw, ssm[l])
        for k in ("w_out", "w_branch_ssm", "w_branch_attn", "ssm_w_glu", "w_in"):
            full[k] = mg[k]
        small_g["mix_norm"][l] = mg["mix_norm"][0]
        small_g["ssm_d"][l] = mg["ssm_d"][0]
        small_g["attn_sink"][l] = mg["attn_sink"]
        per_dir = []
        for d in range(2):
            per_dir.append(ssm_vjp[l][d](ssm_cot[d]))
        for j, k in enumerate(ssm_params):
            small_g[k][l] = jnp.stack([per_dir[0][j], per_dir[1][j]])
        dh, dg, full["ffn1_w_gate"], full["ffn1_w_up"], full["ffn1_w_down"], reduced = _ffn_bwd(
            "ffn", dh, s1, lw["ffn1_norm"], lw["ffn1_w_gate"], lw["ffn1_w_up"], lw["ffn1_w_down"],
            _flatten_full(full, REST_OF_LAYER), sel)
        keep_shard(l, reduced, REST_OF_LAYER)
        small_g["ffn1_norm"][l] = dg[0]
        pending = _flatten_full(full, FIRST_FFN)
    keep_shard(0, _reduce_scatter("reduce_w", pending, sel), FIRST_FFN)

    grad_x = dh[BLOCK:][None]
    small_list = [dh[PAD:BLOCK].reshape(-1)]
    for k in SMALL[1:]:
        small_list.append(d_final.reshape(-1) if k == "final_norm" else jnp.stack(small_g[k]).reshape(-1))
    small_vec = jnp.concatenate(small_list)
    n_small = small_vec.shape[0]
    rows_small = -(-n_small // (64 * 1024)) * 64
    small_vec = jnp.pad(small_vec, (0, rows_small * 1024 - n_small)).reshape(rows_small, 1024)
    everyone = [(fx, fy, fc) for fx in (0, 1) for fy in (0, 1) for fc in (0, 1)][1:]
    index = lambda x_, y_, c_: 4 * x_ + 2 * y_ + c_
    eighths = small_vec.reshape(8, rows_small // 8, 1024)
    sends = [(f, (lambda x_, y_, c_, f=f: index(x_ ^ f[0], y_ ^ f[1], c_ ^ f[2])), index) for f in everyone]
    got = _exchange("scatter_small", eighths, 8, [], sends)
    mine = lax.broadcasted_iota(jnp.int32, (8, 1, 1), 0) == me
    got = jnp.where(mine, lax.dynamic_index_in_dim(eighths, me, 0, keepdims=True), got)
    part = _slot_sum("sum_small", [(got, k) for k in range(8)], sel, rows_small // 8)
    small_sum = _all_gather_all("gather_small", part, 8, everyone, index).reshape(-1)

    grads, deltas, new_m, new_v = {}, {}, {}, {}
    off = 0
    for k in SMALL:
        size = (N_META * D_MODEL) if k == "meta_tokens" else int(np.prod(w[k].shape))
        g = small_sum[off:off + size]
        off += size
        if k == "meta_tokens":
            g = lax.dynamic_slice(g.reshape(N_META, D_MODEL), (0, chip * (D_MODEL // 4)), (N_META, D_MODEL // 4))
        else:
            g = g.reshape(w[k].shape)
        grads[k] = g
        as_2d = (lambda a: a.reshape(-1, a.shape[-1])) if g.ndim >= 2 else (lambda a: a.reshape(1, -1))
        rows_k = as_2d(g).shape[0]
        d_, m_, v_ = _adamw("adamw_" + k, as_2d(w[k]), as_2d(g), as_2d(mom[k]), as_2d(var[k]), min(rows_k, 2048))
        deltas[k], new_m[k], new_v[k] = d_.reshape(g.shape), m_.reshape(g.shape), v_.reshape(g.shape)
    for k in BIG:
        g = jnp.stack(big_g[k])
        grads[k] = g
        rows_k = _shard_2d(g).shape[0]
        tm = 512 if rows_k % 512 == 0 else rows_k // depth
        d_, m_, v_ = _adamw("adamw_" + k, _shard_2d(w[k]), _shard_2d(g), _shard_2d(mom[k]), _shard_2d(var[k]), tm)
        deltas[k], new_m[k], new_v[k] = d_.reshape(g.shape), m_.reshape(g.shape), v_.reshape(g.shape)

    return (loss, grad_x, *[grads[k] for k in WEIGHTS], *[deltas[k] for k in WEIGHTS],
            *[new_m[k] for k in WEIGHTS], *[new_v[k] for k in WEIGHTS])
```
